```python
import math
import jax
import jax.numpy as jnp
from jax import lax
import numpy as np

D_MODEL = 1024
BATCH = 8
SEQ = 2048
DEPTH = 2
DEC_BATCH = 32
DEC_SEQ = 1
PAST_LEN = 8192
PAGE_SIZE = 128

HEAD_DIM = 64
ATT_WIDTH = D_MODEL // 2
N_HEADS = ATT_WIDTH // HEAD_DIM
N_KV = 2
HG = N_HEADS // N_KV
CMP_BLOCK = 32
CMP_STRIDE = 16
CMP_HID = 128
SLC_BLOCK = 64
N_SEL = 16
WINDOW = 512
Q_BLOCK = 128
FORCE_BONUS = 1e6
SSM_INNER = D_MODEL - ATT_WIDTH
SSM_HEAD_DIM = 64
SSM_HEADS = SSM_INNER // SSM_HEAD_DIM
SSM_GROUPS = 2
D_STATE = 64
SSM_CONV = 4
SSM_CHUNK = 128
CONV_DIM = SSM_INNER + 2 * SSM_GROUPS * D_STATE
D_FF = 2816
FFN_CONV = 3

IN_DIM = ATT_WIDTH + 6 * N_KV * HEAD_DIM + 3 * N_HEADS + SSM_INNER + CONV_DIM + SSM_HEADS
EPS = 1e-6
NEG = -1e30
BIG_POS = 2 ** 30
SCALE = HEAD_DIM ** -0.5

kernel_name = 'nsa_mamba2_hymba_convffn_decode_step'


def _rmsnorm(x, g):
    xf = x.astype(jnp.float32)
    y = xf * lax.rsqrt(jnp.mean(xf * xf, axis=-1, keepdims=True) + EPS)
    return (y * g.astype(jnp.float32)).astype(x.dtype)


def _alibi_slopes():
    h = jnp.arange(1, N_HEADS + 1, dtype=jnp.float32)
    return jnp.exp2(-8.0 * h / N_HEADS).reshape(N_KV, HG)


def _masked_softmax(s, mask):
    p = jax.nn.softmax(jnp.where(mask, s, NEG), axis=-1)
    return jnp.where(mask, p, 0.0)


def _causal_dwconv(u, prev, w, b):
    width, T = w.shape[0], u.shape[1]
    ucat = jnp.concatenate([prev.astype(u.dtype), u], axis=1)
    out = b + sum(ucat[:, k:k + T] * w[k] for k in range(width))
    return out, ucat[:, ucat.shape[1] - (width - 1):]


def _compress(k, w1, b1, w2):
    bsz, L = k.shape[:2]
    nh = -(-L // CMP_STRIDE)
    k = jnp.pad(k, ((0, 0), (0, nh * CMP_STRIDE - L), (0, 0), (0, 0)))
    halves = k.reshape(bsz, nh, CMP_STRIDE, N_KV, HEAD_DIM)
    pa = jnp.einsum('bnjgd,jdf->bngf', halves[:, :-1], w1[:CMP_STRIDE])
    pb = jnp.einsum('bnjgd,jdf->bngf', halves[:, 1:], w1[CMP_STRIDE:])
    hid = jax.nn.gelu(pa + pb + b1)
    return jnp.einsum('bngf,fd->bngd', hid, w2)


def _cmp_branch(q, q_pos, kcmp, vcmp, slopes):
    n = kcmp.shape[1]
    ends = jnp.arange(n, dtype=jnp.int32) * CMP_STRIDE + (CMP_BLOCK - 1)
    s = jnp.einsum('bqgjd,bngd->bqgjn', q, kcmp).astype(jnp.float32) * SCALE
    dist = (q_pos[:, None] - ends[None, :]).astype(jnp.float32)
    s = s - slopes[None, None, :, :, None] * dist[None, :, None, None, :]
    p = _masked_softmax(s, (dist >= 0)[None, :, None, None, :])
    o = jnp.einsum('bqgjn,bngd->bqgjd', p.astype(vcmp.dtype), vcmp)
    return o, p.sum(axis=3)


def _select(p_grp, q_pos, n_slc):
    n_cmp = p_grp.shape[-1]
    cs = jnp.arange(n_cmp) * CMP_STRIDE
    ss = jnp.arange(n_slc) * SLC_BLOCK
    overlap = ((cs[:, None] < ss[None, :] + SLC_BLOCK) &
               (cs[:, None] + CMP_BLOCK > ss[None, :])).astype(jnp.float32)
    imp = jnp.einsum('bqgn,nj->bqgj', p_grp, overlap)
    cur = (q_pos // SLC_BLOCK)[:, None]
    j = jnp.arange(n_slc)[None, :]
    forced = ((j == 0) | (j == cur) | (j == cur - 1))[None, :, None, :]
    valid = (ss[None, :] <= q_pos[:, None])[None, :, None, :]
    score = jnp.where(valid, imp + jnp.where(forced, FORCE_BONUS, 0.0), NEG)
    _, idx = lax.top_k(score, min(N_SEL, n_slc))
    return idx


def _slc_block(q, q_pos, idx, kblk, vblk, slopes):
    s = jnp.einsum('bqgjd,bqgnsd->bqgjns', q, kblk).astype(jnp.float32) * SCALE
    kpos = idx[..., None] * SLC_BLOCK + jnp.arange(SLC_BLOCK)
    dist = q_pos[None, :, None, None, None] - kpos
    s = s - slopes[None, None, :, :, None, None] * dist[:, :, :, None].astype(jnp.float32)
    shp = s.shape
    mask = jnp.broadcast_to((dist >= 0)[:, :, :, None], shp)
    p = _masked_softmax(s.reshape(shp[:4] + (-1,)), mask.reshape(shp[:4] + (-1,))).reshape(shp)
    return jnp.einsum('bqgjns,bqgnsd->bqgjd', p.astype(vblk.dtype), vblk)


def _blocked(fn, q, q_pos, idx):
    bsz, T = q.shape[:2]
    qb = min(Q_BLOCK, T)
    nb = -(-T // qb)
    pad = nb * qb - T

    def split(a):
        a = jnp.pad(a, ((0, 0), (0, pad)) + ((0, 0),) * (a.ndim - 2), mode='edge')
        return jnp.moveaxis(a.reshape((bsz, nb, qb) + a.shape[2:]), 1, 0)

    pos = jnp.pad(q_pos, (0, pad), mode='edge').reshape(nb, qb)
    out = lax.map(lambda a: fn(a[0], a[1], a[2]), (split(q), pos, split(idx)))
    out = jnp.moveaxis(out, 0, 1)
    return out.reshape((bsz, nb * qb) + out.shape[3:])[:, :T]


def _win_branch(q, q_pos, k, v, k_pos, slopes):
    bsz, T = q.shape[:2]
    Lk = k.shape[1]
    q_off = Lk - T
    qb = min(Q_BLOCK, T)
    nb = -(-T // qb)
    pad = nb * qb - T
    qp = jnp.pad(q, ((0, 0), (0, pad), (0, 0), (0, 0), (0, 0)))
    qpos = jnp.pad(q_pos, (0, pad), mode='edge').reshape(nb, qb)
    kv_pad = ((0, 0), (WINDOW, pad), (0, 0), (0, 0))
    kpad, vpad = jnp.pad(k, kv_pad), jnp.pad(v, kv_pad)
    kpos = jnp.concatenate([jnp.full((WINDOW,), -WINDOW - 1, jnp.int32), k_pos.astype(jnp.int32),
                            jnp.full((pad,), BIG_POS, jnp.int32)])
    band = q_off + jnp.arange(nb)[:, None] * qb + jnp.arange(qb + WINDOW)[None, :]
    kb, vb, kposb = kpad[:, band], vpad[:, band], kpos[band]
    qr = qp.reshape(bsz, nb, qb, N_KV, HG, HEAD_DIM)
    s = jnp.einsum('bnqgjd,bnkgd->bnqgjk', qr, kb).astype(jnp.float32) * SCALE
    dist = qpos[:, :, None] - kposb[:, None, :]
    s = s - slopes[None, None, None, :, :, None] * dist[None, :, :, None, None, :].astype(jnp.float32)
    mask = ((dist >= 0) & (dist <= WINDOW))[None, :, :, None, None, :]
    p = _masked_softmax(s, mask)
    o = jnp.einsum('bnqgjk,bnkgd->bnqgjd', p.astype(vb.dtype), vb)
    return o.reshape(bsz, nb * qb, N_KV, HG, HEAD_DIM)[:, :T]


def _ssd(x, dt, A, B, C, h0):
    bsz, T = x.shape[:2]
    l = min(SSM_CHUNK, T)
    nc = -(-T // l)
    pad = nc * l - T
    hg = SSM_HEADS // SSM_GROUPS

    def padt(a):
        return jnp.pad(a.astype(jnp.float32), ((0, 0), (0, pad)) + ((0, 0),) * (a.ndim - 2))

    xr = padt(x).reshape(bsz, nc, l, SSM_GROUPS, hg, SSM_HEAD_DIM)
    dtr = padt(dt).reshape(bsz, nc, l, SSM_GROUPS, hg)
    Br = padt(B).reshape(bsz, nc, l, SSM_GROUPS, D_STATE)
    Cr = padt(C).reshape(bsz, nc, l, SSM_GROUPS, D_STATE)
    cs = jnp.cumsum(dtr * A.reshape(SSM_GROUPS, hg), axis=2)
    tril = jnp.tril(jnp.ones((l, l), dtype=bool))
    seg = cs[:, :, :, None] - cs[:, :, None, :]
    Lm = jnp.exp(jnp.where(tril[None, None, :, :, None, None], seg, NEG))
    CB = jnp.einsum('bclgn,bcsgn->bclsg', Cr, Br)
    y_diag = jnp.einsum('bclsg,bclsgh,bcsgh,bcsghp->bclghp', CB, Lm, dtr, xr)
    st = jnp.einsum('bclgn,bclgh,bclghp->bcghpn', Br, jnp.exp(cs[:, :, -1:] - cs) * dtr, xr)
    dec = jnp.exp(cs[:, :, -1])

    def step(h, inp):
        s_c, d_c = inp
        return h * d_c[..., None, None] + s_c, h

    h_init = h0.astype(jnp.float32).reshape(bsz, SSM_GROUPS, hg, SSM_HEAD_DIM, D_STATE)
    h_fin, h_start = lax.scan(step, h_init, (jnp.moveaxis(st, 1, 0), jnp.moveaxis(dec, 1, 0)))
    h_start = jnp.moveaxis(h_start, 0, 1)
    y_off = jnp.einsum('bclgn,bcghpn,bclgh->bclghp', Cr, h_start, jnp.exp(cs))
    y = (y_diag + y_off).reshape(bsz, nc * l, SSM_HEADS, SSM_HEAD_DIM)[:, :T]
    return y, h_fin.reshape(bsz, SSM_HEADS, SSM_HEAD_DIM, D_STATE)


def _layer(x, c, q_pos, lp, slopes, past):
    bsz, T, _ = x.shape
    mod = jnp.einsum('bd,de->be', jax.nn.silu(c), lp['ada_w']) + lp['ada_b']
    sh1, sc1, g1, sh2, sc2, g2 = [m[:, None, :] for m in jnp.split(mod, 6, axis=-1)]
    h = _rmsnorm(x, lp['norm1_g']) * (1 + sc1) + sh1
    proj = jnp.einsum('btd,de->bte', h, lp['w_in'])
    kvw = N_KV * HEAD_DIM
    sizes = [ATT_WIDTH] + [kvw] * 6 + [3 * N_HEADS, SSM_INNER, CONV_DIM]
    cuts = [sum(sizes[:i + 1]) for i in range(len(sizes))]
    q, kc, vc, ks, vs, kw, vw, gt, z, xbc, dt_raw = jnp.split(proj, cuts, axis=-1)
    q = q.reshape(bsz, T, N_KV, HG, HEAD_DIM)
    kc, vc, ks, vs, kw, vw = [a.reshape(bsz, T, N_KV, HEAD_DIM) for a in (kc, vc, ks, vs, kw, vw)]
    bi = jnp.arange(bsz)[:, None, None, None]
    gi = jnp.arange(N_KV)[None, None, :, None]
    n_new = -(-T // SLC_BLOCK)
    padb = ((0, 0), (0, n_new * SLC_BLOCK - T), (0, 0), (0, 0))
    ksb = jnp.pad(ks, padb).reshape(bsz, n_new, SLC_BLOCK, N_KV, HEAD_DIM)
    vsb = jnp.pad(vs, padb).reshape(bsz, n_new, SLC_BLOCK, N_KV, HEAD_DIM)
    if past is None:
        kc_all, vc_all = kc, vc
        n_slc = n_new

        def gather(ib):
            return ksb[bi, ib, :, gi, :], vsb[bi, ib, :, gi, :]

        kw_all, vw_all, kw_pos = kw, vw, q_pos
        h0 = jnp.zeros((bsz, SSM_HEADS, SSM_HEAD_DIM, D_STATE), jnp.float32)
        conv_prev = jnp.zeros((bsz, SSM_CONV - 1, CONV_DIM), x.dtype)
        ffn_prev = jnp.zeros((bsz, FFN_CONV - 1, D_FF), x.dtype)
    else:
        (pc_k, pc_v, pool_k, pool_v, page_table, win_k, win_v, h0, conv_prev, ffn_prev) = past
        kc_all = jnp.concatenate([pc_k.astype(kc.dtype), kc], axis=1)
        vc_all = jnp.concatenate([pc_v.astype(vc.dtype), vc], axis=1)
        bpp = PAGE_SIZE // SLC_BLOCK
        n_past = page_table.shape[1] * bpp
        n_slc = n_past + n_new

        def gather(ib):
            pi = jnp.minimum(ib, n_past - 1)
            phys = page_table[bi, pi // bpp] * bpp + pi % bpp
            ni = jnp.clip(ib - n_past, 0, n_new - 1)
            is_past = (ib < n_past)[..., None, None]
            kg = jnp.where(is_past, pool_k[phys, :, gi, :].astype(ks.dtype), ksb[bi, ni, :, gi, :])
            vg = jnp.where(is_past, pool_v[phys, :, gi, :].astype(vs.dtype), vsb[bi, ni, :, gi, :])
            return kg, vg

        wb = win_k.shape[1]
        past_len = page_table.shape[1] * PAGE_SIZE
        kw_all = jnp.concatenate([win_k.astype(kw.dtype), kw], axis=1)
        vw_all = jnp.concatenate([win_v.astype(vw.dtype), vw], axis=1)
        kw_pos = jnp.concatenate([past_len - wb + jnp.arange(wb, dtype=jnp.int32), q_pos])

    kcmp = _compress(kc_all, lp['cmpk_w1'], lp['cmpk_b1'], lp['cmpk_w2'])
    vcmp = _compress(vc_all, lp['cmpv_w1'], lp['cmpv_b1'], lp['cmpv_w2'])
    o_cmp, p_grp = _cmp_branch(q, q_pos, kcmp, vcmp, slopes)
    idx = _select(p_grp, q_pos, n_slc)

    def slc_fn(qb_, pb_, ib_):
        kg, vg = gather(ib_)
        return _slc_block(qb_, pb_, ib_, kg, vg, slopes)

    o_slc = _blocked(slc_fn, q, q_pos, idx)
    o_win = _win_branch(q, q_pos, kw_all, vw_all, kw_pos, slopes)
    gates = jax.nn.sigmoid(gt.astype(jnp.float32)).reshape(bsz, T, 3, N_KV, HG, 1).astype(x.dtype)
    o_att = (gates[:, :, 0] * o_cmp + gates[:, :, 1] * o_slc + gates[:, :, 2] * o_win).reshape(bsz, T, ATT_WIDTH)

    xbc, conv_new = _causal_dwconv(xbc, conv_prev, lp['ssm_conv_w'], lp['ssm_conv_b'])
    xbc = jax.nn.silu(xbc)
    xm, Bm, Cm = jnp.split(xbc, [SSM_INNER, SSM_INNER + SSM_GROUPS * D_STATE], axis=-1)
    xm = xm.reshape(bsz, T, SSM_HEADS, SSM_HEAD_DIM)
    Bm = Bm.reshape(bsz, T, SSM_GROUPS, D_STATE)
    Cm = Cm.reshape(bsz, T, SSM_GROUPS, D_STATE)
    dt = jax.nn.softplus(dt_raw.astype(jnp.float32) + lp['dt_bias'].astype(jnp.float32))
    A = -jnp.exp(lp['a_log'].astype(jnp.float32))
    y, h_new = _ssd(xm, dt, A, Bm, Cm, h0)
    y = y + lp['d_skip'].astype(jnp.float32)[:, None] * xm.astype(jnp.float32)
    y = y.reshape(bsz, T, SSM_GROUPS, SSM_INNER // SSM_GROUPS) * \
        jax.nn.silu(z.astype(jnp.float32)).reshape(bsz, T, SSM_GROUPS, SSM_INNER // SSM_GROUPS)
    y = y * lax.rsqrt(jnp.mean(y * y, axis=-1, keepdims=True) + EPS)
    y = (y.reshape(bsz, T, SSM_INNER) * lp['ssm_norm_g'].astype(jnp.float32)).astype(x.dtype)

    mix = jnp.einsum('bte,ed->btd', jnp.concatenate([o_att, y], axis=-1), lp['w_out'])
    x = x + g1 * mix

    h2 = _rmsnorm(x, lp['norm2_g']) * (1 + sc2) + sh2
    up = jnp.einsum('btd,df->btf', h2, lp['ffn_w_up'])
    ug, uv = jnp.split(up, 2, axis=-1)
    ugc, ffn_new = _causal_dwconv(ug, ffn_prev, lp['ffn_conv_w'], lp['ffn_conv_b'])
    x = x + g2 * jnp.einsum('btf,fd->btd', jax.nn.silu(ugc) * uv, lp['ffn_w_down'])

    nk = min(WINDOW, kw_all.shape[1])
    new_state = (kc, vc, ks, vs, kw_all[:, kw_all.shape[1] - nk:], vw_all[:, vw_all.shape[1] - nk:],
                 h_new, conv_new, ffn_new)
    return x, new_state


def setup_inputs(seed: int = 0) -> dict:
    key = jax.random.key(seed)
    keys = iter(jax.random.split(key, 48))

    def nrm(shape, scale):
        return scale * jax.random.normal(next(keys), shape, jnp.float32)

    n_pages = PAST_LEN // PAGE_SIZE
    n_used = DEC_BATCH * n_pages
    n_pool = n_used + -(-n_used // 4)
    wb = min(WINDOW, PAST_LEN)
    page_table = jax.random.permutation(next(keys), n_pool)[:n_used].reshape(DEC_BATCH, n_pages).astype(jnp.int32)
    dt0 = jnp.exp(jax.random.uniform(next(keys), (DEPTH, SSM_HEADS), jnp.float32,
                                     math.log(1e-3), math.log(1e-1)))
    dt_bias = dt0 + jnp.log(-jnp.expm1(-dt0))
    a_log = jnp.log(jax.random.uniform(next(keys), (DEPTH, SSM_HEADS), jnp.float32, 1.0, 16.0))
    cache_shape = (DEPTH, n_pool, PAGE_SIZE, N_KV, HEAD_DIM)
    win_shape = (DEPTH, DEC_BATCH, wb, N_KV, HEAD_DIM)
    return {
        'x_prompt': nrm((BATCH, SEQ, D_MODEL), 1.0),
        'x_sample': nrm((DEC_BATCH, DEC_SEQ, D_MODEL), 1.0),
        'cache_cmp_k': nrm(cache_shape, 1.0),
        'cache_cmp_v': nrm(cache_shape, 1.0),
        'cache_slc_k': nrm(cache_shape, 1.0),
        'cache_slc_v': nrm(cache_shape, 1.0),
        'state_win_k': nrm(win_shape, 1.0),
        'state_win_v': nrm(win_shape, 1.0),
        'state_ssm': nrm((DEPTH, DEC_BATCH, SSM_HEADS, SSM_HEAD_DIM, D_STATE), 0.1),
        'state_ssm_conv': nrm((DEPTH, DEC_BATCH, SSM_CONV - 1, CONV_DIM), 1.0),
        'state_ffn_conv': nrm((DEPTH, DEC_BATCH, FFN_CONV - 1, D_FF), 1.0),
        'page_table': page_table,
        'c_prompt': nrm((BATCH, D_MODEL), 1.0),
        'c_sample': nrm((DEC_BATCH, D_MODEL), 1.0),
        'ada_w': nrm((DEPTH, D_MODEL, 6 * D_MODEL), 0.3 * D_MODEL ** -0.5),
        'ada_b': nrm((DEPTH, 6 * D_MODEL), 0.01),
        'norm1_g': 1.0 + nrm((DEPTH, D_MODEL), 0.01),
        'norm2_g': 1.0 + nrm((DEPTH, D_MODEL), 0.01),
        'w_in': nrm((DEPTH, D_MODEL, IN_DIM), D_MODEL ** -0.5),
        'cmpk_w1': nrm((DEPTH, CMP_BLOCK, HEAD_DIM, CMP_HID), (CMP_BLOCK * HEAD_DIM) ** -0.5),
        'cmpk_b1': nrm((DEPTH, CMP_HID), 0.01),
        'cmpk_w2': nrm((DEPTH, CMP_HID, HEAD_DIM), CMP_HID ** -0.5),
        'cmpv_w1': nrm((DEPTH, CMP_BLOCK, HEAD_DIM, CMP_HID), (CMP_BLOCK * HEAD_DIM) ** -0.5),
        'cmpv_b1': nrm((DEPTH, CMP_HID), 0.01),
        'cmpv_w2': nrm((DEPTH, CMP_HID, HEAD_DIM), CMP_HID ** -0.5),
        'ssm_conv_w': nrm((DEPTH, SSM_CONV, CONV_DIM), SSM_CONV ** -0.5),
        'ssm_conv_b': nrm((DEPTH, CONV_DIM), 0.01),
        'dt_bias': dt_bias,
        'a_log': a_log,
        'd_skip': 1.0 + nrm((DEPTH, SSM_HEADS), 0.1),
        'ssm_norm_g': 1.0 + nrm((DEPTH, SSM_INNER), 0.01),
        'w_out': nrm((DEPTH, ATT_WIDTH + SSM_INNER, D_MODEL), (ATT_WIDTH + SSM_INNER) ** -0.5),
        'ffn_w_up': nrm((DEPTH, D_MODEL, 2 * D_FF), D_MODEL ** -0.5),
        'ffn_conv_w': nrm((DEPTH, FFN_CONV, D_FF), FFN_CONV ** -0.5),
        'ffn_conv_b': nrm((DEPTH, D_FF), 0.01),
        'ffn_w_down': nrm((DEPTH, D_FF, D_MODEL), D_FF ** -0.5),
        'final_g': 1.0 + nrm((D_MODEL,), 0.01),
    }


def reference(x_prompt, x_sample, cache_cmp_k, cache_cmp_v, cache_slc_k, cache_slc_v, state_win_k,
              state_win_v, state_ssm, state_ssm_conv, state_ffn_conv, page_table, c_prompt, c_sample,
              ada_w, ada_b, norm1_g, norm2_g, w_in, cmpk_w1, cmpk_b1, cmpk_w2, cmpv_w1, cmpv_b1, cmpv_w2,
              ssm_conv_w, ssm_conv_b, dt_bias, a_log, d_skip, ssm_norm_g, w_out, ffn_w_up, ffn_conv_w,
              ffn_conv_b, ffn_w_down, final_g):
    slopes = _alibi_slopes()
    ns = x_sample.shape[0]
    past_len = page_table.shape[1] * PAGE_SIZE
    pos_p = jnp.arange(x_prompt.shape[1], dtype=jnp.int32)
    pos_s = past_len + jnp.arange(x_sample.shape[1], dtype=jnp.int32)
    xp, xs = x_prompt, x_sample
    new_p, new_s = [], []
    for l in range(DEPTH):
        lp = dict(ada_w=ada_w[l], ada_b=ada_b[l], norm1_g=norm1_g[l], norm2_g=norm2_g[l], w_in=w_in[l],
                  cmpk_w1=cmpk_w1[l], cmpk_b1=cmpk_b1[l], cmpk_w2=cmpk_w2[l],
                  cmpv_w1=cmpv_w1[l], cmpv_b1=cmpv_b1[l], cmpv_w2=cmpv_w2[l],
                  ssm_conv_w=ssm_conv_w[l], ssm_conv_b=ssm_conv_b[l], dt_bias=dt_bias[l], a_log=a_log[l],
                  d_skip=d_skip[l], ssm_norm_g=ssm_norm_g[l], w_out=w_out[l], ffn_w_up=ffn_w_up[l],
                  ffn_conv_w=ffn_conv_w[l], ffn_conv_b=ffn_conv_b[l], ffn_w_down=ffn_w_down[l])
        xp, st_p = _layer(xp, c_prompt, pos_p, lp, slopes, None)
        past = (cache_cmp_k[l][page_table].reshape(ns, -1, N_KV, HEAD_DIM),
                cache_cmp_v[l][page_table].reshape(ns, -1, N_KV, HEAD_DIM),
                cache_slc_k[l].reshape(-1, SLC_BLOCK, N_KV, HEAD_DIM),
                cache_slc_v[l].reshape(-1, SLC_BLOCK, N_KV, HEAD_DIM),
                page_table, state_win_k[l], state_win_v[l], state_ssm[l], state_ssm_conv[l],
                state_ffn_conv[l])
        xs, st_s = _layer(xs, c_sample, pos_s, lp, slopes, past)
        new_p.append(st_p)
        new_s.append(st_s)
    y_prompt = _rmsnorm(xp, final_g)
    y_sample = _rmsnorm(xs, final_g)
    (cmp_k_p, cmp_v_p, slc_k_p, slc_v_p, win_k_p, win_v_p, ssm_p, ssm_conv_p, ffn_conv_p) = \
        [jnp.stack([st[i] for st in new_p]) for i in range(9)]
    (cmp_k_s, cmp_v_s, slc_k_s, slc_v_s, win_k_s, win_v_s, ssm_s, ssm_conv_s, ffn_conv_s) = \
        [jnp.stack([st[i] for st in new_s]) for i in range(9)]
    return (y_prompt, y_sample, cmp_k_p, cmp_k_s, cmp_v_p, cmp_v_s, slc_k_p, slc_k_s, slc_v_p, slc_v_s,
            win_k_p, win_k_s, win_v_p, win_v_s, ssm_p, ssm_s, ssm_conv_p, ssm_conv_s, ffn_conv_p, ffn_conv_s)
```

```python
import functools

import numpy as np
import jax
import jax.numpy as jnp
from jax import lax
from jax.experimental import pallas as pl
from jax.experimental.pallas import tpu as pltpu

F32 = jnp.float32
BF16 = jnp.bfloat16

D_MODEL = 1024
HEAD_DIM = 64
ATT_WIDTH = 512
N_HEADS = 8
N_KV = 2
HG = 4
CMP_BLOCK = 32
CMP_STRIDE = 16
CMP_HID = 128
SLC_BLOCK = 64
N_SEL = 16
WINDOW = 512
FORCE_BONUS = 1e6
SSM_INNER = 512
SSM_HEADS = 8
SSM_GROUPS = 2
D_STATE = 64
SSM_CONV = 4
SSM_CHUNK = 128
CONV_DIM = 768
D_FF = 2816
FFN_CONV = 3
EPS = 1e-6
NEG = -1e30
SCALE = HEAD_DIM ** -0.5
SLOPES = tuple(2.0 ** (-(h + 1)) for h in range(N_HEADS))

LANES = 128
SUBLANES = 8
VMEM_LIMIT = 56 * 1024 * 1024

P_Q = 0
P_KV = 512
P_Z = 1280
P_XBC = 1792
P_GD = 2560
P_W = 2688
GD_DT = 24
MASK_BIG = 2.0 ** 100


def _cparams(sem):
    return pltpu.CompilerParams(dimension_semantics=sem, vmem_limit_bytes=VMEM_LIMIT)


def _resident(shape):
    nd = len(shape)
    return pl.BlockSpec(shape, lambda *_: (0,) * nd, pipeline_mode=pl.Buffered(1))


def _split3(x):
    hi = x.astype(BF16)
    r = x - hi.astype(F32)
    mid = r.astype(BF16)
    lo = (r - mid.astype(F32)).astype(BF16)
    return hi, mid, lo


def _dot(a, b):
    return jnp.dot(a, b, preferred_element_type=F32)


def _dot_nt(a, b):
    return lax.dot_general(a, b, (((1,), (1,)), ((), ())), preferred_element_type=F32)


def _expand(x, e):
    hi, mid, lo = _split3(x)
    return _dot(hi, e) + _dot(mid, e) + _dot(lo, e)


def _expand_nt(e, x):
    hi, mid, lo = _split3(x)
    return _dot_nt(e, hi) + _dot_nt(e, mid) + _dot_nt(e, lo)


def _silu(x):
    return x * jax.nn.sigmoid(x)


def _softplus(x):
    return jnp.maximum(x, 0.0) + jnp.log1p(jnp.exp(-jnp.abs(x)))


def _gelu_tanh(x):
    return 0.5 * x * (1.0 + jnp.tanh(np.sqrt(2.0 / np.pi) * (x + 0.044715 * (x * x * x))))


def _mod_kernel(c_ref, w_ref, b_ref, o_ref):
    c = c_ref[...]
    s = _silu(c).astype(BF16)
    o_ref[0] = _dot(s, w_ref[0].astype(BF16)) + b_ref[0]


def _mod(c_all, ada_w, ada_b):
    depth, d, n6 = ada_w.shape
    rows = c_all.shape[0]
    tn = 512
    return pl.pallas_call(
        _mod_kernel,
        out_shape=jax.ShapeDtypeStruct((depth, rows, n6), F32),
        grid=(depth, n6 // tn),
        in_specs=[pl.BlockSpec((rows, d), lambda l, j: (0, 0)),
                  pl.BlockSpec((1, d, tn), lambda l, j: (l, 0, j)),
                  pl.BlockSpec((1, 1, tn), lambda l, j: (l, 0, j))],
        out_specs=pl.BlockSpec((1, rows, tn), lambda l, j: (l, 0, j)),
        compiler_params=_cparams(("arbitrary", "arbitrary")),
        name="adaln_mod",
    )(c_all, ada_w, ada_b.reshape(depth, 1, n6))


def _mod_spec(tmod, k, per_row):
    if per_row:
        return pl.BlockSpec((1, tmod, D_MODEL), lambda b, t: (b, t, k))
    return pl.BlockSpec((1, 1, D_MODEL), lambda b, t: (b, 0, k))


def _inproj_kernel(x_ref, sh_ref, sc_ref, g_ref, w_ref,
                   q_ref, kc_ref, vc_ref, ks_ref, vs_ref, kw_ref, vw_ref, kvb_ref, z_ref, xbc_ref, gd_ref):
    x = x_ref[0]
    ms = jnp.mean(x * x, axis=-1, keepdims=True)
    xn = x * lax.rsqrt(ms + EPS) * g_ref[...]
    h = xn * (1.0 + sc_ref[0]) + sh_ref[0]
    proj = _dot(h.astype(BF16), w_ref[...])
    q_ref[0] = (proj[:, P_Q:P_Q + ATT_WIDTH] * SCALE).astype(BF16)
    for i, ref in enumerate((kc_ref, vc_ref, ks_ref, vs_ref, kw_ref, vw_ref)):
        ref[0] = proj[:, P_KV + LANES * i:P_KV + LANES * (i + 1)]
    kvb_ref[0] = proj[:, P_KV:P_Z].astype(BF16)
    z_ref[0] = proj[:, P_Z:P_XBC]
    xbc_ref[0] = proj[:, P_XBC:P_GD]
    gd_ref[0] = proj[:, P_GD:P_W]


def _inproj(x, mod, norm_g, w_packed, tm, per_row):
    bsz, t, d = x.shape
    row = lambda w: pl.BlockSpec((1, tm, w), lambda b, i: (b, i, 0))
    shp = lambda w, dt: jax.ShapeDtypeStruct((bsz, t, w), dt)
    return pl.pallas_call(
        _inproj_kernel,
        out_shape=[shp(ATT_WIDTH, BF16)] + [shp(LANES, F32)] * 6
                  + [shp(6 * LANES, BF16), shp(SSM_INNER, F32), shp(CONV_DIM, F32), shp(LANES, F32)],
        grid=(bsz, t // tm),
        in_specs=[row(d), _mod_spec(tm, 0, per_row), _mod_spec(tm, 1, per_row),
                  _resident((1, d)), _resident(w_packed.shape)],
        out_specs=[row(ATT_WIDTH)] + [row(LANES)] * 6
                  + [row(6 * LANES), row(SSM_INNER), row(CONV_DIM), row(LANES)],
        compiler_params=_cparams(("arbitrary", "arbitrary")),
        name="inproj",
    )(x, mod, mod, norm_g.reshape(1, d), w_packed)


def _compress_block(src_ref, nh, w1_ref, b1_ref, w2_ref):
    acc = jnp.zeros((nh, 4 * CMP_HID), F32)
    for j in range(CMP_STRIDE):
        xj = src_ref[pl.ds(j, nh, stride=CMP_STRIDE), :].astype(BF16)
        acc = acc + _dot(xj, w1_ref[j])
    pa = acc[:, :2 * CMP_HID]
    pb = pltpu.roll(acc[:, 2 * CMP_HID:], nh - 1, 0)
    hid = _gelu_tanh(pa + pb + b1_ref[...])
    return _dot(hid.astype(BF16), w2_ref[...])


def _compress_kernel(k_ref, v_ref, w1k_ref, b1k_ref, w2k_ref, w1v_ref, b1v_ref, w2v_ref, ok_ref, ov_ref, *, nh):
    ok_ref[0] = _compress_block(k_ref.at[0], nh, w1k_ref, b1k_ref, w2k_ref).astype(BF16)
    ov_ref[0] = _compress_block(v_ref.at[0], nh, w1v_ref, b1v_ref, w2v_ref).astype(BF16)


def _compress(kc, vc, cw):
    bsz, t, _ = kc.shape
    nh = t // CMP_STRIDE
    seq = pl.BlockSpec((1, t, LANES), lambda b: (b, 0, 0))
    out = pl.BlockSpec((1, nh, LANES), lambda b: (b, 0, 0))
    ws = [cw[n] for n in ("w1k", "b1k", "w2k", "w1v", "b1v", "w2v")]
    return pl.pallas_call(
        functools.partial(_compress_kernel, nh=nh),
        out_shape=[jax.ShapeDtypeStruct((bsz, nh, LANES), BF16)] * 2,
        grid=(bsz,),
        in_specs=[seq, seq] + [_resident(w.shape) for w in ws],
        out_specs=[out, out],
        compiler_params=_cparams(("arbitrary",)),
        name="compress",
    )(kc, vc, *ws)


def _topk_mask_t(score, jrow, n_slc):
    rank = jnp.zeros(score.shape, F32)
    for jp in range(n_slc):
        sj = score[jp:jp + 1, :]
        ahead = (sj > score) | ((sj == score) & (jp < jrow))
        rank = rank + ahead.astype(F32)
    return (rank < N_SEL) & (jrow < n_slc)


def _cmp_sel_kernel(q_ref, kc_ref, vc_ref, gd_ref, ot_ref, e0_ref, ocmp_ref, selt_ref, *, tq, n_cmp, n_slc):
    t = pl.program_id(1)
    q = q_ref[0]
    kc = kc_ref[0]
    vc = vc_ref[0]
    nl = kc.shape[0]
    qpos = t * tq + lax.broadcasted_iota(jnp.int32, (tq, nl), 0)
    nidx = lax.broadcasted_iota(jnp.int32, (tq, nl), 1)
    dist_i = qpos - (nidx * CMP_STRIDE + (CMP_BLOCK - 1))
    mask = (dist_i >= 0) & (nidx < n_cmp)
    dist = dist_i.astype(F32)
    gate = _expand(jax.nn.sigmoid(gd_ref[0]), e0_ref[...])

    js = selt_ref.shape[2]
    jrow = lax.broadcasted_iota(jnp.int32, (js, tq), 0)
    qp = t * tq + lax.broadcasted_iota(jnp.int32, (js, tq), 1)
    cur = qp // SLC_BLOCK
    forced = (jrow == 0) | (jrow == cur) | (jrow == cur - 1)
    valid = (jrow * SLC_BLOCK <= qp)

    for g in range(N_KV):
        kg = kc[:, HEAD_DIM * g:HEAD_DIM * (g + 1)]
        vg = vc[:, HEAD_DIM * g:HEAD_DIM * (g + 1)]
        pg = jnp.zeros((tq, nl), F32)
        for j in range(HG):
            h = HG * g + j
            lo, hi = HEAD_DIM * h, HEAD_DIM * (h + 1)
            s = _dot_nt(q[:, lo:hi], kg) - SLOPES[h] * dist
            sm = jnp.where(mask, s, NEG)
            e = jnp.exp(sm - jnp.max(sm, axis=-1, keepdims=True))
            p = jnp.where(mask, e / jnp.sum(e, axis=-1, keepdims=True), 0.0)
            ocmp_ref[0, :, lo:hi] = gate[:, lo:hi] * _dot(p.astype(BF16), vg)
            pg = pg + p
        imp_t = _expand_nt(ot_ref[...], pg)
        score = jnp.where(valid, imp_t + jnp.where(forced, FORCE_BONUS, 0.0), NEG)
        selt_ref[0, g] = _topk_mask_t(score, jrow, n_slc).astype(F32)


def _cmp_sel(q, kcmp, vcmp, gd, consts, tq):
    bsz, t, _ = q.shape
    nl = kcmp.shape[1]
    n_slc = t // SLC_BLOCK
    js = consts["ot"].shape[0]
    tile = lambda w: pl.BlockSpec((1, tq, w), lambda b, i: (b, i, 0))
    seq = pl.BlockSpec((1, nl, LANES), lambda b, i: (b, 0, 0))
    return pl.pallas_call(
        functools.partial(_cmp_sel_kernel, tq=tq, n_cmp=t // CMP_STRIDE - 1, n_slc=n_slc),
        out_shape=[jax.ShapeDtypeStruct((bsz, t, ATT_WIDTH), F32),
                   jax.ShapeDtypeStruct((bsz, N_KV, js, t), F32)],
        grid=(bsz, t // tq),
        in_specs=[tile(ATT_WIDTH), seq, seq, tile(LANES),
                  _resident(consts["ot"].shape), _resident(consts["e_gate"][0].shape)],
        out_specs=[tile(ATT_WIDTH), pl.BlockSpec((1, N_KV, js, tq), lambda b, i: (b, 0, 0, i))],
        compiler_params=_cparams(("arbitrary", "arbitrary")),
        name="cmp_select",
    )(q, kcmp, vcmp, gd, consts["ot"], consts["e_gate"][0])


def _flash_tile(q4, k_t, v_t, bias, state):
    m, l, acc = state
    s = _dot_nt(q4, k_t) + bias
    m_new = jnp.maximum(m, jnp.max(s, axis=-1, keepdims=True))
    p = jnp.exp(s - m_new)
    alpha = jnp.exp(m - m_new)
    l = alpha * l + jnp.sum(p, axis=-1, keepdims=True)
    acc = alpha * acc + _dot(p.astype(BF16), v_t)
    return m_new, l, acc


def _slc_win_kernel(q_ref, kv_ref, selt_ref, ocmp_ref, gd_ref, eb_ref, e1_ref, e2_ref, o_ref, *, tq):
    t = pl.program_id(1)
    q = q_ref[0]
    g_all = jax.nn.sigmoid(gd_ref[0])
    gate1 = _expand(g_all, e1_ref[...])
    gate2 = _expand(g_all, e2_ref[...])
    rows = HG * tq
    ri = lax.broadcasted_iota(jnp.int32, (rows, tq), 0)
    ci = lax.broadcasted_iota(jnp.int32, (rows, tq), 1)
    rel0 = (ci - ri % tq).astype(F32)
    head = ri // tq
    eye = (lax.broadcasted_iota(jnp.int32, (tq, tq), 0)
           == lax.broadcasted_iota(jnp.int32, (tq, tq), 1)).astype(BF16)
    init = (jnp.full((rows, 1), NEG, F32), jnp.zeros((rows, 1), F32), jnp.zeros((rows, HEAD_DIM), F32))

    for g in range(N_KV):
        slope = jnp.zeros((rows, tq), F32)
        for j in range(HG):
            slope = jnp.where(head == j, SLOPES[HG * g + j], slope)
        q4 = jnp.concatenate([q[:, HEAD_DIM * (HG * g + j):HEAD_DIM * (HG * g + j + 1)] for j in range(HG)], axis=0)
        sel = _dot_nt(eye, selt_ref[0, g].astype(BF16))
        notsel = (1.0 - sel).astype(BF16)
        notsel4 = jnp.concatenate([notsel] * HG, axis=0)

        def kv_tile(kt, lane0):
            r0 = pl.multiple_of(kt * tq, tq)
            return kv_ref[0, pl.ds(r0, tq), lane0:lane0 + HEAD_DIM]

        def slc_bias(kt):
            rel = rel0 + ((kt - t) * tq).astype(F32)
            return rel, slope * rel + _dot(notsel4, eb_ref[kt])

        def slc_body(kt, st):
            _, bias = slc_bias(kt)
            return _flash_tile(q4, kv_tile(kt, 2 * LANES + HEAD_DIM * g), kv_tile(kt, 3 * LANES + HEAD_DIM * g), bias, st)

        st = lax.fori_loop(0, t, slc_body, init)
        rel, bias = slc_bias(t)
        bias = jnp.where(rel <= 0.0, bias, NEG)
        _, l_s, acc_s = _flash_tile(q4, kv_tile(t, 2 * LANES + HEAD_DIM * g), kv_tile(t, 3 * LANES + HEAD_DIM * g), bias, st)
        o_slc = acc_s / l_s

        def win_body(kt, st):
            rel = rel0 + ((kt - t) * tq).astype(F32)
            bias = jnp.where((rel <= 0.0) & (rel >= -float(WINDOW)), slope * rel, NEG)
            return _flash_tile(q4, kv_tile(kt, 4 * LANES + HEAD_DIM * g), kv_tile(kt, 5 * LANES + HEAD_DIM * g), bias, st)

        _, l_w, acc_w = lax.fori_loop(jnp.maximum(t - WINDOW // tq, 0), t + 1, win_body, init)
        o_win = acc_w / l_w

        for j in range(HG):
            h = HG * g + j
            lo, hi = HEAD_DIM * h, HEAD_DIM * (h + 1)
            r = slice(j * tq, (j + 1) * tq)
            o_ref[0, :, lo:hi] = (ocmp_ref[0, :, lo:hi] + gate1[:, lo:hi] * o_slc[r]
                                  + gate2[:, lo:hi] * o_win[r]).astype(BF16)


def _slc_win(q, kvb, selt, ocmp, gd, consts, tq):
    bsz, t, _ = q.shape
    js = selt.shape[2]
    tile = lambda w: pl.BlockSpec((1, tq, w), lambda b, i: (b, i, 0))
    return pl.pallas_call(
        functools.partial(_slc_win_kernel, tq=tq),
        out_shape=jax.ShapeDtypeStruct((bsz, t, ATT_WIDTH), BF16),
        grid=(bsz, t // tq),
        in_specs=[tile(ATT_WIDTH), pl.BlockSpec((1, t, 6 * LANES), lambda b, i: (b, 0, 0)),
                  pl.BlockSpec((1, N_KV, js, tq), lambda b, i: (b, 0, 0, i)), tile(ATT_WIDTH), tile(LANES),
                  _resident(consts["eb"].shape), _resident(consts["e_gate"][1].shape),
                  _resident(consts["e_gate"][2].shape)],
        out_specs=tile(ATT_WIDTH),
        compiler_params=_cparams(("arbitrary", "arbitrary")),
        name="slc_win_attn",
    )(q, kvb, selt, ocmp, gd, consts["eb"], consts["e_gate"][1], consts["e_gate"][2])


def _gated_norm(y, z, ng):
    yz = y * _silu(z)
    half = SSM_INNER // SSM_GROUPS
    outs = []
    for g in range(SSM_GROUPS):
        part = yz[:, half * g:half * (g + 1)]
        ms = jnp.mean(part * part, axis=-1, keepdims=True)
        outs.append(part * lax.rsqrt(ms + EPS))
    return jnp.concatenate(outs, axis=1) * ng


def _ssd_kernel(xbc_ref, z_ref, gd_ref, cw_ref, cb_ref, dtb_ref, a_ref, dsk_ref, ng_ref, edt_ref,
                y_ref, hfin_ref, ubuf, sstate, ybuf, *, lc):
    t = pl.program_id(1)
    nt = pl.num_programs(1)

    @pl.when(t == 0)
    def _():
        ubuf[0:SUBLANES, :] = jnp.zeros((SUBLANES, CONV_DIM), F32)
        sstate[...] = jnp.zeros(sstate.shape, F32)

    ubuf[SUBLANES:SUBLANES + lc, :] = xbc_ref[0]
    conv = cb_ref[...] + jnp.zeros((lc, CONV_DIM), F32)
    for k in range(SSM_CONV):
        off = SUBLANES - (SSM_CONV - 1) + k
        conv = conv + ubuf[off:off + lc, :] * cw_ref[k:k + 1, :]
    ubuf[0:SUBLANES, :] = ubuf[lc:lc + SUBLANES, :]
    act = _silu(conv)
    xm = act[:, :SSM_INNER]
    bm = act[:, SSM_INNER:SSM_INNER + LANES]
    cm = act[:, SSM_INNER + LANES:]

    dt = _softplus(gd_ref[0] + dtb_ref[...])
    a = dt * a_ref[...]
    ri = lax.broadcasted_iota(jnp.int32, (lc, lc), 0)
    ci = lax.broadcasted_iota(jnp.int32, (lc, lc), 1)
    tril = ri >= ci
    cs = jnp.dot(tril.astype(F32), a, preferred_element_type=F32, precision=lax.Precision.HIGHEST)
    cs_t = cs.T
    edt = edt_ref[...]
    dt_x = _expand(dt, edt)
    cs_x = _expand(cs, edt)
    cs_last = cs_x[lc - 1:lc, :]
    bm_t = bm.T
    xd = xm * dt_x
    xw = xm * (jnp.exp(cs_last - cs_x) * dt_x)
    s_old = sstate[...]
    half = SSM_INNER // SSM_GROUPS
    for g in range(SSM_GROUPS):
        bg = bm[:, D_STATE * g:D_STATE * (g + 1)].astype(BF16)
        cg = cm[:, D_STATE * g:D_STATE * (g + 1)].astype(BF16)
        cb = _dot_nt(cg, bg)
        for j in range(SSM_HEADS // SSM_GROUPS):
            h = g * (SSM_HEADS // SSM_GROUPS) + j
            col = cs[:, GD_DT + h:GD_DT + h + 1]
            row = cs_t[GD_DT + h:GD_DT + h + 1, :]
            lm = jnp.exp(jnp.where(tril, col - row, NEG))
            lo, hi = HEAD_DIM * h, HEAD_DIM * (h + 1)
            ybuf[:, lo:hi] = _dot((cb * lm).astype(BF16), xd[:, lo:hi].astype(BF16))
        sg = s_old[:, half * g:half * (g + 1)]
        y_off = _dot(cg, sg.astype(BF16))
        ybuf[:, half * g:half * (g + 1)] = ybuf[:, half * g:half * (g + 1)] + y_off * jnp.exp(cs_x[:, half * g:half * (g + 1)])
        st = _dot(bm_t[D_STATE * g:D_STATE * (g + 1), :].astype(BF16), xw[:, half * g:half * (g + 1)].astype(BF16))
        sstate[:, half * g:half * (g + 1)] = sg * jnp.exp(cs_last[:, half * g:half * (g + 1)]) + st

    y = ybuf[...] + dsk_ref[...] * xm
    y_ref[0] = _gated_norm(y, z_ref[0], ng_ref[...]).astype(BF16)

    @pl.when(t == nt - 1)
    def _():
        s_pad = jnp.concatenate([sstate[...], jnp.zeros((LANES - D_STATE, SSM_INNER), F32)], axis=0)
        hfin_ref[0] = s_pad.T[:, :D_STATE]


def _ssd(xbc, z, gd, sw, consts):
    bsz, t, _ = xbc.shape
    lc = SSM_CHUNK
    tile = lambda w: pl.BlockSpec((1, lc, w), lambda b, i: (b, i, 0))
    ws = [sw[n] for n in ("conv_w", "conv_b", "dt_bias", "a", "d_skip", "norm_g")] + [consts["e_dt"]]
    y, hfin = pl.pallas_call(
        functools.partial(_ssd_kernel, lc=lc),
        out_shape=[jax.ShapeDtypeStruct((bsz, t, SSM_INNER), BF16),
                   jax.ShapeDtypeStruct((bsz, SSM_INNER, D_STATE), F32)],
        grid=(bsz, t // lc),
        in_specs=[tile(CONV_DIM), tile(SSM_INNER), tile(LANES)] + [_resident(w.shape) for w in ws],
        out_specs=[tile(SSM_INNER), pl.BlockSpec((1, SSM_INNER, D_STATE), lambda b, i: (b, 0, 0))],
        scratch_shapes=[pltpu.VMEM((lc + SUBLANES, CONV_DIM), F32),
                        pltpu.VMEM((D_STATE, SSM_INNER), F32),
                        pltpu.VMEM((lc, SSM_INNER), F32)],
        compiler_params=_cparams(("arbitrary", "arbitrary")),
        name="ssd_scan",
    )(xbc, z, gd, *ws)
    return y, hfin.reshape(bsz, SSM_HEADS, HEAD_DIM, D_STATE)


def _mix_ffn_kernel(*refs, tm, carry_conv, last):
    if carry_conv:
        (x_ref, oa_ref, ys_ref, g1_ref, sh2_ref, sc2_ref, g2_ref, n2_ref, wo_ref, wu_ref, fcw_ref, fcb_ref,
         wd_ref, fg_ref, out_ref, ug_ref, ubuf) = refs
    else:
        (x_ref, oa_ref, ys_ref, g1_ref, sh2_ref, sc2_ref, g2_ref, n2_ref, wo_ref, wu_ref, fcw_ref, fcb_ref,
         wd_ref, fg_ref, p0_ref, p1_ref, out_ref, ug_ref) = refs
    t = pl.program_id(1)
    nt = pl.num_programs(1)
    x = x_ref[0]
    mix = _dot(oa_ref[0], wo_ref[:ATT_WIDTH, :]) + _dot(ys_ref[0], wo_ref[ATT_WIDTH:, :])
    x1 = x + g1_ref[0] * mix
    ms = jnp.mean(x1 * x1, axis=-1, keepdims=True)
    h2 = (x1 * lax.rsqrt(ms + EPS) * n2_ref[...]) * (1.0 + sc2_ref[0]) + sh2_ref[0]
    h2 = h2.astype(BF16)
    ug = _dot(h2, wu_ref[:, :D_FF])
    uv = _dot(h2, wu_ref[:, D_FF:])
    if carry_conv:
        @pl.when(t == 0)
        def _():
            ubuf[0:SUBLANES, :] = jnp.zeros((SUBLANES, D_FF), F32)

        ubuf[SUBLANES:SUBLANES + tm, :] = ug
        u1 = ubuf[SUBLANES - 1:SUBLANES - 1 + tm, :]
        u2 = ubuf[SUBLANES - 2:SUBLANES - 2 + tm, :]
        ubuf[0:SUBLANES, :] = ubuf[tm:tm + SUBLANES, :]

        @pl.when(t == nt - 1)
        def _():
            ug_ref[0] = ug[tm - SUBLANES:, :]
    else:
        u1 = p1_ref[0]
        u2 = p0_ref[0]
        ug_ref[0] = ug
    ugc = fcb_ref[...] + u2 * fcw_ref[0:1, :] + u1 * fcw_ref[1:2, :] + ug * fcw_ref[2:3, :]
    act = (_silu(ugc) * uv).astype(BF16)
    x2 = x1 + g2_ref[0] * _dot(act, wd_ref[...])
    if last:
        ms2 = jnp.mean(x2 * x2, axis=-1, keepdims=True)
        out_ref[0] = x2 * lax.rsqrt(ms2 + EPS) * fg_ref[...]
    else:
        out_ref[0] = x2


def _mix_ffn(x, oatt, yssm, mod, fw, tm, per_row, prev, last):
    bsz, t, d = x.shape
    carry = prev is None
    row = lambda w: pl.BlockSpec((1, tm, w), lambda b, i: (b, i, 0))
    ws = [fw[n] for n in ("norm2_g", "w_out", "w_up", "conv_w", "conv_b", "w_down", "final_g")]
    in_specs = ([row(d), row(ATT_WIDTH), row(SSM_INNER), _mod_spec(tm, 2, per_row), _mod_spec(tm, 3, per_row),
                 _mod_spec(tm, 4, per_row), _mod_spec(tm, 5, per_row)] + [_resident(w.shape) for w in ws])
    args = [x, oatt, yssm, mod, mod, mod, mod] + ws
    if carry:
        ug_shape, ug_spec = (bsz, SUBLANES, D_FF), pl.BlockSpec((1, SUBLANES, D_FF), lambda b, i: (b, 0, 0))
        scratch = [pltpu.VMEM((tm + SUBLANES, D_FF), F32)]
    else:
        ug_shape, ug_spec = (bsz, t, D_FF), row(D_FF)
        scratch = []
        in_specs += [row(D_FF), row(D_FF)]
        args += list(prev)
    return pl.pallas_call(
        functools.partial(_mix_ffn_kernel, tm=tm, carry_conv=carry, last=last),
        out_shape=[jax.ShapeDtypeStruct((bsz, t, d), F32), jax.ShapeDtypeStruct(ug_shape, F32)],
        grid=(bsz, t // tm),
        in_specs=in_specs,
        out_specs=[row(d), ug_spec],
        scratch_shapes=scratch,
        compiler_params=_cparams(("arbitrary", "arbitrary")),
        name="mix_ffn",
    )(*args)


def _head_rows(row512):
    r = lax.broadcasted_iota(jnp.int32, (N_HEADS, ATT_WIDTH), 0)
    c = lax.broadcasted_iota(jnp.int32, (N_HEADS, ATT_WIDTH), 1)
    return jnp.where(c // HEAD_DIM == r, jnp.broadcast_to(row512, (N_HEADS, ATT_WIDTH)), 0.0)


def _head_diag(x8):
    r = lax.broadcasted_iota(jnp.int32, (N_HEADS, ATT_WIDTH), 0)
    c = lax.broadcasted_iota(jnp.int32, (N_HEADS, ATT_WIDTH), 1)
    return jnp.sum(jnp.where(c // HEAD_DIM == r, x8, 0.0), axis=0, keepdims=True)


def _slope_col():
    r = lax.broadcasted_iota(jnp.int32, (N_HEADS, 1), 0)
    s = jnp.zeros((N_HEADS, 1), F32)
    for h in range(N_HEADS):
        s = jnp.where(r == h, SLOPES[h], s)
    return s


def _s_cmp_kernel(pt_ref, q_ref, kvn_ref, gd_ref, ck_hbm, cv_hbm,
                  w1k_ref, b1k_ref, w2k_ref, w1v_ref, b1v_ref, w2v_ref, ov_ref, u_ref, e0_ref,
                  ocmp_ref, idx_ref, kbuf, vbuf, sem, *, n_pages, page, nhp, q_pos, n_slc):
    b = pl.program_id(0)
    past = n_pages * page

    def copies(p):
        pg = pt_ref[b * n_pages + p]
        dst = pl.ds(pl.multiple_of(p * page, page), page)
        return (pltpu.make_async_copy(ck_hbm.at[pg], kbuf.at[dst], sem.at[0]),
                pltpu.make_async_copy(cv_hbm.at[pg], vbuf.at[dst], sem.at[1]))

    def start(p, c):
        for cp in copies(p):
            cp.start()
        return c

    lax.fori_loop(0, n_pages, start, 0)
    tail = nhp * CMP_STRIDE - past
    kbuf[past:, :] = jnp.zeros((tail, LANES), F32)
    vbuf[past:, :] = jnp.zeros((tail, LANES), F32)
    kbuf[past:past + 1, :] = kvn_ref[0, :, 0:LANES]
    vbuf[past:past + 1, :] = kvn_ref[0, :, LANES:2 * LANES]

    def wait(p, c):
        for cp in copies(p):
            cp.wait()
        return c

    lax.fori_loop(0, n_pages, wait, 0)

    kc4 = _compress_block(kbuf, nhp, w1k_ref, b1k_ref, w2k_ref).astype(BF16)
    vc4 = _compress_block(vbuf, nhp, w1v_ref, b1v_ref, w2v_ref).astype(BF16)
    n_cmp = (past + 1 + CMP_STRIDE - 1) // CMP_STRIDE - 1

    qm = _head_rows(q_ref[0].astype(F32)).astype(BF16)
    nidx = lax.broadcasted_iota(jnp.int32, (N_HEADS, nhp), 1)
    dist_i = q_pos - (nidx * CMP_STRIDE + (CMP_BLOCK - 1))
    mask = (dist_i >= 0) & (nidx < n_cmp)
    s = _dot_nt(qm, kc4) - _slope_col() * dist_i.astype(F32)
    sm = jnp.where(mask, s, NEG)
    e = jnp.exp(sm - jnp.max(sm, axis=-1, keepdims=True))
    p = jnp.where(mask, e / jnp.sum(e, axis=-1, keepdims=True), 0.0)
    o = _head_diag(_dot(p.astype(BF16), vc4))
    g8 = jnp.broadcast_to(jax.nn.sigmoid(gd_ref[0]), (SUBLANES, LANES))
    gate = _expand(g8, e0_ref[...])[0:1, :]
    ocmp_ref[0] = gate * o

    js = ov_ref.shape[1]
    imp8 = _expand(p, ov_ref[...])
    hrow = lax.broadcasted_iota(jnp.int32, (N_HEADS, js), 0)
    jl = lax.broadcasted_iota(jnp.int32, (1, js), 1)
    cur = q_pos // SLC_BLOCK
    forced = (jl == 0) | (jl == cur) | (jl == cur - 1)
    valid = (jl * SLC_BLOCK <= q_pos) & (jl < n_slc)
    rj = lax.broadcasted_iota(jnp.int32, (js, js), 0)
    cj = lax.broadcasted_iota(jnp.int32, (js, js), 1)
    kk = lax.broadcasted_iota(jnp.int32, (2 * SUBLANES, js), 0)
    jvals = jnp.broadcast_to(jl.astype(F32), (SUBLANES, js)).astype(BF16)
    for g in range(N_KV):
        imp = jnp.sum(jnp.where(hrow // HG == g, imp8, 0.0), axis=0, keepdims=True)
        score = jnp.where(valid, imp + jnp.where(forced, FORCE_BONUS, 0.0), NEG)
        sb = jnp.broadcast_to(score, (js, js))
        col = jnp.sum(jnp.where(rj == cj, sb, 0.0), axis=1, keepdims=True)
        ahead = (col > sb) | ((col == sb) & (rj < cj))
        rank = jnp.sum(ahead.astype(F32), axis=0, keepdims=True)
        sel = ((rank < N_SEL) & (jl < n_slc)).astype(F32)
        pos = _dot(jnp.broadcast_to(sel, (SUBLANES, js)).astype(BF16), u_ref[...])[0:1, :]
        onehot = ((jnp.broadcast_to(pos, (2 * SUBLANES, js)) == kk.astype(F32))
                  & (jnp.broadcast_to(sel, (2 * SUBLANES, js)) > 0.5)).astype(BF16)
        idx = _dot_nt(jvals, onehot)[0:1, :]
        idx_ref[0, g:g + 1, :] = idx.astype(jnp.int32)


def _s_cmp(page_table, q, kvn, gd, cache_k, cache_v, cw4, consts, q_pos):
    ns, n_pages = page_table.shape
    n_pool, page = cache_k.shape[0], cache_k.shape[1]
    past = n_pages * page
    nh = (past + 1 + CMP_STRIDE - 1) // CMP_STRIDE
    nhp = -(-nh // LANES) * LANES
    n_slc = past // SLC_BLOCK + 1
    ws = [cw4[n] for n in ("w1k", "b1k", "w2k", "w1v", "b1v", "w2v")] + [consts["ov_s"], consts["u_s"], consts["e_gate"][0]]
    full = lambda a: pl.BlockSpec(a.shape, lambda b, pt: (0,) * a.ndim)
    per_b = lambda a: pl.BlockSpec((1,) + a.shape[1:], lambda b, pt: (b,) + (0,) * (a.ndim - 1))
    grid_spec = pltpu.PrefetchScalarGridSpec(
        num_scalar_prefetch=1,
        grid=(ns,),
        in_specs=[per_b(q), per_b(kvn), per_b(gd), pl.BlockSpec(memory_space=pl.ANY), pl.BlockSpec(memory_space=pl.ANY)]
                 + [full(w) for w in ws],
        out_specs=[pl.BlockSpec((1, 1, ATT_WIDTH), lambda b, pt: (b, 0, 0)),
                   pl.BlockSpec((1, N_KV, N_SEL), lambda b, pt: (b, 0, 0))],
        scratch_shapes=[pltpu.VMEM((nhp * CMP_STRIDE, LANES), F32), pltpu.VMEM((nhp * CMP_STRIDE, LANES), F32),
                        pltpu.SemaphoreType.DMA((2,))],
    )
    return pl.pallas_call(
        functools.partial(_s_cmp_kernel, n_pages=n_pages, page=page, nhp=nhp, q_pos=q_pos, n_slc=n_slc),
        out_shape=[jax.ShapeDtypeStruct((ns, 1, ATT_WIDTH), F32), jax.ShapeDtypeStruct((ns, N_KV, N_SEL), jnp.int32)],
        grid_spec=grid_spec,
        compiler_params=_cparams(("arbitrary",)),
        name="sample_cmp_select",
    )(page_table.reshape(-1), q, kvn, gd, cache_k, cache_v, *ws)


def _s_slc_win_kernel(idx_ref, pt_ref, q_ref, kvn_ref, gd_ref, ocmp_ref, wk_ref, wv_ref, sk_hbm, sv_hbm,
                      tile_ref, e1_ref, e2_ref, o_ref, wko_ref, wvo_ref, kb, vb, sem,
                      *, n_pages, page, q_pos, wb):
    b = pl.program_id(0)
    bpp = page // SLC_BLOCK
    n_past = n_pages * bpp
    nblk = N_KV * N_SEL

    def copies(i):
        blk = jnp.minimum(idx_ref[b * nblk + i], n_past - 1)
        pg = pt_ref[b * n_pages + blk // bpp]
        src = pl.ds(pl.multiple_of((blk % bpp) * SLC_BLOCK, SLC_BLOCK), SLC_BLOCK)
        return (pltpu.make_async_copy(sk_hbm.at[pg, src], kb.at[i], sem.at[0]),
                pltpu.make_async_copy(sv_hbm.at[pg, src], vb.at[i], sem.at[1]))

    def start(i, c):
        for cp in copies(i):
            cp.start()
        return c

    lax.fori_loop(0, nblk, start, 0)

    qm = _head_rows(q_ref[0].astype(F32)).astype(BF16)
    slope = _slope_col()
    tile_m = tile_ref[...]
    g8 = jnp.broadcast_to(jax.nn.sigmoid(gd_ref[0]), (SUBLANES, LANES))
    gate1 = _expand(g8, e1_ref[...])[0:1, :]
    gate2 = _expand(g8, e2_ref[...])[0:1, :]
    hrow = lax.broadcasted_iota(jnp.int32, (N_HEADS, 1), 0)

    def new_key(lane0):
        kn = _dot(jnp.broadcast_to(kvn_ref[0, :, lane0:lane0 + LANES], (SUBLANES, LANES)).astype(BF16), tile_m)
        vn = _dot(jnp.broadcast_to(kvn_ref[0, :, lane0 + LANES:lane0 + 2 * LANES], (SUBLANES, LANES)).astype(BF16), tile_m)
        s_new = jnp.sum(qm.astype(F32) * kn[0:1, :].astype(BF16).astype(F32), axis=-1, keepdims=True)
        return s_new, vn[0:1, :].astype(BF16).astype(F32)

    wk = wk_ref[0]
    wv = wv_ref[0]
    k4 = _dot(wk.astype(BF16), tile_m).astype(BF16)
    v4 = _dot(wv.astype(BF16), tile_m).astype(BF16)
    kpos = (q_pos - wb) + lax.broadcasted_iota(jnp.int32, (N_HEADS, wb), 1)
    dist = q_pos - kpos
    wmask = (dist >= 0) & (dist <= WINDOW)
    s = jnp.where(wmask, _dot_nt(qm, k4) - slope * dist.astype(F32), NEG)
    s_new, v_new = new_key(4 * LANES)
    m = jnp.maximum(jnp.max(s, axis=-1, keepdims=True), s_new)
    e = jnp.where(wmask, jnp.exp(s - m), 0.0)
    e_new = jnp.exp(s_new - m)
    l = jnp.sum(e, axis=-1, keepdims=True) + e_new
    o_win = _head_diag((_dot(e.astype(BF16), v4) + e_new * v_new) / l)
    wko_ref[0, 0:wb - 1, :] = wk_ref[0, 1:wb, :]
    wko_ref[0, wb - 1:wb, :] = kvn_ref[0, :, 4 * LANES:5 * LANES]
    wvo_ref[0, 0:wb - 1, :] = wv_ref[0, 1:wb, :]
    wvo_ref[0, wb - 1:wb, :] = kvn_ref[0, :, 5 * LANES:6 * LANES]

    def wait(i, c):
        for cp in copies(i):
            cp.wait()
        return c

    lax.fori_loop(0, nblk, wait, 0)
    nk = N_SEL * SLC_BLOCK
    lane = lax.broadcasted_iota(jnp.int32, (1, nk), 1)
    s_new, v_new = new_key(2 * LANES)
    o_slc = jnp.zeros((1, ATT_WIDTH), F32)
    gsel = lax.broadcasted_iota(jnp.int32, (1, ATT_WIDTH), 1) // (HG * HEAD_DIM)
    for g in range(N_KV):
        kpos = lane % SLC_BLOCK
        is_new = jnp.zeros((1, 1), jnp.int32)
        for k in range(N_SEL):
            blk = idx_ref[b * nblk + g * N_SEL + k]
            kpos = kpos + jnp.where(lane // SLC_BLOCK == k, blk * SLC_BLOCK, 0)
            is_new = jnp.maximum(is_new, (blk >= n_past).astype(jnp.int32))
        dist = q_pos - kpos
        kmask = (dist >= 0) & (kpos < n_past * SLC_BLOCK)
        kg = kb[g * N_SEL:(g + 1) * N_SEL].reshape(nk, LANES)
        vg = vb[g * N_SEL:(g + 1) * N_SEL].reshape(nk, LANES)
        k4 = _dot(kg.astype(BF16), tile_m).astype(BF16)
        v4 = _dot(vg.astype(BF16), tile_m).astype(BF16)
        s = jnp.where(kmask, _dot_nt(qm, k4) - slope * dist.astype(F32), NEG)
        has_new = is_new > 0
        sn = jnp.where(has_new, s_new, NEG)
        m = jnp.maximum(jnp.max(s, axis=-1, keepdims=True), sn)
        e = jnp.where(kmask, jnp.exp(s - m), 0.0)
        e_new = jnp.where(has_new, jnp.exp(sn - m), 0.0)
        l = jnp.sum(e, axis=-1, keepdims=True) + e_new
        og = _head_diag((_dot(e.astype(BF16), v4) + e_new * v_new) / l)
        o_slc = jnp.where(gsel == g, og, o_slc)

    o_ref[0] = (ocmp_ref[0] + gate1 * o_slc + gate2 * o_win).astype(BF16)


def _s_slc_win(idx, page_table, q, kvn, gd, ocmp, win_k, win_v, slc_k, slc_v, consts, q_pos):
    ns, n_pages = page_table.shape
    page = slc_k.shape[1]
    wb = win_k.shape[1]
    full = lambda a: pl.BlockSpec(a.shape, lambda b, i, pt: (0,) * a.ndim)
    per_b = lambda a: pl.BlockSpec((1,) + a.shape[1:], lambda b, i, pt: (b,) + (0,) * (a.ndim - 1))
    ws = [consts["tile_m"], consts["e_gate"][1], consts["e_gate"][2]]
    grid_spec = pltpu.PrefetchScalarGridSpec(
        num_scalar_prefetch=2,
        grid=(ns,),
        in_specs=[per_b(q), per_b(kvn), per_b(gd), per_b(ocmp), per_b(win_k), per_b(win_v),
                  pl.BlockSpec(memory_space=pl.ANY), pl.BlockSpec(memory_space=pl.ANY)] + [full(w) for w in ws],
        out_specs=[pl.BlockSpec((1, 1, ATT_WIDTH), lambda b, i, pt: (b, 0, 0)), per_b(win_k), per_b(win_v)],
        scratch_shapes=[pltpu.VMEM((N_KV * N_SEL, SLC_BLOCK, LANES), F32),
                        pltpu.VMEM((N_KV * N_SEL, SLC_BLOCK, LANES), F32),
                        pltpu.SemaphoreType.DMA((2,))],
    )
    return pl.pallas_call(
        functools.partial(_s_slc_win_kernel, n_pages=n_pages, page=page, q_pos=q_pos, wb=wb),
        out_shape=[jax.ShapeDtypeStruct((ns, 1, ATT_WIDTH), BF16),
                   jax.ShapeDtypeStruct(win_k.shape, F32), jax.ShapeDtypeStruct(win_v.shape, F32)],
        grid_spec=grid_spec,
        compiler_params=_cparams(("arbitrary",)),
        name="sample_slc_win_attn",
    )(idx.reshape(-1), page_table.reshape(-1), q, kvn, gd, ocmp, win_k, win_v, slc_k, slc_v, *ws)


def _col_bcast(row):
    blocks = [jnp.broadcast_to(row[:, LANES * i:LANES * (i + 1)], (LANES, LANES)).T for i in range(row.shape[1] // LANES)]
    return jnp.concatenate(blocks, axis=0)


def _s_ssd_kernel(xbc_ref, prev_ref, z_ref, gd_ref, h0_ref, cw_ref, cb_ref, dtb_ref, a_ref, dsk_ref, ng_ref,
                  edt_ref, y_ref, h_ref):
    b = pl.program_id(0)
    conv = cb_ref[...] + xbc_ref[0] * cw_ref[SSM_CONV - 1:SSM_CONV, :]
    for k in range(SSM_CONV - 1):
        conv = conv + prev_ref[0, k:k + 1, :] * cw_ref[k:k + 1, :]
    act = _silu(conv)
    xm = act[:, :SSM_INNER]
    bm = act[:, SSM_INNER:SSM_INNER + LANES]
    cm = act[:, SSM_INNER + LANES:]
    dt = _softplus(gd_ref[0] + dtb_ref[...])
    a = dt * a_ref[...]
    edt = edt_ref[...]
    dt_x = _expand(jnp.broadcast_to(dt, (SUBLANES, LANES)), edt)[0:1, :]
    dec_x = jnp.exp(_expand(jnp.broadcast_to(a, (SUBLANES, LANES)), edt)[0:1, :])
    dtx = dt_x * xm
    h0 = h0_ref[0].reshape(SSM_INNER, D_STATE)
    half = SSM_INNER // SSM_GROUPS
    rsel = lax.broadcasted_iota(jnp.int32, (SSM_INNER, 1), 0) // half
    lsel = lax.broadcasted_iota(jnp.int32, (1, SSM_INNER), 1) // half
    y_off = jnp.zeros((1, SSM_INNER), F32)
    cbx = jnp.zeros((1, SSM_INNER), F32)
    brow = jnp.zeros((SSM_INNER, D_STATE), F32)
    for g in range(SSM_GROUPS):
        bg = bm[:, D_STATE * g:D_STATE * (g + 1)]
        cg = cm[:, D_STATE * g:D_STATE * (g + 1)]
        c8 = jnp.broadcast_to(cg, (SUBLANES, D_STATE)).astype(BF16)
        yo = _dot_nt(c8, h0[half * g:half * (g + 1), :].astype(BF16))[0:1, :]
        y_off = jnp.where(lsel == g, jnp.concatenate([yo] * SSM_GROUPS, axis=1), y_off)
        cbx = jnp.where(lsel == g, jnp.sum(cg * bg, axis=-1, keepdims=True), cbx)
        brow = jnp.where(rsel == g, jnp.broadcast_to(bg, (SSM_INNER, D_STATE)), brow)
    y = y_off * dec_x + cbx * dtx + dsk_ref[...] * xm
    y_ref[0] = _gated_norm(y, z_ref[0], ng_ref[...]).astype(BF16)
    h_new = h0 * _col_bcast(dec_x)[:, :D_STATE] + _col_bcast(dtx)[:, :D_STATE] * brow
    h_ref[0] = h_new.reshape(SSM_HEADS, HEAD_DIM, D_STATE)


def _s_ssd(xbc, prev, z, gd, h0, sw, consts):
    ns = h0.shape[0]
    full = lambda a: pl.BlockSpec(a.shape, lambda b: (0,) * a.ndim)
    per_b = lambda a: pl.BlockSpec((1,) + a.shape[1:], lambda b: (b,) + (0,) * (a.ndim - 1))
    ws = [sw[n] for n in ("conv_w", "conv_b", "dt_bias", "a", "d_skip", "norm_g")] + [consts["e_dt"]]
    return pl.pallas_call(
        _s_ssd_kernel,
        out_shape=[jax.ShapeDtypeStruct((ns, 1, SSM_INNER), BF16), jax.ShapeDtypeStruct(h0.shape, F32)],
        grid=(ns,),
        in_specs=[per_b(xbc), per_b(prev), per_b(z), per_b(gd), per_b(h0)] + [full(w) for w in ws],
        out_specs=[pl.BlockSpec((1, 1, SSM_INNER), lambda b: (b, 0, 0)), per_b(h0)],
        compiler_params=_cparams(("arbitrary",)),
        name="sample_ssd_step",
    )(xbc, prev, z, gd, h0, *ws)


def _constants(t_prompt, past_len):
    c = {}
    eg = np.zeros((3, LANES, ATT_WIDTH), np.float32)
    for br in range(3):
        for h in range(N_HEADS):
            eg[br, br * N_HEADS + h, h * HEAD_DIM:(h + 1) * HEAD_DIM] = 1.0
    c["e_gate"] = [jnp.asarray(eg[i], BF16) for i in range(3)]
    ed = np.zeros((LANES, SSM_INNER), np.float32)
    for h in range(SSM_HEADS):
        ed[GD_DT + h, h * HEAD_DIM:(h + 1) * HEAD_DIM] = 1.0
    c["e_dt"] = jnp.asarray(ed, BF16)
    def overlap(n_cmp_pad, n_slc, n_slc_pad):
        cs = np.arange(n_cmp_pad)[:, None] * CMP_STRIDE
        ss = np.arange(n_slc_pad)[None, :] * SLC_BLOCK
        ov = ((cs < ss + SLC_BLOCK) & (cs + CMP_BLOCK > ss) & (np.arange(n_slc_pad)[None, :] < n_slc))
        return ov.astype(np.float32)
    nh_p = t_prompt // CMP_STRIDE
    n_slc_p = t_prompt // SLC_BLOCK
    js_p = -(-n_slc_p // SUBLANES) * SUBLANES
    ov = overlap(nh_p, n_slc_p, js_p)
    ov[nh_p - 1:, :] = 0.0
    c["ot"] = jnp.asarray(ov.T, BF16)
    eb = np.zeros((js_p, t_prompt), np.float32)
    for j in range(n_slc_p):
        eb[j, j * SLC_BLOCK:(j + 1) * SLC_BLOCK] = -MASK_BIG
    c["eb"] = jnp.asarray(eb.reshape(js_p, t_prompt // SSM_CHUNK, SSM_CHUNK).transpose(1, 0, 2), BF16)
    nh_s = (past_len + 1 + CMP_STRIDE - 1) // CMP_STRIDE
    nhp_s = -(-nh_s // LANES) * LANES
    n_slc_s = past_len // SLC_BLOCK + 1
    js_s = -(-n_slc_s // LANES) * LANES
    ov_s = overlap(nhp_s, n_slc_s, js_s)
    ov_s[nh_s - 1:, :] = 0.0
    c["ov_s"] = jnp.asarray(ov_s, BF16)
    c["u_s"] = jnp.asarray(np.triu(np.ones((js_s, js_s), np.float32), 1), BF16)
    tm = np.zeros((LANES, ATT_WIDTH), np.float32)
    for h in range(N_HEADS):
        g = h // HG
        for d in range(HEAD_DIM):
            tm[g * HEAD_DIM + d, h * HEAD_DIM + d] = 1.0
    c["tile_m"] = jnp.asarray(tm, BF16)
    return c


def _pack_w_in(w_in):
    d = w_in.shape[0]
    cuts = np.cumsum([ATT_WIDTH] + [LANES] * 6 + [3 * N_HEADS, SSM_INNER, CONV_DIM])
    q, kv, gt, z, xbc, dtc = (w_in[:, :cuts[0]], w_in[:, cuts[0]:cuts[6]], w_in[:, cuts[6]:cuts[7]],
                              w_in[:, cuts[7]:cuts[8]], w_in[:, cuts[8]:cuts[9]], w_in[:, cuts[9]:])
    pad = jnp.zeros((d, P_W - P_GD - gt.shape[1] - dtc.shape[1]), w_in.dtype)
    return jnp.concatenate([q, kv, z, xbc, gt, dtc, pad], axis=1).astype(BF16)


def _pack_compress(w1, b1, w2, head_tiled):
    zeros = jnp.zeros((CMP_STRIDE, HEAD_DIM, CMP_HID), w1.dtype)
    parts = []
    for ab in range(2):
        w = w1[ab * CMP_STRIDE:(ab + 1) * CMP_STRIDE]
        top = jnp.concatenate([w, zeros], axis=2)
        bot = jnp.concatenate([zeros, w], axis=2)
        parts.append(jnp.concatenate([top, bot], axis=1))
    w1b = jnp.concatenate(parts, axis=2).astype(BF16)
    b1b = jnp.concatenate([b1, b1]).reshape(1, 2 * CMP_HID)
    z2 = jnp.zeros_like(w2)
    rep = HG if head_tiled else 1
    top = jnp.concatenate([w2] * rep + [z2] * rep, axis=1)
    bot = jnp.concatenate([z2] * rep + [w2] * rep, axis=1)
    w2b = jnp.concatenate([top, bot], axis=0).astype(BF16)
    return w1b, b1b, w2b


def _lane_row(vals, offset, width):
    row = jnp.zeros((1, width), F32)
    return lax.dynamic_update_slice(row, vals.reshape(1, -1).astype(F32), (0, offset))


def kernel(x_prompt, x_sample, cache_cmp_k, cache_cmp_v, cache_slc_k, cache_slc_v, state_win_k, state_win_v, state_ssm, state_ssm_conv, state_ffn_conv, page_table, c_prompt, c_sample, ada_w, ada_b, norm1_g, norm2_g, w_in, cmpk_w1, cmpk_b1, cmpk_w2, cmpv_w1, cmpv_b1, cmpv_w2, ssm_conv_w, ssm_conv_b, dt_bias, a_log, d_skip, ssm_norm_g, w_out, ffn_w_up, ffn_conv_w, ffn_conv_b, ffn_w_down, final_g):
    bp, tp, d = x_prompt.shape
    ns = x_sample.shape[0]
    depth = w_in.shape[0]
    n_pool, page = cache_cmp_k.shape[1], cache_cmp_k.shape[2]
    past_len = page_table.shape[1] * page
    assert x_sample.shape[1] == 1 and d == D_MODEL
    assert tp % SSM_CHUNK == 0 and tp >= WINDOW and past_len >= WINDOW and page % SLC_BLOCK == 0
    consts = _constants(tp, past_len)

    mod = _mod(jnp.concatenate([c_prompt, c_sample], axis=0), ada_w, ada_b)
    tm_p = 512 if tp % 512 == 0 else SSM_CHUNK
    tq = SSM_CHUNK

    xp = x_prompt
    xs = x_sample.reshape(1, ns, d)
    outs_p, outs_s = [], []
    for l in range(depth):
        last = l == depth - 1
        mod_p = mod[l, :bp].reshape(bp, 1, 6 * d)
        mod_s = mod[l, bp:].reshape(1, ns, 6 * d)
        w_packed = _pack_w_in(w_in[l])
        cw = dict(zip(("w1k", "b1k", "w2k"), _pack_compress(cmpk_w1[l], cmpk_b1[l], cmpk_w2[l], False)))
        cw.update(zip(("w1v", "b1v", "w2v"), _pack_compress(cmpv_w1[l], cmpv_b1[l], cmpv_w2[l], False)))
        cw4 = dict(zip(("w1k", "b1k", "w2k"), _pack_compress(cmpk_w1[l], cmpk_b1[l], cmpk_w2[l], True)))
        cw4.update(zip(("w1v", "b1v", "w2v"), _pack_compress(cmpv_w1[l], cmpv_b1[l], cmpv_w2[l], True)))
        expand_heads = lambda v: jnp.repeat(v.astype(F32), HEAD_DIM).reshape(1, SSM_INNER)
        sw = dict(conv_w=ssm_conv_w[l], conv_b=ssm_conv_b[l].reshape(1, CONV_DIM),
                  dt_bias=_lane_row(dt_bias[l], GD_DT, LANES), a=_lane_row(-jnp.exp(a_log[l].astype(F32)), GD_DT, LANES),
                  d_skip=expand_heads(d_skip[l]), norm_g=ssm_norm_g[l].reshape(1, SSM_INNER).astype(F32))
        fw = dict(norm2_g=norm2_g[l].reshape(1, d), w_out=w_out[l].astype(BF16), w_up=ffn_w_up[l].astype(BF16),
                  conv_w=ffn_conv_w[l], conv_b=ffn_conv_b[l].reshape(1, D_FF), w_down=ffn_w_down[l].astype(BF16),
                  final_g=final_g.reshape(1, d))

        q, kc, vc, ks, vs, kw, vw, kvb, z, xbc, gd = _inproj(xp, mod_p, norm1_g[l], w_packed, tm_p, False)
        kcmp, vcmp = _compress(kc, vc, cw)
        ocmp, selt = _cmp_sel(q, kcmp, vcmp, gd, consts, tq)
        oatt = _slc_win(q, kvb, selt, ocmp, gd, consts, tq)
        yssm, h_p = _ssd(xbc, z, gd, sw, consts)
        xp, ug_tail = _mix_ffn(xp, oatt, yssm, mod_p, fw, 256 if tp % 256 == 0 else SSM_CHUNK, False, None, last)
        r4 = lambda a: a.reshape(bp, -1, N_KV, HEAD_DIM)
        outs_p.append((r4(kc), r4(vc), r4(ks), r4(vs), r4(kw[:, tp - WINDOW:]), r4(vw[:, tp - WINDOW:]), h_p,
                       xbc[:, tp - (SSM_CONV - 1):], ug_tail[:, SUBLANES - (FFN_CONV - 1):]))

        q, kc, vc, ks, vs, kw, vw, kvb, z, xbc, gd = _inproj(xs, mod_s, norm1_g[l], w_packed, ns, True)
        per_tok = lambda a: a.reshape(ns, 1, a.shape[-1])
        kvn = per_tok(jnp.concatenate([kc, vc, ks, vs, kw, vw], axis=2))
        q, z, xbc, gd = per_tok(q), per_tok(z), per_tok(xbc), per_tok(gd)
        pool = lambda c: c[l].reshape(n_pool, page, N_KV * HEAD_DIM)
        ocmp, idx = _s_cmp(page_table, q, kvn, gd, pool(cache_cmp_k), pool(cache_cmp_v), cw4, consts, past_len)
        wk_in = state_win_k[l].reshape(ns, -1, LANES)
        wv_in = state_win_v[l].reshape(ns, -1, LANES)
        oatt, wk_new, wv_new = _s_slc_win(idx, page_table, q, kvn, gd, ocmp, wk_in, wv_in,
                                          pool(cache_slc_k), pool(cache_slc_v), consts, past_len)
        yssm, h_s = _s_ssd(xbc, state_ssm_conv[l], z, gd, state_ssm[l], sw, consts)
        prev_ffn = (state_ffn_conv[l][:, 0].reshape(1, ns, D_FF), state_ffn_conv[l][:, 1].reshape(1, ns, D_FF))
        xs, ug_s = _mix_ffn(xs, oatt.reshape(1, ns, ATT_WIDTH), yssm.reshape(1, ns, SSM_INNER), mod_s, fw, ns, True,
                            prev_ffn, last)
        s4 = lambda a: a.reshape(ns, 1, N_KV, HEAD_DIM)
        outs_s.append((s4(kc), s4(vc), s4(ks), s4(vs), wk_new.reshape(ns, -1, N_KV, HEAD_DIM),
                       wv_new.reshape(ns, -1, N_KV, HEAD_DIM), h_s,
                       jnp.concatenate([state_ssm_conv[l][:, 1:], xbc.reshape(ns, 1, CONV_DIM)], axis=1),
                       jnp.concatenate([state_ffn_conv[l][:, 1:], ug_s.reshape(ns, 1, D_FF)], axis=1)))

    stack = lambda seq, i: jnp.stack([st[i] for st in seq])
    res = [xp, xs.reshape(ns, 1, d)]
    for i in range(9):
        res += [stack(outs_p, i), stack(outs_s, i)]
    return tuple(res)
```

```python
import functools

import numpy as np
import jax
import jax.numpy as jnp
from jax import lax
from jax.experimental import pallas as pl
from jax.experimental.pallas import tpu as pltpu

F32 = jnp.float32
BF16 = jnp.bfloat16

D_MODEL = 1024
HEAD_DIM = 64
ATT_WIDTH = 512
N_HEADS = 8
N_KV = 2
HG = 4
CMP_BLOCK = 32
CMP_STRIDE = 16
CMP_HID = 128
SLC_BLOCK = 64
N_SEL = 16
WINDOW = 512
FORCE_BONUS = 1e6
SSM_INNER = 512
SSM_HEADS = 8
SSM_GROUPS = 2
D_STATE = 64
SSM_CONV = 4
SSM_CHUNK = 128
CONV_DIM = 768
D_FF = 2816
FFN_CONV = 3
EPS = 1e-6
NEG = -1e30
SCALE = HEAD_DIM ** -0.5
SLOPES = tuple(2.0 ** (-(h + 1)) for h in range(N_HEADS))

LANES = 128
SUBLANES = 8
VMEM_LIMIT = 56 * 1024 * 1024

TQ = 2 * SLC_BLOCK
NCOL = N_HEADS * TQ

P_Q = 0
P_KV = 512
P_Z = 1280
P_XBC = 1792
P_GD = 2560
P_W = 2688
GD_DT = 24
MASK_BIG = 2.0 ** 100


def _cparams(sem):
    return pltpu.CompilerParams(dimension_semantics=sem, vmem_limit_bytes=VMEM_LIMIT)


def _resident(shape):
    nd = len(shape)
    return pl.BlockSpec(shape, lambda *_: (0,) * nd, pipeline_mode=pl.Buffered(1))


def _split3(x):
    hi = x.astype(BF16)
    r = x - hi.astype(F32)
    mid = r.astype(BF16)
    lo = (r - mid.astype(F32)).astype(BF16)
    return hi, mid, lo


def _dot(a, b):
    return jnp.dot(a, b, preferred_element_type=F32)


def _dot_nt(a, b):
    return lax.dot_general(a, b, (((1,), (1,)), ((), ())), preferred_element_type=F32)


def _expand(x, e):
    hi, mid, lo = _split3(x)
    return _dot(hi, e) + _dot(mid, e) + _dot(lo, e)


def _expand_l(e, x):
    hi, mid, lo = _split3(x)
    return _dot(e, hi) + _dot(e, mid) + _dot(e, lo)


def _silu(x):
    return x * jax.nn.sigmoid(x)


def _softplus(x):
    return jnp.maximum(x, 0.0) + jnp.log1p(jnp.exp(-jnp.abs(x)))


def _gelu_tanh(x):
    return 0.5 * x * (1.0 + jnp.tanh(np.sqrt(2.0 / np.pi) * (x + 0.044715 * (x * x * x))))


def _mod_kernel(c_ref, w_ref, b_ref, o_ref):
    c = c_ref[...]
    s = _silu(c).astype(BF16)
    o_ref[0] = _dot(s, w_ref[0].astype(BF16)) + b_ref[0]


def _mod(c_all, ada_w, ada_b):
    depth, d, n6 = ada_w.shape
    rows = c_all.shape[0]
    tn = 512
    return pl.pallas_call(
        _mod_kernel,
        out_shape=jax.ShapeDtypeStruct((depth, rows, n6), F32),
        grid=(depth, n6 // tn),
        in_specs=[pl.BlockSpec((rows, d), lambda l, j: (0, 0)),
                  pl.BlockSpec((1, d, tn), lambda l, j: (l, 0, j)),
                  pl.BlockSpec((1, 1, tn), lambda l, j: (l, 0, j))],
        out_specs=pl.BlockSpec((1, rows, tn), lambda l, j: (l, 0, j)),
        compiler_params=_cparams(("arbitrary", "arbitrary")),
        name="adaln_mod",
    )(c_all, ada_w, ada_b.reshape(depth, 1, n6))


def _mod_spec(tmod, k, per_row):
    if per_row:
        return pl.BlockSpec((1, tmod, D_MODEL), lambda b, t: (b, t, k))
    return pl.BlockSpec((1, 1, D_MODEL), lambda b, t: (b, 0, k))


def _inproj_kernel(x_ref, sh_ref, sc_ref, g_ref, w_ref, *outs, tm, token_minor):
    x = x_ref[0]
    ms = jnp.mean(x * x, axis=-1, keepdims=True)
    xn = x * lax.rsqrt(ms + EPS) * g_ref[...]
    h = xn * (1.0 + sc_ref[0]) + sh_ref[0]
    proj = _dot(h.astype(BF16), w_ref[...])
    kv = [proj[:, P_KV + LANES * i:P_KV + LANES * (i + 1)] for i in range(6)]
    if token_minor:
        qt_ref, kc_ref, vc_ref, kvb_ref, vt_ref, z_ref, xbc_ref, gd_ref = outs[:8]
        qt_ref[0] = (proj[:, P_Q:P_Q + ATT_WIDTH] * SCALE).T.astype(BF16)
        kc_ref[0] = kv[0]
        vc_ref[0] = kv[1]
        kvt = [a.T for a in kv]
        for ref, a in zip(outs[8:], kvt):
            ref[0] = a
        for r in range(tm // TQ):
            cols = slice(r * TQ, (r + 1) * TQ)
            vt_ref[0, r] = jnp.concatenate([kvt[3][:, cols], kvt[5][:, cols]], axis=0).astype(BF16)
    else:
        q_ref, kvb_ref, z_ref, xbc_ref, gd_ref = outs[:5]
        q_ref[0] = (proj[:, P_Q:P_Q + ATT_WIDTH] * SCALE).astype(BF16)
        for ref, a in zip(outs[5:], kv):
            ref[0] = a
    kvb_ref[0] = proj[:, P_KV:P_Z].astype(BF16)
    z_ref[0] = proj[:, P_Z:P_XBC]
    xbc_ref[0] = proj[:, P_XBC:P_GD]
    gd_ref[0] = proj[:, P_GD:P_W]


def _inproj(x, mod, norm_g, w_packed, tm, per_row, token_minor):
    bsz, t, d = x.shape
    row = lambda w: pl.BlockSpec((1, tm, w), lambda b, i: (b, i, 0))
    col = lambda w: pl.BlockSpec((1, w, tm), lambda b, i: (b, 0, i))
    shp = lambda w, dt: jax.ShapeDtypeStruct((bsz, t, w), dt)
    shpt = lambda w, dt: jax.ShapeDtypeStruct((bsz, w, t), dt)
    if token_minor:
        out_shape = ([shpt(ATT_WIDTH, BF16), shp(LANES, F32), shp(LANES, F32), shp(6 * LANES, BF16),
                      jax.ShapeDtypeStruct((bsz, t // TQ, 2 * LANES, TQ), BF16),
                      shp(SSM_INNER, F32), shp(CONV_DIM, F32), shp(LANES, F32)] + [shpt(LANES, F32)] * 6)
        out_specs = ([col(ATT_WIDTH), row(LANES), row(LANES), row(6 * LANES),
                      pl.BlockSpec((1, tm // TQ, 2 * LANES, TQ), lambda b, i: (b, i, 0, 0)),
                      row(SSM_INNER), row(CONV_DIM), row(LANES)] + [col(LANES)] * 6)
    else:
        out_shape = ([shp(ATT_WIDTH, BF16), shp(6 * LANES, BF16), shp(SSM_INNER, F32), shp(CONV_DIM, F32),
                      shp(LANES, F32)] + [shp(LANES, F32)] * 6)
        out_specs = [row(ATT_WIDTH), row(6 * LANES), row(SSM_INNER), row(CONV_DIM), row(LANES)] + [row(LANES)] * 6
    return pl.pallas_call(
        functools.partial(_inproj_kernel, tm=tm, token_minor=token_minor),
        out_shape=out_shape,
        grid=(bsz, t // tm),
        in_specs=[row(d), _mod_spec(tm, 0, per_row), _mod_spec(tm, 1, per_row),
                  _resident((1, d)), _resident(w_packed.shape)],
        out_specs=out_specs,
        compiler_params=_cparams(("arbitrary", "arbitrary")),
        name="inproj",
    )(x, mod, mod, norm_g.reshape(1, d), w_packed)


def _compress_block(src_ref, nh, w1_ref, b1_ref, w2_ref):
    acc = jnp.zeros((nh, 4 * CMP_HID), F32)
    for i in range(CMP_STRIDE // 2):
        xa = src_ref[pl.ds(2 * i, nh, stride=CMP_STRIDE), :]
        xb = src_ref[pl.ds(2 * i + 1, nh, stride=CMP_STRIDE), :]
        acc = acc + _dot(jnp.concatenate([xa, xb], axis=1).astype(BF16), w1_ref[i])
    pa = acc[:, :2 * CMP_HID]
    pb = pltpu.roll(acc[:, 2 * CMP_HID:], nh - 1, 0)
    hid = _gelu_tanh(pa + pb + b1_ref[...])
    return _dot(hid.astype(BF16), w2_ref[...])


def _compress_kernel(k_ref, v_ref, w1k_ref, b1k_ref, w2k_ref, w1v_ref, b1v_ref, w2v_ref, ok_ref, ovt_ref, *, nh):
    ok_ref[0] = _compress_block(k_ref.at[0], nh, w1k_ref, b1k_ref, w2k_ref).astype(BF16)
    ovt_ref[0] = _compress_block(v_ref.at[0], nh, w1v_ref, b1v_ref, w2v_ref).T.astype(BF16)


def _compress(kc, vc, cw):
    bsz, t, _ = kc.shape
    nh = t // CMP_STRIDE
    seq = pl.BlockSpec((1, t, LANES), lambda b: (b, 0, 0))
    ws = [cw[n] for n in ("w1k", "b1k", "w2k", "w1v", "b1v", "w2v")]
    return pl.pallas_call(
        functools.partial(_compress_kernel, nh=nh),
        out_shape=[jax.ShapeDtypeStruct((bsz, nh, LANES), BF16), jax.ShapeDtypeStruct((bsz, LANES, nh), BF16)],
        grid=(bsz,),
        in_specs=[seq, seq] + [_resident(w.shape) for w in ws],
        out_specs=[pl.BlockSpec((1, nh, LANES), lambda b: (b, 0, 0)), pl.BlockSpec((1, LANES, nh), lambda b: (b, 0, 0))],
        compiler_params=_cparams(("arbitrary",)),
        name="compress",
    )(kc, vc, *ws)


def _query_blockdiag(qt):
    zero = jnp.zeros((HEAD_DIM, HG * TQ), BF16)
    per_g = [jnp.concatenate([qt[HEAD_DIM * (HG * g + j):HEAD_DIM * (HG * g + j + 1), :] for j in range(HG)], axis=1)
             for g in range(N_KV)]
    return jnp.concatenate([jnp.concatenate([per_g[0], zero], axis=1),
                            jnp.concatenate([zero, per_g[1]], axis=1)], axis=0)


def _topk_mask_t(score, jrow, n_slc):
    rank = jnp.zeros(score.shape, F32)
    for jp in range(n_slc):
        sj = score[jp:jp + 1, :]
        ahead = (sj > score) | ((sj == score) & (jp < jrow))
        rank = rank + ahead.astype(F32)
    return (rank < N_SEL) & (jrow < n_slc)


def _cmp_sel_kernel(qt_ref, kc_ref, vct_ref, ot_ref, c0_ref, srow_ref, ocmpt_ref, selt_ref, *, n_cmp, n_slc):
    t = pl.program_id(1)
    qbd = _query_blockdiag(qt_ref[0])
    kc = kc_ref[0]
    nl = kc.shape[0]
    srow = srow_ref[...]
    s = _dot(kc, qbd) - (c0_ref[...] + srow * (t * TQ).astype(F32))
    nidx = lax.broadcasted_iota(jnp.int32, (nl, NCOL), 0)
    qpos = t * TQ + lax.broadcasted_iota(jnp.int32, (nl, NCOL), 1) % TQ
    mask = (qpos - (nidx * CMP_STRIDE + (CMP_BLOCK - 1)) >= 0) & (nidx < n_cmp)
    sm = jnp.where(mask, s, NEG)
    e = jnp.exp(sm - jnp.max(sm, axis=0, keepdims=True))
    p = jnp.where(mask, e * (1.0 / jnp.sum(e, axis=0, keepdims=True)), 0.0)
    pb = p.astype(BF16)
    vct = vct_ref[0]

    js = selt_ref.shape[2]
    jrow = lax.broadcasted_iota(jnp.int32, (js, TQ), 0)
    qp = t * TQ + lax.broadcasted_iota(jnp.int32, (js, TQ), 1)
    cur = qp // SLC_BLOCK
    forced = (jrow == 0) | (jrow == cur) | (jrow == cur - 1)
    valid = (jrow * SLC_BLOCK <= qp)
    for g in range(N_KV):
        c0 = g * HG * TQ
        og = _dot(vct[HEAD_DIM * g:HEAD_DIM * (g + 1), :], pb[:, c0:c0 + HG * TQ])
        pg = jnp.zeros((nl, TQ), F32)
        for j in range(HG):
            h = HG * g + j
            ocmpt_ref[0, HEAD_DIM * h:HEAD_DIM * (h + 1), :] = og[:, j * TQ:(j + 1) * TQ]
            pg = pg + p[:, c0 + j * TQ:c0 + (j + 1) * TQ]
        imp_t = _expand_l(ot_ref[...], pg)
        score = jnp.where(valid, imp_t + jnp.where(forced, FORCE_BONUS, 0.0), NEG)
        selt_ref[0, g] = _topk_mask_t(score, jrow, n_slc).astype(F32)


def _cmp_sel(qt, kcmp, vcmpt, consts):
    bsz, _, t = qt.shape
    nl = kcmp.shape[1]
    n_slc = t // SLC_BLOCK
    js = consts["ot"].shape[0]
    return pl.pallas_call(
        functools.partial(_cmp_sel_kernel, n_cmp=t // CMP_STRIDE - 1, n_slc=n_slc),
        out_shape=[jax.ShapeDtypeStruct((bsz, ATT_WIDTH, t), F32),
                   jax.ShapeDtypeStruct((bsz, N_KV, js, t), F32)],
        grid=(bsz, t // TQ),
        in_specs=[pl.BlockSpec((1, ATT_WIDTH, TQ), lambda b, i: (b, 0, i)),
                  pl.BlockSpec((1, nl, LANES), lambda b, i: (b, 0, 0)),
                  pl.BlockSpec((1, LANES, nl), lambda b, i: (b, 0, 0)),
                  _resident(consts["ot"].shape), _resident(consts["c0"].shape), _resident(consts["srow"].shape)],
        out_specs=[pl.BlockSpec((1, ATT_WIDTH, TQ), lambda b, i: (b, 0, i)),
                   pl.BlockSpec((1, N_KV, js, TQ), lambda b, i: (b, 0, 0, i))],
        compiler_params=_cparams(("arbitrary", "arbitrary")),
        name="cmp_select",
    )(qt, kcmp, vcmpt, consts["ot"], consts["c0"], consts["srow"])


def _attn_kernel(qt_ref, kvb_ref, vt_ref, selt_ref, ocmpt_ref, gd_ref, a_ref, srow_ref, egt_ref, o_ref,
                 m_ref, l_ref, acc_ref, pen_ref):
    t = pl.program_id(1)
    qbd = _query_blockdiag(qt_ref[0])
    srow = srow_ref[...]
    gw = HG * TQ
    for g in range(N_KV):
        pen = (selt_ref[0, g] - 1.0) * MASK_BIG
        pen_ref[:, g * gw:(g + 1) * gw] = jnp.concatenate([pen] * HG, axis=1)
    nwin = WINDOW // TQ

    def scores(kt, kcol, slc):
        r0 = pl.multiple_of(kt * TQ, TQ)
        k_t = kvb_ref[0, pl.ds(r0, TQ), kcol:kcol + LANES]
        kind = jnp.where(kt == t, 1, 0) if slc else jnp.where(kt == t, 1, jnp.where(kt == t - nwin, 2, 0))
        s = _dot(k_t, qbd) + a_ref[kind]
        brow = srow * ((kt - t) * TQ).astype(F32)
        if slc:
            pa = jnp.broadcast_to(pen_ref[pl.ds(2 * kt, 1), :] + brow, (SLC_BLOCK, NCOL))
            pb = jnp.broadcast_to(pen_ref[pl.ds(2 * kt + 1, 1), :] + brow, (SLC_BLOCK, NCOL))
            return s + jnp.concatenate([pa, pb], axis=0)
        return s + brow

    def update(kt, s, vrow):
        m_old = m_ref[...]
        m_new = jnp.maximum(m_old, jnp.max(s, axis=0, keepdims=True))
        alpha = jnp.exp(m_old - m_new)
        p = jnp.exp(s - m_new)
        l_ref[...] = alpha * l_ref[...] + jnp.sum(p, axis=0, keepdims=True)
        m_ref[...] = m_new
        pb16 = p.astype(BF16)
        v_t = vt_ref[0, kt, vrow:vrow + LANES, :]
        for g in range(N_KV):
            cs = slice(g * gw, (g + 1) * gw)
            acc_ref[g] = acc_ref[g] * alpha[:, cs] + _dot(v_t[HEAD_DIM * g:HEAD_DIM * (g + 1), :], pb16[:, cs])

    def reset():
        m_ref[...] = jnp.full(m_ref.shape, NEG, F32)
        l_ref[...] = jnp.zeros(l_ref.shape, F32)
        acc_ref[...] = jnp.zeros(acc_ref.shape, F32)

    def result():
        inv = 1.0 / l_ref[...]
        return [acc_ref[g] * inv[:, g * gw:(g + 1) * gw] for g in range(N_KV)]

    def branch(first, kcol, vrow, slc):
        reset()
        count = t - first + 1

        def pair(i, c):
            k0 = first + 2 * i
            s0 = scores(k0, kcol, slc)
            s1 = scores(k0 + 1, kcol, slc)
            update(k0, s0, vrow)
            update(k0 + 1, s1, vrow)
            return c

        lax.fori_loop(0, count // 2, pair, 0)

        @pl.when(count % 2 == 1)
        def _():
            update(t, scores(t, kcol, slc), vrow)

        return result()

    o_slc = branch(0, 2 * LANES, 0, True)
    o_win = branch(jnp.maximum(t - nwin, 0), 4 * LANES, LANES, False)

    g_t = jax.nn.sigmoid(gd_ref[0]).T
    gates = [_expand_l(egt_ref[br], g_t) for br in range(3)]
    for pr in range(N_HEADS // 2):
        g, j0 = pr // 2, 2 * (pr % 2)
        rows = slice(LANES * pr, LANES * (pr + 1))
        pair = lambda o: jnp.concatenate([o[g][:, j0 * TQ:(j0 + 1) * TQ], o[g][:, (j0 + 1) * TQ:(j0 + 2) * TQ]], axis=0)
        tot = gates[0][rows] * ocmpt_ref[0, rows, :] + gates[1][rows] * pair(o_slc) + gates[2][rows] * pair(o_win)
        o_ref[0, :, rows] = tot.T.astype(BF16)


def _attn(qt, kvb, vt, selt, ocmpt, gd, consts):
    bsz, _, t = qt.shape
    js = selt.shape[2]
    return pl.pallas_call(
        _attn_kernel,
        out_shape=jax.ShapeDtypeStruct((bsz, t, ATT_WIDTH), BF16),
        grid=(bsz, t // TQ),
        in_specs=[pl.BlockSpec((1, ATT_WIDTH, TQ), lambda b, i: (b, 0, i)),
                  pl.BlockSpec((1, t, 6 * LANES), lambda b, i: (b, 0, 0)),
                  pl.BlockSpec((1, t // TQ, 2 * LANES, TQ), lambda b, i: (b, 0, 0, 0)),
                  pl.BlockSpec((1, N_KV, js, TQ), lambda b, i: (b, 0, 0, i)),
                  pl.BlockSpec((1, ATT_WIDTH, TQ), lambda b, i: (b, 0, i)),
                  pl.BlockSpec((1, TQ, LANES), lambda b, i: (b, i, 0)),
                  _resident(consts["a_rel"].shape), _resident(consts["srow"].shape), _resident(consts["e_gate_t"].shape)],
        out_specs=pl.BlockSpec((1, TQ, ATT_WIDTH), lambda b, i: (b, i, 0)),
        scratch_shapes=[pltpu.VMEM((1, NCOL), F32), pltpu.VMEM((1, NCOL), F32),
                        pltpu.VMEM((N_KV, HEAD_DIM, HG * TQ), F32), pltpu.VMEM((js, NCOL), F32)],
        compiler_params=_cparams(("arbitrary", "arbitrary")),
        name="slc_win_attn",
    )(qt, kvb, vt, selt, ocmpt, gd, consts["a_rel"], consts["srow"], consts["e_gate_t"])


def _gated_norm(y, z, ng):
    yz = y * _silu(z)
    half = SSM_INNER // SSM_GROUPS
    outs = []
    for g in range(SSM_GROUPS):
        part = yz[:, half * g:half * (g + 1)]
        ms = jnp.mean(part * part, axis=-1, keepdims=True)
        outs.append(part * lax.rsqrt(ms + EPS))
    return jnp.concatenate(outs, axis=1) * ng


def _ssd_kernel(xbc_ref, z_ref, gd_ref, cw_ref, cb_ref, dtb_ref, a_ref, dsk_ref, ng_ref, edt_ref,
                y_ref, hfin_ref, ubuf, sstate, ybuf, *, lc):
    t = pl.program_id(1)
    nt = pl.num_programs(1)

    @pl.when(t == 0)
    def _():
        ubuf[0:SUBLANES, :] = jnp.zeros((SUBLANES, CONV_DIM), F32)
        sstate[...] = jnp.zeros(sstate.shape, F32)

    ubuf[SUBLANES:SUBLANES + lc, :] = xbc_ref[0]
    conv = cb_ref[...] + jnp.zeros((lc, CONV_DIM), F32)
    for k in range(SSM_CONV):
        off = SUBLANES - (SSM_CONV - 1) + k
        conv = conv + ubuf[off:off + lc, :] * cw_ref[k:k + 1, :]
    ubuf[0:SUBLANES, :] = ubuf[lc:lc + SUBLANES, :]
    act = _silu(conv)
    xm = act[:, :SSM_INNER]
    bm = act[:, SSM_INNER:SSM_INNER + LANES]
    cm = act[:, SSM_INNER + LANES:]

    dt = _softplus(gd_ref[0] + dtb_ref[...])
    a = dt * a_ref[...]
    ri = lax.broadcasted_iota(jnp.int32, (lc, lc), 0)
    ci = lax.broadcasted_iota(jnp.int32, (lc, lc), 1)
    tril = ri >= ci
    cs = jnp.dot(tril.astype(F32), a, preferred_element_type=F32, precision=lax.Precision.HIGHEST)
    cs_t = cs.T
    edt = edt_ref[...]
    dt_x = _expand(dt, edt)
    cs_x = _expand(cs, edt)
    cs_last = cs_x[lc - 1:lc, :]
    bm_t = bm.T
    xd = xm * dt_x
    xw = xm * (jnp.exp(cs_last - cs_x) * dt_x)
    s_old = sstate[...]
    half = SSM_INNER // SSM_GROUPS
    for g in range(SSM_GROUPS):
        bg = bm[:, D_STATE * g:D_STATE * (g + 1)].astype(BF16)
        cg = cm[:, D_STATE * g:D_STATE * (g + 1)].astype(BF16)
        cb = _dot_nt(cg, bg)
        for j in range(SSM_HEADS // SSM_GROUPS):
            h = g * (SSM_HEADS // SSM_GROUPS) + j
            col = cs[:, GD_DT + h:GD_DT + h + 1]
            row = cs_t[GD_DT + h:GD_DT + h + 1, :]
            lm = jnp.exp(jnp.where(tril, col - row, NEG))
            lo, hi = HEAD_DIM * h, HEAD_DIM * (h + 1)
            ybuf[:, lo:hi] = _dot((cb * lm).astype(BF16), xd[:, lo:hi].astype(BF16))
        sg = s_old[:, half * g:half * (g + 1)]
        y_off = _dot(cg, sg.astype(BF16))
        ybuf[:, half * g:half * (g + 1)] = ybuf[:, half * g:half * (g + 1)] + y_off * jnp.exp(cs_x[:, half * g:half * (g + 1)])
        st = _dot(bm_t[D_STATE * g:D_STATE * (g + 1), :].astype(BF16), xw[:, half * g:half * (g + 1)].astype(BF16))
        sstate[:, half * g:half * (g + 1)] = sg * jnp.exp(cs_last[:, half * g:half * (g + 1)]) + st

    y = ybuf[...] + dsk_ref[...] * xm
    y_ref[0] = _gated_norm(y, z_ref[0], ng_ref[...]).astype(BF16)

    @pl.when(t == nt - 1)
    def _():
        s_pad = jnp.concatenate([sstate[...], jnp.zeros((LANES - D_STATE, SSM_INNER), F32)], axis=0)
        hfin_ref[0] = s_pad.T[:, :D_STATE]


def _ssd(xbc, z, gd, sw, consts):
    bsz, t, _ = xbc.shape
    lc = SSM_CHUNK
    tile = lambda w: pl.BlockSpec((1, lc, w), lambda b, i: (b, i, 0))
    ws = [sw[n] for n in ("conv_w", "conv_b", "dt_bias", "a", "d_skip", "norm_g")] + [consts["e_dt"]]
    y, hfin = pl.pallas_call(
        functools.partial(_ssd_kernel, lc=lc),
        out_shape=[jax.ShapeDtypeStruct((bsz, t, SSM_INNER), BF16),
                   jax.ShapeDtypeStruct((bsz, SSM_INNER, D_STATE), F32)],
        grid=(bsz, t // lc),
        in_specs=[tile(CONV_DIM), tile(SSM_INNER), tile(LANES)] + [_resident(w.shape) for w in ws],
        out_specs=[tile(SSM_INNER), pl.BlockSpec((1, SSM_INNER, D_STATE), lambda b, i: (b, 0, 0))],
        scratch_shapes=[pltpu.VMEM((lc + SUBLANES, CONV_DIM), F32),
                        pltpu.VMEM((D_STATE, SSM_INNER), F32),
                        pltpu.VMEM((lc, SSM_INNER), F32)],
        compiler_params=_cparams(("arbitrary", "arbitrary")),
        name="ssd_scan",
    )(xbc, z, gd, *ws)
    return y, hfin.reshape(bsz, SSM_HEADS, HEAD_DIM, D_STATE)


def _mix_ffn_kernel(*refs, tm, carry_conv, last):
    if carry_conv:
        (x_ref, oa_ref, ys_ref, g1_ref, sh2_ref, sc2_ref, g2_ref, n2_ref, wo_ref, wu_ref, fcw_ref, fcb_ref,
         wd_ref, fg_ref, out_ref, ug_ref, ubuf) = refs
    else:
        (x_ref, oa_ref, ys_ref, g1_ref, sh2_ref, sc2_ref, g2_ref, n2_ref, wo_ref, wu_ref, fcw_ref, fcb_ref,
         wd_ref, fg_ref, p0_ref, p1_ref, out_ref, ug_ref) = refs
    t = pl.program_id(1)
    nt = pl.num_programs(1)
    x = x_ref[0]
    mix = _dot(oa_ref[0], wo_ref[:ATT_WIDTH, :]) + _dot(ys_ref[0], wo_ref[ATT_WIDTH:, :])
    x1 = x + g1_ref[0] * mix
    ms = jnp.mean(x1 * x1, axis=-1, keepdims=True)
    h2 = (x1 * lax.rsqrt(ms + EPS) * n2_ref[...]) * (1.0 + sc2_ref[0]) + sh2_ref[0]
    h2 = h2.astype(BF16)
    ug = _dot(h2, wu_ref[:, :D_FF])
    uv = _dot(h2, wu_ref[:, D_FF:])
    if carry_conv:
        @pl.when(t == 0)
        def _():
            ubuf[0:SUBLANES, :] = jnp.zeros((SUBLANES, D_FF), F32)

        ubuf[SUBLANES:SUBLANES + tm, :] = ug
        u1 = ubuf[SUBLANES - 1:SUBLANES - 1 + tm, :]
        u2 = ubuf[SUBLANES - 2:SUBLANES - 2 + tm, :]
        ubuf[0:SUBLANES, :] = ubuf[tm:tm + SUBLANES, :]

        @pl.when(t == nt - 1)
        def _():
            ug_ref[0] = ug[tm - SUBLANES:, :]
    else:
        u1 = p1_ref[0]
        u2 = p0_ref[0]
        ug_ref[0] = ug
    ugc = fcb_ref[...] + u2 * fcw_ref[0:1, :] + u1 * fcw_ref[1:2, :] + ug * fcw_ref[2:3, :]
    act = (_silu(ugc) * uv).astype(BF16)
    x2 = x1 + g2_ref[0] * _dot(act, wd_ref[...])
    if last:
        ms2 = jnp.mean(x2 * x2, axis=-1, keepdims=True)
        out_ref[0] = x2 * lax.rsqrt(ms2 + EPS) * fg_ref[...]
    else:
        out_ref[0] = x2


def _mix_ffn(x, oatt, yssm, mod, fw, tm, per_row, prev, last):
    bsz, t, d = x.shape
    carry = prev is None
    row = lambda w: pl.BlockSpec((1, tm, w), lambda b, i: (b, i, 0))
    ws = [fw[n] for n in ("norm2_g", "w_out", "w_up", "conv_w", "conv_b", "w_down", "final_g")]
    in_specs = ([row(d), row(ATT_WIDTH), row(SSM_INNER), _mod_spec(tm, 2, per_row), _mod_spec(tm, 3, per_row),
                 _mod_spec(tm, 4, per_row), _mod_spec(tm, 5, per_row)] + [_resident(w.shape) for w in ws])
    args = [x, oatt, yssm, mod, mod, mod, mod] + ws
    if carry:
        ug_shape, ug_spec = (bsz, SUBLANES, D_FF), pl.BlockSpec((1, SUBLANES, D_FF), lambda b, i: (b, 0, 0))
        scratch = [pltpu.VMEM((tm + SUBLANES, D_FF), F32)]
    else:
        ug_shape, ug_spec = (bsz, t, D_FF), row(D_FF)
        scratch = []
        in_specs += [row(D_FF), row(D_FF)]
        args += list(prev)
    return pl.pallas_call(
        functools.partial(_mix_ffn_kernel, tm=tm, carry_conv=carry, last=last),
        out_shape=[jax.ShapeDtypeStruct((bsz, t, d), F32), jax.ShapeDtypeStruct(ug_shape, F32)],
        grid=(bsz, t // tm),
        in_specs=in_specs,
        out_specs=[row(d), ug_spec],
        scratch_shapes=scratch,
        compiler_params=_cparams(("arbitrary", "arbitrary")),
        name="mix_ffn",
    )(*args)


def _head_rows(row512):
    r = lax.broadcasted_iota(jnp.int32, (N_HEADS, ATT_WIDTH), 0)
    c = lax.broadcasted_iota(jnp.int32, (N_HEADS, ATT_WIDTH), 1)
    return jnp.where(c // HEAD_DIM == r, jnp.broadcast_to(row512, (N_HEADS, ATT_WIDTH)), 0.0)


def _head_diag(x8):
    r = lax.broadcasted_iota(jnp.int32, (N_HEADS, ATT_WIDTH), 0)
    c = lax.broadcasted_iota(jnp.int32, (N_HEADS, ATT_WIDTH), 1)
    return jnp.sum(jnp.where(c // HEAD_DIM == r, x8, 0.0), axis=0, keepdims=True)


def _slope_col():
    r = lax.broadcasted_iota(jnp.int32, (N_HEADS, 1), 0)
    s = jnp.zeros((N_HEADS, 1), F32)
    for h in range(N_HEADS):
        s = jnp.where(r == h, SLOPES[h], s)
    return s


def _s_cmp_kernel(pt_ref, q_ref, kvn_ref, gd_ref, ck_hbm, cv_hbm,
                  w1k_ref, b1k_ref, w2k_ref, w1v_ref, b1v_ref, w2v_ref, ov_ref, u_ref, e0_ref,
                  ocmp_ref, idx_ref, stage_k, stage_v, kbuf, vbuf, sem,
                  *, layer, n_pages, page, nh8, nhp, q_pos, n_slc):
    b = pl.program_id(0)
    nb = pl.num_programs(0)
    past = n_pages * page

    def copies(bb, p, slot):
        pg = pt_ref[bb * n_pages + p]
        return (pltpu.make_async_copy(ck_hbm.at[layer, pg], stage_k.at[slot, p], sem.at[0, slot]),
                pltpu.make_async_copy(cv_hbm.at[layer, pg], stage_v.at[slot, p], sem.at[1, slot]))

    def issue(bb, slot):
        def body(p, c):
            for cp in copies(bb, p, slot):
                cp.start()
            return c
        lax.fori_loop(0, n_pages, body, 0)

    @pl.when(b == 0)
    def _():
        issue(0, 0)

    @pl.when(b + 1 < nb)
    def _():
        issue(b + 1, (b + 1) % 2)

    slot = b % 2

    def wait(p, c):
        for cp in copies(b, p, slot):
            cp.wait()
        return c

    lax.fori_loop(0, n_pages, wait, 0)

    def untranspose(p, c):
        rows = pl.ds(pl.multiple_of(p * page, page), page)
        kbuf[rows, :] = stage_k[slot, p].T
        vbuf[rows, :] = stage_v[slot, p].T
        return c

    lax.fori_loop(0, n_pages, untranspose, 0)
    tail = nh8 * CMP_STRIDE - past
    kbuf[past:, :] = jnp.zeros((tail, LANES), F32)
    vbuf[past:, :] = jnp.zeros((tail, LANES), F32)
    kbuf[past:past + 1, :] = kvn_ref[0, :, 0:LANES]
    vbuf[past:past + 1, :] = kvn_ref[0, :, LANES:2 * LANES]

    pad = jnp.zeros((nhp - nh8, ATT_WIDTH), F32)
    kc4 = jnp.concatenate([_compress_block(kbuf, nh8, w1k_ref, b1k_ref, w2k_ref), pad], axis=0).astype(BF16)
    vc4 = jnp.concatenate([_compress_block(vbuf, nh8, w1v_ref, b1v_ref, w2v_ref), pad], axis=0).astype(BF16)
    n_cmp = (past + 1 + CMP_STRIDE - 1) // CMP_STRIDE - 1

    qm = _head_rows(q_ref[0].astype(F32)).astype(BF16)
    nidx = lax.broadcasted_iota(jnp.int32, (N_HEADS, nhp), 1)
    dist_i = q_pos - (nidx * CMP_STRIDE + (CMP_BLOCK - 1))
    mask = (dist_i >= 0) & (nidx < n_cmp)
    s = _dot_nt(qm, kc4) - _slope_col() * dist_i.astype(F32)
    sm = jnp.where(mask, s, NEG)
    e = jnp.exp(sm - jnp.max(sm, axis=-1, keepdims=True))
    p = jnp.where(mask, e / jnp.sum(e, axis=-1, keepdims=True), 0.0)
    o = _head_diag(_dot(p.astype(BF16), vc4))
    g8 = jnp.broadcast_to(jax.nn.sigmoid(gd_ref[0]), (SUBLANES, LANES))
    gate = _expand(g8, e0_ref[...])[0:1, :]
    ocmp_ref[0] = gate * o

    js = ov_ref.shape[1]
    imp8 = _expand(p, ov_ref[...])
    hrow = lax.broadcasted_iota(jnp.int32, (N_HEADS, js), 0)
    jl = lax.broadcasted_iota(jnp.int32, (1, js), 1)
    cur = q_pos // SLC_BLOCK
    forced = (jl == 0) | (jl == cur) | (jl == cur - 1)
    valid = (jl * SLC_BLOCK <= q_pos) & (jl < n_slc)
    rj = lax.broadcasted_iota(jnp.int32, (js, js), 0)
    cj = lax.broadcasted_iota(jnp.int32, (js, js), 1)
    kk = lax.broadcasted_iota(jnp.int32, (2 * SUBLANES, js), 0)
    jvals = jnp.broadcast_to(jl.astype(F32), (SUBLANES, js)).astype(BF16)
    for g in range(N_KV):
        imp = jnp.sum(jnp.where(hrow // HG == g, imp8, 0.0), axis=0, keepdims=True)
        score = jnp.where(valid, imp + jnp.where(forced, FORCE_BONUS, 0.0), NEG)
        sb = jnp.broadcast_to(score, (js, js))
        col = jnp.sum(jnp.where(rj == cj, sb, 0.0), axis=1, keepdims=True)
        ahead = (col > sb) | ((col == sb) & (rj < cj))
        rank = jnp.sum(ahead.astype(F32), axis=0, keepdims=True)
        sel = ((rank < N_SEL) & (jl < n_slc)).astype(F32)
        pos = _dot(jnp.broadcast_to(sel, (SUBLANES, js)).astype(BF16), u_ref[...])[0:1, :]
        onehot = ((jnp.broadcast_to(pos, (2 * SUBLANES, js)) == kk.astype(F32))
                  & (jnp.broadcast_to(sel, (2 * SUBLANES, js)) > 0.5)).astype(BF16)
        idx = _dot_nt(jvals, onehot)[0:1, :]
        idx_ref[0, g:g + 1, :] = idx.astype(jnp.int32)


def _s_cmp(layer, page_table, q, kvn, gd, cache_k, cache_v, cw4, consts, q_pos):
    ns, n_pages = page_table.shape
    page = cache_k.shape[3]
    past = n_pages * page
    nh = (past + 1 + CMP_STRIDE - 1) // CMP_STRIDE
    nh8 = -(-nh // SUBLANES) * SUBLANES
    nhp = -(-nh // LANES) * LANES
    n_slc = past // SLC_BLOCK + 1
    ws = [cw4[n] for n in ("w1k", "b1k", "w2k", "w1v", "b1v", "w2v")] + [consts["ov_s"], consts["u_s"], consts["e_gate"][0]]
    full = lambda a: pl.BlockSpec(a.shape, lambda b, pt: (0,) * a.ndim)
    per_b = lambda a: pl.BlockSpec((1,) + a.shape[1:], lambda b, pt: (b,) + (0,) * (a.ndim - 1))
    grid_spec = pltpu.PrefetchScalarGridSpec(
        num_scalar_prefetch=1,
        grid=(ns,),
        in_specs=[per_b(q), per_b(kvn), per_b(gd), pl.BlockSpec(memory_space=pl.ANY), pl.BlockSpec(memory_space=pl.ANY)]
                 + [full(w) for w in ws],
        out_specs=[pl.BlockSpec((1, 1, ATT_WIDTH), lambda b, pt: (b, 0, 0)),
                   pl.BlockSpec((1, N_KV, N_SEL), lambda b, pt: (b, 0, 0))],
        scratch_shapes=[pltpu.VMEM((2, n_pages, LANES, page), F32), pltpu.VMEM((2, n_pages, LANES, page), F32),
                        pltpu.VMEM((nh8 * CMP_STRIDE, LANES), F32), pltpu.VMEM((nh8 * CMP_STRIDE, LANES), F32),
                        pltpu.SemaphoreType.DMA((2, 2))],
    )
    return pl.pallas_call(
        functools.partial(_s_cmp_kernel, layer=layer, n_pages=n_pages, page=page, nh8=nh8, nhp=nhp, q_pos=q_pos,
                          n_slc=n_slc),
        out_shape=[jax.ShapeDtypeStruct((ns, 1, ATT_WIDTH), F32), jax.ShapeDtypeStruct((ns, N_KV, N_SEL), jnp.int32)],
        grid_spec=grid_spec,
        compiler_params=_cparams(("arbitrary",)),
        name="sample_cmp_select",
    )(page_table.reshape(-1), q, kvn, gd, cache_k, cache_v, *ws)


def _s_slc_win_kernel(idx_ref, pt_ref, q_ref, kvn_ref, gd_ref, ocmp_ref, wk_ref, wv_ref, sk_hbm, sv_hbm,
                      tile_ref, e1_ref, e2_ref, o_ref, wko_ref, wvo_ref, kb, vb, sem,
                      *, layer, n_pages, page, q_pos, wb):
    b = pl.program_id(0)
    bpp = page // SLC_BLOCK
    n_past = n_pages * bpp
    nblk = N_KV * N_SEL
    blks = [idx_ref[b * nblk + i] for i in range(nblk)]

    def copies(i):
        pg = pt_ref[b * n_pages + jnp.minimum(blks[i], n_past - 1) // bpp]
        dst = slice(i * page, (i + 1) * page)
        return (pltpu.make_async_copy(sk_hbm.at[layer, pg], kb.at[:, dst], sem.at[0]),
                pltpu.make_async_copy(sv_hbm.at[layer, pg], vb.at[:, dst], sem.at[1]))

    for i in range(nblk):
        for cp in copies(i):
            cp.start()

    tile_m = tile_ref[...]
    q8 = _dot_nt(_head_rows(q_ref[0].astype(F32)).astype(BF16), tile_m).astype(BF16)
    slope = _slope_col()
    g8 = jnp.broadcast_to(jax.nn.sigmoid(gd_ref[0]), (SUBLANES, LANES))
    gate1 = _expand(g8, e1_ref[...])[0:1, :]
    gate2 = _expand(g8, e2_ref[...])[0:1, :]
    rgrp = lax.broadcasted_iota(jnp.int32, (N_HEADS, 1), 0) // HG

    def new_key(lane0):
        kn = kvn_ref[0, :, lane0:lane0 + LANES].astype(BF16).astype(F32)
        vn = kvn_ref[0, :, lane0 + LANES:lane0 + 2 * LANES].astype(BF16).astype(F32)
        return jnp.sum(q8.astype(F32) * kn, axis=-1, keepdims=True), vn

    def finish(e, e_new, v_t, v_new, l):
        o8 = (_dot_nt(e.astype(BF16), v_t.astype(BF16)) + e_new * v_new) / l
        return _head_diag(_expand(o8, tile_m))

    def shifted(w_t, new_row):
        col = jnp.broadcast_to(new_row, (LANES, LANES)).T[:, 0:1]
        lane = lax.broadcasted_iota(jnp.int32, (LANES, wb), 1)
        return jnp.where(lane == wb - 1, col, pltpu.roll(w_t, wb - 1, 1))

    wk_t = wk_ref[0, 0]
    wv_t = wv_ref[0, 0]
    kpos = (q_pos - wb) + lax.broadcasted_iota(jnp.int32, (N_HEADS, wb), 1)
    dist = q_pos - kpos
    wmask = (dist >= 0) & (dist <= WINDOW)
    s = jnp.where(wmask, _dot(q8, wk_t.astype(BF16)) - slope * dist.astype(F32), NEG)
    s_new, v_new = new_key(4 * LANES)
    m = jnp.maximum(jnp.max(s, axis=-1, keepdims=True), s_new)
    e = jnp.where(wmask, jnp.exp(s - m), 0.0)
    e_new = jnp.exp(s_new - m)
    o_win = finish(e, e_new, wv_t, v_new, jnp.sum(e, axis=-1, keepdims=True) + e_new)
    wko_ref[0] = shifted(wk_t, kvn_ref[0, :, 4 * LANES:5 * LANES])
    wvo_ref[0] = shifted(wv_t, kvn_ref[0, :, 5 * LANES:6 * LANES])

    for i in range(nblk):
        for cp in copies(i):
            cp.wait()
    lane = lax.broadcasted_iota(jnp.int32, (1, page), 1)
    kpos_t, ok_t = [], []
    has_new = [jnp.zeros((1, 1), jnp.int32) for _ in range(N_KV)]
    for i in range(nblk):
        blk = blks[i]
        kpos_t.append(blk * SLC_BLOCK + lane % SLC_BLOCK)
        ok_t.append(((lane // SLC_BLOCK) == blk % bpp) & (blk < n_past))
        has_new[i // N_SEL] = jnp.maximum(has_new[i // N_SEL], (blk >= n_past).astype(jnp.int32))
    kpos = jnp.concatenate(kpos_t, axis=1)
    ok = jnp.concatenate(ok_t, axis=1)
    pgrp = lax.broadcasted_iota(jnp.int32, (1, nblk * page), 1) // (N_SEL * page)
    dist = q_pos - kpos
    kmask = ok & (dist >= 0) & (rgrp == pgrp)
    s = jnp.where(kmask, _dot(q8, kb[...].astype(BF16)) - slope * dist.astype(F32), NEG)
    s_new, v_new = new_key(2 * LANES)
    new_on = jnp.where(rgrp == 0, has_new[0], has_new[1]) > 0
    sn = jnp.where(new_on, s_new, NEG)
    m = jnp.maximum(jnp.max(s, axis=-1, keepdims=True), sn)
    e = jnp.where(kmask, jnp.exp(s - m), 0.0)
    e_new = jnp.where(new_on, jnp.exp(sn - m), 0.0)
    o_slc = finish(e, e_new, vb[...], v_new, jnp.sum(e, axis=-1, keepdims=True) + e_new)

    o_ref[0] = (ocmp_ref[0] + gate1 * o_slc + gate2 * o_win).astype(BF16)


def _s_slc_win(layer, idx, page_table, q, kvn, gd, ocmp, win_k, win_v, slc_k, slc_v, consts, q_pos):
    ns, n_pages = page_table.shape
    page = slc_k.shape[3]
    wb = win_k.shape[3]
    full = lambda a: pl.BlockSpec(a.shape, lambda b, i, pt: (0,) * a.ndim)
    per_b = lambda a: pl.BlockSpec((1,) + a.shape[1:], lambda b, i, pt: (b,) + (0,) * (a.ndim - 1))
    win = pl.BlockSpec((1, 1, LANES, wb), lambda b, i, pt: (layer, b, 0, 0))
    ws = [consts["tile_m"], consts["e_gate"][1], consts["e_gate"][2]]
    nlane = N_KV * N_SEL * page
    grid_spec = pltpu.PrefetchScalarGridSpec(
        num_scalar_prefetch=2,
        grid=(ns,),
        in_specs=[per_b(q), per_b(kvn), per_b(gd), per_b(ocmp), win, win,
                  pl.BlockSpec(memory_space=pl.ANY), pl.BlockSpec(memory_space=pl.ANY)] + [full(w) for w in ws],
        out_specs=[pl.BlockSpec((1, 1, ATT_WIDTH), lambda b, i, pt: (b, 0, 0)),
                   pl.BlockSpec((1, LANES, wb), lambda b, i, pt: (b, 0, 0)),
                   pl.BlockSpec((1, LANES, wb), lambda b, i, pt: (b, 0, 0))],
        scratch_shapes=[pltpu.VMEM((LANES, nlane), F32), pltpu.VMEM((LANES, nlane), F32),
                        pltpu.SemaphoreType.DMA((2,))],
    )
    return pl.pallas_call(
        functools.partial(_s_slc_win_kernel, layer=layer, n_pages=n_pages, page=page, q_pos=q_pos, wb=wb),
        out_shape=[jax.ShapeDtypeStruct((ns, 1, ATT_WIDTH), BF16),
                   jax.ShapeDtypeStruct((ns, LANES, wb), F32), jax.ShapeDtypeStruct((ns, LANES, wb), F32)],
        grid_spec=grid_spec,
        compiler_params=_cparams(("arbitrary",)),
        name="sample_slc_win_attn",
    )(idx.reshape(-1), page_table.reshape(-1), q, kvn, gd, ocmp, win_k, win_v, slc_k, slc_v, *ws)


def _col_bcast(row):
    blocks = [jnp.broadcast_to(row[:, LANES * i:LANES * (i + 1)], (LANES, LANES)).T for i in range(row.shape[1] // LANES)]
    return jnp.concatenate(blocks, axis=0)


def _s_ssd_kernel(xbc_ref, prev_ref, z_ref, gd_ref, h0_ref, cw_ref, cb_ref, dtb_ref, a_ref, dsk_ref, ng_ref,
                  edt_ref, y_ref, h_ref):
    conv = cb_ref[...] + xbc_ref[0] * cw_ref[SSM_CONV - 1:SSM_CONV, :]
    for k in range(SSM_CONV - 1):
        conv = conv + prev_ref[0, k:k + 1, :] * cw_ref[k:k + 1, :]
    act = _silu(conv)
    xm = act[:, :SSM_INNER]
    bm = act[:, SSM_INNER:SSM_INNER + LANES]
    cm = act[:, SSM_INNER + LANES:]
    dt = _softplus(gd_ref[0] + dtb_ref[...])
    a = dt * a_ref[...]
    edt = edt_ref[...]
    dt_x = _expand(jnp.broadcast_to(dt, (SUBLANES, LANES)), edt)[0:1, :]
    dec_x = jnp.exp(_expand(jnp.broadcast_to(a, (SUBLANES, LANES)), edt)[0:1, :])
    dtx = dt_x * xm
    h0 = h0_ref[0, 0].reshape(SSM_INNER, D_STATE)
    half = SSM_INNER // SSM_GROUPS
    rsel = lax.broadcasted_iota(jnp.int32, (SSM_INNER, 1), 0) // half
    lsel = lax.broadcasted_iota(jnp.int32, (1, SSM_INNER), 1) // half
    y_off = jnp.zeros((1, SSM_INNER), F32)
    cbx = jnp.zeros((1, SSM_INNER), F32)
    brow = jnp.zeros((SSM_INNER, D_STATE), F32)
    for g in range(SSM_GROUPS):
        bg = bm[:, D_STATE * g:D_STATE * (g + 1)]
        cg = cm[:, D_STATE * g:D_STATE * (g + 1)]
        c8 = jnp.broadcast_to(cg, (SUBLANES, D_STATE)).astype(BF16)
        yo = _dot_nt(c8, h0[half * g:half * (g + 1), :].astype(BF16))[0:1, :]
        y_off = jnp.where(lsel == g, jnp.concatenate([yo] * SSM_GROUPS, axis=1), y_off)
        cbx = jnp.where(lsel == g, jnp.sum(cg * bg, axis=-1, keepdims=True), cbx)
        brow = jnp.where(rsel == g, jnp.broadcast_to(bg, (SSM_INNER, D_STATE)), brow)
    y = y_off * dec_x + cbx * dtx + dsk_ref[...] * xm
    y_ref[0] = _gated_norm(y, z_ref[0], ng_ref[...]).astype(BF16)
    h_new = h0 * _col_bcast(dec_x)[:, :D_STATE] + _col_bcast(dtx)[:, :D_STATE] * brow
    h_ref[0] = h_new.reshape(SSM_HEADS, HEAD_DIM, D_STATE)


def _s_ssd(layer, xbc, prev, z, gd, state_ssm, sw, consts):
    ns = state_ssm.shape[1]
    full = lambda a: pl.BlockSpec(a.shape, lambda b: (0,) * a.ndim)
    per_b = lambda a: pl.BlockSpec((1,) + a.shape[1:], lambda b: (b,) + (0,) * (a.ndim - 1))
    hshape = state_ssm.shape[2:]
    ws = [sw[n] for n in ("conv_w", "conv_b", "dt_bias", "a", "d_skip", "norm_g")] + [consts["e_dt"]]
    return pl.pallas_call(
        _s_ssd_kernel,
        out_shape=[jax.ShapeDtypeStruct((ns, 1, SSM_INNER), BF16), jax.ShapeDtypeStruct((ns,) + hshape, F32)],
        grid=(ns,),
        in_specs=[per_b(xbc), per_b(prev), per_b(z), per_b(gd),
                  pl.BlockSpec((1, 1) + hshape, lambda b: (layer, b, 0, 0, 0))] + [full(w) for w in ws],
        out_specs=[pl.BlockSpec((1, 1, SSM_INNER), lambda b: (b, 0, 0)),
                   pl.BlockSpec((1,) + hshape, lambda b: (b, 0, 0, 0))],
        compiler_params=_cparams(("arbitrary",)),
        name="sample_ssd_step",
    )(xbc, prev, z, gd, state_ssm, *ws)


def _constants(t_prompt, past_len):
    c = {}
    eg = np.zeros((3, LANES, ATT_WIDTH), np.float32)
    for br in range(3):
        for h in range(N_HEADS):
            eg[br, br * N_HEADS + h, h * HEAD_DIM:(h + 1) * HEAD_DIM] = 1.0
    c["e_gate"] = [jnp.asarray(eg[i], BF16) for i in range(3)]
    c["e_gate_t"] = jnp.asarray(eg.transpose(0, 2, 1), BF16)
    ed = np.zeros((LANES, SSM_INNER), np.float32)
    for h in range(SSM_HEADS):
        ed[GD_DT + h, h * HEAD_DIM:(h + 1) * HEAD_DIM] = 1.0
    c["e_dt"] = jnp.asarray(ed, BF16)

    def overlap(n_cmp_pad, n_slc, n_slc_pad):
        cs = np.arange(n_cmp_pad)[:, None] * CMP_STRIDE
        ss = np.arange(n_slc_pad)[None, :] * SLC_BLOCK
        ov = ((cs < ss + SLC_BLOCK) & (cs + CMP_BLOCK > ss) & (np.arange(n_slc_pad)[None, :] < n_slc))
        return ov.astype(np.float32)
    nh_p = t_prompt // CMP_STRIDE
    n_slc_p = t_prompt // SLC_BLOCK
    js_p = -(-n_slc_p // SUBLANES) * SUBLANES
    ov = overlap(nh_p, n_slc_p, js_p)
    ov[nh_p - 1:, :] = 0.0
    c["ot"] = jnp.asarray(ov.T, BF16)
    col = np.arange(NCOL)
    slope = np.asarray(SLOPES, np.float64)[col // TQ]
    c["srow"] = jnp.asarray(slope[None, :], F32)
    rel = np.arange(TQ)[:, None] - (col % TQ)[None, :]
    a_rel = slope[None, :] * rel
    c["a_rel"] = jnp.asarray(np.stack([a_rel, np.where(rel > 0, -MASK_BIG, a_rel), np.where(rel < 0, -MASK_BIG, a_rel)]), F32)
    c["c0"] = jnp.asarray(slope[None, :] * ((col % TQ)[None, :]
                                            - (np.arange(nh_p)[:, None] * CMP_STRIDE + CMP_BLOCK - 1)), F32)
    nh_s = (past_len + 1 + CMP_STRIDE - 1) // CMP_STRIDE
    nhp_s = -(-nh_s // LANES) * LANES
    n_slc_s = past_len // SLC_BLOCK + 1
    js_s = -(-n_slc_s // LANES) * LANES
    ov_s = overlap(nhp_s, n_slc_s, js_s)
    ov_s[nh_s - 1:, :] = 0.0
    c["ov_s"] = jnp.asarray(ov_s, BF16)
    c["u_s"] = jnp.asarray(np.triu(np.ones((js_s, js_s), np.float32), 1), BF16)
    tm = np.zeros((LANES, ATT_WIDTH), np.float32)
    for h in range(N_HEADS):
        g = h // HG
        for d in range(HEAD_DIM):
            tm[g * HEAD_DIM + d, h * HEAD_DIM + d] = 1.0
    c["tile_m"] = jnp.asarray(tm, BF16)
    return c


def _pack_w_in(w_in):
    d = w_in.shape[0]
    cuts = np.cumsum([ATT_WIDTH] + [LANES] * 6 + [3 * N_HEADS, SSM_INNER, CONV_DIM])
    q, kv, gt, z, xbc, dtc = (w_in[:, :cuts[0]], w_in[:, cuts[0]:cuts[6]], w_in[:, cuts[6]:cuts[7]],
                              w_in[:, cuts[7]:cuts[8]], w_in[:, cuts[8]:cuts[9]], w_in[:, cuts[9]:])
    pad = jnp.zeros((d, P_W - P_GD - gt.shape[1] - dtc.shape[1]), w_in.dtype)
    return jnp.concatenate([q, kv, z, xbc, gt, dtc, pad], axis=1).astype(BF16)


def _pack_compress(w1, b1, w2, head_tiled):
    zeros = jnp.zeros((CMP_STRIDE, HEAD_DIM, CMP_HID), w1.dtype)
    parts = []
    for ab in range(2):
        w = w1[ab * CMP_STRIDE:(ab + 1) * CMP_STRIDE]
        top = jnp.concatenate([w, zeros], axis=2)
        bot = jnp.concatenate([zeros, w], axis=2)
        parts.append(jnp.concatenate([top, bot], axis=1))
    w1b = jnp.concatenate(parts, axis=2)
    w1b = w1b.reshape(CMP_STRIDE // 2, 2 * LANES, 4 * CMP_HID).astype(BF16)
    b1b = jnp.concatenate([b1, b1]).reshape(1, 2 * CMP_HID)
    z2 = jnp.zeros_like(w2)
    rep = HG if head_tiled else 1
    top = jnp.concatenate([w2] * rep + [z2] * rep, axis=1)
    bot = jnp.concatenate([z2] * rep + [w2] * rep, axis=1)
    w2b = jnp.concatenate([top, bot], axis=0).astype(BF16)
    return w1b, b1b, w2b


def _lane_row(vals, offset, width):
    row = jnp.zeros((1, width), F32)
    return lax.dynamic_update_slice(row, vals.reshape(1, -1).astype(F32), (0, offset))


def _token_minor(a):
    lead = a.shape[:-3]
    n = len(lead)
    return jnp.transpose(a, tuple(range(n)) + (n + 1, n + 2, n)).reshape(lead + (N_KV * HEAD_DIM, a.shape[-3]))


def _token_major(a):
    lead = a.shape[:-2]
    n = len(lead)
    a = a.reshape(lead + (N_KV, HEAD_DIM, a.shape[-1]))
    return jnp.transpose(a, tuple(range(n)) + (n + 2, n, n + 1))


def kernel(x_prompt, x_sample, cache_cmp_k, cache_cmp_v, cache_slc_k, cache_slc_v, state_win_k, state_win_v, state_ssm, state_ssm_conv, state_ffn_conv, page_table, c_prompt, c_sample, ada_w, ada_b, norm1_g, norm2_g, w_in, cmpk_w1, cmpk_b1, cmpk_w2, cmpv_w1, cmpv_b1, cmpv_w2, ssm_conv_w, ssm_conv_b, dt_bias, a_log, d_skip, ssm_norm_g, w_out, ffn_w_up, ffn_conv_w, ffn_conv_b, ffn_w_down, final_g):
    bp, tp, d = x_prompt.shape
    ns = x_sample.shape[0]
    depth = w_in.shape[0]
    page = cache_cmp_k.shape[2]
    past_len = page_table.shape[1] * page
    assert x_sample.shape[1] == 1 and d == D_MODEL
    assert tp % SSM_CHUNK == 0 and tp >= WINDOW and past_len >= WINDOW and page % SLC_BLOCK == 0
    consts = _constants(tp, past_len)
    caches_t = [_token_minor(c) for c in (cache_cmp_k, cache_cmp_v, cache_slc_k, cache_slc_v)]
    win_t = [_token_minor(w) for w in (state_win_k, state_win_v)]

    mod = _mod(jnp.concatenate([c_prompt, c_sample], axis=0), ada_w, ada_b)
    tm_p = 512 if tp % 512 == 0 else SSM_CHUNK

    xp = x_prompt
    xs = x_sample.reshape(1, ns, d)
    outs_p, outs_s = [], []
    for l in range(depth):
        last = l == depth - 1
        mod_p = mod[l, :bp].reshape(bp, 1, 6 * d)
        mod_s = mod[l, bp:].reshape(1, ns, 6 * d)
        w_packed = _pack_w_in(w_in[l])
        cw = dict(zip(("w1k", "b1k", "w2k"), _pack_compress(cmpk_w1[l], cmpk_b1[l], cmpk_w2[l], False)))
        cw.update(zip(("w1v", "b1v", "w2v"), _pack_compress(cmpv_w1[l], cmpv_b1[l], cmpv_w2[l], False)))
        cw4 = dict(zip(("w1k", "b1k", "w2k"), _pack_compress(cmpk_w1[l], cmpk_b1[l], cmpk_w2[l], True)))
        cw4.update(zip(("w1v", "b1v", "w2v"), _pack_compress(cmpv_w1[l], cmpv_b1[l], cmpv_w2[l], True)))
        expand_heads = lambda v: jnp.repeat(v.astype(F32), HEAD_DIM).reshape(1, SSM_INNER)
        sw = dict(conv_w=ssm_conv_w[l], conv_b=ssm_conv_b[l].reshape(1, CONV_DIM),
                  dt_bias=_lane_row(dt_bias[l], GD_DT, LANES), a=_lane_row(-jnp.exp(a_log[l].astype(F32)), GD_DT, LANES),
                  d_skip=expand_heads(d_skip[l]), norm_g=ssm_norm_g[l].reshape(1, SSM_INNER).astype(F32))
        fw = dict(norm2_g=norm2_g[l].reshape(1, d), w_out=w_out[l].astype(BF16), w_up=ffn_w_up[l].astype(BF16),
                  conv_w=ffn_conv_w[l], conv_b=ffn_conv_b[l].reshape(1, D_FF), w_down=ffn_w_down[l].astype(BF16),
                  final_g=final_g.reshape(1, d))

        (qt, kc, vc, kvb, vt, z, xbc, gd, kct, vct, kst, vst, kwt, vwt) = _inproj(
            xp, mod_p, norm1_g[l], w_packed, tm_p, False, True)
        kcmp, vcmpt = _compress(kc, vc, cw)
        ocmpt, selt = _cmp_sel(qt, kcmp, vcmpt, consts)
        oatt = _attn(qt, kvb, vt, selt, ocmpt, gd, consts)
        yssm, h_p = _ssd(xbc, z, gd, sw, consts)
        xp, ug_tail = _mix_ffn(xp, oatt, yssm, mod_p, fw, 256 if tp % 256 == 0 else SSM_CHUNK, False, None, last)
        outs_p.append((kct, vct, kst, vst, kwt[:, :, tp - WINDOW:], vwt[:, :, tp - WINDOW:], h_p,
                       xbc[:, tp - (SSM_CONV - 1):], ug_tail[:, SUBLANES - (FFN_CONV - 1):]))

        q, kvb, z, xbc, gd, kc, vc, ks, vs, kw, vw = _inproj(xs, mod_s, norm1_g[l], w_packed, ns, True, False)
        per_tok = lambda a: a.reshape(ns, 1, a.shape[-1])
        kvn = per_tok(jnp.concatenate([kc, vc, ks, vs, kw, vw], axis=2))
        q, z, xbc, gd = per_tok(q), per_tok(z), per_tok(xbc), per_tok(gd)
        ocmp, idx = _s_cmp(l, page_table, q, kvn, gd, caches_t[0], caches_t[1], cw4, consts, past_len)
        oatt, wk_new, wv_new = _s_slc_win(l, idx, page_table, q, kvn, gd, ocmp, win_t[0], win_t[1],
                                          caches_t[2], caches_t[3], consts, past_len)
        yssm, h_s = _s_ssd(l, xbc, state_ssm_conv[l], z, gd, state_ssm, sw, consts)
        prev_ffn = (state_ffn_conv[l][:, 0].reshape(1, ns, D_FF), state_ffn_conv[l][:, 1].reshape(1, ns, D_FF))
        xs, ug_s = _mix_ffn(xs, oatt.reshape(1, ns, ATT_WIDTH), yssm.reshape(1, ns, SSM_INNER), mod_s, fw, ns, True,
                            prev_ffn, last)
        s4 = lambda a: a.reshape(ns, 1, N_KV, HEAD_DIM)
        outs_s.append((s4(kc), s4(vc), s4(ks), s4(vs), wk_new, wv_new, h_s,
                       jnp.concatenate([state_ssm_conv[l][:, 1:], xbc.reshape(ns, 1, CONV_DIM)], axis=1),
                       jnp.concatenate([state_ffn_conv[l][:, 1:], ug_s.reshape(ns, 1, D_FF)], axis=1)))

    stack = lambda seq, i: jnp.stack([st[i] for st in seq])
    res = [xp, xs.reshape(ns, 1, d)]
    for i in range(9):
        p_i, s_i = stack(outs_p, i), stack(outs_s, i)
        if i < 6:
            p_i = _token_major(p_i)
        if i in (4, 5):
            s_i = _token_major(s_i)
        res += [p_i, s_i]
    return tuple(res)
```

```python
import functools

import numpy as np
import jax
import jax.numpy as jnp
from jax import lax
from jax.experimental import pallas as pl
from jax.experimental.pallas import tpu as pltpu

F32 = jnp.float32
BF16 = jnp.bfloat16

D_MODEL = 1024
HEAD_DIM = 64
ATT_WIDTH = 512
N_HEADS = 8
N_KV = 2
HG = 4
CMP_BLOCK = 32
CMP_STRIDE = 16
CMP_HID = 128
SLC_BLOCK = 64
N_SEL = 16
WINDOW = 512
FORCE_BONUS = 1e6
SSM_INNER = 512
SSM_HEADS = 8
SSM_GROUPS = 2
D_STATE = 64
SSM_CONV = 4
SSM_CHUNK = 128
CONV_DIM = 768
D_FF = 2816
FFN_CONV = 3
EPS = 1e-6
NEG = -1e30
SCALE = HEAD_DIM ** -0.5
LOG2E = 1.4426950408889634
SLOPES = tuple(2.0 ** (-(h + 1)) for h in range(N_HEADS))

LANES = 128
SUBLANES = 8
VMEM_LIMIT = 56 * 1024 * 1024

TQ = 2 * SLC_BLOCK
NCOL = N_HEADS * TQ

P_Q = 0
P_KV = 512
P_Z = 1280
P_XBC = 1792
P_GD = 2560
P_W = 2688
GD_DT = 24
MASK_BIG = 2.0 ** 100


def _cparams(sem):
    return pltpu.CompilerParams(dimension_semantics=sem, vmem_limit_bytes=VMEM_LIMIT)


def _resident(shape):
    nd = len(shape)
    return pl.BlockSpec(shape, lambda *_: (0,) * nd, pipeline_mode=pl.Buffered(1))


def _split3(x):
    hi = x.astype(BF16)
    r = x - hi.astype(F32)
    mid = r.astype(BF16)
    lo = (r - mid.astype(F32)).astype(BF16)
    return hi, mid, lo


def _dot(a, b):
    return jnp.dot(a, b, preferred_element_type=F32)


def _dot_nt(a, b):
    return lax.dot_general(a, b, (((1,), (1,)), ((), ())), preferred_element_type=F32)


def _expand(x, e):
    hi, mid, lo = _split3(x)
    return _dot(hi, e) + _dot(mid, e) + _dot(lo, e)


def _expand_l(e, x):
    hi, mid, lo = _split3(x)
    return _dot(e, hi) + _dot(e, mid) + _dot(e, lo)


def _silu(x):
    return x * jax.nn.sigmoid(x)


def _softplus(x):
    return jnp.maximum(x, 0.0) + jnp.log1p(jnp.exp(-jnp.abs(x)))


def _gelu_tanh(x):
    return 0.5 * x * (1.0 + jnp.tanh(np.sqrt(2.0 / np.pi) * (x + 0.044715 * (x * x * x))))


def _mod_kernel(c_ref, w_ref, b_ref, o_ref):
    c = c_ref[...]
    s = _silu(c).astype(BF16)
    o_ref[0] = _dot(s, w_ref[0].astype(BF16)) + b_ref[0]


def _mod(c_all, ada_w, ada_b):
    depth, d, n6 = ada_w.shape
    rows = c_all.shape[0]
    tn = 512
    return pl.pallas_call(
        _mod_kernel,
        out_shape=jax.ShapeDtypeStruct((depth, rows, n6), F32),
        grid=(depth, n6 // tn),
        in_specs=[pl.BlockSpec((rows, d), lambda l, j: (0, 0)),
                  pl.BlockSpec((1, d, tn), lambda l, j: (l, 0, j)),
                  pl.BlockSpec((1, 1, tn), lambda l, j: (l, 0, j))],
        out_specs=pl.BlockSpec((1, rows, tn), lambda l, j: (l, 0, j)),
        compiler_params=_cparams(("arbitrary", "arbitrary")),
        name="adaln_mod",
    )(c_all, ada_w, ada_b.reshape(depth, 1, n6))


def _mod_spec(tmod, k, per_row):
    if per_row:
        return pl.BlockSpec((1, tmod, D_MODEL), lambda b, t: (b, t, k))
    return pl.BlockSpec((1, 1, D_MODEL), lambda b, t: (b, 0, k))


def _inproj_kernel(x_ref, sh_ref, sc_ref, g_ref, w_ref, *outs, tm, token_minor):
    x = x_ref[0]
    ms = jnp.mean(x * x, axis=-1, keepdims=True)
    xn = x * lax.rsqrt(ms + EPS) * g_ref[...]
    h = xn * (1.0 + sc_ref[0]) + sh_ref[0]
    proj = _dot(h.astype(BF16), w_ref[...])
    kv = [proj[:, P_KV + LANES * i:P_KV + LANES * (i + 1)] for i in range(6)]
    if token_minor:
        qt_ref, kc_ref, vc_ref, kvb_ref, vt_ref, z_ref, xbc_ref, gd_ref = outs[:8]
        qt_ref[0] = (proj[:, P_Q:P_Q + ATT_WIDTH] * (SCALE * LOG2E)).T.astype(BF16)
        kc_ref[0] = kv[0]
        vc_ref[0] = kv[1]
        kvt = [a.T for a in kv]
        for ref, a in zip(outs[8:], kvt):
            ref[0] = a
        for r in range(tm // TQ):
            cols = slice(r * TQ, (r + 1) * TQ)
            vt_ref[0, r] = jnp.concatenate([kvt[3][:, cols], kvt[5][:, cols]], axis=0).astype(BF16)
    else:
        q_ref, kvb_ref, z_ref, xbc_ref, gd_ref = outs[:5]
        q_ref[0] = (proj[:, P_Q:P_Q + ATT_WIDTH] * SCALE).astype(BF16)
        for ref, a in zip(outs[5:], kv):
            ref[0] = a
    kvb_ref[0] = proj[:, P_KV:P_Z].astype(BF16)
    z_ref[0] = proj[:, P_Z:P_XBC]
    xbc_ref[0] = proj[:, P_XBC:P_GD]
    gd_ref[0] = proj[:, P_GD:P_W]


def _inproj(x, mod, norm_g, w_packed, tm, per_row, token_minor):
    bsz, t, d = x.shape
    row = lambda w: pl.BlockSpec((1, tm, w), lambda b, i: (b, i, 0))
    col = lambda w: pl.BlockSpec((1, w, tm), lambda b, i: (b, 0, i))
    shp = lambda w, dt: jax.ShapeDtypeStruct((bsz, t, w), dt)
    shpt = lambda w, dt: jax.ShapeDtypeStruct((bsz, w, t), dt)
    if token_minor:
        out_shape = ([shpt(ATT_WIDTH, BF16), shp(LANES, F32), shp(LANES, F32), shp(6 * LANES, BF16),
                      jax.ShapeDtypeStruct((bsz, t // TQ, 2 * LANES, TQ), BF16),
                      shp(SSM_INNER, F32), shp(CONV_DIM, F32), shp(LANES, F32)] + [shpt(LANES, F32)] * 6)
        out_specs = ([col(ATT_WIDTH), row(LANES), row(LANES), row(6 * LANES),
                      pl.BlockSpec((1, tm // TQ, 2 * LANES, TQ), lambda b, i: (b, i, 0, 0)),
                      row(SSM_INNER), row(CONV_DIM), row(LANES)] + [col(LANES)] * 6)
    else:
        out_shape = ([shp(ATT_WIDTH, BF16), shp(6 * LANES, BF16), shp(SSM_INNER, F32), shp(CONV_DIM, F32),
                      shp(LANES, F32)] + [shp(LANES, F32)] * 6)
        out_specs = [row(ATT_WIDTH), row(6 * LANES), row(SSM_INNER), row(CONV_DIM), row(LANES)] + [row(LANES)] * 6
    return pl.pallas_call(
        functools.partial(_inproj_kernel, tm=tm, token_minor=token_minor),
        out_shape=out_shape,
        grid=(bsz, t // tm),
        in_specs=[row(d), _mod_spec(tm, 0, per_row), _mod_spec(tm, 1, per_row),
                  _resident((1, d)), _resident(w_packed.shape)],
        out_specs=out_specs,
        compiler_params=_cparams(("arbitrary", "arbitrary")),
        name="inproj",
    )(x, mod, mod, norm_g.reshape(1, d), w_packed)


def _compress_block(src_ref, nh, w1_ref, b1_ref, w2_ref):
    acc = jnp.zeros((nh, 4 * CMP_HID), F32)
    for i in range(CMP_STRIDE // 2):
        xa = src_ref[pl.ds(2 * i, nh, stride=CMP_STRIDE), :]
        xb = src_ref[pl.ds(2 * i + 1, nh, stride=CMP_STRIDE), :]
        acc = acc + _dot(jnp.concatenate([xa, xb], axis=1).astype(BF16), w1_ref[i])
    pa = acc[:, :2 * CMP_HID]
    pb = pltpu.roll(acc[:, 2 * CMP_HID:], nh - 1, 0)
    hid = _gelu_tanh(pa + pb + b1_ref[...])
    return _dot(hid.astype(BF16), w2_ref[...])


def _compress_kernel(k_ref, v_ref, w1k_ref, b1k_ref, w2k_ref, w1v_ref, b1v_ref, w2v_ref, ok_ref, ovt_ref, *, nh):
    ok_ref[0] = _compress_block(k_ref.at[0], nh, w1k_ref, b1k_ref, w2k_ref).astype(BF16)
    ovt_ref[0] = _compress_block(v_ref.at[0], nh, w1v_ref, b1v_ref, w2v_ref).T.astype(BF16)


def _compress(kc, vc, cw):
    bsz, t, _ = kc.shape
    nh = t // CMP_STRIDE
    seq = pl.BlockSpec((1, t, LANES), lambda b: (b, 0, 0))
    ws = [cw[n] for n in ("w1k", "b1k", "w2k", "w1v", "b1v", "w2v")]
    return pl.pallas_call(
        functools.partial(_compress_kernel, nh=nh),
        out_shape=[jax.ShapeDtypeStruct((bsz, nh, LANES), BF16), jax.ShapeDtypeStruct((bsz, LANES, nh), BF16)],
        grid=(bsz,),
        in_specs=[seq, seq] + [_resident(w.shape) for w in ws],
        out_specs=[pl.BlockSpec((1, nh, LANES), lambda b: (b, 0, 0)), pl.BlockSpec((1, LANES, nh), lambda b: (b, 0, 0))],
        compiler_params=_cparams(("arbitrary",)),
        name="compress",
    )(kc, vc, *ws)


def _query_blockdiag(qt):
    zero = jnp.zeros((HEAD_DIM, HG * TQ), BF16)
    per_g = [jnp.concatenate([qt[HEAD_DIM * (HG * g + j):HEAD_DIM * (HG * g + j + 1), :] for j in range(HG)], axis=1)
             for g in range(N_KV)]
    return jnp.concatenate([jnp.concatenate([per_g[0], zero], axis=1),
                            jnp.concatenate([zero, per_g[1]], axis=1)], axis=0)


def _topk_mask_t(score, jrow, n_slc):
    rank = jnp.zeros(score.shape, F32)
    for jp in range(n_slc):
        sj = score[jp:jp + 1, :]
        ahead = (sj > score) | ((sj == score) & (jp < jrow))
        rank = rank + ahead.astype(F32)
    return (rank < N_SEL) & (jrow < n_slc)


def _cmp_sel_kernel(qt_ref, kc_ref, vct_ref, ot_ref, c0_ref, srow_ref, ocmpt_ref, selt_ref, *, n_cmp, n_slc):
    t = pl.program_id(1)
    qbd = _query_blockdiag(qt_ref[0])
    kc = kc_ref[0]
    nl = kc.shape[0]
    srow = srow_ref[...]
    s = _dot(kc, qbd) - (c0_ref[...] + srow * (t * TQ).astype(F32))
    nidx = lax.broadcasted_iota(jnp.int32, (nl, NCOL), 0)
    qpos = t * TQ + lax.broadcasted_iota(jnp.int32, (nl, NCOL), 1) % TQ
    mask = (qpos - (nidx * CMP_STRIDE + (CMP_BLOCK - 1)) >= 0) & (nidx < n_cmp)
    sm = jnp.where(mask, s, NEG)
    e = jnp.exp2(sm - jnp.max(sm, axis=0, keepdims=True))
    p = jnp.where(mask, e * (1.0 / jnp.sum(e, axis=0, keepdims=True)), 0.0)
    pb = p.astype(BF16)
    vct = vct_ref[0]

    js = selt_ref.shape[2]
    jrow = lax.broadcasted_iota(jnp.int32, (js, TQ), 0)
    qp = t * TQ + lax.broadcasted_iota(jnp.int32, (js, TQ), 1)
    cur = qp // SLC_BLOCK
    forced = (jrow == 0) | (jrow == cur) | (jrow == cur - 1)
    valid = (jrow * SLC_BLOCK <= qp)
    for g in range(N_KV):
        c0 = g * HG * TQ
        og = _dot(vct[HEAD_DIM * g:HEAD_DIM * (g + 1), :], pb[:, c0:c0 + HG * TQ])
        pg = jnp.zeros((nl, TQ), F32)
        for j in range(HG):
            h = HG * g + j
            ocmpt_ref[0, HEAD_DIM * h:HEAD_DIM * (h + 1), :] = og[:, j * TQ:(j + 1) * TQ]
            pg = pg + p[:, c0 + j * TQ:c0 + (j + 1) * TQ]
        imp_t = _expand_l(ot_ref[...], pg)
        score = jnp.where(valid, imp_t + jnp.where(forced, FORCE_BONUS, 0.0), NEG)
        selt_ref[0, g] = _topk_mask_t(score, jrow, n_slc).astype(F32)


def _cmp_sel(qt, kcmp, vcmpt, consts):
    bsz, _, t = qt.shape
    nl = kcmp.shape[1]
    n_slc = t // SLC_BLOCK
    js = consts["ot"].shape[0]
    return pl.pallas_call(
        functools.partial(_cmp_sel_kernel, n_cmp=t // CMP_STRIDE - 1, n_slc=n_slc),
        out_shape=[jax.ShapeDtypeStruct((bsz, ATT_WIDTH, t), F32),
                   jax.ShapeDtypeStruct((bsz, N_KV, js, t), F32)],
        grid=(bsz, t // TQ),
        in_specs=[pl.BlockSpec((1, ATT_WIDTH, TQ), lambda b, i: (b, 0, i)),
                  pl.BlockSpec((1, nl, LANES), lambda b, i: (b, 0, 0)),
                  pl.BlockSpec((1, LANES, nl), lambda b, i: (b, 0, 0)),
                  _resident(consts["ot"].shape), _resident(consts["c0"].shape), _resident(consts["srow2"].shape)],
        out_specs=[pl.BlockSpec((1, ATT_WIDTH, TQ), lambda b, i: (b, 0, i)),
                   pl.BlockSpec((1, N_KV, js, TQ), lambda b, i: (b, 0, 0, i))],
        compiler_params=_cparams(("arbitrary", "arbitrary")),
        name="cmp_select",
    )(qt, kcmp, vcmpt, consts["ot"], consts["c0"], consts["srow2"])


def _attn_kernel(qt_ref, kvb_ref, vt_ref, selt_ref, ocmpt_ref, gd_ref, fk_ref, coef_ref, amask_ref, egt_ref, o_ref,
                 pen_ref, sbuf, *state):
    nchunk = len(state) // 3
    m_refs, l_refs, acc_refs = state[:nchunk], state[nchunk:2 * nchunk], state[2 * nchunk:]
    t = pl.program_id(1)
    qbd = _query_blockdiag(qt_ref[0])
    gw = HG * TQ
    for g in range(N_KV):
        pen = (selt_ref[0, g] - 1.0) * MASK_BIG
        pen_ref[:, g * gw:(g + 1) * gw] = jnp.concatenate([pen] * HG, axis=1)
    nwin = WINDOW // TQ
    coef = coef_ref[...]
    zero8 = jnp.zeros((SUBLANES, NCOL), F32)
    cw = 2 * TQ
    zpad = jnp.zeros((LANES - 2 * SUBLANES, cw), BF16)
    rid = lax.broadcasted_iota(jnp.int32, (SUBLANES, NCOL), 0)

    def query_operands(c8):
        c16 = jnp.concatenate([c8, zero8], axis=0).astype(BF16)
        return [jnp.concatenate([qbd[:, c * cw:(c + 1) * cw], c16[:, c * cw:(c + 1) * cw], zpad], axis=0)
                for c in range(NCOL // cw)]

    q_win = query_operands(coef)

    def score_tile(kt, slot, kcol, slc, mask):
        r0 = pl.multiple_of(kt * TQ, TQ)
        k_aug = jnp.concatenate([kvb_ref[0, pl.ds(r0, TQ), kcol:kcol + LANES], fk_ref[kt]], axis=1)
        if slc:
            pa = jnp.broadcast_to(pen_ref[pl.ds(2 * kt, 1), :], (SUBLANES, NCOL))
            pb = jnp.broadcast_to(pen_ref[pl.ds(2 * kt + 1, 1), :], (SUBLANES, NCOL))
            q_aug = query_operands(jnp.where(rid == 6, pa, jnp.where(rid == 7, pb, coef)))
        else:
            q_aug = q_win
        for c in range(nchunk):
            cols = slice(c * cw, (c + 1) * cw)
            sbuf[slot, :, cols] = _dot(k_aug, q_aug[c]) + amask_ref[mask, :, cols]

    def consume_tile(kt, slot, vrow):
        v_t = vt_ref[0, kt, vrow:vrow + LANES, :]
        for c in range(nchunk):
            cols = slice(c * cw, (c + 1) * cw)
            g = (c * cw) // gw
            s = sbuf[slot, :, cols]
            m_old = m_refs[c][...]
            m_new = jnp.maximum(m_old, jnp.max(s, axis=0, keepdims=True))
            alpha = jnp.exp2(m_old - m_new)
            p = jnp.exp2(s - m_new)
            l_refs[c][...] = alpha * l_refs[c][...] + jnp.sum(p, axis=0, keepdims=True)
            m_refs[c][...] = m_new
            acc_refs[c][...] = (acc_refs[c][...] * alpha
                                + _dot(v_t[HEAD_DIM * g:HEAD_DIM * (g + 1), :], p.astype(BF16)))

    def reset():
        for c in range(nchunk):
            m_refs[c][...] = jnp.full(m_refs[c].shape, NEG, F32)
            l_refs[c][...] = jnp.zeros(l_refs[c].shape, F32)
            acc_refs[c][...] = jnp.zeros(acc_refs[c].shape, F32)

    def result():
        per_chunk = [acc_refs[c][...] * (1.0 / l_refs[c][...]) for c in range(nchunk)]
        half = nchunk // N_KV
        return [jnp.concatenate(per_chunk[g * half:(g + 1) * half], axis=1) for g in range(N_KV)]

    def branch(first, kcol, vrow, slc):
        def mask_of(kt):
            diag = jnp.where(kt == t, 1, 0)
            return diag if slc else jnp.where(kt == t - nwin, 2, diag)

        score = lambda kt, slot: score_tile(kt, slot, kcol, slc, mask_of(kt))
        reset()
        score(first, 0)
        ahead = t - first

        def body(j, c):
            kt = first + 2 * j
            score(kt + 1, 1)
            consume_tile(kt, 0, vrow)
            score(kt + 2, 0)
            consume_tile(kt + 1, 1, vrow)
            return c

        lax.fori_loop(0, ahead // 2, body, 0)

        @pl.when(ahead % 2 == 0)
        def _():
            consume_tile(t, 0, vrow)

        @pl.when(ahead % 2 == 1)
        def _():
            score(t, 1)
            consume_tile(t - 1, 0, vrow)
            consume_tile(t, 1, vrow)

        return result()

    o_slc = branch(0, 2 * LANES, 0, True)
    o_win = branch(jnp.maximum(t - nwin, 0), 4 * LANES, LANES, False)

    g_t = jax.nn.sigmoid(gd_ref[0]).T
    gates = [_expand_l(egt_ref[br], g_t) for br in range(3)]
    for pr in range(N_HEADS // 2):
        g, j0 = pr // 2, 2 * (pr % 2)
        rows = slice(LANES * pr, LANES * (pr + 1))
        pair = lambda o: jnp.concatenate([o[g][:, j0 * TQ:(j0 + 1) * TQ], o[g][:, (j0 + 1) * TQ:(j0 + 2) * TQ]], axis=0)
        tot = gates[0][rows] * ocmpt_ref[0, rows, :] + gates[1][rows] * pair(o_slc) + gates[2][rows] * pair(o_win)
        o_ref[0, :, rows] = tot.T.astype(BF16)


def _attn(qt, kvb, vt, selt, ocmpt, gd, consts):
    bsz, _, t = qt.shape
    js = selt.shape[2]
    return pl.pallas_call(
        _attn_kernel,
        out_shape=jax.ShapeDtypeStruct((bsz, t, ATT_WIDTH), BF16),
        grid=(bsz, t // TQ),
        in_specs=[pl.BlockSpec((1, ATT_WIDTH, TQ), lambda b, i: (b, 0, i)),
                  pl.BlockSpec((1, t, 6 * LANES), lambda b, i: (b, 0, 0)),
                  pl.BlockSpec((1, t // TQ, 2 * LANES, TQ), lambda b, i: (b, 0, 0, 0)),
                  pl.BlockSpec((1, N_KV, js, TQ), lambda b, i: (b, 0, 0, i)),
                  pl.BlockSpec((1, ATT_WIDTH, TQ), lambda b, i: (b, 0, i)),
                  pl.BlockSpec((1, TQ, LANES), lambda b, i: (b, i, 0)),
                  _resident(consts["fk"].shape), _resident(consts["coef"].shape), _resident(consts["amask"].shape),
                  _resident(consts["e_gate_t"].shape)],
        out_specs=pl.BlockSpec((1, TQ, ATT_WIDTH), lambda b, i: (b, i, 0)),
        scratch_shapes=([pltpu.VMEM((js, NCOL), F32), pltpu.VMEM((2, TQ, NCOL), F32)]
                        + [pltpu.VMEM((1, 2 * TQ), F32)] * (2 * N_HEADS // 2)
                        + [pltpu.VMEM((HEAD_DIM, 2 * TQ), F32)] * (N_HEADS // 2)),
        compiler_params=_cparams(("arbitrary", "arbitrary")),
        name="slc_win_attn",
    )(qt, kvb, vt, selt, ocmpt, gd, consts["fk"], consts["coef"], consts["amask"], consts["e_gate_t"])


def _gated_norm(y, z, ng):
    yz = y * _silu(z)
    half = SSM_INNER // SSM_GROUPS
    outs = []
    for g in range(SSM_GROUPS):
        part = yz[:, half * g:half * (g + 1)]
        ms = jnp.mean(part * part, axis=-1, keepdims=True)
        outs.append(part * lax.rsqrt(ms + EPS))
    return jnp.concatenate(outs, axis=1) * ng


def _ssd_kernel(xbc_ref, z_ref, gd_ref, cw_ref, cb_ref, dtb_ref, a_ref, dsk_ref, ng_ref, edt_ref,
                y_ref, hfin_ref, ubuf, sstate, ybuf, *, lc):
    t = pl.program_id(1)
    nt = pl.num_programs(1)

    @pl.when(t == 0)
    def _():
        ubuf[0:SUBLANES, :] = jnp.zeros((SUBLANES, CONV_DIM), F32)
        sstate[...] = jnp.zeros(sstate.shape, F32)

    ubuf[SUBLANES:SUBLANES + lc, :] = xbc_ref[0]
    conv = cb_ref[...] + jnp.zeros((lc, CONV_DIM), F32)
    for k in range(SSM_CONV):
        off = SUBLANES - (SSM_CONV - 1) + k
        conv = conv + ubuf[off:off + lc, :] * cw_ref[k:k + 1, :]
    ubuf[0:SUBLANES, :] = ubuf[lc:lc + SUBLANES, :]
    act = _silu(conv)
    xm = act[:, :SSM_INNER]
    bm = act[:, SSM_INNER:SSM_INNER + LANES]
    cm = act[:, SSM_INNER + LANES:]

    dt = _softplus(gd_ref[0] + dtb_ref[...])
    a = dt * a_ref[...]
    ri = lax.broadcasted_iota(jnp.int32, (lc, lc), 0)
    ci = lax.broadcasted_iota(jnp.int32, (lc, lc), 1)
    tril = ri >= ci
    cs = jnp.dot(tril.astype(F32), a, preferred_element_type=F32, precision=lax.Precision.HIGHEST)
    cs_t = cs.T
    edt = edt_ref[...]
    dt_x = _expand(dt, edt)
    cs_x = _expand(cs, edt)
    cs_last = cs_x[lc - 1:lc, :]
    bm_t = bm.T
    xd = xm * dt_x
    xw = xm * (jnp.exp(cs_last - cs_x) * dt_x)
    s_old = sstate[...]
    half = SSM_INNER // SSM_GROUPS
    for g in range(SSM_GROUPS):
        bg = bm[:, D_STATE * g:D_STATE * (g + 1)].astype(BF16)
        cg = cm[:, D_STATE * g:D_STATE * (g + 1)].astype(BF16)
        cb = _dot_nt(cg, bg)
        for j in range(SSM_HEADS // SSM_GROUPS):
            h = g * (SSM_HEADS // SSM_GROUPS) + j
            col = cs[:, GD_DT + h:GD_DT + h + 1]
            row = cs_t[GD_DT + h:GD_DT + h + 1, :]
            lm = jnp.exp(jnp.where(tril, col - row, NEG))
            lo, hi = HEAD_DIM * h, HEAD_DIM * (h + 1)
            ybuf[:, lo:hi] = _dot((cb * lm).astype(BF16), xd[:, lo:hi].astype(BF16))
        sg = s_old[:, half * g:half * (g + 1)]
        y_off = _dot(cg, sg.astype(BF16))
        ybuf[:, half * g:half * (g + 1)] = ybuf[:, half * g:half * (g + 1)] + y_off * jnp.exp(cs_x[:, half * g:half * (g + 1)])
        st = _dot(bm_t[D_STATE * g:D_STATE * (g + 1), :].astype(BF16), xw[:, half * g:half * (g + 1)].astype(BF16))
        sstate[:, half * g:half * (g + 1)] = sg * jnp.exp(cs_last[:, half * g:half * (g + 1)]) + st

    y = ybuf[...] + dsk_ref[...] * xm
    y_ref[0] = _gated_norm(y, z_ref[0], ng_ref[...]).astype(BF16)

    @pl.when(t == nt - 1)
    def _():
        s_pad = jnp.concatenate([sstate[...], jnp.zeros((LANES - D_STATE, SSM_INNER), F32)], axis=0)
        hfin_ref[0] = s_pad.T[:, :D_STATE]


def _ssd(xbc, z, gd, sw, consts):
    bsz, t, _ = xbc.shape
    lc = SSM_CHUNK
    tile = lambda w: pl.BlockSpec((1, lc, w), lambda b, i: (b, i, 0))
    ws = [sw[n] for n in ("conv_w", "conv_b", "dt_bias", "a", "d_skip", "norm_g")] + [consts["e_dt"]]
    y, hfin = pl.pallas_call(
        functools.partial(_ssd_kernel, lc=lc),
        out_shape=[jax.ShapeDtypeStruct((bsz, t, SSM_INNER), BF16),
                   jax.ShapeDtypeStruct((bsz, SSM_INNER, D_STATE), F32)],
        grid=(bsz, t // lc),
        in_specs=[tile(CONV_DIM), tile(SSM_INNER), tile(LANES)] + [_resident(w.shape) for w in ws],
        out_specs=[tile(SSM_INNER), pl.BlockSpec((1, SSM_INNER, D_STATE), lambda b, i: (b, 0, 0))],
        scratch_shapes=[pltpu.VMEM((lc + SUBLANES, CONV_DIM), F32),
                        pltpu.VMEM((D_STATE, SSM_INNER), F32),
                        pltpu.VMEM((lc, SSM_INNER), F32)],
        compiler_params=_cparams(("arbitrary", "arbitrary")),
        name="ssd_scan",
    )(xbc, z, gd, *ws)
    return y, hfin.reshape(bsz, SSM_HEADS, HEAD_DIM, D_STATE)


def _mix_ffn_kernel(*refs, tm, carry_conv, last):
    if carry_conv:
        (x_ref, oa_ref, ys_ref, g1_ref, sh2_ref, sc2_ref, g2_ref, n2_ref, wo_ref, wu_ref, fcw_ref, fcb_ref,
         wd_ref, fg_ref, out_ref, ug_ref, ubuf) = refs
    else:
        (x_ref, oa_ref, ys_ref, g1_ref, sh2_ref, sc2_ref, g2_ref, n2_ref, wo_ref, wu_ref, fcw_ref, fcb_ref,
         wd_ref, fg_ref, p0_ref, p1_ref, out_ref, ug_ref) = refs
    t = pl.program_id(1)
    nt = pl.num_programs(1)
    x = x_ref[0]
    mix = _dot(oa_ref[0], wo_ref[:ATT_WIDTH, :]) + _dot(ys_ref[0], wo_ref[ATT_WIDTH:, :])
    x1 = x + g1_ref[0] * mix
    ms = jnp.mean(x1 * x1, axis=-1, keepdims=True)
    h2 = (x1 * lax.rsqrt(ms + EPS) * n2_ref[...]) * (1.0 + sc2_ref[0]) + sh2_ref[0]
    h2 = h2.astype(BF16)
    ug = _dot(h2, wu_ref[:, :D_FF])
    uv = _dot(h2, wu_ref[:, D_FF:])
    if carry_conv:
        @pl.when(t == 0)
        def _():
            ubuf[0:SUBLANES, :] = jnp.zeros((SUBLANES, D_FF), F32)

        ubuf[SUBLANES:SUBLANES + tm, :] = ug
        u1 = ubuf[SUBLANES - 1:SUBLANES - 1 + tm, :]
        u2 = ubuf[SUBLANES - 2:SUBLANES - 2 + tm, :]
        ubuf[0:SUBLANES, :] = ubuf[tm:tm + SUBLANES, :]

        @pl.when(t == nt - 1)
        def _():
            ug_ref[0] = ug[tm - SUBLANES:, :]
    else:
        u1 = p1_ref[0]
        u2 = p0_ref[0]
        ug_ref[0] = ug
    ugc = fcb_ref[...] + u2 * fcw_ref[0:1, :] + u1 * fcw_ref[1:2, :] + ug * fcw_ref[2:3, :]
    act = (_silu(ugc) * uv).astype(BF16)
    x2 = x1 + g2_ref[0] * _dot(act, wd_ref[...])
    if last:
        ms2 = jnp.mean(x2 * x2, axis=-1, keepdims=True)
        out_ref[0] = x2 * lax.rsqrt(ms2 + EPS) * fg_ref[...]
    else:
        out_ref[0] = x2


def _mix_ffn(x, oatt, yssm, mod, fw, tm, per_row, prev, last):
    bsz, t, d = x.shape
    carry = prev is None
    row = lambda w: pl.BlockSpec((1, tm, w), lambda b, i: (b, i, 0))
    ws = [fw[n] for n in ("norm2_g", "w_out", "w_up", "conv_w", "conv_b", "w_down", "final_g")]
    in_specs = ([row(d), row(ATT_WIDTH), row(SSM_INNER), _mod_spec(tm, 2, per_row), _mod_spec(tm, 3, per_row),
                 _mod_spec(tm, 4, per_row), _mod_spec(tm, 5, per_row)] + [_resident(w.shape) for w in ws])
    args = [x, oatt, yssm, mod, mod, mod, mod] + ws
    if carry:
        ug_shape, ug_spec = (bsz, SUBLANES, D_FF), pl.BlockSpec((1, SUBLANES, D_FF), lambda b, i: (b, 0, 0))
        scratch = [pltpu.VMEM((tm + SUBLANES, D_FF), F32)]
    else:
        ug_shape, ug_spec = (bsz, t, D_FF), row(D_FF)
        scratch = []
        in_specs += [row(D_FF), row(D_FF)]
        args += list(prev)
    return pl.pallas_call(
        functools.partial(_mix_ffn_kernel, tm=tm, carry_conv=carry, last=last),
        out_shape=[jax.ShapeDtypeStruct((bsz, t, d), F32), jax.ShapeDtypeStruct(ug_shape, F32)],
        grid=(bsz, t // tm),
        in_specs=in_specs,
        out_specs=[row(d), ug_spec],
        scratch_shapes=scratch,
        compiler_params=_cparams(("arbitrary", "arbitrary")),
        name="mix_ffn",
    )(*args)


def _head_rows(row512):
    r = lax.broadcasted_iota(jnp.int32, (N_HEADS, ATT_WIDTH), 0)
    c = lax.broadcasted_iota(jnp.int32, (N_HEADS, ATT_WIDTH), 1)
    return jnp.where(c // HEAD_DIM == r, jnp.broadcast_to(row512, (N_HEADS, ATT_WIDTH)), 0.0)


def _head_diag(x8):
    r = lax.broadcasted_iota(jnp.int32, (N_HEADS, ATT_WIDTH), 0)
    c = lax.broadcasted_iota(jnp.int32, (N_HEADS, ATT_WIDTH), 1)
    return jnp.sum(jnp.where(c // HEAD_DIM == r, x8, 0.0), axis=0, keepdims=True)


def _slope_col():
    r = lax.broadcasted_iota(jnp.int32, (N_HEADS, 1), 0)
    s = jnp.zeros((N_HEADS, 1), F32)
    for h in range(N_HEADS):
        s = jnp.where(r == h, SLOPES[h], s)
    return s


def _s_cmp_kernel(pt_ref, q_ref, kvn_ref, gd_ref, ck_hbm, cv_hbm,
                  w1k_ref, b1k_ref, w2k_ref, w1v_ref, b1v_ref, w2v_ref, ov_ref, u_ref, e0_ref,
                  ocmp_ref, idx_ref, stage_k, stage_v, kbuf, vbuf, sem,
                  *, layer, n_pages, page, nh8, nhp, q_pos, n_slc):
    b = pl.program_id(0)
    nb = pl.num_programs(0)
    past = n_pages * page

    def copies(bb, p, slot):
        pg = pt_ref[bb * n_pages + p]
        return (pltpu.make_async_copy(ck_hbm.at[layer, pg], stage_k.at[slot, p], sem.at[0, slot]),
                pltpu.make_async_copy(cv_hbm.at[layer, pg], stage_v.at[slot, p], sem.at[1, slot]))

    def issue(bb, slot):
        def body(p, c):
            for cp in copies(bb, p, slot):
                cp.start()
            return c
        lax.fori_loop(0, n_pages, body, 0)

    @pl.when(b == 0)
    def _():
        issue(0, 0)

    @pl.when(b + 1 < nb)
    def _():
        issue(b + 1, (b + 1) % 2)

    slot = b % 2

    def wait(p, c):
        for cp in copies(b, p, slot):
            cp.wait()
        return c

    lax.fori_loop(0, n_pages, wait, 0)

    unroll = 8 if n_pages % 8 == 0 else 1

    def untranspose(i, c):
        for u in range(unroll):
            p = i * unroll + u
            rows = pl.ds(pl.multiple_of(p * page, page), page)
            kbuf[rows, :] = stage_k[slot, p].T
            vbuf[rows, :] = stage_v[slot, p].T
        return c

    lax.fori_loop(0, n_pages // unroll, untranspose, 0)
    tail = nh8 * CMP_STRIDE - past
    kbuf[past:, :] = jnp.zeros((tail, LANES), F32)
    vbuf[past:, :] = jnp.zeros((tail, LANES), F32)
    kbuf[past:past + 1, :] = kvn_ref[0, :, 0:LANES]
    vbuf[past:past + 1, :] = kvn_ref[0, :, LANES:2 * LANES]

    pad = jnp.zeros((nhp - nh8, ATT_WIDTH), F32)
    kc4 = jnp.concatenate([_compress_block(kbuf, nh8, w1k_ref, b1k_ref, w2k_ref), pad], axis=0).astype(BF16)
    vc4 = jnp.concatenate([_compress_block(vbuf, nh8, w1v_ref, b1v_ref, w2v_ref), pad], axis=0).astype(BF16)
    n_cmp = (past + 1 + CMP_STRIDE - 1) // CMP_STRIDE - 1

    qm = _head_rows(q_ref[0].astype(F32)).astype(BF16)
    nidx = lax.broadcasted_iota(jnp.int32, (N_HEADS, nhp), 1)
    dist_i = q_pos - (nidx * CMP_STRIDE + (CMP_BLOCK - 1))
    mask = (dist_i >= 0) & (nidx < n_cmp)
    s = _dot_nt(qm, kc4) - _slope_col() * dist_i.astype(F32)
    sm = jnp.where(mask, s, NEG)
    e = jnp.exp(sm - jnp.max(sm, axis=-1, keepdims=True))
    p = jnp.where(mask, e / jnp.sum(e, axis=-1, keepdims=True), 0.0)
    o = _head_diag(_dot(p.astype(BF16), vc4))
    g8 = jnp.broadcast_to(jax.nn.sigmoid(gd_ref[0]), (SUBLANES, LANES))
    gate = _expand(g8, e0_ref[...])[0:1, :]
    ocmp_ref[0] = gate * o

    js = ov_ref.shape[1]
    imp8 = _expand(p, ov_ref[...])
    hrow = lax.broadcasted_iota(jnp.int32, (N_HEADS, js), 0)
    jl = lax.broadcasted_iota(jnp.int32, (1, js), 1)
    cur = q_pos // SLC_BLOCK
    forced = (jl == 0) | (jl == cur) | (jl == cur - 1)
    valid = (jl * SLC_BLOCK <= q_pos) & (jl < n_slc)
    rj = lax.broadcasted_iota(jnp.int32, (js, js), 0)
    cj = lax.broadcasted_iota(jnp.int32, (js, js), 1)
    kk = lax.broadcasted_iota(jnp.int32, (2 * SUBLANES, js), 0)
    jvals = jnp.broadcast_to(jl.astype(F32), (SUBLANES, js)).astype(BF16)
    for g in range(N_KV):
        imp = jnp.sum(jnp.where(hrow // HG == g, imp8, 0.0), axis=0, keepdims=True)
        score = jnp.where(valid, imp + jnp.where(forced, FORCE_BONUS, 0.0), NEG)
        sb = jnp.broadcast_to(score, (js, js))
        col = jnp.sum(jnp.where(rj == cj, sb, 0.0), axis=1, keepdims=True)
        ahead = (col > sb) | ((col == sb) & (rj < cj))
        rank = jnp.sum(ahead.astype(F32), axis=0, keepdims=True)
        sel = ((rank < N_SEL) & (jl < n_slc)).astype(F32)
        pos = _dot(jnp.broadcast_to(sel, (SUBLANES, js)).astype(BF16), u_ref[...])[0:1, :]
        onehot = ((jnp.broadcast_to(pos, (2 * SUBLANES, js)) == kk.astype(F32))
                  & (jnp.broadcast_to(sel, (2 * SUBLANES, js)) > 0.5)).astype(BF16)
        idx = _dot_nt(jvals, onehot)[0:1, :]
        idx_ref[0, g:g + 1, :] = idx.astype(jnp.int32)


def _s_cmp(layer, page_table, q, kvn, gd, cache_k, cache_v, cw4, consts, q_pos):
    ns, n_pages = page_table.shape
    page = cache_k.shape[3]
    past = n_pages * page
    nh = (past + 1 + CMP_STRIDE - 1) // CMP_STRIDE
    nh8 = -(-nh // SUBLANES) * SUBLANES
    nhp = -(-nh // LANES) * LANES
    n_slc = past // SLC_BLOCK + 1
    ws = [cw4[n] for n in ("w1k", "b1k", "w2k", "w1v", "b1v", "w2v")] + [consts["ov_s"], consts["u_s"], consts["e_gate"][0]]
    full = lambda a: pl.BlockSpec(a.shape, lambda b, pt: (0,) * a.ndim)
    per_b = lambda a: pl.BlockSpec((1,) + a.shape[1:], lambda b, pt: (b,) + (0,) * (a.ndim - 1))
    grid_spec = pltpu.PrefetchScalarGridSpec(
        num_scalar_prefetch=1,
        grid=(ns,),
        in_specs=[per_b(q), per_b(kvn), per_b(gd), pl.BlockSpec(memory_space=pl.ANY), pl.BlockSpec(memory_space=pl.ANY)]
                 + [full(w) for w in ws],
        out_specs=[pl.BlockSpec((1, 1, ATT_WIDTH), lambda b, pt: (b, 0, 0)),
                   pl.BlockSpec((1, N_KV, N_SEL), lambda b, pt: (b, 0, 0))],
        scratch_shapes=[pltpu.VMEM((2, n_pages, LANES, page), F32), pltpu.VMEM((2, n_pages, LANES, page), F32),
                        pltpu.VMEM((nh8 * CMP_STRIDE, LANES), F32), pltpu.VMEM((nh8 * CMP_STRIDE, LANES), F32),
                        pltpu.SemaphoreType.DMA((2, 2))],
    )
    return pl.pallas_call(
        functools.partial(_s_cmp_kernel, layer=layer, n_pages=n_pages, page=page, nh8=nh8, nhp=nhp, q_pos=q_pos,
                          n_slc=n_slc),
        out_shape=[jax.ShapeDtypeStruct((ns, 1, ATT_WIDTH), F32), jax.ShapeDtypeStruct((ns, N_KV, N_SEL), jnp.int32)],
        grid_spec=grid_spec,
        compiler_params=_cparams(("arbitrary",)),
        name="sample_cmp_select",
    )(page_table.reshape(-1), q, kvn, gd, cache_k, cache_v, *ws)


def _s_slc_win_kernel(idx_ref, pt_ref, q_ref, kvn_ref, gd_ref, ocmp_ref, wk_ref, wv_ref, sk_hbm, sv_hbm,
                      tile_ref, e1_ref, e2_ref, o_ref, wko_ref, wvo_ref, kb, vb, sem,
                      *, layer, n_pages, page, q_pos, wb):
    b = pl.program_id(0)
    nb = pl.num_programs(0)
    bpp = page // SLC_BLOCK
    n_past = n_pages * bpp
    nblk = N_KV * N_SEL
    blks = [idx_ref[b * nblk + i] for i in range(nblk)]
    slot = b % 2

    def copies(bb, sl, i):
        blk = idx_ref[bb * nblk + i]
        pg = pt_ref[bb * n_pages + jnp.minimum(blk, n_past - 1) // bpp]
        dst = slice(i * page, (i + 1) * page)
        return (pltpu.make_async_copy(sk_hbm.at[layer, pg], kb.at[sl, :, dst], sem.at[0, sl]),
                pltpu.make_async_copy(sv_hbm.at[layer, pg], vb.at[sl, :, dst], sem.at[1, sl]))

    def issue(bb, sl):
        for i in range(nblk):
            for cp in copies(bb, sl, i):
                cp.start()

    @pl.when(b == 0)
    def _():
        issue(0, 0)

    @pl.when(b + 1 < nb)
    def _():
        issue(b + 1, (b + 1) % 2)

    tile_m = tile_ref[...]
    q8 = _dot_nt(_head_rows(q_ref[0].astype(F32)).astype(BF16), tile_m).astype(BF16)
    slope = _slope_col()
    g8 = jnp.broadcast_to(jax.nn.sigmoid(gd_ref[0]), (SUBLANES, LANES))
    gate1 = _expand(g8, e1_ref[...])[0:1, :]
    gate2 = _expand(g8, e2_ref[...])[0:1, :]
    rgrp = lax.broadcasted_iota(jnp.int32, (N_HEADS, 1), 0) // HG

    def new_key(lane0):
        kn = kvn_ref[0, :, lane0:lane0 + LANES].astype(BF16).astype(F32)
        vn = kvn_ref[0, :, lane0 + LANES:lane0 + 2 * LANES].astype(BF16).astype(F32)
        return jnp.sum(q8.astype(F32) * kn, axis=-1, keepdims=True), vn

    def finish(e, e_new, v_t, v_new, l):
        o8 = (_dot_nt(e.astype(BF16), v_t.astype(BF16)) + e_new * v_new) / l
        return _head_diag(_expand(o8, tile_m))

    def shifted(w_t, new_row):
        col = jnp.broadcast_to(new_row, (LANES, LANES)).T[:, 0:1]
        lane = lax.broadcasted_iota(jnp.int32, (LANES, wb), 1)
        return jnp.where(lane == wb - 1, col, pltpu.roll(w_t, wb - 1, 1))

    wk_t = wk_ref[0, 0]
    wv_t = wv_ref[0, 0]
    kpos = (q_pos - wb) + lax.broadcasted_iota(jnp.int32, (N_HEADS, wb), 1)
    dist = q_pos - kpos
    wmask = (dist >= 0) & (dist <= WINDOW)
    s = jnp.where(wmask, _dot(q8, wk_t.astype(BF16)) - slope * dist.astype(F32), NEG)
    s_new, v_new = new_key(4 * LANES)
    m = jnp.maximum(jnp.max(s, axis=-1, keepdims=True), s_new)
    e = jnp.where(wmask, jnp.exp(s - m), 0.0)
    e_new = jnp.exp(s_new - m)
    o_win = finish(e, e_new, wv_t, v_new, jnp.sum(e, axis=-1, keepdims=True) + e_new)
    wko_ref[0] = shifted(wk_t, kvn_ref[0, :, 4 * LANES:5 * LANES])
    wvo_ref[0] = shifted(wv_t, kvn_ref[0, :, 5 * LANES:6 * LANES])

    for i in range(nblk):
        for cp in copies(b, slot, i):
            cp.wait()
    lane = lax.broadcasted_iota(jnp.int32, (1, page), 1)
    kpos_t, ok_t = [], []
    has_new = [jnp.zeros((1, 1), jnp.int32) for _ in range(N_KV)]
    for i in range(nblk):
        blk = blks[i]
        kpos_t.append(blk * SLC_BLOCK + lane % SLC_BLOCK)
        ok_t.append(((lane // SLC_BLOCK) == blk % bpp) & (blk < n_past))
        has_new[i // N_SEL] = jnp.maximum(has_new[i // N_SEL], (blk >= n_past).astype(jnp.int32))
    kpos = jnp.concatenate(kpos_t, axis=1)
    ok = jnp.concatenate(ok_t, axis=1)
    pgrp = lax.broadcasted_iota(jnp.int32, (1, nblk * page), 1) // (N_SEL * page)
    dist = q_pos - kpos
    kmask = ok & (dist >= 0) & (rgrp == pgrp)
    s = jnp.where(kmask, _dot(q8, kb[slot].astype(BF16)) - slope * dist.astype(F32), NEG)
    s_new, v_new = new_key(2 * LANES)
    new_on = jnp.where(rgrp == 0, has_new[0], has_new[1]) > 0
    sn = jnp.where(new_on, s_new, NEG)
    m = jnp.maximum(jnp.max(s, axis=-1, keepdims=True), sn)
    e = jnp.where(kmask, jnp.exp(s - m), 0.0)
    e_new = jnp.where(new_on, jnp.exp(sn - m), 0.0)
    o_slc = finish(e, e_new, vb[slot], v_new, jnp.sum(e, axis=-1, keepdims=True) + e_new)

    o_ref[0] = (ocmp_ref[0] + gate1 * o_slc + gate2 * o_win).astype(BF16)


def _s_slc_win(layer, idx, page_table, q, kvn, gd, ocmp, win_k, win_v, slc_k, slc_v, consts, q_pos):
    ns, n_pages = page_table.shape
    page = slc_k.shape[3]
    wb = win_k.shape[3]
    full = lambda a: pl.BlockSpec(a.shape, lambda b, i, pt: (0,) * a.ndim)
    per_b = lambda a: pl.BlockSpec((1,) + a.shape[1:], lambda b, i, pt: (b,) + (0,) * (a.ndim - 1))
    win = pl.BlockSpec((1, 1, LANES, wb), lambda b, i, pt: (layer, b, 0, 0))
    ws = [consts["tile_m"], consts["e_gate"][1], consts["e_gate"][2]]
    nlane = N_KV * N_SEL * page
    grid_spec = pltpu.PrefetchScalarGridSpec(
        num_scalar_prefetch=2,
        grid=(ns,),
        in_specs=[per_b(q), per_b(kvn), per_b(gd), per_b(ocmp), win, win,
                  pl.BlockSpec(memory_space=pl.ANY), pl.BlockSpec(memory_space=pl.ANY)] + [full(w) for w in ws],
        out_specs=[pl.BlockSpec((1, 1, ATT_WIDTH), lambda b, i, pt: (b, 0, 0)),
                   pl.BlockSpec((1, LANES, wb), lambda b, i, pt: (b, 0, 0)),
                   pl.BlockSpec((1, LANES, wb), lambda b, i, pt: (b, 0, 0))],
        scratch_shapes=[pltpu.VMEM((2, LANES, nlane), F32), pltpu.VMEM((2, LANES, nlane), F32),
                        pltpu.SemaphoreType.DMA((2, 2))],
    )
    return pl.pallas_call(
        functools.partial(_s_slc_win_kernel, layer=layer, n_pages=n_pages, page=page, q_pos=q_pos, wb=wb),
        out_shape=[jax.ShapeDtypeStruct((ns, 1, ATT_WIDTH), BF16),
                   jax.ShapeDtypeStruct((ns, LANES, wb), F32), jax.ShapeDtypeStruct((ns, LANES, wb), F32)],
        grid_spec=grid_spec,
        compiler_params=_cparams(("arbitrary",)),
        name="sample_slc_win_attn",
    )(idx.reshape(-1), page_table.reshape(-1), q, kvn, gd, ocmp, win_k, win_v, slc_k, slc_v, *ws)


def _col_bcast(row):
    blocks = [jnp.broadcast_to(row[:, LANES * i:LANES * (i + 1)], (LANES, LANES)).T for i in range(row.shape[1] // LANES)]
    return jnp.concatenate(blocks, axis=0)


def _s_ssd_kernel(xbc_ref, prev_ref, z_ref, gd_ref, h0_ref, cw_ref, cb_ref, dtb_ref, a_ref, dsk_ref, ng_ref,
                  edt_ref, y_ref, h_ref):
    conv = cb_ref[...] + xbc_ref[0] * cw_ref[SSM_CONV - 1:SSM_CONV, :]
    for k in range(SSM_CONV - 1):
        conv = conv + prev_ref[0, k:k + 1, :] * cw_ref[k:k + 1, :]
    act = _silu(conv)
    xm = act[:, :SSM_INNER]
    bm = act[:, SSM_INNER:SSM_INNER + LANES]
    cm = act[:, SSM_INNER + LANES:]
    dt = _softplus(gd_ref[0] + dtb_ref[...])
    a = dt * a_ref[...]
    edt = edt_ref[...]
    dt_x = _expand(jnp.broadcast_to(dt, (SUBLANES, LANES)), edt)[0:1, :]
    dec_x = jnp.exp(_expand(jnp.broadcast_to(a, (SUBLANES, LANES)), edt)[0:1, :])
    dtx = dt_x * xm
    h0 = h0_ref[0, 0].reshape(SSM_INNER, D_STATE)
    half = SSM_INNER // SSM_GROUPS
    rsel = lax.broadcasted_iota(jnp.int32, (SSM_INNER, 1), 0) // half
    lsel = lax.broadcasted_iota(jnp.int32, (1, SSM_INNER), 1) // half
    y_off = jnp.zeros((1, SSM_INNER), F32)
    cbx = jnp.zeros((1, SSM_INNER), F32)
    brow = jnp.zeros((SSM_INNER, D_STATE), F32)
    for g in range(SSM_GROUPS):
        bg = bm[:, D_STATE * g:D_STATE * (g + 1)]
        cg = cm[:, D_STATE * g:D_STATE * (g + 1)]
        c8 = jnp.broadcast_to(cg, (SUBLANES, D_STATE)).astype(BF16)
        yo = _dot_nt(c8, h0[half * g:half * (g + 1), :].astype(BF16))[0:1, :]
        y_off = jnp.where(lsel == g, jnp.concatenate([yo] * SSM_GROUPS, axis=1), y_off)
        cbx = jnp.where(lsel == g, jnp.sum(cg * bg, axis=-1, keepdims=True), cbx)
        brow = jnp.where(rsel == g, jnp.broadcast_to(bg, (SSM_INNER, D_STATE)), brow)
    y = y_off * dec_x + cbx * dtx + dsk_ref[...] * xm
    y_ref[0] = _gated_norm(y, z_ref[0], ng_ref[...]).astype(BF16)
    h_new = h0 * _col_bcast(dec_x)[:, :D_STATE] + _col_bcast(dtx)[:, :D_STATE] * brow
    h_ref[0] = h_new.reshape(SSM_HEADS, HEAD_DIM, D_STATE)


def _s_ssd(layer, xbc, prev, z, gd, state_ssm, sw, consts):
    ns = state_ssm.shape[1]
    full = lambda a: pl.BlockSpec(a.shape, lambda b: (0,) * a.ndim)
    per_b = lambda a: pl.BlockSpec((1,) + a.shape[1:], lambda b: (b,) + (0,) * (a.ndim - 1))
    hshape = state_ssm.shape[2:]
    ws = [sw[n] for n in ("conv_w", "conv_b", "dt_bias", "a", "d_skip", "norm_g")] + [consts["e_dt"]]
    return pl.pallas_call(
        _s_ssd_kernel,
        out_shape=[jax.ShapeDtypeStruct((ns, 1, SSM_INNER), BF16), jax.ShapeDtypeStruct((ns,) + hshape, F32)],
        grid=(ns,),
        in_specs=[per_b(xbc), per_b(prev), per_b(z), per_b(gd),
                  pl.BlockSpec((1, 1) + hshape, lambda b: (layer, b, 0, 0, 0))] + [full(w) for w in ws],
        out_specs=[pl.BlockSpec((1, 1, SSM_INNER), lambda b: (b, 0, 0)),
                   pl.BlockSpec((1,) + hshape, lambda b: (b, 0, 0, 0))],
        compiler_params=_cparams(("arbitrary",)),
        name="sample_ssd_step",
    )(xbc, prev, z, gd, state_ssm, *ws)


def _constants(t_prompt, past_len):
    c = {}
    eg = np.zeros((3, LANES, ATT_WIDTH), np.float32)
    for br in range(3):
        for h in range(N_HEADS):
            eg[br, br * N_HEADS + h, h * HEAD_DIM:(h + 1) * HEAD_DIM] = 1.0
    c["e_gate"] = [jnp.asarray(eg[i], BF16) for i in range(3)]
    c["e_gate_t"] = jnp.asarray(eg.transpose(0, 2, 1), BF16)
    ed = np.zeros((LANES, SSM_INNER), np.float32)
    for h in range(SSM_HEADS):
        ed[GD_DT + h, h * HEAD_DIM:(h + 1) * HEAD_DIM] = 1.0
    c["e_dt"] = jnp.asarray(ed, BF16)

    def overlap(n_cmp_pad, n_slc, n_slc_pad):
        cs = np.arange(n_cmp_pad)[:, None] * CMP_STRIDE
        ss = np.arange(n_slc_pad)[None, :] * SLC_BLOCK
        ov = ((cs < ss + SLC_BLOCK) & (cs + CMP_BLOCK > ss) & (np.arange(n_slc_pad)[None, :] < n_slc))
        return ov.astype(np.float32)
    nh_p = t_prompt // CMP_STRIDE
    n_slc_p = t_prompt // SLC_BLOCK
    js_p = -(-n_slc_p // SUBLANES) * SUBLANES
    ov = overlap(nh_p, n_slc_p, js_p)
    ov[nh_p - 1:, :] = 0.0
    c["ot"] = jnp.asarray(ov.T, BF16)
    col = np.arange(NCOL)
    slope = np.asarray(SLOPES, np.float64)[col // TQ]
    c["srow2"] = jnp.asarray(LOG2E * slope[None, :], F32)
    c["c0"] = jnp.asarray(LOG2E * slope[None, :] * ((col % TQ)[None, :]
                                                    - (np.arange(nh_p)[:, None] * CMP_STRIDE + CMP_BLOCK - 1)), F32)
    def bf16_terms(x):
        terms, r = [], np.asarray(x, np.float64)
        for _ in range(3):
            tb = np.asarray(r, np.float32).astype(BF16).astype(np.float64)
            terms.append(tb)
            r = r - tb
        return terms
    coef = np.zeros((SUBLANES, NCOL), np.float64)
    coef[0:3] = np.stack(bf16_terms(LOG2E * SLC_BLOCK * slope))
    coef[3:6] = np.stack(bf16_terms(LOG2E * slope))
    c["coef"] = jnp.asarray(coef, F32)
    ntile = t_prompt // TQ
    kpos = np.arange(t_prompt).reshape(ntile, TQ)
    fk = np.zeros((ntile, TQ, LANES), np.float32)
    fk[:, :, 0:3] = (kpos // SLC_BLOCK)[:, :, None]
    fk[:, :, 3:6] = (kpos % SLC_BLOCK)[:, :, None]
    fk[:, :, 6] = (np.arange(TQ) < SLC_BLOCK)[None, :]
    fk[:, :, 7] = (np.arange(TQ) >= SLC_BLOCK)[None, :]
    c["fk"] = jnp.asarray(fk, BF16)
    rel = np.arange(TQ)[:, None] - (col % TQ)[None, :]
    c["amask"] = jnp.asarray(np.stack([np.zeros(rel.shape), np.where(rel > 0, -MASK_BIG, 0.0),
                                       np.where(rel < 0, -MASK_BIG, 0.0)]), F32)
    nh_s = (past_len + 1 + CMP_STRIDE - 1) // CMP_STRIDE
    nhp_s = -(-nh_s // LANES) * LANES
    n_slc_s = past_len // SLC_BLOCK + 1
    js_s = -(-n_slc_s // LANES) * LANES
    ov_s = overlap(nhp_s, n_slc_s, js_s)
    ov_s[nh_s - 1:, :] = 0.0
    c["ov_s"] = jnp.asarray(ov_s, BF16)
    c["u_s"] = jnp.asarray(np.triu(np.ones((js_s, js_s), np.float32), 1), BF16)
    tm = np.zeros((LANES, ATT_WIDTH), np.float32)
    for h in range(N_HEADS):
        g = h // HG
        for d in range(HEAD_DIM):
            tm[g * HEAD_DIM + d, h * HEAD_DIM + d] = 1.0
    c["tile_m"] = jnp.asarray(tm, BF16)
    return c


def _pack_w_in(w_in):
    d = w_in.shape[0]
    cuts = np.cumsum([ATT_WIDTH] + [LANES] * 6 + [3 * N_HEADS, SSM_INNER, CONV_DIM])
    q, kv, gt, z, xbc, dtc = (w_in[:, :cuts[0]], w_in[:, cuts[0]:cuts[6]], w_in[:, cuts[6]:cuts[7]],
                              w_in[:, cuts[7]:cuts[8]], w_in[:, cuts[8]:cuts[9]], w_in[:, cuts[9]:])
    pad = jnp.zeros((d, P_W - P_GD - gt.shape[1] - dtc.shape[1]), w_in.dtype)
    return jnp.concatenate([q, kv, z, xbc, gt, dtc, pad], axis=1).astype(BF16)


def _pack_compress(w1, b1, w2, head_tiled):
    zeros = jnp.zeros((CMP_STRIDE, HEAD_DIM, CMP_HID), w1.dtype)
    parts = []
    for ab in range(2):
        w = w1[ab * CMP_STRIDE:(ab + 1) * CMP_STRIDE]
        top = jnp.concatenate([w, zeros], axis=2)
        bot = jnp.concatenate([zeros, w], axis=2)
        parts.append(jnp.concatenate([top, bot], axis=1))
    w1b = jnp.concatenate(parts, axis=2)
    w1b = w1b.reshape(CMP_STRIDE // 2, 2 * LANES, 4 * CMP_HID).astype(BF16)
    b1b = jnp.concatenate([b1, b1]).reshape(1, 2 * CMP_HID)
    z2 = jnp.zeros_like(w2)
    rep = HG if head_tiled else 1
    top = jnp.concatenate([w2] * rep + [z2] * rep, axis=1)
    bot = jnp.concatenate([z2] * rep + [w2] * rep, axis=1)
    w2b = jnp.concatenate([top, bot], axis=0).astype(BF16)
    return w1b, b1b, w2b


def _lane_row(vals, offset, width):
    row = jnp.zeros((1, width), F32)
    return lax.dynamic_update_slice(row, vals.reshape(1, -1).astype(F32), (0, offset))


def _token_minor(a):
    lead = a.shape[:-3]
    n = len(lead)
    return jnp.transpose(a, tuple(range(n)) + (n + 1, n + 2, n)).reshape(lead + (N_KV * HEAD_DIM, a.shape[-3]))


def _token_major(a):
    lead = a.shape[:-2]
    n = len(lead)
    a = a.reshape(lead + (N_KV, HEAD_DIM, a.shape[-1]))
    return jnp.transpose(a, tuple(range(n)) + (n + 2, n, n + 1))


def kernel(x_prompt, x_sample, cache_cmp_k, cache_cmp_v, cache_slc_k, cache_slc_v, state_win_k, state_win_v, state_ssm, state_ssm_conv, state_ffn_conv, page_table, c_prompt, c_sample, ada_w, ada_b, norm1_g, norm2_g, w_in, cmpk_w1, cmpk_b1, cmpk_w2, cmpv_w1, cmpv_b1, cmpv_w2, ssm_conv_w, ssm_conv_b, dt_bias, a_log, d_skip, ssm_norm_g, w_out, ffn_w_up, ffn_conv_w, ffn_conv_b, ffn_w_down, final_g):
    bp, tp, d = x_prompt.shape
    ns = x_sample.shape[0]
    depth = w_in.shape[0]
    page = cache_cmp_k.shape[2]
    past_len = page_table.shape[1] * page
    assert x_sample.shape[1] == 1 and d == D_MODEL
    assert tp % SSM_CHUNK == 0 and tp >= WINDOW and past_len >= WINDOW and page % SLC_BLOCK == 0
    consts = _constants(tp, past_len)
    caches_t = [_token_minor(c) for c in (cache_cmp_k, cache_cmp_v, cache_slc_k, cache_slc_v)]
    win_t = [_token_minor(w) for w in (state_win_k, state_win_v)]

    mod = _mod(jnp.concatenate([c_prompt, c_sample], axis=0), ada_w, ada_b)
    tm_p = 512 if tp % 512 == 0 else SSM_CHUNK

    xp = x_prompt
    xs = x_sample.reshape(1, ns, d)
    outs_p, outs_s = [], []
    for l in range(depth):
        last = l == depth - 1
        mod_p = mod[l, :bp].reshape(bp, 1, 6 * d)
        mod_s = mod[l, bp:].reshape(1, ns, 6 * d)
        w_packed = _pack_w_in(w_in[l])
        cw = dict(zip(("w1k", "b1k", "w2k"), _pack_compress(cmpk_w1[l], cmpk_b1[l], cmpk_w2[l], False)))
        cw.update(zip(("w1v", "b1v", "w2v"), _pack_compress(cmpv_w1[l], cmpv_b1[l], cmpv_w2[l], False)))
        cw4 = dict(zip(("w1k", "b1k", "w2k"), _pack_compress(cmpk_w1[l], cmpk_b1[l], cmpk_w2[l], True)))
        cw4.update(zip(("w1v", "b1v", "w2v"), _pack_compress(cmpv_w1[l], cmpv_b1[l], cmpv_w2[l], True)))
        expand_heads = lambda v: jnp.repeat(v.astype(F32), HEAD_DIM).reshape(1, SSM_INNER)
        sw = dict(conv_w=ssm_conv_w[l], conv_b=ssm_conv_b[l].reshape(1, CONV_DIM),
                  dt_bias=_lane_row(dt_bias[l], GD_DT, LANES), a=_lane_row(-jnp.exp(a_log[l].astype(F32)), GD_DT, LANES),
                  d_skip=expand_heads(d_skip[l]), norm_g=ssm_norm_g[l].reshape(1, SSM_INNER).astype(F32))
        fw = dict(norm2_g=norm2_g[l].reshape(1, d), w_out=w_out[l].astype(BF16), w_up=ffn_w_up[l].astype(BF16),
                  conv_w=ffn_conv_w[l], conv_b=ffn_conv_b[l].reshape(1, D_FF), w_down=ffn_w_down[l].astype(BF16),
                  final_g=final_g.reshape(1, d))

        (qt, kc, vc, kvb, vt, z, xbc, gd, kct, vct, kst, vst, kwt, vwt) = _inproj(
            xp, mod_p, norm1_g[l], w_packed, tm_p, False, True)
        kcmp, vcmpt = _compress(kc, vc, cw)
        ocmpt, selt = _cmp_sel(qt, kcmp, vcmpt, consts)
        oatt = _attn(qt, kvb, vt, selt, ocmpt, gd, consts)
        yssm, h_p = _ssd(xbc, z, gd, sw, consts)
        xp, ug_tail = _mix_ffn(xp, oatt, yssm, mod_p, fw, tm_p, False, None, last)
        outs_p.append((kct, vct, kst, vst, kwt[:, :, tp - WINDOW:], vwt[:, :, tp - WINDOW:], h_p,
                       xbc[:, tp - (SSM_CONV - 1):], ug_tail[:, SUBLANES - (FFN_CONV - 1):]))

        q, kvb, z, xbc, gd, kc, vc, ks, vs, kw, vw = _inproj(xs, mod_s, norm1_g[l], w_packed, ns, True, False)
        per_tok = lambda a: a.reshape(ns, 1, a.shape[-1])
        kvn = per_tok(jnp.concatenate([kc, vc, ks, vs, kw, vw], axis=2))
        q, z, xbc, gd = per_tok(q), per_tok(z), per_tok(xbc), per_tok(gd)
        ocmp, idx = _s_cmp(l, page_table, q, kvn, gd, caches_t[0], caches_t[1], cw4, consts, past_len)
        oatt, wk_new, wv_new = _s_slc_win(l, idx, page_table, q, kvn, gd, ocmp, win_t[0], win_t[1],
                                          caches_t[2], caches_t[3], consts, past_len)
        yssm, h_s = _s_ssd(l, xbc, state_ssm_conv[l], z, gd, state_ssm, sw, consts)
        prev_ffn = (state_ffn_conv[l][:, 0].reshape(1, ns, D_FF), state_ffn_conv[l][:, 1].reshape(1, ns, D_FF))
        xs, ug_s = _mix_ffn(xs, oatt.reshape(1, ns, ATT_WIDTH), yssm.reshape(1, ns, SSM_INNER), mod_s, fw, ns, True,
                            prev_ffn, last)
        s4 = lambda a: a.reshape(ns, 1, N_KV, HEAD_DIM)
        outs_s.append((s4(kc), s4(vc), s4(ks), s4(vs), wk_new, wv_new, h_s,
                       jnp.concatenate([state_ssm_conv[l][:, 1:], xbc.reshape(ns, 1, CONV_DIM)], axis=1),
                       jnp.concatenate([state_ffn_conv[l][:, 1:], ug_s.reshape(ns, 1, D_FF)], axis=1)))

    stack = lambda seq, i: jnp.stack([st[i] for st in seq])
    res = [xp, xs.reshape(ns, 1, d)]
    for i in range(9):
        p_i, s_i = stack(outs_p, i), stack(outs_s, i)
        if i < 6:
            p_i = _token_major(p_i)
        if i in (4, 5):
            s_i = _token_major(s_i)
        res += [p_i, s_i]
    return tuple(res)
```

```python
import functools

import numpy as np
import jax
import jax.numpy as jnp
from jax import lax
from jax.experimental import pallas as pl
from jax.experimental.pallas import tpu as pltpu

F32 = jnp.float32
BF16 = jnp.bfloat16

D_MODEL = 1024
HEAD_DIM = 64
ATT_WIDTH = 512
N_HEADS = 8
N_KV = 2
HG = 4
CMP_BLOCK = 32
CMP_STRIDE = 16
CMP_HID = 128
SLC_BLOCK = 64
N_SEL = 16
WINDOW = 512
FORCE_BONUS = 1e6
SSM_INNER = 512
SSM_HEADS = 8
SSM_GROUPS = 2
D_STATE = 64
SSM_CONV = 4
SSM_CHUNK = 128
CONV_DIM = 768
D_FF = 2816
FFN_CONV = 3
EPS = 1e-6
NEG = -1e30
SCALE = HEAD_DIM ** -0.5
LOG2E = 1.4426950408889634
SLOPES = tuple(2.0 ** (-(h + 1)) for h in range(N_HEADS))

LANES = 128
SUBLANES = 8
VMEM_LIMIT = 56 * 1024 * 1024

FF_CHUNKS = ((0, 768), (768, 1536), (1536, 2176), (2176, 2816))
ROW_PITCH = 24
TQ = 2 * SLC_BLOCK
NCOL = N_HEADS * TQ

P_Q = 0
P_KV = 512
P_Z = 1280
P_XBC = 1792
P_GD = 2560
P_W = 2688
GD_DT = 24
MASK_BIG = 2.0 ** 100


def _cparams(sem):
    return pltpu.CompilerParams(dimension_semantics=sem, vmem_limit_bytes=VMEM_LIMIT)


def _resident(shape):
    nd = len(shape)
    return pl.BlockSpec(shape, lambda *_: (0,) * nd, pipeline_mode=pl.Buffered(1))


def _wspec(w):
    if isinstance(w, tuple):
        arr, layer = w
        nd = arr.ndim
        return pl.BlockSpec((None,) + arr.shape[1:], lambda *_: (layer,) + (0,) * (nd - 1),
                            pipeline_mode=pl.Buffered(1))
    return _resident(w.shape)


def _warr(w):
    return w[0] if isinstance(w, tuple) else w


def _split3(x):
    hi = x.astype(BF16)
    r = x - hi.astype(F32)
    mid = r.astype(BF16)
    lo = (r - mid.astype(F32)).astype(BF16)
    return hi, mid, lo


def _dot(a, b):
    return jnp.dot(a, b, preferred_element_type=F32)


def _dot_nt(a, b):
    return lax.dot_general(a, b, (((1,), (1,)), ((), ())), preferred_element_type=F32)


def _expand(x, e):
    hi, mid, lo = _split3(x)
    return _dot(hi, e) + _dot(mid, e) + _dot(lo, e)


def _expand_l(e, x):
    hi, mid, lo = _split3(x)
    return _dot(e, hi) + _dot(e, mid) + _dot(e, lo)


def _silu(x):
    return x * jax.nn.sigmoid(x)


def _softplus(x):
    return jnp.maximum(x, 0.0) + jnp.log1p(jnp.exp(-jnp.abs(x)))


def _gelu_tanh(x):
    return 0.5 * x * (1.0 + jnp.tanh(np.sqrt(2.0 / np.pi) * (x + 0.044715 * (x * x * x))))


def _mod_kernel(c_ref, w_ref, b_ref, o_ref):
    c = c_ref[...]
    s = _silu(c).astype(BF16)
    o_ref[0] = _dot(s, w_ref[0].astype(BF16)) + b_ref[0]


def _mod(c_all, ada_w, ada_b):
    depth, d, n6 = ada_w.shape
    rows = c_all.shape[0]
    tn = 512
    return pl.pallas_call(
        _mod_kernel,
        out_shape=jax.ShapeDtypeStruct((depth, rows, n6), F32),
        grid=(depth, n6 // tn),
        in_specs=[pl.BlockSpec((rows, d), lambda l, j: (0, 0)),
                  pl.BlockSpec((1, d, tn), lambda l, j: (l, 0, j)),
                  pl.BlockSpec((1, 1, tn), lambda l, j: (l, 0, j))],
        out_specs=pl.BlockSpec((1, rows, tn), lambda l, j: (l, 0, j)),
        compiler_params=_cparams(("arbitrary", "arbitrary")),
        name="adaln_mod",
    )(c_all, ada_w, ada_b.reshape(depth, 1, n6))


def _mod_spec(mod, k):
    arr, layer, row0, per_row = mod
    if per_row:
        return pl.BlockSpec((None, 1, row0, D_MODEL), lambda b, t: (layer, 0, 0, k))
    return pl.BlockSpec((None, 1, 1, D_MODEL), lambda b, t: (layer, row0 + b, 0, k))


def _inproj_kernel(x_ref, sh_ref, sc_ref, g_ref, w_ref, *outs, tm, token_minor):
    x = x_ref[0]
    ms = jnp.mean(x * x, axis=-1, keepdims=True)
    xn = x * lax.rsqrt(ms + EPS) * g_ref[...]
    h = xn * (1.0 + sc_ref[0]) + sh_ref[0]
    proj = _dot(h.astype(BF16), w_ref[...])
    kv = [proj[:, P_KV + LANES * i:P_KV + LANES * (i + 1)] for i in range(6)]
    if token_minor:
        qt_ref, kc_ref, vc_ref, kvb_ref, vt_ref, z_ref, xbc_ref, gd_ref = outs[:8]
        qt_ref[0] = (proj[:, P_Q:P_Q + ATT_WIDTH] * (SCALE * LOG2E)).T.astype(BF16)
        kc_ref[0] = kv[0]
        vc_ref[0] = kv[1]
        kvt = [a.T for a in kv]
        for ref, a in zip(outs[8:], kvt):
            ref[0] = a
        for r in range(tm // TQ):
            cols = slice(r * TQ, (r + 1) * TQ)
            vt_ref[0, r] = jnp.concatenate([kvt[3][:, cols], kvt[5][:, cols]], axis=0).astype(BF16)
    else:
        q_ref, kvb_ref, z_ref, xbc_ref, gd_ref = outs[:5]
        q_ref[0] = (proj[:, P_Q:P_Q + ATT_WIDTH] * SCALE).astype(BF16)
        for ref, a in zip(outs[5:], kv):
            ref[0] = a
    kvb_ref[0] = proj[:, P_KV:P_Z].astype(BF16)
    z_ref[0] = proj[:, P_Z:P_XBC]
    xbc_ref[0] = proj[:, P_XBC:P_GD]
    gd_ref[0] = proj[:, P_GD:P_W]


def _inproj(x, mod, norm_g, w_packed, tm, token_minor):
    bsz, t, d = x.shape
    row = lambda w: pl.BlockSpec((1, tm, w), lambda b, i: (b, i, 0))
    col = lambda w: pl.BlockSpec((1, w, tm), lambda b, i: (b, 0, i))
    shp = lambda w, dt: jax.ShapeDtypeStruct((bsz, t, w), dt)
    shpt = lambda w, dt: jax.ShapeDtypeStruct((bsz, w, t), dt)
    if token_minor:
        out_shape = ([shpt(ATT_WIDTH, BF16), shp(LANES, F32), shp(LANES, F32), shp(6 * LANES, BF16),
                      jax.ShapeDtypeStruct((bsz, t // TQ, 2 * LANES, TQ), BF16),
                      shp(SSM_INNER, F32), shp(CONV_DIM, F32), shp(LANES, F32)] + [shpt(LANES, F32)] * 6)
        out_specs = ([col(ATT_WIDTH), row(LANES), row(LANES), row(6 * LANES),
                      pl.BlockSpec((1, tm // TQ, 2 * LANES, TQ), lambda b, i: (b, i, 0, 0)),
                      row(SSM_INNER), row(CONV_DIM), row(LANES)] + [col(LANES)] * 6)
    else:
        out_shape = ([shp(ATT_WIDTH, BF16), shp(6 * LANES, BF16), shp(SSM_INNER, F32), shp(CONV_DIM, F32),
                      shp(LANES, F32)] + [shp(LANES, F32)] * 6)
        out_specs = [row(ATT_WIDTH), row(6 * LANES), row(SSM_INNER), row(CONV_DIM), row(LANES)] + [row(LANES)] * 6
    return pl.pallas_call(
        functools.partial(_inproj_kernel, tm=tm, token_minor=token_minor),
        out_shape=out_shape,
        grid=(bsz, t // tm),
        in_specs=[row(d), _mod_spec(mod, 0), _mod_spec(mod, 1), _wspec(norm_g), _wspec(w_packed)],
        out_specs=out_specs,
        compiler_params=_cparams(("arbitrary", "arbitrary")),
        name="inproj",
    )(x, mod[0], mod[0], _warr(norm_g), _warr(w_packed))


def _compress_block(src_ref, nh, w1_ref, b1_ref, w2_ref, pitch=CMP_STRIDE):
    acc = jnp.zeros((nh, 4 * CMP_HID), F32)
    for i in range(CMP_STRIDE // 2):
        xa = src_ref[pl.ds(2 * i, nh, stride=pitch), :]
        xb = src_ref[pl.ds(2 * i + 1, nh, stride=pitch), :]
        acc = acc + _dot(jnp.concatenate([xa, xb], axis=1).astype(BF16), w1_ref[i])
    pa = acc[:, :2 * CMP_HID]
    pb = pltpu.roll(acc[:, 2 * CMP_HID:], nh - 1, 0)
    hid = _gelu_tanh(pa + pb + b1_ref[...])
    return _dot(hid.astype(BF16), w2_ref[...])


def _compress_kernel(k_ref, v_ref, w1k_ref, b1k_ref, w2k_ref, w1v_ref, b1v_ref, w2v_ref, ok_ref, ovt_ref, *, nh):
    ok_ref[0] = _compress_block(k_ref.at[0], nh, w1k_ref, b1k_ref, w2k_ref).astype(BF16)
    ovt_ref[0] = _compress_block(v_ref.at[0], nh, w1v_ref, b1v_ref, w2v_ref).T.astype(BF16)


def _compress(kc, vc, cw):
    bsz, t, _ = kc.shape
    nh = t // CMP_STRIDE
    seq = pl.BlockSpec((1, t, LANES), lambda b: (b, 0, 0))
    ws = [cw[n] for n in ("w1k", "b1k", "w2k", "w1v", "b1v", "w2v")]
    return pl.pallas_call(
        functools.partial(_compress_kernel, nh=nh),
        out_shape=[jax.ShapeDtypeStruct((bsz, nh, LANES), BF16), jax.ShapeDtypeStruct((bsz, LANES, nh), BF16)],
        grid=(bsz,),
        in_specs=[seq, seq] + [_wspec(w) for w in ws],
        out_specs=[pl.BlockSpec((1, nh, LANES), lambda b: (b, 0, 0)), pl.BlockSpec((1, LANES, nh), lambda b: (b, 0, 0))],
        compiler_params=_cparams(("arbitrary",)),
        name="compress",
    )(kc, vc, *[_warr(w) for w in ws])


def _query_blockdiag(qt):
    zero = jnp.zeros((HEAD_DIM, HG * TQ), BF16)
    per_g = [jnp.concatenate([qt[HEAD_DIM * (HG * g + j):HEAD_DIM * (HG * g + j + 1), :] for j in range(HG)], axis=1)
             for g in range(N_KV)]
    return jnp.concatenate([jnp.concatenate([per_g[0], zero], axis=1),
                            jnp.concatenate([zero, per_g[1]], axis=1)], axis=0)


def _topk_mask_t(score, jrow, n_slc):
    rank = jnp.zeros(score.shape, F32)
    for jp in range(n_slc):
        sj = score[jp:jp + 1, :]
        ahead = (sj > score) | ((sj == score) & (jp < jrow))
        rank = rank + ahead.astype(F32)
    return (rank < N_SEL) & (jrow < n_slc)


def _cmp_sel_kernel(qt_ref, kc_ref, vct_ref, ot_ref, c0_ref, srow_ref, ocmpt_ref, selt_ref, *, n_cmp, n_slc):
    t = pl.program_id(1)
    qbd = _query_blockdiag(qt_ref[0])
    kc = kc_ref[0]
    nl = kc.shape[0]
    srow = srow_ref[...]
    s = _dot(kc, qbd) - (c0_ref[...] + srow * (t * TQ).astype(F32))
    nidx = lax.broadcasted_iota(jnp.int32, (nl, NCOL), 0)
    qpos = t * TQ + lax.broadcasted_iota(jnp.int32, (nl, NCOL), 1) % TQ
    mask = (qpos - (nidx * CMP_STRIDE + (CMP_BLOCK - 1)) >= 0) & (nidx < n_cmp)
    sm = jnp.where(mask, s, NEG)
    e = jnp.exp2(sm - jnp.max(sm, axis=0, keepdims=True))
    p = jnp.where(mask, e * (1.0 / jnp.sum(e, axis=0, keepdims=True)), 0.0)
    pb = p.astype(BF16)
    vct = vct_ref[0]

    js = selt_ref.shape[2]
    jrow = lax.broadcasted_iota(jnp.int32, (js, TQ), 0)
    qp = t * TQ + lax.broadcasted_iota(jnp.int32, (js, TQ), 1)
    cur = qp // SLC_BLOCK
    forced = (jrow == 0) | (jrow == cur) | (jrow == cur - 1)
    valid = (jrow * SLC_BLOCK <= qp)
    for g in range(N_KV):
        c0 = g * HG * TQ
        og = _dot(vct[HEAD_DIM * g:HEAD_DIM * (g + 1), :], pb[:, c0:c0 + HG * TQ])
        pg = jnp.zeros((nl, TQ), F32)
        for j in range(HG):
            h = HG * g + j
            ocmpt_ref[0, HEAD_DIM * h:HEAD_DIM * (h + 1), :] = og[:, j * TQ:(j + 1) * TQ]
            pg = pg + p[:, c0 + j * TQ:c0 + (j + 1) * TQ]
        imp_t = _expand_l(ot_ref[...], pg)
        score = jnp.where(valid, imp_t + jnp.where(forced, FORCE_BONUS, 0.0), NEG)
        selt_ref[0, g] = _topk_mask_t(score, jrow, n_slc).astype(F32)


def _cmp_sel(qt, kcmp, vcmpt, consts):
    bsz, _, t = qt.shape
    nl = kcmp.shape[1]
    n_slc = t // SLC_BLOCK
    js = consts["ot"].shape[0]
    return pl.pallas_call(
        functools.partial(_cmp_sel_kernel, n_cmp=t // CMP_STRIDE - 1, n_slc=n_slc),
        out_shape=[jax.ShapeDtypeStruct((bsz, ATT_WIDTH, t), F32),
                   jax.ShapeDtypeStruct((bsz, N_KV, js, t), F32)],
        grid=(bsz, t // TQ),
        in_specs=[pl.BlockSpec((1, ATT_WIDTH, TQ), lambda b, i: (b, 0, i)),
                  pl.BlockSpec((1, nl, LANES), lambda b, i: (b, 0, 0)),
                  pl.BlockSpec((1, LANES, nl), lambda b, i: (b, 0, 0)),
                  _resident(consts["ot"].shape), _resident(consts["c0"].shape), _resident(consts["srow2"].shape)],
        out_specs=[pl.BlockSpec((1, ATT_WIDTH, TQ), lambda b, i: (b, 0, i)),
                   pl.BlockSpec((1, N_KV, js, TQ), lambda b, i: (b, 0, 0, i))],
        compiler_params=_cparams(("arbitrary", "arbitrary")),
        name="cmp_select",
    )(qt, kcmp, vcmpt, consts["ot"], consts["c0"], consts["srow2"])


def _attn_kernel(qt_ref, kvb_ref, vt_ref, selt_ref, ocmpt_ref, gd_ref, fk_ref, coef_ref, amask_ref, egt_ref, o_ref,
                 pen_ref, sbuf, *state):
    nchunk = len(state) // 3
    m_refs, l_refs, acc_refs = state[:nchunk], state[nchunk:2 * nchunk], state[2 * nchunk:]
    t = pl.program_id(1)
    qbd = _query_blockdiag(qt_ref[0])
    gw = HG * TQ
    for g in range(N_KV):
        pen = (selt_ref[0, g] - 1.0) * MASK_BIG
        pen_ref[:, g * gw:(g + 1) * gw] = jnp.concatenate([pen] * HG, axis=1)
    nwin = WINDOW // TQ
    coef = coef_ref[...]
    zero8 = jnp.zeros((SUBLANES, NCOL), F32)
    cw = 2 * TQ
    zpad = jnp.zeros((LANES - 2 * SUBLANES, cw), BF16)
    rid = lax.broadcasted_iota(jnp.int32, (SUBLANES, NCOL), 0)

    def query_operands(c8):
        c16 = jnp.concatenate([c8, zero8], axis=0).astype(BF16)
        return [jnp.concatenate([qbd[:, c * cw:(c + 1) * cw], c16[:, c * cw:(c + 1) * cw], zpad], axis=0)
                for c in range(NCOL // cw)]

    q_win = query_operands(coef)

    def score_tile(kt, slot, kcol, slc, mask):
        r0 = pl.multiple_of(kt * TQ, TQ)
        k_aug = jnp.concatenate([kvb_ref[0, pl.ds(r0, TQ), kcol:kcol + LANES], fk_ref[kt]], axis=1)
        if slc:
            pa = jnp.broadcast_to(pen_ref[pl.ds(2 * kt, 1), :], (SUBLANES, NCOL))
            pb = jnp.broadcast_to(pen_ref[pl.ds(2 * kt + 1, 1), :], (SUBLANES, NCOL))
            q_aug = query_operands(jnp.where(rid == 6, pa, jnp.where(rid == 7, pb, coef)))
        else:
            q_aug = q_win
        for c in range(nchunk):
            cols = slice(c * cw, (c + 1) * cw)
            sbuf[slot, :, cols] = _dot(k_aug, q_aug[c]) + amask_ref[mask, :, cols]

    def consume_tile(kt, slot, vrow):
        v_t = vt_ref[0, kt, vrow:vrow + LANES, :]
        for c in range(nchunk):
            cols = slice(c * cw, (c + 1) * cw)
            g = (c * cw) // gw
            s = sbuf[slot, :, cols]
            m_old = m_refs[c][...]
            m_new = jnp.maximum(m_old, jnp.max(s, axis=0, keepdims=True))
            alpha = jnp.exp2(m_old - m_new)
            p = jnp.exp2(s - m_new)
            l_refs[c][...] = alpha * l_refs[c][...] + jnp.sum(p, axis=0, keepdims=True)
            m_refs[c][...] = m_new
            acc_refs[c][...] = (acc_refs[c][...] * alpha
                                + _dot(v_t[HEAD_DIM * g:HEAD_DIM * (g + 1), :], p.astype(BF16)))

    def reset():
        for c in range(nchunk):
            m_refs[c][...] = jnp.full(m_refs[c].shape, NEG, F32)
            l_refs[c][...] = jnp.zeros(l_refs[c].shape, F32)
            acc_refs[c][...] = jnp.zeros(acc_refs[c].shape, F32)

    def result():
        per_chunk = [acc_refs[c][...] * (1.0 / l_refs[c][...]) for c in range(nchunk)]
        half = nchunk // N_KV
        return [jnp.concatenate(per_chunk[g * half:(g + 1) * half], axis=1) for g in range(N_KV)]

    def branch(first, kcol, vrow, slc):
        def mask_of(kt):
            diag = jnp.where(kt == t, 1, 0)
            return diag if slc else jnp.where(kt == t - nwin, 2, diag)

        score = lambda kt, slot: score_tile(kt, slot, kcol, slc, mask_of(kt))
        reset()
        score(first, 0)
        ahead = t - first

        def body(j, c):
            kt = first + 2 * j
            score(kt + 1, 1)
            consume_tile(kt, 0, vrow)
            score(kt + 2, 0)
            consume_tile(kt + 1, 1, vrow)
            return c

        lax.fori_loop(0, ahead // 2, body, 0)

        @pl.when(ahead % 2 == 0)
        def _():
            consume_tile(t, 0, vrow)

        @pl.when(ahead % 2 == 1)
        def _():
            score(t, 1)
            consume_tile(t - 1, 0, vrow)
            consume_tile(t, 1, vrow)

        return result()

    o_slc = branch(0, 2 * LANES, 0, True)
    o_win = branch(jnp.maximum(t - nwin, 0), 4 * LANES, LANES, False)

    g_t = jax.nn.sigmoid(gd_ref[0]).T
    gates = [_expand_l(egt_ref[br], g_t) for br in range(3)]
    for pr in range(N_HEADS // 2):
        g, j0 = pr // 2, 2 * (pr % 2)
        rows = slice(LANES * pr, LANES * (pr + 1))
        pair = lambda o: jnp.concatenate([o[g][:, j0 * TQ:(j0 + 1) * TQ], o[g][:, (j0 + 1) * TQ:(j0 + 2) * TQ]], axis=0)
        tot = gates[0][rows] * ocmpt_ref[0, rows, :] + gates[1][rows] * pair(o_slc) + gates[2][rows] * pair(o_win)
        o_ref[0, :, rows] = tot.T.astype(BF16)


def _attn(qt, kvb, vt, selt, ocmpt, gd, consts):
    bsz, _, t = qt.shape
    js = selt.shape[2]
    return pl.pallas_call(
        _attn_kernel,
        out_shape=jax.ShapeDtypeStruct((bsz, t, ATT_WIDTH), BF16),
        grid=(bsz, t // TQ),
        in_specs=[pl.BlockSpec((1, ATT_WIDTH, TQ), lambda b, i: (b, 0, i)),
                  pl.BlockSpec((1, t, 6 * LANES), lambda b, i: (b, 0, 0)),
                  pl.BlockSpec((1, t // TQ, 2 * LANES, TQ), lambda b, i: (b, 0, 0, 0)),
                  pl.BlockSpec((1, N_KV, js, TQ), lambda b, i: (b, 0, 0, i)),
                  pl.BlockSpec((1, ATT_WIDTH, TQ), lambda b, i: (b, 0, i)),
                  pl.BlockSpec((1, TQ, LANES), lambda b, i: (b, i, 0)),
                  _resident(consts["fk"].shape), _resident(consts["coef"].shape), _resident(consts["amask"].shape),
                  _resident(consts["e_gate_t"].shape)],
        out_specs=pl.BlockSpec((1, TQ, ATT_WIDTH), lambda b, i: (b, i, 0)),
        scratch_shapes=([pltpu.VMEM((js, NCOL), F32), pltpu.VMEM((2, TQ, NCOL), F32)]
                        + [pltpu.VMEM((1, 2 * TQ), F32)] * (2 * N_HEADS // 2)
                        + [pltpu.VMEM((HEAD_DIM, 2 * TQ), F32)] * (N_HEADS // 2)),
        compiler_params=_cparams(("arbitrary", "arbitrary")),
        name="slc_win_attn",
    )(qt, kvb, vt, selt, ocmpt, gd, consts["fk"], consts["coef"], consts["amask"], consts["e_gate_t"])


def _gated_norm(y, z, ng):
    yz = y * _silu(z)
    half = SSM_INNER // SSM_GROUPS
    outs = []
    for g in range(SSM_GROUPS):
        part = yz[:, half * g:half * (g + 1)]
        ms = jnp.mean(part * part, axis=-1, keepdims=True)
        outs.append(part * lax.rsqrt(ms + EPS))
    return jnp.concatenate(outs, axis=1) * ng


def _ssd_kernel(xbc_ref, z_ref, gd_ref, cw_ref, cb_ref, dtb_ref, a_ref, dsk_ref, ng_ref, edt_ref,
                y_ref, hfin_ref, ubuf, sstate, ybuf, *, lc):
    t = pl.program_id(1)
    nt = pl.num_programs(1)

    @pl.when(t == 0)
    def _():
        ubuf[0:SUBLANES, :] = jnp.zeros((SUBLANES, CONV_DIM), F32)
        sstate[...] = jnp.zeros(sstate.shape, F32)

    ubuf[SUBLANES:SUBLANES + lc, :] = xbc_ref[0]
    conv = cb_ref[...] + jnp.zeros((lc, CONV_DIM), F32)
    for k in range(SSM_CONV):
        off = SUBLANES - (SSM_CONV - 1) + k
        conv = conv + ubuf[off:off + lc, :] * cw_ref[k:k + 1, :]
    ubuf[0:SUBLANES, :] = ubuf[lc:lc + SUBLANES, :]
    act = _silu(conv)
    xm = act[:, :SSM_INNER]
    bm = act[:, SSM_INNER:SSM_INNER + LANES]
    cm = act[:, SSM_INNER + LANES:]

    dt = _softplus(gd_ref[0] + dtb_ref[...])
    a = dt * a_ref[...]
    ri = lax.broadcasted_iota(jnp.int32, (lc, lc), 0)
    ci = lax.broadcasted_iota(jnp.int32, (lc, lc), 1)
    tril = ri >= ci
    cs = jnp.dot(tril.astype(F32), a, preferred_element_type=F32, precision=lax.Precision.HIGHEST)
    cs_t = cs.T
    edt = edt_ref[...]
    dt_x = _expand(dt, edt)
    cs_x = _expand(cs, edt)
    cs_last = cs_x[lc - 1:lc, :]
    bm_t = bm.T
    xd = xm * dt_x
    xw = xm * (jnp.exp(cs_last - cs_x) * dt_x)
    s_old = sstate[...]
    half = SSM_INNER // SSM_GROUPS
    for g in range(SSM_GROUPS):
        bg = bm[:, D_STATE * g:D_STATE * (g + 1)].astype(BF16)
        cg = cm[:, D_STATE * g:D_STATE * (g + 1)].astype(BF16)
        cb = _dot_nt(cg, bg)
        for j in range(SSM_HEADS // SSM_GROUPS):
            h = g * (SSM_HEADS // SSM_GROUPS) + j
            col = cs[:, GD_DT + h:GD_DT + h + 1]
            row = cs_t[GD_DT + h:GD_DT + h + 1, :]
            lm = jnp.exp(jnp.where(tril, col - row, NEG))
            lo, hi = HEAD_DIM * h, HEAD_DIM * (h + 1)
            ybuf[:, lo:hi] = _dot((cb * lm).astype(BF16), xd[:, lo:hi].astype(BF16))
        sg = s_old[:, half * g:half * (g + 1)]
        y_off = _dot(cg, sg.astype(BF16))
        ybuf[:, half * g:half * (g + 1)] = ybuf[:, half * g:half * (g + 1)] + y_off * jnp.exp(cs_x[:, half * g:half * (g + 1)])
        st = _dot(bm_t[D_STATE * g:D_STATE * (g + 1), :].astype(BF16), xw[:, half * g:half * (g + 1)].astype(BF16))
        sstate[:, half * g:half * (g + 1)] = sg * jnp.exp(cs_last[:, half * g:half * (g + 1)]) + st

    y = ybuf[...] + dsk_ref[...] * xm
    y_ref[0] = _gated_norm(y, z_ref[0], ng_ref[...]).astype(BF16)

    @pl.when(t == nt - 1)
    def _():
        s_pad = jnp.concatenate([sstate[...], jnp.zeros((LANES - D_STATE, SSM_INNER), F32)], axis=0)
        hfin_ref[0] = s_pad.T[:, :D_STATE]


def _ssd(xbc, z, gd, sw, consts):
    bsz, t, _ = xbc.shape
    lc = SSM_CHUNK
    tile = lambda w: pl.BlockSpec((1, lc, w), lambda b, i: (b, i, 0))
    ws = [sw[n] for n in ("conv_w", "conv_b", "dt_bias", "a", "d_skip", "norm_g")] + [consts["e_dt"]]
    y, hfin = pl.pallas_call(
        functools.partial(_ssd_kernel, lc=lc),
        out_shape=[jax.ShapeDtypeStruct((bsz, t, SSM_INNER), BF16),
                   jax.ShapeDtypeStruct((bsz, SSM_INNER, D_STATE), F32)],
        grid=(bsz, t // lc),
        in_specs=[tile(CONV_DIM), tile(SSM_INNER), tile(LANES)] + [_wspec(w) for w in ws],
        out_specs=[tile(SSM_INNER), pl.BlockSpec((1, SSM_INNER, D_STATE), lambda b, i: (b, 0, 0))],
        scratch_shapes=[pltpu.VMEM((lc + SUBLANES, CONV_DIM), F32),
                        pltpu.VMEM((D_STATE, SSM_INNER), F32),
                        pltpu.VMEM((lc, SSM_INNER), F32)],
        compiler_params=_cparams(("arbitrary", "arbitrary")),
        name="ssd_scan",
    )(xbc, z, gd, *[_warr(w) for w in ws])
    return y, hfin.reshape(bsz, SSM_HEADS, HEAD_DIM, D_STATE)


def _mix_ffn_kernel(*refs, tm, carry_conv, last):
    if carry_conv:
        (x_ref, oa_ref, ys_ref, g1_ref, sh2_ref, sc2_ref, g2_ref, n2_ref, wo_ref, wu_ref, fcw_ref, fcb_ref,
         wd_ref, fg_ref, out_ref, ug_ref, ubuf) = refs
    else:
        (x_ref, oa_ref, ys_ref, g1_ref, sh2_ref, sc2_ref, g2_ref, n2_ref, wo_ref, wu_ref, fcw_ref, fcb_ref,
         wd_ref, fg_ref, p0_ref, p1_ref, out_ref, ug_ref) = refs
    t = pl.program_id(1)
    nt = pl.num_programs(1)
    x = x_ref[0]
    mix = _dot(oa_ref[0], wo_ref[:ATT_WIDTH, :]) + _dot(ys_ref[0], wo_ref[ATT_WIDTH:, :])
    x1 = x + g1_ref[0] * mix
    ms = jnp.mean(x1 * x1, axis=-1, keepdims=True)
    h2 = (x1 * lax.rsqrt(ms + EPS) * n2_ref[...]) * (1.0 + sc2_ref[0]) + sh2_ref[0]
    h2 = h2.astype(BF16)
    if carry_conv:
        @pl.when(t == 0)
        def _():
            ubuf[0:SUBLANES, :] = jnp.zeros((SUBLANES, D_FF), F32)

    def up(c):
        lo, hi = FF_CHUNKS[c]
        return _dot(h2, wu_ref[:, lo:hi]), _dot(h2, wu_ref[:, D_FF + lo:D_FF + hi])

    nxt = up(0)
    down = None
    for c, (lo, hi) in enumerate(FF_CHUNKS):
        ug, uv = nxt
        if c + 1 < len(FF_CHUNKS):
            nxt = up(c + 1)
        if carry_conv:
            ubuf[SUBLANES:SUBLANES + tm, lo:hi] = ug
            u1 = ubuf[SUBLANES - 1:SUBLANES - 1 + tm, lo:hi]
            u2 = ubuf[SUBLANES - 2:SUBLANES - 2 + tm, lo:hi]
        else:
            u1 = p1_ref[0, :, lo:hi]
            u2 = p0_ref[0, :, lo:hi]
            ug_ref[0, :, lo:hi] = ug
        ugc = (fcb_ref[:, lo:hi] + u2 * fcw_ref[0:1, lo:hi] + u1 * fcw_ref[1:2, lo:hi] + ug * fcw_ref[2:3, lo:hi])
        d = _dot((_silu(ugc) * uv).astype(BF16), wd_ref[lo:hi, :])
        down = d if down is None else down + d
    if carry_conv:
        @pl.when(t == nt - 1)
        def _():
            ug_ref[0] = ubuf[tm:tm + SUBLANES, :]

        ubuf[0:SUBLANES, :] = ubuf[tm:tm + SUBLANES, :]
    x2 = x1 + g2_ref[0] * down
    if last:
        ms2 = jnp.mean(x2 * x2, axis=-1, keepdims=True)
        out_ref[0] = x2 * lax.rsqrt(ms2 + EPS) * fg_ref[...]
    else:
        out_ref[0] = x2


def _mix_ffn(x, oatt, yssm, mod, fw, tm, prev, last):
    bsz, t, d = x.shape
    carry = prev is None
    row = lambda w: pl.BlockSpec((1, tm, w), lambda b, i: (b, i, 0))
    ws = [fw[n] for n in ("norm2_g", "w_out", "w_up", "conv_w", "conv_b", "w_down", "final_g")]
    in_specs = ([row(d), row(ATT_WIDTH), row(SSM_INNER)] + [_mod_spec(mod, k) for k in (2, 3, 4, 5)]
                + [_wspec(w) for w in ws])
    args = [x, oatt, yssm] + [mod[0]] * 4 + [_warr(w) for w in ws]
    if carry:
        ug_shape, ug_spec = (bsz, SUBLANES, D_FF), pl.BlockSpec((1, SUBLANES, D_FF), lambda b, i: (b, 0, 0))
        scratch = [pltpu.VMEM((tm + SUBLANES, D_FF), F32)]
    else:
        ug_shape, ug_spec = (bsz, t, D_FF), row(D_FF)
        scratch = []
        in_specs += [row(D_FF), row(D_FF)]
        args += list(prev)
    return pl.pallas_call(
        functools.partial(_mix_ffn_kernel, tm=tm, carry_conv=carry, last=last),
        out_shape=[jax.ShapeDtypeStruct((bsz, t, d), F32), jax.ShapeDtypeStruct(ug_shape, F32)],
        grid=(bsz, t // tm),
        in_specs=in_specs,
        out_specs=[row(d), ug_spec],
        scratch_shapes=scratch,
        compiler_params=_cparams(("arbitrary", "arbitrary")),
        name="mix_ffn",
    )(*args)


def _head_rows(row512):
    r = lax.broadcasted_iota(jnp.int32, (N_HEADS, ATT_WIDTH), 0)
    c = lax.broadcasted_iota(jnp.int32, (N_HEADS, ATT_WIDTH), 1)
    return jnp.where(c // HEAD_DIM == r, jnp.broadcast_to(row512, (N_HEADS, ATT_WIDTH)), 0.0)


def _head_diag(x8):
    r = lax.broadcasted_iota(jnp.int32, (N_HEADS, ATT_WIDTH), 0)
    c = lax.broadcasted_iota(jnp.int32, (N_HEADS, ATT_WIDTH), 1)
    return jnp.sum(jnp.where(c // HEAD_DIM == r, x8, 0.0), axis=0, keepdims=True)


def _slope_col():
    r = lax.broadcasted_iota(jnp.int32, (N_HEADS, 1), 0)
    s = jnp.zeros((N_HEADS, 1), F32)
    for h in range(N_HEADS):
        s = jnp.where(r == h, SLOPES[h], s)
    return s


def _s_cmp_kernel(pt_ref, q_ref, kvn_ref, gd_ref, ck_hbm, cv_hbm,
                  w1k_ref, b1k_ref, w2k_ref, w1v_ref, b1v_ref, w2v_ref, ov_ref, u_ref, e0_ref,
                  ocmp_ref, idx_ref, stage_k, stage_v, kbuf, vbuf, sem,
                  *, layer, n_pages, page, nh8, nhp, q_pos, n_slc):
    b = pl.program_id(0)
    nb = pl.num_programs(0)
    past = n_pages * page

    def copies(bb, p, slot):
        pg = pt_ref[bb * n_pages + p]
        return (pltpu.make_async_copy(ck_hbm.at[layer, pg], stage_k.at[slot, p], sem.at[0, slot]),
                pltpu.make_async_copy(cv_hbm.at[layer, pg], stage_v.at[slot, p], sem.at[1, slot]))

    def issue(bb, slot):
        def body(p, c):
            for cp in copies(bb, p, slot):
                cp.start()
            return c
        lax.fori_loop(0, n_pages, body, 0)

    @pl.when(b == 0)
    def _():
        issue(0, 0)

    @pl.when(b + 1 < nb)
    def _():
        issue(b + 1, (b + 1) % 2)

    slot = b % 2

    def wait(p, c):
        for cp in copies(b, p, slot):
            cp.wait()
        return c

    lax.fori_loop(0, n_pages, wait, 0)

    unroll = 8 if n_pages % 8 == 0 else 1

    hpp = page // CMP_STRIDE

    def untranspose(i, c):
        for u in range(unroll):
            p = i * unroll + u
            base = pl.multiple_of(p * (hpp * ROW_PITCH), SUBLANES)
            kp = stage_k[slot, p].T
            vp = stage_v[slot, p].T
            for n in range(hpp):
                rows = pl.ds(base + n * ROW_PITCH, CMP_STRIDE)
                kbuf[rows, :] = kp[n * CMP_STRIDE:(n + 1) * CMP_STRIDE]
                vbuf[rows, :] = vp[n * CMP_STRIDE:(n + 1) * CMP_STRIDE]
        return c

    lax.fori_loop(0, n_pages // unroll, untranspose, 0)
    first_new = (past // CMP_STRIDE) * ROW_PITCH
    tail = nh8 * ROW_PITCH - first_new
    kbuf[first_new:, :] = jnp.zeros((tail, LANES), F32)
    vbuf[first_new:, :] = jnp.zeros((tail, LANES), F32)
    kbuf[first_new:first_new + 1, :] = kvn_ref[0, :, 0:LANES]
    vbuf[first_new:first_new + 1, :] = kvn_ref[0, :, LANES:2 * LANES]

    pad = jnp.zeros((nhp - nh8, ATT_WIDTH), F32)
    kc4 = jnp.concatenate([_compress_block(kbuf, nh8, w1k_ref, b1k_ref, w2k_ref, ROW_PITCH), pad], axis=0).astype(BF16)
    vc4 = jnp.concatenate([_compress_block(vbuf, nh8, w1v_ref, b1v_ref, w2v_ref, ROW_PITCH), pad], axis=0).astype(BF16)
    n_cmp = (past + 1 + CMP_STRIDE - 1) // CMP_STRIDE - 1

    qm = _head_rows(q_ref[0].astype(F32)).astype(BF16)
    nidx = lax.broadcasted_iota(jnp.int32, (N_HEADS, nhp), 1)
    dist_i = q_pos - (nidx * CMP_STRIDE + (CMP_BLOCK - 1))
    mask = (dist_i >= 0) & (nidx < n_cmp)
    s = _dot_nt(qm, kc4) - _slope_col() * dist_i.astype(F32)
    sm = jnp.where(mask, s, NEG)
    e = jnp.exp(sm - jnp.max(sm, axis=-1, keepdims=True))
    p = jnp.where(mask, e / jnp.sum(e, axis=-1, keepdims=True), 0.0)
    o = _head_diag(_dot(p.astype(BF16), vc4))
    g8 = jnp.broadcast_to(jax.nn.sigmoid(gd_ref[0]), (SUBLANES, LANES))
    gate = _expand(g8, e0_ref[...])[0:1, :]
    ocmp_ref[0] = gate * o

    js = ov_ref.shape[1]
    imp8 = _expand(p, ov_ref[...])
    hrow = lax.broadcasted_iota(jnp.int32, (N_HEADS, js), 0)
    jl = lax.broadcasted_iota(jnp.int32, (1, js), 1)
    cur = q_pos // SLC_BLOCK
    forced = (jl == 0) | (jl == cur) | (jl == cur - 1)
    valid = (jl * SLC_BLOCK <= q_pos) & (jl < n_slc)
    rj = lax.broadcasted_iota(jnp.int32, (js, js), 0)
    cj = lax.broadcasted_iota(jnp.int32, (js, js), 1)
    kk = lax.broadcasted_iota(jnp.int32, (2 * SUBLANES, js), 0)
    jvals = jnp.broadcast_to(jl.astype(F32), (SUBLANES, js)).astype(BF16)
    for g in range(N_KV):
        imp = jnp.sum(jnp.where(hrow // HG == g, imp8, 0.0), axis=0, keepdims=True)
        score = jnp.where(valid, imp + jnp.where(forced, FORCE_BONUS, 0.0), NEG)
        sb = jnp.broadcast_to(score, (js, js))
        col = jnp.sum(jnp.where(rj == cj, sb, 0.0), axis=1, keepdims=True)
        ahead = (col > sb) | ((col == sb) & (rj < cj))
        rank = jnp.sum(ahead.astype(F32), axis=0, keepdims=True)
        sel = ((rank < N_SEL) & (jl < n_slc)).astype(F32)
        pos = _dot(jnp.broadcast_to(sel, (SUBLANES, js)).astype(BF16), u_ref[...])[0:1, :]
        onehot = ((jnp.broadcast_to(pos, (2 * SUBLANES, js)) == kk.astype(F32))
                  & (jnp.broadcast_to(sel, (2 * SUBLANES, js)) > 0.5)).astype(BF16)
        idx = _dot_nt(jvals, onehot)[0:1, :]
        idx_ref[0, g:g + 1, :] = idx.astype(jnp.int32)


def _s_cmp(layer, page_table, q, kvn, gd, cache_k, cache_v, cw4, consts, q_pos):
    ns, n_pages = page_table.shape
    page = cache_k.shape[3]
    past = n_pages * page
    nh = (past + 1 + CMP_STRIDE - 1) // CMP_STRIDE
    nh8 = -(-nh // SUBLANES) * SUBLANES
    nhp = -(-nh // LANES) * LANES
    n_slc = past // SLC_BLOCK + 1
    ws = [cw4[n] for n in ("w1k", "b1k", "w2k", "w1v", "b1v", "w2v")] + [consts["ov_s"], consts["u_s"], consts["e_gate"][0]]
    full = lambda a: pl.BlockSpec(a.shape, lambda b, pt: (0,) * a.ndim)
    per_b = lambda a: pl.BlockSpec((1,) + a.shape[1:], lambda b, pt: (b,) + (0,) * (a.ndim - 1))
    grid_spec = pltpu.PrefetchScalarGridSpec(
        num_scalar_prefetch=1,
        grid=(ns,),
        in_specs=[per_b(q), per_b(kvn), per_b(gd), pl.BlockSpec(memory_space=pl.ANY), pl.BlockSpec(memory_space=pl.ANY)]
                 + [_wspec(w) for w in ws],
        out_specs=[pl.BlockSpec((1, 1, ATT_WIDTH), lambda b, pt: (b, 0, 0)),
                   pl.BlockSpec((1, N_KV, N_SEL), lambda b, pt: (b, 0, 0))],
        scratch_shapes=[pltpu.VMEM((2, n_pages, LANES, page), F32), pltpu.VMEM((2, n_pages, LANES, page), F32),
                        pltpu.VMEM((nh8 * ROW_PITCH, LANES), F32), pltpu.VMEM((nh8 * ROW_PITCH, LANES), F32),
                        pltpu.SemaphoreType.DMA((2, 2))],
    )
    return pl.pallas_call(
        functools.partial(_s_cmp_kernel, layer=layer, n_pages=n_pages, page=page, nh8=nh8, nhp=nhp, q_pos=q_pos,
                          n_slc=n_slc),
        out_shape=[jax.ShapeDtypeStruct((ns, 1, ATT_WIDTH), F32), jax.ShapeDtypeStruct((ns, N_KV, N_SEL), jnp.int32)],
        grid_spec=grid_spec,
        compiler_params=_cparams(("arbitrary",)),
        name="sample_cmp_select",
    )(page_table.reshape(-1), q, kvn, gd, cache_k, cache_v, *[_warr(w) for w in ws])


def _s_slc_win_kernel(idx_ref, pt_ref, q_ref, kvn_ref, gd_ref, ocmp_ref, wk_ref, wv_ref, sk_hbm, sv_hbm,
                      tile_ref, e1_ref, e2_ref, o_ref, wko_ref, wvo_ref, kb, vb, sem,
                      *, layer, n_pages, page, q_pos, wb):
    b = pl.program_id(0)
    nb = pl.num_programs(0)
    bpp = page // SLC_BLOCK
    n_past = n_pages * bpp
    nblk = N_KV * N_SEL
    blks = [idx_ref[b * nblk + i] for i in range(nblk)]
    slot = b % 2

    def copies(bb, sl, i):
        blk = idx_ref[bb * nblk + i]
        pg = pt_ref[bb * n_pages + jnp.minimum(blk, n_past - 1) // bpp]
        dst = slice(i * page, (i + 1) * page)
        return (pltpu.make_async_copy(sk_hbm.at[layer, pg], kb.at[sl, :, dst], sem.at[0, sl]),
                pltpu.make_async_copy(sv_hbm.at[layer, pg], vb.at[sl, :, dst], sem.at[1, sl]))

    def issue(bb, sl):
        for i in range(nblk):
            for cp in copies(bb, sl, i):
                cp.start()

    @pl.when(b == 0)
    def _():
        issue(0, 0)

    @pl.when(b + 1 < nb)
    def _():
        issue(b + 1, (b + 1) % 2)

    tile_m = tile_ref[...]
    q8 = _dot_nt(_head_rows(q_ref[0].astype(F32)).astype(BF16), tile_m).astype(BF16)
    slope = _slope_col()
    g8 = jnp.broadcast_to(jax.nn.sigmoid(gd_ref[0]), (SUBLANES, LANES))
    gate1 = _expand(g8, e1_ref[...])[0:1, :]
    gate2 = _expand(g8, e2_ref[...])[0:1, :]
    rgrp = lax.broadcasted_iota(jnp.int32, (N_HEADS, 1), 0) // HG

    def new_key(lane0):
        kn = kvn_ref[0, :, lane0:lane0 + LANES].astype(BF16).astype(F32)
        vn = kvn_ref[0, :, lane0 + LANES:lane0 + 2 * LANES].astype(BF16).astype(F32)
        return jnp.sum(q8.astype(F32) * kn, axis=-1, keepdims=True), vn

    def finish(e, e_new, v_t, v_new, l):
        o8 = (_dot_nt(e.astype(BF16), v_t.astype(BF16)) + e_new * v_new) / l
        return _head_diag(_expand(o8, tile_m))

    def shifted(w_t, new_row):
        col = jnp.broadcast_to(new_row, (LANES, LANES)).T[:, 0:1]
        lane = lax.broadcasted_iota(jnp.int32, (LANES, wb), 1)
        return jnp.where(lane == wb - 1, col, pltpu.roll(w_t, wb - 1, 1))

    wk_t = wk_ref[0, 0]
    wv_t = wv_ref[0, 0]
    kpos = (q_pos - wb) + lax.broadcasted_iota(jnp.int32, (N_HEADS, wb), 1)
    dist = q_pos - kpos
    wmask = (dist >= 0) & (dist <= WINDOW)
    s = jnp.where(wmask, _dot(q8, wk_t.astype(BF16)) - slope * dist.astype(F32), NEG)
    s_new, v_new = new_key(4 * LANES)
    m = jnp.maximum(jnp.max(s, axis=-1, keepdims=True), s_new)
    e = jnp.where(wmask, jnp.exp(s - m), 0.0)
    e_new = jnp.exp(s_new - m)
    o_win = finish(e, e_new, wv_t, v_new, jnp.sum(e, axis=-1, keepdims=True) + e_new)
    wko_ref[0] = shifted(wk_t, kvn_ref[0, :, 4 * LANES:5 * LANES])
    wvo_ref[0] = shifted(wv_t, kvn_ref[0, :, 5 * LANES:6 * LANES])

    for i in range(nblk):
        for cp in copies(b, slot, i):
            cp.wait()
    lane = lax.broadcasted_iota(jnp.int32, (1, page), 1)
    kpos_t, ok_t = [], []
    has_new = [jnp.zeros((1, 1), jnp.int32) for _ in range(N_KV)]
    for i in range(nblk):
        blk = blks[i]
        kpos_t.append(blk * SLC_BLOCK + lane % SLC_BLOCK)
        ok_t.append(((lane // SLC_BLOCK) == blk % bpp) & (blk < n_past))
        has_new[i // N_SEL] = jnp.maximum(has_new[i // N_SEL], (blk >= n_past).astype(jnp.int32))
    kpos = jnp.concatenate(kpos_t, axis=1)
    ok = jnp.concatenate(ok_t, axis=1)
    pgrp = lax.broadcasted_iota(jnp.int32, (1, nblk * page), 1) // (N_SEL * page)
    dist = q_pos - kpos
    kmask = ok & (dist >= 0) & (rgrp == pgrp)
    s = jnp.where(kmask, _dot(q8, kb[slot].astype(BF16)) - slope * dist.astype(F32), NEG)
    s_new, v_new = new_key(2 * LANES)
    new_on = jnp.where(rgrp == 0, has_new[0], has_new[1]) > 0
    sn = jnp.where(new_on, s_new, NEG)
    m = jnp.maximum(jnp.max(s, axis=-1, keepdims=True), sn)
    e = jnp.where(kmask, jnp.exp(s - m), 0.0)
    e_new = jnp.where(new_on, jnp.exp(sn - m), 0.0)
    o_slc = finish(e, e_new, vb[slot], v_new, jnp.sum(e, axis=-1, keepdims=True) + e_new)

    o_ref[0] = (ocmp_ref[0] + gate1 * o_slc + gate2 * o_win).astype(BF16)


def _s_slc_win(layer, idx, page_table, q, kvn, gd, ocmp, win_k, win_v, slc_k, slc_v, consts, q_pos):
    ns, n_pages = page_table.shape
    page = slc_k.shape[3]
    wb = win_k.shape[3]
    full = lambda a: pl.BlockSpec(a.shape, lambda b, i, pt: (0,) * a.ndim)
    per_b = lambda a: pl.BlockSpec((1,) + a.shape[1:], lambda b, i, pt: (b,) + (0,) * (a.ndim - 1))
    win = pl.BlockSpec((1, 1, LANES, wb), lambda b, i, pt: (layer, b, 0, 0))
    ws = [consts["tile_m"], consts["e_gate"][1], consts["e_gate"][2]]
    nlane = N_KV * N_SEL * page
    grid_spec = pltpu.PrefetchScalarGridSpec(
        num_scalar_prefetch=2,
        grid=(ns,),
        in_specs=[per_b(q), per_b(kvn), per_b(gd), per_b(ocmp), win, win,
                  pl.BlockSpec(memory_space=pl.ANY), pl.BlockSpec(memory_space=pl.ANY)] + [full(w) for w in ws],
        out_specs=[pl.BlockSpec((1, 1, ATT_WIDTH), lambda b, i, pt: (b, 0, 0)),
                   pl.BlockSpec((1, LANES, wb), lambda b, i, pt: (b, 0, 0)),
                   pl.BlockSpec((1, LANES, wb), lambda b, i, pt: (b, 0, 0))],
        scratch_shapes=[pltpu.VMEM((2, LANES, nlane), F32), pltpu.VMEM((2, LANES, nlane), F32),
                        pltpu.SemaphoreType.DMA((2, 2))],
    )
    return pl.pallas_call(
        functools.partial(_s_slc_win_kernel, layer=layer, n_pages=n_pages, page=page, q_pos=q_pos, wb=wb),
        out_shape=[jax.ShapeDtypeStruct((ns, 1, ATT_WIDTH), BF16),
                   jax.ShapeDtypeStruct((ns, LANES, wb), F32), jax.ShapeDtypeStruct((ns, LANES, wb), F32)],
        grid_spec=grid_spec,
        compiler_params=_cparams(("arbitrary",)),
        name="sample_slc_win_attn",
    )(idx.reshape(-1), page_table.reshape(-1), q, kvn, gd, ocmp, win_k, win_v, slc_k, slc_v, *ws)


def _col_bcast(row):
    blocks = [jnp.broadcast_to(row[:, LANES * i:LANES * (i + 1)], (LANES, LANES)).T for i in range(row.shape[1] // LANES)]
    return jnp.concatenate(blocks, axis=0)


def _s_ssd_kernel(xbc_ref, prev_ref, z_ref, gd_ref, h0_ref, cw_ref, cb_ref, dtb_ref, a_ref, dsk_ref, ng_ref,
                  edt_ref, y_ref, h_ref):
    conv = cb_ref[...] + xbc_ref[0] * cw_ref[SSM_CONV - 1:SSM_CONV, :]
    for k in range(SSM_CONV - 1):
        conv = conv + prev_ref[0, k:k + 1, :] * cw_ref[k:k + 1, :]
    act = _silu(conv)
    xm = act[:, :SSM_INNER]
    bm = act[:, SSM_INNER:SSM_INNER + LANES]
    cm = act[:, SSM_INNER + LANES:]
    dt = _softplus(gd_ref[0] + dtb_ref[...])
    a = dt * a_ref[...]
    edt = edt_ref[...]
    dt_x = _expand(jnp.broadcast_to(dt, (SUBLANES, LANES)), edt)[0:1, :]
    dec_x = jnp.exp(_expand(jnp.broadcast_to(a, (SUBLANES, LANES)), edt)[0:1, :])
    dtx = dt_x * xm
    h0 = h0_ref[0, 0].reshape(SSM_INNER, D_STATE)
    half = SSM_INNER // SSM_GROUPS
    rsel = lax.broadcasted_iota(jnp.int32, (SSM_INNER, 1), 0) // half
    lsel = lax.broadcasted_iota(jnp.int32, (1, SSM_INNER), 1) // half
    y_off = jnp.zeros((1, SSM_INNER), F32)
    cbx = jnp.zeros((1, SSM_INNER), F32)
    brow = jnp.zeros((SSM_INNER, D_STATE), F32)
    for g in range(SSM_GROUPS):
        bg = bm[:, D_STATE * g:D_STATE * (g + 1)]
        cg = cm[:, D_STATE * g:D_STATE * (g + 1)]
        c8 = jnp.broadcast_to(cg, (SUBLANES, D_STATE)).astype(BF16)
        yo = _dot_nt(c8, h0[half * g:half * (g + 1), :].astype(BF16))[0:1, :]
        y_off = jnp.where(lsel == g, jnp.concatenate([yo] * SSM_GROUPS, axis=1), y_off)
        cbx = jnp.where(lsel == g, jnp.sum(cg * bg, axis=-1, keepdims=True), cbx)
        brow = jnp.where(rsel == g, jnp.broadcast_to(bg, (SSM_INNER, D_STATE)), brow)
    y = y_off * dec_x + cbx * dtx + dsk_ref[...] * xm
    y_ref[0] = _gated_norm(y, z_ref[0], ng_ref[...]).astype(BF16)
    h_new = h0 * _col_bcast(dec_x)[:, :D_STATE] + _col_bcast(dtx)[:, :D_STATE] * brow
    h_ref[0] = h_new.reshape(SSM_HEADS, HEAD_DIM, D_STATE)


def _s_ssd(layer, xbc, prev, z, gd, state_ssm, sw, consts):
    ns = state_ssm.shape[1]
    full = lambda a: pl.BlockSpec(a.shape, lambda b: (0,) * a.ndim)
    per_b = lambda a: pl.BlockSpec((1,) + a.shape[1:], lambda b: (b,) + (0,) * (a.ndim - 1))
    hshape = state_ssm.shape[2:]
    ws = [sw[n] for n in ("conv_w", "conv_b", "dt_bias", "a", "d_skip", "norm_g")] + [consts["e_dt"]]
    return pl.pallas_call(
        _s_ssd_kernel,
        out_shape=[jax.ShapeDtypeStruct((ns, 1, SSM_INNER), BF16), jax.ShapeDtypeStruct((ns,) + hshape, F32)],
        grid=(ns,),
        in_specs=[per_b(xbc), per_b(prev), per_b(z), per_b(gd),
                  pl.BlockSpec((1, 1) + hshape, lambda b: (layer, b, 0, 0, 0))] + [_wspec(w) for w in ws],
        out_specs=[pl.BlockSpec((1, 1, SSM_INNER), lambda b: (b, 0, 0)),
                   pl.BlockSpec((1,) + hshape, lambda b: (b, 0, 0, 0))],
        compiler_params=_cparams(("arbitrary",)),
        name="sample_ssd_step",
    )(xbc, prev, z, gd, state_ssm, *[_warr(w) for w in ws])


def _constants(t_prompt, past_len):
    c = {}
    eg = np.zeros((3, LANES, ATT_WIDTH), np.float32)
    for br in range(3):
        for h in range(N_HEADS):
            eg[br, br * N_HEADS + h, h * HEAD_DIM:(h + 1) * HEAD_DIM] = 1.0
    c["e_gate"] = [jnp.asarray(eg[i], BF16) for i in range(3)]
    c["e_gate_t"] = jnp.asarray(eg.transpose(0, 2, 1), BF16)
    ed = np.zeros((LANES, SSM_INNER), np.float32)
    for h in range(SSM_HEADS):
        ed[GD_DT + h, h * HEAD_DIM:(h + 1) * HEAD_DIM] = 1.0
    c["e_dt"] = jnp.asarray(ed, BF16)

    def overlap(n_cmp_pad, n_slc, n_slc_pad):
        cs = np.arange(n_cmp_pad)[:, None] * CMP_STRIDE
        ss = np.arange(n_slc_pad)[None, :] * SLC_BLOCK
        ov = ((cs < ss + SLC_BLOCK) & (cs + CMP_BLOCK > ss) & (np.arange(n_slc_pad)[None, :] < n_slc))
        return ov.astype(np.float32)
    nh_p = t_prompt // CMP_STRIDE
    n_slc_p = t_prompt // SLC_BLOCK
    js_p = -(-n_slc_p // SUBLANES) * SUBLANES
    ov = overlap(nh_p, n_slc_p, js_p)
    ov[nh_p - 1:, :] = 0.0
    c["ot"] = jnp.asarray(ov.T, BF16)
    col = np.arange(NCOL)
    slope = np.asarray(SLOPES, np.float64)[col // TQ]
    c["srow2"] = jnp.asarray(LOG2E * slope[None, :], F32)
    c["c0"] = jnp.asarray(LOG2E * slope[None, :] * ((col % TQ)[None, :]
                                                    - (np.arange(nh_p)[:, None] * CMP_STRIDE + CMP_BLOCK - 1)), F32)
    def bf16_terms(x):
        terms, r = [], np.asarray(x, np.float64)
        for _ in range(3):
            tb = np.asarray(r, np.float32).astype(BF16).astype(np.float64)
            terms.append(tb)
            r = r - tb
        return terms
    coef = np.zeros((SUBLANES, NCOL), np.float64)
    coef[0:3] = np.stack(bf16_terms(LOG2E * SLC_BLOCK * slope))
    coef[3:6] = np.stack(bf16_terms(LOG2E * slope))
    c["coef"] = jnp.asarray(coef, F32)
    ntile = t_prompt // TQ
    kpos = np.arange(t_prompt).reshape(ntile, TQ)
    fk = np.zeros((ntile, TQ, LANES), np.float32)
    fk[:, :, 0:3] = (kpos // SLC_BLOCK)[:, :, None]
    fk[:, :, 3:6] = (kpos % SLC_BLOCK)[:, :, None]
    fk[:, :, 6] = (np.arange(TQ) < SLC_BLOCK)[None, :]
    fk[:, :, 7] = (np.arange(TQ) >= SLC_BLOCK)[None, :]
    c["fk"] = jnp.asarray(fk, BF16)
    rel = np.arange(TQ)[:, None] - (col % TQ)[None, :]
    c["amask"] = jnp.asarray(np.stack([np.zeros(rel.shape), np.where(rel > 0, -MASK_BIG, 0.0),
                                       np.where(rel < 0, -MASK_BIG, 0.0)]), F32)
    nh_s = (past_len + 1 + CMP_STRIDE - 1) // CMP_STRIDE
    nhp_s = -(-nh_s // LANES) * LANES
    n_slc_s = past_len // SLC_BLOCK + 1
    js_s = -(-n_slc_s // LANES) * LANES
    ov_s = overlap(nhp_s, n_slc_s, js_s)
    ov_s[nh_s - 1:, :] = 0.0
    c["ov_s"] = jnp.asarray(ov_s, BF16)
    c["u_s"] = jnp.asarray(np.triu(np.ones((js_s, js_s), np.float32), 1), BF16)
    tm = np.zeros((LANES, ATT_WIDTH), np.float32)
    for h in range(N_HEADS):
        g = h // HG
        for d in range(HEAD_DIM):
            tm[g * HEAD_DIM + d, h * HEAD_DIM + d] = 1.0
    c["tile_m"] = jnp.asarray(tm, BF16)
    return c


def _pack_w_in(w_in):
    depth, d, _ = w_in.shape
    cuts = np.cumsum([ATT_WIDTH] + [LANES] * 6 + [3 * N_HEADS, SSM_INNER, CONV_DIM])
    q, kv, gt, z, xbc, dtc = (w_in[..., :cuts[0]], w_in[..., cuts[0]:cuts[6]], w_in[..., cuts[6]:cuts[7]],
                              w_in[..., cuts[7]:cuts[8]], w_in[..., cuts[8]:cuts[9]], w_in[..., cuts[9]:])
    pad = jnp.zeros((depth, d, P_W - P_GD - gt.shape[-1] - dtc.shape[-1]), w_in.dtype)
    return jnp.concatenate([q, kv, z, xbc, gt, dtc, pad], axis=-1).astype(BF16)


def _pack_compress(w1, b1, w2):
    depth = w1.shape[0]
    zeros = jnp.zeros((depth, CMP_STRIDE, HEAD_DIM, CMP_HID), w1.dtype)
    parts = []
    for ab in range(2):
        w = w1[:, ab * CMP_STRIDE:(ab + 1) * CMP_STRIDE]
        top = jnp.concatenate([w, zeros], axis=3)
        bot = jnp.concatenate([zeros, w], axis=3)
        parts.append(jnp.concatenate([top, bot], axis=2))
    w1b = jnp.concatenate(parts, axis=3)
    w1b = w1b.reshape(depth, CMP_STRIDE // 2, 2 * LANES, 4 * CMP_HID).astype(BF16)
    b1b = jnp.concatenate([b1, b1], axis=1).reshape(depth, 1, 2 * CMP_HID)
    z2 = jnp.zeros_like(w2)
    def blockdiag(rep):
        top = jnp.concatenate([w2] * rep + [z2] * rep, axis=2)
        bot = jnp.concatenate([z2] * rep + [w2] * rep, axis=2)
        return jnp.concatenate([top, bot], axis=1).astype(BF16)
    return w1b, b1b, blockdiag(1), blockdiag(HG)


def _lane_rows(vals, offset, width):
    depth, n = vals.shape
    return jnp.pad(vals.astype(F32), ((0, 0), (offset, width - offset - n))).reshape(depth, 1, width)


def _token_minor(a):
    lead = a.shape[:-3]
    n = len(lead)
    return jnp.transpose(a, tuple(range(n)) + (n + 1, n + 2, n)).reshape(lead + (N_KV * HEAD_DIM, a.shape[-3]))


def _token_major(a):
    lead = a.shape[:-2]
    n = len(lead)
    a = a.reshape(lead + (N_KV, HEAD_DIM, a.shape[-1]))
    return jnp.transpose(a, tuple(range(n)) + (n + 2, n, n + 1))


def kernel(x_prompt, x_sample, cache_cmp_k, cache_cmp_v, cache_slc_k, cache_slc_v, state_win_k, state_win_v, state_ssm, state_ssm_conv, state_ffn_conv, page_table, c_prompt, c_sample, ada_w, ada_b, norm1_g, norm2_g, w_in, cmpk_w1, cmpk_b1, cmpk_w2, cmpv_w1, cmpv_b1, cmpv_w2, ssm_conv_w, ssm_conv_b, dt_bias, a_log, d_skip, ssm_norm_g, w_out, ffn_w_up, ffn_conv_w, ffn_conv_b, ffn_w_down, final_g):
    bp, tp, d = x_prompt.shape
    ns = x_sample.shape[0]
    depth = w_in.shape[0]
    page = cache_cmp_k.shape[2]
    past_len = page_table.shape[1] * page
    assert x_sample.shape[1] == 1 and d == D_MODEL
    assert tp % SSM_CHUNK == 0 and tp >= WINDOW and past_len >= WINDOW and page % SLC_BLOCK == 0
    consts = _constants(tp, past_len)
    caches_t = [_token_minor(c) for c in (cache_cmp_k, cache_cmp_v, cache_slc_k, cache_slc_v)]
    win_t = [_token_minor(w) for w in (state_win_k, state_win_v)]

    mod = _mod(jnp.concatenate([c_sample, c_prompt], axis=0), ada_w, ada_b)
    mod_rows_p = mod.reshape(depth, ns + bp, 1, 6 * d)
    mod_rows_s = mod.reshape(depth, 1, ns + bp, 6 * d)
    tm_p = 512 if tp % 512 == 0 else SSM_CHUNK

    w_packed = _pack_w_in(w_in)
    w1k, b1k, w2k, w2k4 = _pack_compress(cmpk_w1, cmpk_b1, cmpk_w2)
    w1v, b1v, w2v, w2v4 = _pack_compress(cmpv_w1, cmpv_b1, cmpv_w2)
    norm1 = norm1_g.reshape(depth, 1, d)
    sw_all = dict(conv_w=ssm_conv_w, conv_b=ssm_conv_b.reshape(depth, 1, CONV_DIM),
                  dt_bias=_lane_rows(dt_bias, GD_DT, LANES), a=_lane_rows(-jnp.exp(a_log.astype(F32)), GD_DT, LANES),
                  d_skip=jnp.repeat(d_skip.astype(F32), HEAD_DIM, axis=1).reshape(depth, 1, SSM_INNER),
                  norm_g=ssm_norm_g.astype(F32).reshape(depth, 1, SSM_INNER))
    fw_all = dict(norm2_g=norm2_g.reshape(depth, 1, d), w_out=w_out.astype(BF16), w_up=ffn_w_up.astype(BF16),
                  conv_w=ffn_conv_w, conv_b=ffn_conv_b.reshape(depth, 1, D_FF), w_down=ffn_w_down.astype(BF16))

    xp = x_prompt
    xs = x_sample.reshape(1, ns, d)
    outs_p, outs_s = [], []
    for l in range(depth):
        last = l == depth - 1
        mod_p = (mod_rows_p, l, ns, False)
        mod_s = (mod_rows_s, l, ns, True)
        at = lambda a: (a, l)
        cw = dict(w1k=at(w1k), b1k=at(b1k), w2k=at(w2k), w1v=at(w1v), b1v=at(b1v), w2v=at(w2v))
        cw4 = dict(w1k=at(w1k), b1k=at(b1k), w2k=at(w2k4), w1v=at(w1v), b1v=at(b1v), w2v=at(w2v4))
        sw = {n: at(a) for n, a in sw_all.items()}
        fw = {n: at(a) for n, a in fw_all.items()}
        fw["final_g"] = final_g.reshape(1, d)

        (qt, kc, vc, kvb, vt, z, xbc, gd, kct, vct, kst, vst, kwt, vwt) = _inproj(
            xp, mod_p, at(norm1), at(w_packed), tm_p, True)
        kcmp, vcmpt = _compress(kc, vc, cw)
        ocmpt, selt = _cmp_sel(qt, kcmp, vcmpt, consts)
        oatt = _attn(qt, kvb, vt, selt, ocmpt, gd, consts)
        yssm, h_p = _ssd(xbc, z, gd, sw, consts)
        xp, ug_tail = _mix_ffn(xp, oatt, yssm, mod_p, fw, tm_p, None, last)
        outs_p.append((kct, vct, kst, vst, kwt[:, :, tp - WINDOW:], vwt[:, :, tp - WINDOW:], h_p,
                       xbc[:, tp - (SSM_CONV - 1):], ug_tail[:, SUBLANES - (FFN_CONV - 1):]))

        q, kvb, z, xbc, gd, kc, vc, ks, vs, kw, vw = _inproj(xs, mod_s, at(norm1), at(w_packed), ns, False)
        per_tok = lambda a: a.reshape(ns, 1, a.shape[-1])
        kvn = per_tok(jnp.concatenate([kc, vc, ks, vs, kw, vw], axis=2))
        q, z, xbc, gd = per_tok(q), per_tok(z), per_tok(xbc), per_tok(gd)
        ocmp, idx = _s_cmp(l, page_table, q, kvn, gd, caches_t[0], caches_t[1], cw4, consts, past_len)
        oatt, wk_new, wv_new = _s_slc_win(l, idx, page_table, q, kvn, gd, ocmp, win_t[0], win_t[1],
                                          caches_t[2], caches_t[3], consts, past_len)
        yssm, h_s = _s_ssd(l, xbc, state_ssm_conv[l], z, gd, state_ssm, sw, consts)
        prev_ffn = (state_ffn_conv[l][:, 0].reshape(1, ns, D_FF), state_ffn_conv[l][:, 1].reshape(1, ns, D_FF))
        xs, ug_s = _mix_ffn(xs, oatt.reshape(1, ns, ATT_WIDTH), yssm.reshape(1, ns, SSM_INNER), mod_s, fw, ns,
                            prev_ffn, last)
        s4 = lambda a: a.reshape(ns, 1, N_KV, HEAD_DIM)
        outs_s.append((s4(kc), s4(vc), s4(ks), s4(vs), wk_new, wv_new, h_s,
                       jnp.concatenate([state_ssm_conv[l][:, 1:], xbc.reshape(ns, 1, CONV_DIM)], axis=1),
                       jnp.concatenate([state_ffn_conv[l][:, 1:], ug_s.reshape(ns, 1, D_FF)], axis=1)))

    stack = lambda seq, i: jnp.stack([st[i] for st in seq])
    res = [xp, xs.reshape(ns, 1, d)]
    for i in range(9):
        p_i, s_i = stack(outs_p, i), stack(outs_s, i)
        if i < 6:
            p_i = _token_major(p_i)
        if i in (4, 5):
            s_i = _token_major(s_i)
        res += [p_i, s_i]
    return tuple(res)
```

```python
import functools

import numpy as np
import jax
import jax.numpy as jnp
from jax import lax
from jax.experimental import pallas as pl
from jax.experimental.pallas import tpu as pltpu

F32 = jnp.float32
BF16 = jnp.bfloat16

D_MODEL = 1024
HEAD_DIM = 64
ATT_WIDTH = 512
N_HEADS = 8
N_KV = 2
HG = 4
CMP_BLOCK = 32
CMP_STRIDE = 16
CMP_HID = 128
SLC_BLOCK = 64
N_SEL = 16
WINDOW = 512
FORCE_BONUS = 1e6
SSM_INNER = 512
SSM_HEADS = 8
SSM_GROUPS = 2
D_STATE = 64
SSM_CONV = 4
SSM_CHUNK = 128
CONV_DIM = 768
D_FF = 2816
FFN_CONV = 3
EPS = 1e-6
NEG = -1e30
SCALE = HEAD_DIM ** -0.5
LOG2E = 1.4426950408889634
SLOPES = tuple(2.0 ** (-(h + 1)) for h in range(N_HEADS))

LANES = 128
SUBLANES = 8
VMEM_LIMIT = 56 * 1024 * 1024

FF_CHUNKS = ((0, 768), (768, 1536), (1536, 2176), (2176, 2816))
ROW_PITCH = 24
TQ = 2 * SLC_BLOCK
NCOL = N_HEADS * TQ

P_Q = 0
P_KV = 512
P_Z = 1280
P_XBC = 1792
P_GD = 2560
P_W = 2688
GD_DT = 24
MASK_BIG = 2.0 ** 100


def _cparams(sem):
    return pltpu.CompilerParams(dimension_semantics=sem, vmem_limit_bytes=VMEM_LIMIT)


def _resident(shape):
    nd = len(shape)
    return pl.BlockSpec(shape, lambda *_: (0,) * nd, pipeline_mode=pl.Buffered(1))


def _wspec(w):
    if isinstance(w, tuple):
        arr, layer = w
        nd = arr.ndim
        return pl.BlockSpec((None,) + arr.shape[1:], lambda *_: (layer,) + (0,) * (nd - 1),
                            pipeline_mode=pl.Buffered(1))
    return _resident(w.shape)


def _warr(w):
    return w[0] if isinstance(w, tuple) else w


def _split3(x):
    hi = x.astype(BF16)
    r = x - hi.astype(F32)
    mid = r.astype(BF16)
    lo = (r - mid.astype(F32)).astype(BF16)
    return hi, mid, lo


def _dot(a, b):
    return jnp.dot(a, b, preferred_element_type=F32)


def _dot_nt(a, b):
    return lax.dot_general(a, b, (((1,), (1,)), ((), ())), preferred_element_type=F32)


def _expand(x, e):
    hi, mid, lo = _split3(x)
    return _dot(hi, e) + _dot(mid, e) + _dot(lo, e)


def _expand_l(e, x):
    hi, mid, lo = _split3(x)
    return _dot(e, hi) + _dot(e, mid) + _dot(e, lo)


def _silu(x):
    return x * jax.nn.sigmoid(x)


def _softplus(x):
    return jnp.maximum(x, 0.0) + jnp.log1p(jnp.exp(-jnp.abs(x)))


def _gelu_tanh(x):
    return 0.5 * x * (1.0 + jnp.tanh(np.sqrt(2.0 / np.pi) * (x + 0.044715 * (x * x * x))))


def _mod_kernel(c_ref, w_ref, b_ref, o_ref):
    c = c_ref[...]
    s = _silu(c).astype(BF16)
    o_ref[0] = _dot(s, w_ref[0].astype(BF16)) + b_ref[0]


def _mod(c_all, ada_w, ada_b):
    depth, d, n6 = ada_w.shape
    rows = c_all.shape[0]
    tn = 512
    return pl.pallas_call(
        _mod_kernel,
        out_shape=jax.ShapeDtypeStruct((depth, rows, n6), F32),
        grid=(depth, n6 // tn),
        in_specs=[pl.BlockSpec((rows, d), lambda l, j: (0, 0)),
                  pl.BlockSpec((1, d, tn), lambda l, j: (l, 0, j)),
                  pl.BlockSpec((1, 1, tn), lambda l, j: (l, 0, j))],
        out_specs=pl.BlockSpec((1, rows, tn), lambda l, j: (l, 0, j)),
        compiler_params=_cparams(("arbitrary", "arbitrary")),
        name="adaln_mod",
    )(c_all, ada_w, ada_b.reshape(depth, 1, n6))


def _mod_spec(mod, k):
    arr, layer, row0, per_row = mod
    if per_row:
        return pl.BlockSpec((None, 1, row0, D_MODEL), lambda b, t: (layer, 0, 0, k))
    return pl.BlockSpec((None, 1, 1, D_MODEL), lambda b, t: (layer, row0 + b, 0, k))


def _inproj_kernel(x_ref, sh_ref, sc_ref, g_ref, w_ref, *outs, tm, token_minor):
    x = x_ref[0]
    ms = jnp.mean(x * x, axis=-1, keepdims=True)
    xn = x * lax.rsqrt(ms + EPS) * g_ref[...]
    h = xn * (1.0 + sc_ref[0]) + sh_ref[0]
    proj = _dot(h.astype(BF16), w_ref[...])
    kv = [proj[:, P_KV + LANES * i:P_KV + LANES * (i + 1)] for i in range(6)]
    if token_minor:
        qt_ref, kc_ref, vc_ref, kvb_ref, vt_ref, z_ref, xbc_ref, gd_ref = outs[:8]
        qt_ref[0] = (proj[:, P_Q:P_Q + ATT_WIDTH] * (SCALE * LOG2E)).T.astype(BF16)
        kc_ref[0] = kv[0]
        vc_ref[0] = kv[1]
        kvt = [a.T for a in kv]
        for ref, a in zip(outs[8:], kvt):
            ref[0] = a
        for r in range(tm // TQ):
            cols = slice(r * TQ, (r + 1) * TQ)
            vt_ref[0, r] = jnp.concatenate([kvt[3][:, cols], kvt[5][:, cols]], axis=0).astype(BF16)
    else:
        q_ref, kvb_ref, z_ref, xbc_ref, gd_ref = outs[:5]
        q_ref[0] = (proj[:, P_Q:P_Q + ATT_WIDTH] * SCALE).astype(BF16)
        for ref, a in zip(outs[5:], kv):
            ref[0] = a
    kvb_ref[0] = proj[:, P_KV:P_Z].astype(BF16)
    z_ref[0] = proj[:, P_Z:P_XBC]
    xbc_ref[0] = proj[:, P_XBC:P_GD]
    gd_ref[0] = proj[:, P_GD:P_W]


def _inproj(x, mod, norm_g, w_packed, tm, token_minor):
    bsz, t, d = x.shape
    row = lambda w: pl.BlockSpec((1, tm, w), lambda b, i: (b, i, 0))
    col = lambda w: pl.BlockSpec((1, w, tm), lambda b, i: (b, 0, i))
    shp = lambda w, dt: jax.ShapeDtypeStruct((bsz, t, w), dt)
    shpt = lambda w, dt: jax.ShapeDtypeStruct((bsz, w, t), dt)
    if token_minor:
        out_shape = ([shpt(ATT_WIDTH, BF16), shp(LANES, F32), shp(LANES, F32), shp(6 * LANES, BF16),
                      jax.ShapeDtypeStruct((bsz, t // TQ, 2 * LANES, TQ), BF16),
                      shp(SSM_INNER, F32), shp(CONV_DIM, F32), shp(LANES, F32)] + [shpt(LANES, F32)] * 6)
        out_specs = ([col(ATT_WIDTH), row(LANES), row(LANES), row(6 * LANES),
                      pl.BlockSpec((1, tm // TQ, 2 * LANES, TQ), lambda b, i: (b, i, 0, 0)),
                      row(SSM_INNER), row(CONV_DIM), row(LANES)] + [col(LANES)] * 6)
    else:
        out_shape = ([shp(ATT_WIDTH, BF16), shp(6 * LANES, BF16), shp(SSM_INNER, F32), shp(CONV_DIM, F32),
                      shp(LANES, F32)] + [shp(LANES, F32)] * 6)
        out_specs = [row(ATT_WIDTH), row(6 * LANES), row(SSM_INNER), row(CONV_DIM), row(LANES)] + [row(LANES)] * 6
    return pl.pallas_call(
        functools.partial(_inproj_kernel, tm=tm, token_minor=token_minor),
        out_shape=out_shape,
        grid=(bsz, t // tm),
        in_specs=[row(d), _mod_spec(mod, 0), _mod_spec(mod, 1), _wspec(norm_g), _wspec(w_packed)],
        out_specs=out_specs,
        compiler_params=_cparams(("arbitrary", "arbitrary")),
        name="inproj",
    )(x, mod[0], mod[0], _warr(norm_g), _warr(w_packed))


def _compress_block(src_ref, nh, w1_ref, b1_ref, w2_ref, pitch=CMP_STRIDE):
    acc = jnp.zeros((nh, 4 * CMP_HID), F32)
    for i in range(CMP_STRIDE // 2):
        xa = src_ref[pl.ds(2 * i, nh, stride=pitch), :]
        xb = src_ref[pl.ds(2 * i + 1, nh, stride=pitch), :]
        acc = acc + _dot(jnp.concatenate([xa, xb], axis=1).astype(BF16), w1_ref[i])
    pa = acc[:, :2 * CMP_HID]
    pb = pltpu.roll(acc[:, 2 * CMP_HID:], nh - 1, 0)
    hid = _gelu_tanh(pa + pb + b1_ref[...])
    return _dot(hid.astype(BF16), w2_ref[...])


def _compress_kernel(k_ref, v_ref, w1k_ref, b1k_ref, w2k_ref, w1v_ref, b1v_ref, w2v_ref, ok_ref, ovt_ref, *, nh):
    ok_ref[0] = _compress_block(k_ref.at[0], nh, w1k_ref, b1k_ref, w2k_ref).astype(BF16)
    ovt_ref[0] = _compress_block(v_ref.at[0], nh, w1v_ref, b1v_ref, w2v_ref).T.astype(BF16)


def _compress(kc, vc, cw):
    bsz, t, _ = kc.shape
    nh = t // CMP_STRIDE
    seq = pl.BlockSpec((1, t, LANES), lambda b: (b, 0, 0))
    ws = [cw[n] for n in ("w1k", "b1k", "w2k", "w1v", "b1v", "w2v")]
    return pl.pallas_call(
        functools.partial(_compress_kernel, nh=nh),
        out_shape=[jax.ShapeDtypeStruct((bsz, nh, LANES), BF16), jax.ShapeDtypeStruct((bsz, LANES, nh), BF16)],
        grid=(bsz,),
        in_specs=[seq, seq] + [_wspec(w) for w in ws],
        out_specs=[pl.BlockSpec((1, nh, LANES), lambda b: (b, 0, 0)), pl.BlockSpec((1, LANES, nh), lambda b: (b, 0, 0))],
        compiler_params=_cparams(("arbitrary",)),
        name="compress",
    )(kc, vc, *[_warr(w) for w in ws])


def _all_sublanes(x, op):
    for shift in (4, 2, 1):
        x = op(x, pltpu.roll(x, shift, 0))
    return x


def _query_blockdiag(qt):
    zero = jnp.zeros((HEAD_DIM, HG * TQ), BF16)
    per_g = [jnp.concatenate([qt[HEAD_DIM * (HG * g + j):HEAD_DIM * (HG * g + j + 1), :] for j in range(HG)], axis=1)
             for g in range(N_KV)]
    return jnp.concatenate([jnp.concatenate([per_g[0], zero], axis=1),
                            jnp.concatenate([zero, per_g[1]], axis=1)], axis=0)


def _topk_mask_t(score, jrow, n_slc):
    rank = jnp.zeros(score.shape, F32)
    for jp in range(n_slc):
        sj = score[jp:jp + 1, :]
        ahead = (sj > score) | ((sj == score) & (jp < jrow))
        rank = rank + ahead.astype(F32)
    return (rank < N_SEL) & (jrow < n_slc)


def _cmp_sel_kernel(qt_ref, kc_ref, vct_ref, ot_ref, c0_ref, srow_ref, ocmpt_ref, selt_ref, *, n_cmp, n_slc):
    t = pl.program_id(1)
    qbd = _query_blockdiag(qt_ref[0])
    kc = kc_ref[0]
    nl = kc.shape[0]
    srow = srow_ref[...]
    s = _dot(kc, qbd) - (c0_ref[...] + srow * (t * TQ).astype(F32))
    nidx = lax.broadcasted_iota(jnp.int32, (nl, NCOL), 0)
    qpos = t * TQ + lax.broadcasted_iota(jnp.int32, (nl, NCOL), 1) % TQ
    mask = (qpos - (nidx * CMP_STRIDE + (CMP_BLOCK - 1)) >= 0) & (nidx < n_cmp)
    sm = jnp.where(mask, s, NEG)
    e = jnp.exp2(sm - jnp.max(sm, axis=0, keepdims=True))
    p = jnp.where(mask, e * (1.0 / jnp.sum(e, axis=0, keepdims=True)), 0.0)
    pb = p.astype(BF16)
    vct = vct_ref[0]

    js = selt_ref.shape[2]
    jrow = lax.broadcasted_iota(jnp.int32, (js, TQ), 0)
    qp = t * TQ + lax.broadcasted_iota(jnp.int32, (js, TQ), 1)
    cur = qp // SLC_BLOCK
    forced = (jrow == 0) | (jrow == cur) | (jrow == cur - 1)
    valid = (jrow * SLC_BLOCK <= qp)
    for g in range(N_KV):
        c0 = g * HG * TQ
        og = _dot(vct[HEAD_DIM * g:HEAD_DIM * (g + 1), :], pb[:, c0:c0 + HG * TQ])
        pg = jnp.zeros((nl, TQ), F32)
        for j in range(HG):
            h = HG * g + j
            ocmpt_ref[0, HEAD_DIM * h:HEAD_DIM * (h + 1), :] = og[:, j * TQ:(j + 1) * TQ]
            pg = pg + p[:, c0 + j * TQ:c0 + (j + 1) * TQ]
        imp_t = _expand_l(ot_ref[...], pg)
        score = jnp.where(valid, imp_t + jnp.where(forced, FORCE_BONUS, 0.0), NEG)
        selt_ref[0, g] = _topk_mask_t(score, jrow, n_slc).astype(F32)


def _cmp_sel(qt, kcmp, vcmpt, consts):
    bsz, _, t = qt.shape
    nl = kcmp.shape[1]
    n_slc = t // SLC_BLOCK
    js = consts["ot"].shape[0]
    return pl.pallas_call(
        functools.partial(_cmp_sel_kernel, n_cmp=t // CMP_STRIDE - 1, n_slc=n_slc),
        out_shape=[jax.ShapeDtypeStruct((bsz, ATT_WIDTH, t), F32),
                   jax.ShapeDtypeStruct((bsz, N_KV, js, t), F32)],
        grid=(bsz, t // TQ),
        in_specs=[pl.BlockSpec((1, ATT_WIDTH, TQ), lambda b, i: (b, 0, i)),
                  pl.BlockSpec((1, nl, LANES), lambda b, i: (b, 0, 0)),
                  pl.BlockSpec((1, LANES, nl), lambda b, i: (b, 0, 0)),
                  _resident(consts["ot"].shape), _resident(consts["c0"].shape), _resident(consts["srow2"].shape)],
        out_specs=[pl.BlockSpec((1, ATT_WIDTH, TQ), lambda b, i: (b, 0, i)),
                   pl.BlockSpec((1, N_KV, js, TQ), lambda b, i: (b, 0, 0, i))],
        compiler_params=_cparams(("arbitrary", "arbitrary")),
        name="cmp_select",
    )(qt, kcmp, vcmpt, consts["ot"], consts["c0"], consts["srow2"])


def _attn_kernel(qt_ref, kvb_ref, vt_ref, selt_ref, ocmpt_ref, gd_ref, fk_ref, coef_ref, amask_ref, egt_ref, o_ref,
                 pen_ref, sbuf, *state):
    nchunk = len(state) // 3
    m_refs, l_refs, acc_refs = state[:nchunk], state[nchunk:2 * nchunk], state[2 * nchunk:]
    t = pl.program_id(1)
    qbd = _query_blockdiag(qt_ref[0])
    gw = HG * TQ
    for g in range(N_KV):
        pen = (selt_ref[0, g] - 1.0) * MASK_BIG
        pen_ref[:, g * gw:(g + 1) * gw] = jnp.concatenate([pen] * HG, axis=1)
    nwin = WINDOW // TQ
    coef = coef_ref[...]
    zero8 = jnp.zeros((SUBLANES, NCOL), F32)
    cw = 2 * TQ
    zpad = jnp.zeros((LANES - 2 * SUBLANES, cw), BF16)
    rid = lax.broadcasted_iota(jnp.int32, (SUBLANES, NCOL), 0)

    def query_operands(c8):
        c16 = jnp.concatenate([c8, zero8], axis=0).astype(BF16)
        return [jnp.concatenate([qbd[:, c * cw:(c + 1) * cw], c16[:, c * cw:(c + 1) * cw], zpad], axis=0)
                for c in range(NCOL // cw)]

    q_win = query_operands(coef)

    def score_tile(kt, slot, kcol, slc, mask):
        r0 = pl.multiple_of(kt * TQ, TQ)
        k_aug = jnp.concatenate([kvb_ref[0, pl.ds(r0, TQ), kcol:kcol + LANES], fk_ref[kt]], axis=1)
        if slc:
            pa = jnp.broadcast_to(pen_ref[pl.ds(2 * kt, 1), :], (SUBLANES, NCOL))
            pb = jnp.broadcast_to(pen_ref[pl.ds(2 * kt + 1, 1), :], (SUBLANES, NCOL))
            q_aug = query_operands(jnp.where(rid == 6, pa, jnp.where(rid == 7, pb, coef)))
        else:
            q_aug = q_win
        for c in range(nchunk):
            cols = slice(c * cw, (c + 1) * cw)
            sbuf[slot, :, cols] = _dot(k_aug, q_aug[c]) + amask_ref[mask, :, cols]

    def consume_tile(kt, slot, vrow):
        v_t = vt_ref[0, kt, vrow:vrow + LANES, :]
        for c in range(nchunk):
            cols = slice(c * cw, (c + 1) * cw)
            g = (c * cw) // gw
            s = sbuf[slot, :, cols].reshape(TQ // SUBLANES, SUBLANES, cw)
            m_old = m_refs[c][...]
            m_new = jnp.maximum(m_old, _all_sublanes(jnp.max(s, axis=0), jnp.maximum))
            alpha = jnp.exp2(m_old - m_new)
            p = jnp.exp2(s - m_new[None])
            l_refs[c][...] = alpha * l_refs[c][...] + _all_sublanes(jnp.sum(p, axis=0), jnp.add)
            m_refs[c][...] = m_new
            pv = _dot(v_t[HEAD_DIM * g:HEAD_DIM * (g + 1), :], p.reshape(TQ, cw).astype(BF16))
            acc = acc_refs[c][...].reshape(HEAD_DIM // SUBLANES, SUBLANES, cw) * alpha[None]
            acc_refs[c][...] = acc.reshape(HEAD_DIM, cw) + pv

    def reset():
        for c in range(nchunk):
            m_refs[c][...] = jnp.full(m_refs[c].shape, NEG, F32)
            l_refs[c][...] = jnp.zeros(l_refs[c].shape, F32)
            acc_refs[c][...] = jnp.zeros(acc_refs[c].shape, F32)

    def result():
        per_chunk = [(acc_refs[c][...].reshape(HEAD_DIM // SUBLANES, SUBLANES, cw)
                      * (1.0 / l_refs[c][...])[None]).reshape(HEAD_DIM, cw) for c in range(nchunk)]
        half = nchunk // N_KV
        return [jnp.concatenate(per_chunk[g * half:(g + 1) * half], axis=1) for g in range(N_KV)]

    def branch(first, kcol, vrow, slc, s0, s1):
        n = t - first + 1
        pad = n % 2

        def tile_of(i):
            return jnp.maximum(first + i - pad, first)

        def mask_of(i):
            kt = first + i - pad
            diag = jnp.where(kt == t, 1, 0)
            m = diag if slc else jnp.where(kt == t - nwin, 2, diag)
            return jnp.where(i < pad, 3, m)

        score = lambda i, slot: score_tile(tile_of(i), slot, kcol, slc, mask_of(i))
        consume = lambda i, slot: consume_tile(tile_of(i), slot, vrow)
        total = n + pad
        reset()
        score(0, s0)

        def body(j, c):
            i = 2 * j
            score(i + 1, s1)
            consume(i, s0)
            score(i + 2, s0)
            consume(i + 1, s1)
            return c

        lax.fori_loop(0, total // 2 - 1, body, 0)
        score(total - 1, s1)
        consume(total - 2, s0)
        consume(total - 1, s1)
        return result()

    o_slc = branch(0, 2 * LANES, 0, True, 0, 1)
    o_win = branch(jnp.maximum(t - nwin, 0), 4 * LANES, LANES, False, 2, 3)

    g_t = jax.nn.sigmoid(gd_ref[0]).T
    gates = [_expand_l(egt_ref[br], g_t) for br in range(3)]
    for pr in range(N_HEADS // 2):
        g, j0 = pr // 2, 2 * (pr % 2)
        rows = slice(LANES * pr, LANES * (pr + 1))
        pair = lambda o: jnp.concatenate([o[g][:, j0 * TQ:(j0 + 1) * TQ], o[g][:, (j0 + 1) * TQ:(j0 + 2) * TQ]], axis=0)
        tot = gates[0][rows] * ocmpt_ref[0, rows, :] + gates[1][rows] * pair(o_slc) + gates[2][rows] * pair(o_win)
        o_ref[0, :, rows] = tot.T.astype(BF16)


def _attn(qt, kvb, vt, selt, ocmpt, gd, consts):
    bsz, _, t = qt.shape
    js = selt.shape[2]
    return pl.pallas_call(
        _attn_kernel,
        out_shape=jax.ShapeDtypeStruct((bsz, t, ATT_WIDTH), BF16),
        grid=(bsz, t // TQ),
        in_specs=[pl.BlockSpec((1, ATT_WIDTH, TQ), lambda b, i: (b, 0, i)),
                  pl.BlockSpec((1, t, 6 * LANES), lambda b, i: (b, 0, 0)),
                  pl.BlockSpec((1, t // TQ, 2 * LANES, TQ), lambda b, i: (b, 0, 0, 0)),
                  pl.BlockSpec((1, N_KV, js, TQ), lambda b, i: (b, 0, 0, i)),
                  pl.BlockSpec((1, ATT_WIDTH, TQ), lambda b, i: (b, 0, i)),
                  pl.BlockSpec((1, TQ, LANES), lambda b, i: (b, i, 0)),
                  _resident(consts["fk"].shape), _resident(consts["coef"].shape), _resident(consts["amask"].shape),
                  _resident(consts["e_gate_t"].shape)],
        out_specs=pl.BlockSpec((1, TQ, ATT_WIDTH), lambda b, i: (b, i, 0)),
        scratch_shapes=([pltpu.VMEM((js, NCOL), F32), pltpu.VMEM((4, TQ, NCOL), F32)]
                        + [pltpu.VMEM((SUBLANES, 2 * TQ), F32)] * (2 * N_HEADS // 2)
                        + [pltpu.VMEM((HEAD_DIM, 2 * TQ), F32)] * (N_HEADS // 2)),
        compiler_params=_cparams(("arbitrary", "arbitrary")),
        name="slc_win_attn",
    )(qt, kvb, vt, selt, ocmpt, gd, consts["fk"], consts["coef"], consts["amask"], consts["e_gate_t"])


def _gated_norm(y, z, ng):
    yz = y * _silu(z)
    half = SSM_INNER // SSM_GROUPS
    outs = []
    for g in range(SSM_GROUPS):
        part = yz[:, half * g:half * (g + 1)]
        ms = jnp.mean(part * part, axis=-1, keepdims=True)
        outs.append(part * lax.rsqrt(ms + EPS))
    return jnp.concatenate(outs, axis=1) * ng


def _ssd_kernel(xbc_ref, z_ref, gd_ref, cw_ref, cb_ref, dtb_ref, a_ref, dsk_ref, ng_ref, edt_ref,
                y_ref, hfin_ref, ubuf, sstate, ybuf, *, lc):
    t = pl.program_id(1)
    nt = pl.num_programs(1)

    @pl.when(t == 0)
    def _():
        ubuf[0:SUBLANES, :] = jnp.zeros((SUBLANES, CONV_DIM), F32)
        sstate[...] = jnp.zeros(sstate.shape, F32)

    ubuf[SUBLANES:SUBLANES + lc, :] = xbc_ref[0]
    conv = cb_ref[...] + jnp.zeros((lc, CONV_DIM), F32)
    for k in range(SSM_CONV):
        off = SUBLANES - (SSM_CONV - 1) + k
        conv = conv + ubuf[off:off + lc, :] * cw_ref[k:k + 1, :]
    ubuf[0:SUBLANES, :] = ubuf[lc:lc + SUBLANES, :]
    act = _silu(conv)
    xm = act[:, :SSM_INNER]
    bm = act[:, SSM_INNER:SSM_INNER + LANES]
    cm = act[:, SSM_INNER + LANES:]

    dt = _softplus(gd_ref[0] + dtb_ref[...])
    a = dt * a_ref[...]
    ri = lax.broadcasted_iota(jnp.int32, (lc, lc), 0)
    ci = lax.broadcasted_iota(jnp.int32, (lc, lc), 1)
    tril = ri >= ci
    cs = jnp.dot(tril.astype(F32), a, preferred_element_type=F32, precision=lax.Precision.HIGHEST)
    cs_t = cs.T
    edt = edt_ref[...]
    dt_x = _expand(dt, edt)
    cs_x = _expand(cs, edt)
    cs_last = cs_x[lc - 1:lc, :]
    bm_t = bm.T
    xd = xm * dt_x
    xw = xm * (jnp.exp(cs_last - cs_x) * dt_x)
    s_old = sstate[...]
    half = SSM_INNER // SSM_GROUPS
    for g in range(SSM_GROUPS):
        bg = bm[:, D_STATE * g:D_STATE * (g + 1)].astype(BF16)
        cg = cm[:, D_STATE * g:D_STATE * (g + 1)].astype(BF16)
        cb = _dot_nt(cg, bg)
        for j in range(SSM_HEADS // SSM_GROUPS):
            h = g * (SSM_HEADS // SSM_GROUPS) + j
            col = cs[:, GD_DT + h:GD_DT + h + 1]
            row = cs_t[GD_DT + h:GD_DT + h + 1, :]
            lm = jnp.exp(jnp.where(tril, col - row, NEG))
            lo, hi = HEAD_DIM * h, HEAD_DIM * (h + 1)
            ybuf[:, lo:hi] = _dot((cb * lm).astype(BF16), xd[:, lo:hi].astype(BF16))
        sg = s_old[:, half * g:half * (g + 1)]
        y_off = _dot(cg, sg.astype(BF16))
        ybuf[:, half * g:half * (g + 1)] = ybuf[:, half * g:half * (g + 1)] + y_off * jnp.exp(cs_x[:, half * g:half * (g + 1)])
        st = _dot(bm_t[D_STATE * g:D_STATE * (g + 1), :].astype(BF16), xw[:, half * g:half * (g + 1)].astype(BF16))
        sstate[:, half * g:half * (g + 1)] = sg * jnp.exp(cs_last[:, half * g:half * (g + 1)]) + st

    y = ybuf[...] + dsk_ref[...] * xm
    y_ref[0] = _gated_norm(y, z_ref[0], ng_ref[...]).astype(BF16)

    @pl.when(t == nt - 1)
    def _():
        s_pad = jnp.concatenate([sstate[...], jnp.zeros((LANES - D_STATE, SSM_INNER), F32)], axis=0)
        hfin_ref[0] = s_pad.T[:, :D_STATE]


def _ssd(xbc, z, gd, sw, consts):
    bsz, t, _ = xbc.shape
    lc = SSM_CHUNK
    tile = lambda w: pl.BlockSpec((1, lc, w), lambda b, i: (b, i, 0))
    ws = [sw[n] for n in ("conv_w", "conv_b", "dt_bias", "a", "d_skip", "norm_g")] + [consts["e_dt"]]
    y, hfin = pl.pallas_call(
        functools.partial(_ssd_kernel, lc=lc),
        out_shape=[jax.ShapeDtypeStruct((bsz, t, SSM_INNER), BF16),
                   jax.ShapeDtypeStruct((bsz, SSM_INNER, D_STATE), F32)],
        grid=(bsz, t // lc),
        in_specs=[tile(CONV_DIM), tile(SSM_INNER), tile(LANES)] + [_wspec(w) for w in ws],
        out_specs=[tile(SSM_INNER), pl.BlockSpec((1, SSM_INNER, D_STATE), lambda b, i: (b, 0, 0))],
        scratch_shapes=[pltpu.VMEM((lc + SUBLANES, CONV_DIM), F32),
                        pltpu.VMEM((D_STATE, SSM_INNER), F32),
                        pltpu.VMEM((lc, SSM_INNER), F32)],
        compiler_params=_cparams(("arbitrary", "arbitrary")),
        name="ssd_scan",
    )(xbc, z, gd, *[_warr(w) for w in ws])
    return y, hfin.reshape(bsz, SSM_HEADS, HEAD_DIM, D_STATE)


def _mix_ffn_kernel(*refs, tm, carry_conv, last):
    if carry_conv:
        (x_ref, oa_ref, ys_ref, g1_ref, sh2_ref, sc2_ref, g2_ref, n2_ref, wo_ref, wu_ref, fcw_ref, fcb_ref,
         wd_ref, fg_ref, out_ref, ug_ref, ubuf) = refs
    else:
        (x_ref, oa_ref, ys_ref, g1_ref, sh2_ref, sc2_ref, g2_ref, n2_ref, wo_ref, wu_ref, fcw_ref, fcb_ref,
         wd_ref, fg_ref, p0_ref, p1_ref, out_ref, ug_ref) = refs
    t = pl.program_id(1)
    nt = pl.num_programs(1)
    x = x_ref[0]
    mix = _dot(oa_ref[0], wo_ref[:ATT_WIDTH, :]) + _dot(ys_ref[0], wo_ref[ATT_WIDTH:, :])
    x1 = x + g1_ref[0] * mix
    ms = jnp.mean(x1 * x1, axis=-1, keepdims=True)
    h2 = (x1 * lax.rsqrt(ms + EPS) * n2_ref[...]) * (1.0 + sc2_ref[0]) + sh2_ref[0]
    h2 = h2.astype(BF16)
    if carry_conv:
        @pl.when(t == 0)
        def _():
            ubuf[0:SUBLANES, :] = jnp.zeros((SUBLANES, D_FF), F32)

    def up(c):
        lo, hi = FF_CHUNKS[c]
        return _dot(h2, wu_ref[:, lo:hi]), _dot(h2, wu_ref[:, D_FF + lo:D_FF + hi])

    nxt = up(0)
    down = None
    for c, (lo, hi) in enumerate(FF_CHUNKS):
        ug, uv = nxt
        if c + 1 < len(FF_CHUNKS):
            nxt = up(c + 1)
        if carry_conv:
            ubuf[SUBLANES:SUBLANES + tm, lo:hi] = ug
            u1 = ubuf[SUBLANES - 1:SUBLANES - 1 + tm, lo:hi]
            u2 = ubuf[SUBLANES - 2:SUBLANES - 2 + tm, lo:hi]
        else:
            u1 = p1_ref[0, :, lo:hi]
            u2 = p0_ref[0, :, lo:hi]
            ug_ref[0, :, lo:hi] = ug
        ugc = (fcb_ref[:, lo:hi] + u2 * fcw_ref[0:1, lo:hi] + u1 * fcw_ref[1:2, lo:hi] + ug * fcw_ref[2:3, lo:hi])
        d = _dot((_silu(ugc) * uv).astype(BF16), wd_ref[lo:hi, :])
        down = d if down is None else down + d
    if carry_conv:
        @pl.when(t == nt - 1)
        def _():
            ug_ref[0] = ubuf[tm:tm + SUBLANES, :]

        ubuf[0:SUBLANES, :] = ubuf[tm:tm + SUBLANES, :]
    x2 = x1 + g2_ref[0] * down
    if last:
        ms2 = jnp.mean(x2 * x2, axis=-1, keepdims=True)
        out_ref[0] = x2 * lax.rsqrt(ms2 + EPS) * fg_ref[...]
    else:
        out_ref[0] = x2


def _mix_ffn(x, oatt, yssm, mod, fw, tm, prev, last):
    bsz, t, d = x.shape
    carry = prev is None
    row = lambda w: pl.BlockSpec((1, tm, w), lambda b, i: (b, i, 0))
    ws = [fw[n] for n in ("norm2_g", "w_out", "w_up", "conv_w", "conv_b", "w_down", "final_g")]
    in_specs = ([row(d), row(ATT_WIDTH), row(SSM_INNER)] + [_mod_spec(mod, k) for k in (2, 3, 4, 5)]
                + [_wspec(w) for w in ws])
    args = [x, oatt, yssm] + [mod[0]] * 4 + [_warr(w) for w in ws]
    if carry:
        ug_shape, ug_spec = (bsz, SUBLANES, D_FF), pl.BlockSpec((1, SUBLANES, D_FF), lambda b, i: (b, 0, 0))
        scratch = [pltpu.VMEM((tm + SUBLANES, D_FF), F32)]
    else:
        ug_shape, ug_spec = (bsz, t, D_FF), row(D_FF)
        scratch = []
        in_specs += [row(D_FF), row(D_FF)]
        args += list(prev)
    return pl.pallas_call(
        functools.partial(_mix_ffn_kernel, tm=tm, carry_conv=carry, last=last),
        out_shape=[jax.ShapeDtypeStruct((bsz, t, d), F32), jax.ShapeDtypeStruct(ug_shape, F32)],
        grid=(bsz, t // tm),
        in_specs=in_specs,
        out_specs=[row(d), ug_spec],
        scratch_shapes=scratch,
        compiler_params=_cparams(("arbitrary", "arbitrary")),
        name="mix_ffn",
    )(*args)


def _head_rows(row512):
    r = lax.broadcasted_iota(jnp.int32, (N_HEADS, ATT_WIDTH), 0)
    c = lax.broadcasted_iota(jnp.int32, (N_HEADS, ATT_WIDTH), 1)
    return jnp.where(c // HEAD_DIM == r, jnp.broadcast_to(row512, (N_HEADS, ATT_WIDTH)), 0.0)


def _head_diag(x8):
    r = lax.broadcasted_iota(jnp.int32, (N_HEADS, ATT_WIDTH), 0)
    c = lax.broadcasted_iota(jnp.int32, (N_HEADS, ATT_WIDTH), 1)
    return jnp.sum(jnp.where(c // HEAD_DIM == r, x8, 0.0), axis=0, keepdims=True)


def _slope_col():
    r = lax.broadcasted_iota(jnp.int32, (N_HEADS, 1), 0)
    s = jnp.zeros((N_HEADS, 1), F32)
    for h in range(N_HEADS):
        s = jnp.where(r == h, SLOPES[h], s)
    return s


def _s_cmp_kernel(pt_ref, q_ref, kvn_ref, gd_ref, ck_hbm, cv_hbm,
                  w1k_ref, b1k_ref, w2k_ref, w1v_ref, b1v_ref, w2v_ref, ov_ref, u_ref, e0_ref,
                  ocmp_ref, idx_ref, stage_k, stage_v, kbuf, vbuf, sem,
                  *, layer, n_pages, page, nh8, nhp, q_pos, n_slc):
    b = pl.program_id(0)
    nb = pl.num_programs(0)
    past = n_pages * page

    def copies(bb, p, slot):
        pg = pt_ref[bb * n_pages + p]
        return (pltpu.make_async_copy(ck_hbm.at[layer, pg], stage_k.at[slot, p], sem.at[0, slot]),
                pltpu.make_async_copy(cv_hbm.at[layer, pg], stage_v.at[slot, p], sem.at[1, slot]))

    def issue(bb, slot):
        def body(p, c):
            for cp in copies(bb, p, slot):
                cp.start()
            return c
        lax.fori_loop(0, n_pages, body, 0)

    @pl.when(b == 0)
    def _():
        issue(0, 0)

    @pl.when(b + 1 < nb)
    def _():
        issue(b + 1, (b + 1) % 2)

    slot = b % 2

    def wait(p, c):
        for cp in copies(b, p, slot):
            cp.wait()
        return c

    lax.fori_loop(0, n_pages, wait, 0)

    unroll = 8 if n_pages % 8 == 0 else 1

    hpp = page // CMP_STRIDE

    def untranspose(i, c):
        for u in range(unroll):
            p = i * unroll + u
            base = pl.multiple_of(p * (hpp * ROW_PITCH), SUBLANES)
            kp = stage_k[slot, p].T
            vp = stage_v[slot, p].T
            for n in range(hpp):
                rows = pl.ds(base + n * ROW_PITCH, CMP_STRIDE)
                kbuf[rows, :] = kp[n * CMP_STRIDE:(n + 1) * CMP_STRIDE]
                vbuf[rows, :] = vp[n * CMP_STRIDE:(n + 1) * CMP_STRIDE]
        return c

    lax.fori_loop(0, n_pages // unroll, untranspose, 0)
    first_new = (past // CMP_STRIDE) * ROW_PITCH
    tail = nh8 * ROW_PITCH - first_new
    kbuf[first_new:, :] = jnp.zeros((tail, LANES), F32)
    vbuf[first_new:, :] = jnp.zeros((tail, LANES), F32)
    kbuf[first_new:first_new + 1, :] = kvn_ref[0, :, 0:LANES]
    vbuf[first_new:first_new + 1, :] = kvn_ref[0, :, LANES:2 * LANES]

    pad = jnp.zeros((nhp - nh8, ATT_WIDTH), F32)
    kc4 = jnp.concatenate([_compress_block(kbuf, nh8, w1k_ref, b1k_ref, w2k_ref, ROW_PITCH), pad], axis=0).astype(BF16)
    vc4 = jnp.concatenate([_compress_block(vbuf, nh8, w1v_ref, b1v_ref, w2v_ref, ROW_PITCH), pad], axis=0).astype(BF16)
    n_cmp = (past + 1 + CMP_STRIDE - 1) // CMP_STRIDE - 1

    qm = _head_rows(q_ref[0].astype(F32)).astype(BF16)
    nidx = lax.broadcasted_iota(jnp.int32, (N_HEADS, nhp), 1)
    dist_i = q_pos - (nidx * CMP_STRIDE + (CMP_BLOCK - 1))
    mask = (dist_i >= 0) & (nidx < n_cmp)
    s = _dot_nt(qm, kc4) - _slope_col() * dist_i.astype(F32)
    sm = jnp.where(mask, s, NEG)
    e = jnp.exp(sm - jnp.max(sm, axis=-1, keepdims=True))
    p = jnp.where(mask, e / jnp.sum(e, axis=-1, keepdims=True), 0.0)
    o = _head_diag(_dot(p.astype(BF16), vc4))
    g8 = jnp.broadcast_to(jax.nn.sigmoid(gd_ref[0]), (SUBLANES, LANES))
    gate = _expand(g8, e0_ref[...])[0:1, :]
    ocmp_ref[0] = gate * o

    js = ov_ref.shape[1]
    imp8 = _expand(p, ov_ref[...])
    hrow = lax.broadcasted_iota(jnp.int32, (N_HEADS, js), 0)
    jl = lax.broadcasted_iota(jnp.int32, (1, js), 1)
    cur = q_pos // SLC_BLOCK
    forced = (jl == 0) | (jl == cur) | (jl == cur - 1)
    valid = (jl * SLC_BLOCK <= q_pos) & (jl < n_slc)
    rj = lax.broadcasted_iota(jnp.int32, (js, js), 0)
    cj = lax.broadcasted_iota(jnp.int32, (js, js), 1)
    kk = lax.broadcasted_iota(jnp.int32, (2 * SUBLANES, js), 0)
    jvals = jnp.broadcast_to(jl.astype(F32), (SUBLANES, js)).astype(BF16)
    for g in range(N_KV):
        imp = jnp.sum(jnp.where(hrow // HG == g, imp8, 0.0), axis=0, keepdims=True)
        score = jnp.where(valid, imp + jnp.where(forced, FORCE_BONUS, 0.0), NEG)
        sb = jnp.broadcast_to(score, (js, js))
        col = jnp.sum(jnp.where(rj == cj, sb, 0.0), axis=1, keepdims=True)
        ahead = (col > sb) | ((col == sb) & (rj < cj))
        rank = jnp.sum(ahead.astype(F32), axis=0, keepdims=True)
        sel = ((rank < N_SEL) & (jl < n_slc)).astype(F32)
        pos = _dot(jnp.broadcast_to(sel, (SUBLANES, js)).astype(BF16), u_ref[...])[0:1, :]
        onehot = ((jnp.broadcast_to(pos, (2 * SUBLANES, js)) == kk.astype(F32))
                  & (jnp.broadcast_to(sel, (2 * SUBLANES, js)) > 0.5)).astype(BF16)
        idx = _dot_nt(jvals, onehot)[0:1, :]
        idx_ref[0, g:g + 1, :] = idx.astype(jnp.int32)


def _s_cmp(layer, page_table, q, kvn, gd, cache_k, cache_v, cw4, consts, q_pos):
    ns, n_pages = page_table.shape
    page = cache_k.shape[3]
    past = n_pages * page
    nh = (past + 1 + CMP_STRIDE - 1) // CMP_STRIDE
    nh8 = -(-nh // SUBLANES) * SUBLANES
    nhp = -(-nh // LANES) * LANES
    n_slc = past // SLC_BLOCK + 1
    ws = [cw4[n] for n in ("w1k", "b1k", "w2k", "w1v", "b1v", "w2v")] + [consts["ov_s"], consts["u_s"], consts["e_gate"][0]]
    full = lambda a: pl.BlockSpec(a.shape, lambda b, pt: (0,) * a.ndim)
    per_b = lambda a: pl.BlockSpec((1,) + a.shape[1:], lambda b, pt: (b,) + (0,) * (a.ndim - 1))
    grid_spec = pltpu.PrefetchScalarGridSpec(
        num_scalar_prefetch=1,
        grid=(ns,),
        in_specs=[per_b(q), per_b(kvn), per_b(gd), pl.BlockSpec(memory_space=pl.ANY), pl.BlockSpec(memory_space=pl.ANY)]
                 + [_wspec(w) for w in ws],
        out_specs=[pl.BlockSpec((1, 1, ATT_WIDTH), lambda b, pt: (b, 0, 0)),
                   pl.BlockSpec((1, N_KV, N_SEL), lambda b, pt: (b, 0, 0))],
        scratch_shapes=[pltpu.VMEM((2, n_pages, LANES, page), F32), pltpu.VMEM((2, n_pages, LANES, page), F32),
                        pltpu.VMEM((nh8 * ROW_PITCH, LANES), F32), pltpu.VMEM((nh8 * ROW_PITCH, LANES), F32),
                        pltpu.SemaphoreType.DMA((2, 2))],
    )
    return pl.pallas_call(
        functools.partial(_s_cmp_kernel, layer=layer, n_pages=n_pages, page=page, nh8=nh8, nhp=nhp, q_pos=q_pos,
                          n_slc=n_slc),
        out_shape=[jax.ShapeDtypeStruct((ns, 1, ATT_WIDTH), F32), jax.ShapeDtypeStruct((ns, N_KV, N_SEL), jnp.int32)],
        grid_spec=grid_spec,
        compiler_params=_cparams(("arbitrary",)),
        name="sample_cmp_select",
    )(page_table.reshape(-1), q, kvn, gd, cache_k, cache_v, *[_warr(w) for w in ws])


def _s_slc_win_kernel(idx_ref, pt_ref, q_ref, kvn_ref, gd_ref, ocmp_ref, wk_ref, wv_ref, sk_hbm, sv_hbm,
                      tile_ref, e1_ref, e2_ref, o_ref, wko_ref, wvo_ref, kb, vb, sem,
                      *, layer, n_pages, page, q_pos, wb):
    b = pl.program_id(0)
    nb = pl.num_programs(0)
    bpp = page // SLC_BLOCK
    n_past = n_pages * bpp
    nblk = N_KV * N_SEL
    blks = [idx_ref[b * nblk + i] for i in range(nblk)]
    slot = b % 2

    def copies(bb, sl, i):
        blk = idx_ref[bb * nblk + i]
        pg = pt_ref[bb * n_pages + jnp.minimum(blk, n_past - 1) // bpp]
        dst = slice(i * page, (i + 1) * page)
        return (pltpu.make_async_copy(sk_hbm.at[layer, pg], kb.at[sl, :, dst], sem.at[0, sl]),
                pltpu.make_async_copy(sv_hbm.at[layer, pg], vb.at[sl, :, dst], sem.at[1, sl]))

    def issue(bb, sl):
        for i in range(nblk):
            for cp in copies(bb, sl, i):
                cp.start()

    @pl.when(b == 0)
    def _():
        issue(0, 0)

    @pl.when(b + 1 < nb)
    def _():
        issue(b + 1, (b + 1) % 2)

    tile_m = tile_ref[...]
    q8 = _dot_nt(_head_rows(q_ref[0].astype(F32)).astype(BF16), tile_m).astype(BF16)
    slope = _slope_col()
    g8 = jnp.broadcast_to(jax.nn.sigmoid(gd_ref[0]), (SUBLANES, LANES))
    gate1 = _expand(g8, e1_ref[...])[0:1, :]
    gate2 = _expand(g8, e2_ref[...])[0:1, :]
    rgrp = lax.broadcasted_iota(jnp.int32, (N_HEADS, 1), 0) // HG

    def new_key(lane0):
        kn = kvn_ref[0, :, lane0:lane0 + LANES].astype(BF16).astype(F32)
        vn = kvn_ref[0, :, lane0 + LANES:lane0 + 2 * LANES].astype(BF16).astype(F32)
        return jnp.sum(q8.astype(F32) * kn, axis=-1, keepdims=True), vn

    def finish(e, e_new, v_t, v_new, l):
        o8 = (_dot_nt(e.astype(BF16), v_t.astype(BF16)) + e_new * v_new) / l
        return _head_diag(_expand(o8, tile_m))

    def shifted(w_t, new_row):
        col = jnp.broadcast_to(new_row, (LANES, LANES)).T[:, 0:1]
        lane = lax.broadcasted_iota(jnp.int32, (LANES, wb), 1)
        return jnp.where(lane == wb - 1, col, pltpu.roll(w_t, wb - 1, 1))

    wk_t = wk_ref[0, 0]
    wv_t = wv_ref[0, 0]
    kpos = (q_pos - wb) + lax.broadcasted_iota(jnp.int32, (N_HEADS, wb), 1)
    dist = q_pos - kpos
    wmask = (dist >= 0) & (dist <= WINDOW)
    s = jnp.where(wmask, _dot(q8, wk_t.astype(BF16)) - slope * dist.astype(F32), NEG)
    s_new, v_new = new_key(4 * LANES)
    m = jnp.maximum(jnp.max(s, axis=-1, keepdims=True), s_new)
    e = jnp.where(wmask, jnp.exp(s - m), 0.0)
    e_new = jnp.exp(s_new - m)
    o_win = finish(e, e_new, wv_t, v_new, jnp.sum(e, axis=-1, keepdims=True) + e_new)
    wko_ref[0] = shifted(wk_t, kvn_ref[0, :, 4 * LANES:5 * LANES])
    wvo_ref[0] = shifted(wv_t, kvn_ref[0, :, 5 * LANES:6 * LANES])

    for i in range(nblk):
        for cp in copies(b, slot, i):
            cp.wait()
    lane = lax.broadcasted_iota(jnp.int32, (1, page), 1)
    kpos_t, ok_t = [], []
    has_new = [jnp.zeros((1, 1), jnp.int32) for _ in range(N_KV)]
    for i in range(nblk):
        blk = blks[i]
        kpos_t.append(blk * SLC_BLOCK + lane % SLC_BLOCK)
        ok_t.append(((lane // SLC_BLOCK) == blk % bpp) & (blk < n_past))
        has_new[i // N_SEL] = jnp.maximum(has_new[i // N_SEL], (blk >= n_past).astype(jnp.int32))
    kpos = jnp.concatenate(kpos_t, axis=1)
    ok = jnp.concatenate(ok_t, axis=1)
    pgrp = lax.broadcasted_iota(jnp.int32, (1, nblk * page), 1) // (N_SEL * page)
    dist = q_pos - kpos
    kmask = ok & (dist >= 0) & (rgrp == pgrp)
    s = jnp.where(kmask, _dot(q8, kb[slot].astype(BF16)) - slope * dist.astype(F32), NEG)
    s_new, v_new = new_key(2 * LANES)
    new_on = jnp.where(rgrp == 0, has_new[0], has_new[1]) > 0
    sn = jnp.where(new_on, s_new, NEG)
    m = jnp.maximum(jnp.max(s, axis=-1, keepdims=True), sn)
    e = jnp.where(kmask, jnp.exp(s - m), 0.0)
    e_new = jnp.where(new_on, jnp.exp(sn - m), 0.0)
    o_slc = finish(e, e_new, vb[slot], v_new, jnp.sum(e, axis=-1, keepdims=True) + e_new)

    o_ref[0] = (ocmp_ref[0] + gate1 * o_slc + gate2 * o_win).astype(BF16)


def _s_slc_win(layer, idx, page_table, q, kvn, gd, ocmp, win_k, win_v, slc_k, slc_v, consts, q_pos):
    ns, n_pages = page_table.shape
    page = slc_k.shape[3]
    wb = win_k.shape[3]
    full = lambda a: pl.BlockSpec(a.shape, lambda b, i, pt: (0,) * a.ndim)
    per_b = lambda a: pl.BlockSpec((1,) + a.shape[1:], lambda b, i, pt: (b,) + (0,) * (a.ndim - 1))
    win = pl.BlockSpec((1, 1, LANES, wb), lambda b, i, pt: (layer, b, 0, 0))
    ws = [consts["tile_m"], consts["e_gate"][1], consts["e_gate"][2]]
    nlane = N_KV * N_SEL * page
    grid_spec = pltpu.PrefetchScalarGridSpec(
        num_scalar_prefetch=2,
        grid=(ns,),
        in_specs=[per_b(q), per_b(kvn), per_b(gd), per_b(ocmp), win, win,
                  pl.BlockSpec(memory_space=pl.ANY), pl.BlockSpec(memory_space=pl.ANY)] + [full(w) for w in ws],
        out_specs=[pl.BlockSpec((1, 1, ATT_WIDTH), lambda b, i, pt: (b, 0, 0)),
                   pl.BlockSpec((1, LANES, wb), lambda b, i, pt: (b, 0, 0)),
                   pl.BlockSpec((1, LANES, wb), lambda b, i, pt: (b, 0, 0))],
        scratch_shapes=[pltpu.VMEM((2, LANES, nlane), F32), pltpu.VMEM((2, LANES, nlane), F32),
                        pltpu.SemaphoreType.DMA((2, 2))],
    )
    return pl.pallas_call(
        functools.partial(_s_slc_win_kernel, layer=layer, n_pages=n_pages, page=page, q_pos=q_pos, wb=wb),
        out_shape=[jax.ShapeDtypeStruct((ns, 1, ATT_WIDTH), BF16),
                   jax.ShapeDtypeStruct((ns, LANES, wb), F32), jax.ShapeDtypeStruct((ns, LANES, wb), F32)],
        grid_spec=grid_spec,
        compiler_params=_cparams(("arbitrary",)),
        name="sample_slc_win_attn",
    )(idx.reshape(-1), page_table.reshape(-1), q, kvn, gd, ocmp, win_k, win_v, slc_k, slc_v, *ws)


def _col_bcast(row):
    blocks = [jnp.broadcast_to(row[:, LANES * i:LANES * (i + 1)], (LANES, LANES)).T for i in range(row.shape[1] // LANES)]
    return jnp.concatenate(blocks, axis=0)


def _s_ssd_kernel(xbc_ref, prev_ref, z_ref, gd_ref, h0_ref, cw_ref, cb_ref, dtb_ref, a_ref, dsk_ref, ng_ref,
                  edt_ref, y_ref, h_ref):
    conv = cb_ref[...] + xbc_ref[0] * cw_ref[SSM_CONV - 1:SSM_CONV, :]
    for k in range(SSM_CONV - 1):
        conv = conv + prev_ref[0, k:k + 1, :] * cw_ref[k:k + 1, :]
    act = _silu(conv)
    xm = act[:, :SSM_INNER]
    bm = act[:, SSM_INNER:SSM_INNER + LANES]
    cm = act[:, SSM_INNER + LANES:]
    dt = _softplus(gd_ref[0] + dtb_ref[...])
    a = dt * a_ref[...]
    edt = edt_ref[...]
    dt_x = _expand(jnp.broadcast_to(dt, (SUBLANES, LANES)), edt)[0:1, :]
    dec_x = jnp.exp(_expand(jnp.broadcast_to(a, (SUBLANES, LANES)), edt)[0:1, :])
    dtx = dt_x * xm
    h0 = h0_ref[0, 0].reshape(SSM_INNER, D_STATE)
    half = SSM_INNER // SSM_GROUPS
    rsel = lax.broadcasted_iota(jnp.int32, (SSM_INNER, 1), 0) // half
    lsel = lax.broadcasted_iota(jnp.int32, (1, SSM_INNER), 1) // half
    y_off = jnp.zeros((1, SSM_INNER), F32)
    cbx = jnp.zeros((1, SSM_INNER), F32)
    brow = jnp.zeros((SSM_INNER, D_STATE), F32)
    for g in range(SSM_GROUPS):
        bg = bm[:, D_STATE * g:D_STATE * (g + 1)]
        cg = cm[:, D_STATE * g:D_STATE * (g + 1)]
        c8 = jnp.broadcast_to(cg, (SUBLANES, D_STATE)).astype(BF16)
        yo = _dot_nt(c8, h0[half * g:half * (g + 1), :].astype(BF16))[0:1, :]
        y_off = jnp.where(lsel == g, jnp.concatenate([yo] * SSM_GROUPS, axis=1), y_off)
        cbx = jnp.where(lsel == g, jnp.sum(cg * bg, axis=-1, keepdims=True), cbx)
        brow = jnp.where(rsel == g, jnp.broadcast_to(bg, (SSM_INNER, D_STATE)), brow)
    y = y_off * dec_x + cbx * dtx + dsk_ref[...] * xm
    y_ref[0] = _gated_norm(y, z_ref[0], ng_ref[...]).astype(BF16)
    h_new = h0 * _col_bcast(dec_x)[:, :D_STATE] + _col_bcast(dtx)[:, :D_STATE] * brow
    h_ref[0] = h_new.reshape(SSM_HEADS, HEAD_DIM, D_STATE)


def _s_ssd(layer, xbc, prev, z, gd, state_ssm, sw, consts):
    ns = state_ssm.shape[1]
    full = lambda a: pl.BlockSpec(a.shape, lambda b: (0,) * a.ndim)
    per_b = lambda a: pl.BlockSpec((1,) + a.shape[1:], lambda b: (b,) + (0,) * (a.ndim - 1))
    hshape = state_ssm.shape[2:]
    ws = [sw[n] for n in ("conv_w", "conv_b", "dt_bias", "a", "d_skip", "norm_g")] + [consts["e_dt"]]
    return pl.pallas_call(
        _s_ssd_kernel,
        out_shape=[jax.ShapeDtypeStruct((ns, 1, SSM_INNER), BF16), jax.ShapeDtypeStruct((ns,) + hshape, F32)],
        grid=(ns,),
        in_specs=[per_b(xbc), per_b(prev), per_b(z), per_b(gd),
                  pl.BlockSpec((1, 1) + hshape, lambda b: (layer, b, 0, 0, 0))] + [_wspec(w) for w in ws],
        out_specs=[pl.BlockSpec((1, 1, SSM_INNER), lambda b: (b, 0, 0)),
                   pl.BlockSpec((1,) + hshape, lambda b: (b, 0, 0, 0))],
        compiler_params=_cparams(("arbitrary",)),
        name="sample_ssd_step",
    )(xbc, prev, z, gd, state_ssm, *[_warr(w) for w in ws])


def _constants(t_prompt, past_len):
    c = {}
    eg = np.zeros((3, LANES, ATT_WIDTH), np.float32)
    for br in range(3):
        for h in range(N_HEADS):
            eg[br, br * N_HEADS + h, h * HEAD_DIM:(h + 1) * HEAD_DIM] = 1.0
    c["e_gate"] = [jnp.asarray(eg[i], BF16) for i in range(3)]
    c["e_gate_t"] = jnp.asarray(eg.transpose(0, 2, 1), BF16)
    ed = np.zeros((LANES, SSM_INNER), np.float32)
    for h in range(SSM_HEADS):
        ed[GD_DT + h, h * HEAD_DIM:(h + 1) * HEAD_DIM] = 1.0
    c["e_dt"] = jnp.asarray(ed, BF16)

    def overlap(n_cmp_pad, n_slc, n_slc_pad):
        cs = np.arange(n_cmp_pad)[:, None] * CMP_STRIDE
        ss = np.arange(n_slc_pad)[None, :] * SLC_BLOCK
        ov = ((cs < ss + SLC_BLOCK) & (cs + CMP_BLOCK > ss) & (np.arange(n_slc_pad)[None, :] < n_slc))
        return ov.astype(np.float32)
    nh_p = t_prompt // CMP_STRIDE
    n_slc_p = t_prompt // SLC_BLOCK
    js_p = -(-n_slc_p // SUBLANES) * SUBLANES
    ov = overlap(nh_p, n_slc_p, js_p)
    ov[nh_p - 1:, :] = 0.0
    c["ot"] = jnp.asarray(ov.T, BF16)
    col = np.arange(NCOL)
    slope = np.asarray(SLOPES, np.float64)[col // TQ]
    c["srow2"] = jnp.asarray(LOG2E * slope[None, :], F32)
    c["c0"] = jnp.asarray(LOG2E * slope[None, :] * ((col % TQ)[None, :]
                                                    - (np.arange(nh_p)[:, None] * CMP_STRIDE + CMP_BLOCK - 1)), F32)
    def bf16_terms(x):
        terms, r = [], np.asarray(x, np.float64)
        for _ in range(3):
            tb = np.asarray(r, np.float32).astype(BF16).astype(np.float64)
            terms.append(tb)
            r = r - tb
        return terms
    coef = np.zeros((SUBLANES, NCOL), np.float64)
    coef[0:3] = np.stack(bf16_terms(LOG2E * SLC_BLOCK * slope))
    coef[3:6] = np.stack(bf16_terms(LOG2E * slope))
    c["coef"] = jnp.asarray(coef, F32)
    ntile = t_prompt // TQ
    kpos = np.arange(t_prompt).reshape(ntile, TQ)
    fk = np.zeros((ntile, TQ, LANES), np.float32)
    fk[:, :, 0:3] = (kpos // SLC_BLOCK)[:, :, None]
    fk[:, :, 3:6] = (kpos % SLC_BLOCK)[:, :, None]
    fk[:, :, 6] = (np.arange(TQ) < SLC_BLOCK)[None, :]
    fk[:, :, 7] = (np.arange(TQ) >= SLC_BLOCK)[None, :]
    c["fk"] = jnp.asarray(fk, BF16)
    rel = np.arange(TQ)[:, None] - (col % TQ)[None, :]
    c["amask"] = jnp.asarray(np.stack([np.zeros(rel.shape), np.where(rel > 0, -MASK_BIG, 0.0),
                                       np.where(rel < 0, -MASK_BIG, 0.0), np.full(rel.shape, -MASK_BIG)]), F32)
    nh_s = (past_len + 1 + CMP_STRIDE - 1) // CMP_STRIDE
    nhp_s = -(-nh_s // LANES) * LANES
    n_slc_s = past_len // SLC_BLOCK + 1
    js_s = -(-n_slc_s // LANES) * LANES
    ov_s = overlap(nhp_s, n_slc_s, js_s)
    ov_s[nh_s - 1:, :] = 0.0
    c["ov_s"] = jnp.asarray(ov_s, BF16)
    c["u_s"] = jnp.asarray(np.triu(np.ones((js_s, js_s), np.float32), 1), BF16)
    tm = np.zeros((LANES, ATT_WIDTH), np.float32)
    for h in range(N_HEADS):
        g = h // HG
        for d in range(HEAD_DIM):
            tm[g * HEAD_DIM + d, h * HEAD_DIM + d] = 1.0
    c["tile_m"] = jnp.asarray(tm, BF16)
    return c


def _pack_w_in(w_in):
    depth, d, _ = w_in.shape
    cuts = np.cumsum([ATT_WIDTH] + [LANES] * 6 + [3 * N_HEADS, SSM_INNER, CONV_DIM])
    q, kv, gt, z, xbc, dtc = (w_in[..., :cuts[0]], w_in[..., cuts[0]:cuts[6]], w_in[..., cuts[6]:cuts[7]],
                              w_in[..., cuts[7]:cuts[8]], w_in[..., cuts[8]:cuts[9]], w_in[..., cuts[9]:])
    pad = jnp.zeros((depth, d, P_W - P_GD - gt.shape[-1] - dtc.shape[-1]), w_in.dtype)
    return jnp.concatenate([q, kv, z, xbc, gt, dtc, pad], axis=-1).astype(BF16)


def _pack_compress(w1, b1, w2):
    depth = w1.shape[0]
    zeros = jnp.zeros((depth, CMP_STRIDE, HEAD_DIM, CMP_HID), w1.dtype)
    parts = []
    for ab in range(2):
        w = w1[:, ab * CMP_STRIDE:(ab + 1) * CMP_STRIDE]
        top = jnp.concatenate([w, zeros], axis=3)
        bot = jnp.concatenate([zeros, w], axis=3)
        parts.append(jnp.concatenate([top, bot], axis=2))
    w1b = jnp.concatenate(parts, axis=3)
    w1b = w1b.reshape(depth, CMP_STRIDE // 2, 2 * LANES, 4 * CMP_HID).astype(BF16)
    b1b = jnp.concatenate([b1, b1], axis=1).reshape(depth, 1, 2 * CMP_HID)
    z2 = jnp.zeros_like(w2)
    def blockdiag(rep):
        top = jnp.concatenate([w2] * rep + [z2] * rep, axis=2)
        bot = jnp.concatenate([z2] * rep + [w2] * rep, axis=2)
        return jnp.concatenate([top, bot], axis=1).astype(BF16)
    return w1b, b1b, blockdiag(1), blockdiag(HG)


def _lane_rows(vals, offset, width):
    depth, n = vals.shape
    return jnp.pad(vals.astype(F32), ((0, 0), (offset, width - offset - n))).reshape(depth, 1, width)


def _token_minor(a):
    lead = a.shape[:-3]
    n = len(lead)
    return jnp.transpose(a, tuple(range(n)) + (n + 1, n + 2, n)).reshape(lead + (N_KV * HEAD_DIM, a.shape[-3]))


def _token_major(a):
    lead = a.shape[:-2]
    n = len(lead)
    a = a.reshape(lead + (N_KV, HEAD_DIM, a.shape[-1]))
    return jnp.transpose(a, tuple(range(n)) + (n + 2, n, n + 1))


def kernel(x_prompt, x_sample, cache_cmp_k, cache_cmp_v, cache_slc_k, cache_slc_v, state_win_k, state_win_v, state_ssm, state_ssm_conv, state_ffn_conv, page_table, c_prompt, c_sample, ada_w, ada_b, norm1_g, norm2_g, w_in, cmpk_w1, cmpk_b1, cmpk_w2, cmpv_w1, cmpv_b1, cmpv_w2, ssm_conv_w, ssm_conv_b, dt_bias, a_log, d_skip, ssm_norm_g, w_out, ffn_w_up, ffn_conv_w, ffn_conv_b, ffn_w_down, final_g):
    bp, tp, d = x_prompt.shape
    ns = x_sample.shape[0]
    depth = w_in.shape[0]
    page = cache_cmp_k.shape[2]
    past_len = page_table.shape[1] * page
    assert x_sample.shape[1] == 1 and d == D_MODEL
    assert tp % SSM_CHUNK == 0 and tp >= WINDOW and past_len >= WINDOW and page % SLC_BLOCK == 0
    consts = _constants(tp, past_len)
    caches_t = [_token_minor(c) for c in (cache_cmp_k, cache_cmp_v, cache_slc_k, cache_slc_v)]
    win_t = [_token_minor(w) for w in (state_win_k, state_win_v)]

    mod = _mod(jnp.concatenate([c_sample, c_prompt], axis=0), ada_w, ada_b)
    mod_rows_p = mod.reshape(depth, ns + bp, 1, 6 * d)
    mod_rows_s = mod.reshape(depth, 1, ns + bp, 6 * d)
    tm_p = 512 if tp % 512 == 0 else SSM_CHUNK

    w_packed = _pack_w_in(w_in)
    w1k, b1k, w2k, w2k4 = _pack_compress(cmpk_w1, cmpk_b1, cmpk_w2)
    w1v, b1v, w2v, w2v4 = _pack_compress(cmpv_w1, cmpv_b1, cmpv_w2)
    norm1 = norm1_g.reshape(depth, 1, d)
    sw_all = dict(conv_w=ssm_conv_w, conv_b=ssm_conv_b.reshape(depth, 1, CONV_DIM),
                  dt_bias=_lane_rows(dt_bias, GD_DT, LANES), a=_lane_rows(-jnp.exp(a_log.astype(F32)), GD_DT, LANES),
                  d_skip=jnp.repeat(d_skip.astype(F32), HEAD_DIM, axis=1).reshape(depth, 1, SSM_INNER),
                  norm_g=ssm_norm_g.astype(F32).reshape(depth, 1, SSM_INNER))
    fw_all = dict(norm2_g=norm2_g.reshape(depth, 1, d), w_out=w_out.astype(BF16), w_up=ffn_w_up.astype(BF16),
                  conv_w=ffn_conv_w, conv_b=ffn_conv_b.reshape(depth, 1, D_FF), w_down=ffn_w_down.astype(BF16))

    xp = x_prompt
    xs = x_sample.reshape(1, ns, d)
    outs_p, outs_s = [], []
    for l in range(depth):
        last = l == depth - 1
        mod_p = (mod_rows_p, l, ns, False)
        mod_s = (mod_rows_s, l, ns, True)
        at = lambda a: (a, l)
        cw = dict(w1k=at(w1k), b1k=at(b1k), w2k=at(w2k), w1v=at(w1v), b1v=at(b1v), w2v=at(w2v))
        cw4 = dict(w1k=at(w1k), b1k=at(b1k), w2k=at(w2k4), w1v=at(w1v), b1v=at(b1v), w2v=at(w2v4))
        sw = {n: at(a) for n, a in sw_all.items()}
        fw = {n: at(a) for n, a in fw_all.items()}
        fw["final_g"] = final_g.reshape(1, d)

        (qt, kc, vc, kvb, vt, z, xbc, gd, kct, vct, kst, vst, kwt, vwt) = _inproj(
            xp, mod_p, at(norm1), at(w_packed), tm_p, True)
        kcmp, vcmpt = _compress(kc, vc, cw)
        ocmpt, selt = _cmp_sel(qt, kcmp, vcmpt, consts)
        oatt = _attn(qt, kvb, vt, selt, ocmpt, gd, consts)
        yssm, h_p = _ssd(xbc, z, gd, sw, consts)
        xp, ug_tail = _mix_ffn(xp, oatt, yssm, mod_p, fw, tm_p, None, last)
        outs_p.append((kct, vct, kst, vst, kwt[:, :, tp - WINDOW:], vwt[:, :, tp - WINDOW:], h_p,
                       xbc[:, tp - (SSM_CONV - 1):], ug_tail[:, SUBLANES - (FFN_CONV - 1):]))

        q, kvb, z, xbc, gd, kc, vc, ks, vs, kw, vw = _inproj(xs, mod_s, at(norm1), at(w_packed), ns, False)
        per_tok = lambda a: a.reshape(ns, 1, a.shape[-1])
        kvn = per_tok(jnp.concatenate([kc, vc, ks, vs, kw, vw], axis=2))
        q, z, xbc, gd = per_tok(q), per_tok(z), per_tok(xbc), per_tok(gd)
        ocmp, idx = _s_cmp(l, page_table, q, kvn, gd, caches_t[0], caches_t[1], cw4, consts, past_len)
        oatt, wk_new, wv_new = _s_slc_win(l, idx, page_table, q, kvn, gd, ocmp, win_t[0], win_t[1],
                                          caches_t[2], caches_t[3], consts, past_len)
        yssm, h_s = _s_ssd(l, xbc, state_ssm_conv[l], z, gd, state_ssm, sw, consts)
        prev_ffn = (state_ffn_conv[l][:, 0].reshape(1, ns, D_FF), state_ffn_conv[l][:, 1].reshape(1, ns, D_FF))
        xs, ug_s = _mix_ffn(xs, oatt.reshape(1, ns, ATT_WIDTH), yssm.reshape(1, ns, SSM_INNER), mod_s, fw, ns,
                            prev_ffn, last)
        s4 = lambda a: a.reshape(ns, 1, N_KV, HEAD_DIM)
        outs_s.append((s4(kc), s4(vc), s4(ks), s4(vs), wk_new, wv_new, h_s,
                       jnp.concatenate([state_ssm_conv[l][:, 1:], xbc.reshape(ns, 1, CONV_DIM)], axis=1),
                       jnp.concatenate([state_ffn_conv[l][:, 1:], ug_s.reshape(ns, 1, D_FF)], axis=1)))

    stack = lambda seq, i: jnp.stack([st[i] for st in seq])
    res = [xp, xs.reshape(ns, 1, d)]
    for i in range(9):
        p_i, s_i = stack(outs_p, i), stack(outs_s, i)
        if i < 6:
            p_i = _token_major(p_i)
        if i in (4, 5):
            s_i = _token_major(s_i)
        res += [p_i, s_i]
    return tuple(res)
```

```python
import functools

import numpy as np
import jax
import jax.numpy as jnp
from jax import lax
from jax.experimental import pallas as pl
from jax.experimental.pallas import tpu as pltpu

F32 = jnp.float32
BF16 = jnp.bfloat16

D_MODEL = 1024
HEAD_DIM = 64
ATT_WIDTH = 512
N_HEADS = 8
N_KV = 2
HG = 4
CMP_BLOCK = 32
CMP_STRIDE = 16
CMP_HID = 128
SLC_BLOCK = 64
N_SEL = 16
WINDOW = 512
FORCE_BONUS = 1e6
SSM_INNER = 512
SSM_HEADS = 8
SSM_GROUPS = 2
D_STATE = 64
SSM_CONV = 4
SSM_CHUNK = 128
CONV_DIM = 768
D_FF = 2816
FFN_CONV = 3
EPS = 1e-6
NEG = -1e30
SCALE = HEAD_DIM ** -0.5
LOG2E = 1.4426950408889634
SLOPES = tuple(2.0 ** (-(h + 1)) for h in range(N_HEADS))

LANES = 128
SUBLANES = 8
VMEM_LIMIT = 56 * 1024 * 1024

FF_CHUNKS = ((0, 768), (768, 1536), (1536, 2176), (2176, 2816))
ROW_PITCH = 24
TQ = 2 * SLC_BLOCK
NCOL = N_HEADS * TQ

P_Q = 0
P_KV = 512
P_Z = 1280
P_XBC = 1792
P_GD = 2560
P_W = 2688
GD_DT = 24
MASK_BIG = 2.0 ** 100


def _cparams(sem):
    return pltpu.CompilerParams(dimension_semantics=sem, vmem_limit_bytes=VMEM_LIMIT)


def _resident(shape):
    nd = len(shape)
    return pl.BlockSpec(shape, lambda *_: (0,) * nd, pipeline_mode=pl.Buffered(1))


def _wspec(w):
    if isinstance(w, tuple):
        arr, layer = w
        nd = arr.ndim
        return pl.BlockSpec((None,) + arr.shape[1:], lambda *_: (layer,) + (0,) * (nd - 1),
                            pipeline_mode=pl.Buffered(1))
    return _resident(w.shape)


def _warr(w):
    return w[0] if isinstance(w, tuple) else w


def _split3(x):
    hi = x.astype(BF16)
    r = x - hi.astype(F32)
    mid = r.astype(BF16)
    lo = (r - mid.astype(F32)).astype(BF16)
    return hi, mid, lo


def _dot(a, b):
    return jnp.dot(a, b, preferred_element_type=F32)


def _dot_nt(a, b):
    return lax.dot_general(a, b, (((1,), (1,)), ((), ())), preferred_element_type=F32)


def _expand(x, e):
    hi, mid, lo = _split3(x)
    return _dot(hi, e) + _dot(mid, e) + _dot(lo, e)


def _expand_l(e, x):
    hi, mid, lo = _split3(x)
    return _dot(e, hi) + _dot(e, mid) + _dot(e, lo)


def _silu(x):
    return x * jax.nn.sigmoid(x)


def _softplus(x):
    return jnp.maximum(x, 0.0) + jnp.log1p(jnp.exp(-jnp.abs(x)))


def _gelu_tanh(x):
    return 0.5 * x * (1.0 + jnp.tanh(np.sqrt(2.0 / np.pi) * (x + 0.044715 * (x * x * x))))


def _mod_kernel(c_ref, w_ref, b_ref, o_ref):
    c = c_ref[...]
    s = _silu(c).astype(BF16)
    o_ref[0] = _dot(s, w_ref[0].astype(BF16)) + b_ref[0]


def _mod(c_all, ada_w, ada_b):
    depth, d, n6 = ada_w.shape
    rows = c_all.shape[0]
    tn = 512
    return pl.pallas_call(
        _mod_kernel,
        out_shape=jax.ShapeDtypeStruct((depth, rows, n6), F32),
        grid=(depth, n6 // tn),
        in_specs=[pl.BlockSpec((rows, d), lambda l, j: (0, 0)),
                  pl.BlockSpec((1, d, tn), lambda l, j: (l, 0, j)),
                  pl.BlockSpec((1, 1, tn), lambda l, j: (l, 0, j))],
        out_specs=pl.BlockSpec((1, rows, tn), lambda l, j: (l, 0, j)),
        compiler_params=_cparams(("arbitrary", "arbitrary")),
        name="adaln_mod",
    )(c_all, ada_w, ada_b.reshape(depth, 1, n6))


def _mod_spec(mod, k):
    arr, layer, row0, per_row = mod
    if per_row:
        return pl.BlockSpec((None, 1, row0, D_MODEL), lambda b, t: (layer, 0, 0, k))
    return pl.BlockSpec((None, 1, 1, D_MODEL), lambda b, t: (layer, row0 + b, 0, k))


def _inproj_kernel(x_ref, sh_ref, sc_ref, g_ref, w_ref, *outs, tm, token_minor):
    x = x_ref[0]
    ms = jnp.mean(x * x, axis=-1, keepdims=True)
    xn = x * lax.rsqrt(ms + EPS) * g_ref[...]
    h = xn * (1.0 + sc_ref[0]) + sh_ref[0]
    proj = _dot(h.astype(BF16), w_ref[...])
    kv = [proj[:, P_KV + LANES * i:P_KV + LANES * (i + 1)] for i in range(6)]
    if token_minor:
        qt_ref, kc_ref, vc_ref, kvb_ref, vt_ref, z_ref, xbc_ref, gd_ref = outs[:8]
        qt_ref[0] = (proj[:, P_Q:P_Q + ATT_WIDTH] * (SCALE * LOG2E)).T.astype(BF16)
        kc_ref[0] = kv[0]
        vc_ref[0] = kv[1]
        kvt = [a.T for a in kv]
        for ref, a in zip(outs[8:], kvt):
            ref[0] = a
        for r in range(tm // TQ):
            cols = slice(r * TQ, (r + 1) * TQ)
            vt_ref[0, r] = jnp.concatenate([kvt[3][:, cols], kvt[5][:, cols]], axis=0).astype(BF16)
    else:
        q_ref, kvb_ref, z_ref, xbc_ref, gd_ref = outs[:5]
        q_ref[0] = (proj[:, P_Q:P_Q + ATT_WIDTH] * SCALE).astype(BF16)
        for ref, a in zip(outs[5:], kv):
            ref[0] = a
    kvb_ref[0] = proj[:, P_KV:P_Z].astype(BF16)
    z_ref[0] = proj[:, P_Z:P_XBC]
    xbc_ref[0] = proj[:, P_XBC:P_GD]
    gd_ref[0] = proj[:, P_GD:P_W]


def _inproj(x, mod, norm_g, w_packed, tm, token_minor):
    bsz, t, d = x.shape
    row = lambda w: pl.BlockSpec((1, tm, w), lambda b, i: (b, i, 0))
    col = lambda w: pl.BlockSpec((1, w, tm), lambda b, i: (b, 0, i))
    shp = lambda w, dt: jax.ShapeDtypeStruct((bsz, t, w), dt)
    shpt = lambda w, dt: jax.ShapeDtypeStruct((bsz, w, t), dt)
    if token_minor:
        out_shape = ([shpt(ATT_WIDTH, BF16), shp(LANES, F32), shp(LANES, F32), shp(6 * LANES, BF16),
                      jax.ShapeDtypeStruct((bsz, t // TQ, 2 * LANES, TQ), BF16),
                      shp(SSM_INNER, F32), shp(CONV_DIM, F32), shp(LANES, F32)] + [shpt(LANES, F32)] * 6)
        out_specs = ([col(ATT_WIDTH), row(LANES), row(LANES), row(6 * LANES),
                      pl.BlockSpec((1, tm // TQ, 2 * LANES, TQ), lambda b, i: (b, i, 0, 0)),
                      row(SSM_INNER), row(CONV_DIM), row(LANES)] + [col(LANES)] * 6)
    else:
        out_shape = ([shp(ATT_WIDTH, BF16), shp(6 * LANES, BF16), shp(SSM_INNER, F32), shp(CONV_DIM, F32),
                      shp(LANES, F32)] + [shp(LANES, F32)] * 6)
        out_specs = [row(ATT_WIDTH), row(6 * LANES), row(SSM_INNER), row(CONV_DIM), row(LANES)] + [row(LANES)] * 6
    return pl.pallas_call(
        functools.partial(_inproj_kernel, tm=tm, token_minor=token_minor),
        out_shape=out_shape,
        grid=(bsz, t // tm),
        in_specs=[row(d), _mod_spec(mod, 0), _mod_spec(mod, 1), _wspec(norm_g), _wspec(w_packed)],
        out_specs=out_specs,
        compiler_params=_cparams(("arbitrary", "arbitrary")),
        name="inproj",
    )(x, mod[0], mod[0], _warr(norm_g), _warr(w_packed))


def _compress_block(src_ref, nh, w1_ref, b1_ref, w2_ref, pitch=CMP_STRIDE):
    acc = jnp.zeros((nh, 4 * CMP_HID), F32)
    for i in range(CMP_STRIDE // 2):
        xa = src_ref[pl.ds(2 * i, nh, stride=pitch), :]
        xb = src_ref[pl.ds(2 * i + 1, nh, stride=pitch), :]
        acc = acc + _dot(jnp.concatenate([xa, xb], axis=1).astype(BF16), w1_ref[i])
    pa = acc[:, :2 * CMP_HID]
    pb = pltpu.roll(acc[:, 2 * CMP_HID:], nh - 1, 0)
    hid = _gelu_tanh(pa + pb + b1_ref[...])
    return _dot(hid.astype(BF16), w2_ref[...])


def _compress_kernel(k_ref, v_ref, w1k_ref, b1k_ref, w2k_ref, w1v_ref, b1v_ref, w2v_ref, ok_ref, ovt_ref, *, nh):
    ok_ref[0] = _compress_block(k_ref.at[0], nh, w1k_ref, b1k_ref, w2k_ref).astype(BF16)
    ovt_ref[0] = _compress_block(v_ref.at[0], nh, w1v_ref, b1v_ref, w2v_ref).T.astype(BF16)


def _compress(kc, vc, cw):
    bsz, t, _ = kc.shape
    nh = t // CMP_STRIDE
    seq = pl.BlockSpec((1, t, LANES), lambda b: (b, 0, 0))
    ws = [cw[n] for n in ("w1k", "b1k", "w2k", "w1v", "b1v", "w2v")]
    return pl.pallas_call(
        functools.partial(_compress_kernel, nh=nh),
        out_shape=[jax.ShapeDtypeStruct((bsz, nh, LANES), BF16), jax.ShapeDtypeStruct((bsz, LANES, nh), BF16)],
        grid=(bsz,),
        in_specs=[seq, seq] + [_wspec(w) for w in ws],
        out_specs=[pl.BlockSpec((1, nh, LANES), lambda b: (b, 0, 0)), pl.BlockSpec((1, LANES, nh), lambda b: (b, 0, 0))],
        compiler_params=_cparams(("arbitrary",)),
        name="compress",
    )(kc, vc, *[_warr(w) for w in ws])


def _all_sublanes(x, op):
    for shift in (4, 2, 1):
        x = op(x, pltpu.roll(x, shift, 0))
    return x


def _query_blockdiag(qt):
    zero = jnp.zeros((HEAD_DIM, HG * TQ), BF16)
    per_g = [jnp.concatenate([qt[HEAD_DIM * (HG * g + j):HEAD_DIM * (HG * g + j + 1), :] for j in range(HG)], axis=1)
             for g in range(N_KV)]
    return jnp.concatenate([jnp.concatenate([per_g[0], zero], axis=1),
                            jnp.concatenate([zero, per_g[1]], axis=1)], axis=0)


def _topk_mask_t(score, jrow, n_slc):
    rank = jnp.zeros(score.shape, F32)
    for jp in range(n_slc):
        sj = score[jp:jp + 1, :]
        ahead = (sj > score) | ((sj == score) & (jp < jrow))
        rank = rank + ahead.astype(F32)
    return (rank < N_SEL) & (jrow < n_slc)


def _cmp_sel_kernel(qt_ref, kc_ref, vct_ref, ot_ref, c0_ref, srow_ref, ocmpt_ref, selt_ref, *, n_cmp, n_slc):
    t = pl.program_id(1)
    qbd = _query_blockdiag(qt_ref[0])
    kc = kc_ref[0]
    nl = kc.shape[0]
    srow = srow_ref[...]
    s = _dot(kc, qbd) - (c0_ref[...] + srow * (t * TQ).astype(F32))
    nidx = lax.broadcasted_iota(jnp.int32, (nl, NCOL), 0)
    qpos = t * TQ + lax.broadcasted_iota(jnp.int32, (nl, NCOL), 1) % TQ
    mask = (qpos - (nidx * CMP_STRIDE + (CMP_BLOCK - 1)) >= 0) & (nidx < n_cmp)
    sm = jnp.where(mask, s, NEG)
    e = jnp.exp2(sm - jnp.max(sm, axis=0, keepdims=True))
    p = jnp.where(mask, e * (1.0 / jnp.sum(e, axis=0, keepdims=True)), 0.0)
    pb = p.astype(BF16)
    vct = vct_ref[0]

    js = selt_ref.shape[2]
    jrow = lax.broadcasted_iota(jnp.int32, (js, TQ), 0)
    qp = t * TQ + lax.broadcasted_iota(jnp.int32, (js, TQ), 1)
    cur = qp // SLC_BLOCK
    forced = (jrow == 0) | (jrow == cur) | (jrow == cur - 1)
    valid = (jrow * SLC_BLOCK <= qp)
    for g in range(N_KV):
        c0 = g * HG * TQ
        og = _dot(vct[HEAD_DIM * g:HEAD_DIM * (g + 1), :], pb[:, c0:c0 + HG * TQ])
        pg = jnp.zeros((nl, TQ), F32)
        for j in range(HG):
            h = HG * g + j
            ocmpt_ref[0, HEAD_DIM * h:HEAD_DIM * (h + 1), :] = og[:, j * TQ:(j + 1) * TQ]
            pg = pg + p[:, c0 + j * TQ:c0 + (j + 1) * TQ]
        imp_t = _expand_l(ot_ref[...], pg)
        score = jnp.where(valid, imp_t + jnp.where(forced, FORCE_BONUS, 0.0), NEG)
        selt_ref[0, g] = _topk_mask_t(score, jrow, n_slc).astype(F32)


def _cmp_sel(qt, kcmp, vcmpt, consts):
    bsz, _, t = qt.shape
    nl = kcmp.shape[1]
    n_slc = t // SLC_BLOCK
    js = consts["ot"].shape[0]
    return pl.pallas_call(
        functools.partial(_cmp_sel_kernel, n_cmp=t // CMP_STRIDE - 1, n_slc=n_slc),
        out_shape=[jax.ShapeDtypeStruct((bsz, ATT_WIDTH, t), F32),
                   jax.ShapeDtypeStruct((bsz, N_KV, js, t), F32)],
        grid=(bsz, t // TQ),
        in_specs=[pl.BlockSpec((1, ATT_WIDTH, TQ), lambda b, i: (b, 0, i)),
                  pl.BlockSpec((1, nl, LANES), lambda b, i: (b, 0, 0)),
                  pl.BlockSpec((1, LANES, nl), lambda b, i: (b, 0, 0)),
                  _resident(consts["ot"].shape), _resident(consts["c0"].shape), _resident(consts["srow2"].shape)],
        out_specs=[pl.BlockSpec((1, ATT_WIDTH, TQ), lambda b, i: (b, 0, i)),
                   pl.BlockSpec((1, N_KV, js, TQ), lambda b, i: (b, 0, 0, i))],
        compiler_params=_cparams(("arbitrary", "arbitrary")),
        name="cmp_select",
    )(qt, kcmp, vcmpt, consts["ot"], consts["c0"], consts["srow2"])


def _attn_kernel(qt_ref, kvb_ref, vt_ref, selt_ref, ocmpt_ref, gd_ref, fk_ref, coef_ref, amask_ref, egt_ref, o_ref,
                 pen_ref, sbuf, *state):
    nchunk = len(state) // 6
    states = [tuple(state[(3 * br + k) * nchunk:(3 * br + k + 1) * nchunk] for k in range(3)) for br in range(2)]
    t = pl.program_id(1)
    qbd = _query_blockdiag(qt_ref[0])
    gw = HG * TQ
    for g in range(N_KV):
        pen = (selt_ref[0, g] - 1.0) * MASK_BIG
        pen_ref[:, g * gw:(g + 1) * gw] = jnp.concatenate([pen] * HG, axis=1)
    nwin = WINDOW // TQ
    coef = coef_ref[...]
    zero8 = jnp.zeros((SUBLANES, NCOL), F32)
    cw = 2 * TQ
    zpad = jnp.zeros((LANES - 2 * SUBLANES, cw), BF16)
    rid = lax.broadcasted_iota(jnp.int32, (SUBLANES, NCOL), 0)

    def query_operands(c8):
        c16 = jnp.concatenate([c8, zero8], axis=0).astype(BF16)
        return [jnp.concatenate([qbd[:, c * cw:(c + 1) * cw], c16[:, c * cw:(c + 1) * cw], zpad], axis=0)
                for c in range(NCOL // cw)]

    q_win = query_operands(coef)

    def score_tile(kt, slot, kcol, slc, mask):
        r0 = pl.multiple_of(kt * TQ, TQ)
        k_aug = jnp.concatenate([kvb_ref[0, pl.ds(r0, TQ), kcol:kcol + LANES], fk_ref[kt]], axis=1)
        if slc:
            pa = jnp.broadcast_to(pen_ref[pl.ds(2 * kt, 1), :], (SUBLANES, NCOL))
            pb = jnp.broadcast_to(pen_ref[pl.ds(2 * kt + 1, 1), :], (SUBLANES, NCOL))
            q_aug = query_operands(jnp.where(rid == 6, pa, jnp.where(rid == 7, pb, coef)))
        else:
            q_aug = q_win
        for c in range(nchunk):
            cols = slice(c * cw, (c + 1) * cw)
            sbuf[slot, :, cols] = _dot(k_aug, q_aug[c]) + amask_ref[mask, :, cols]

    def consume_tile(kt, slot, vrow, st):
        m_refs, l_refs, acc_refs = st
        v_t = vt_ref[0, kt, vrow:vrow + LANES, :]
        for c in range(nchunk):
            cols = slice(c * cw, (c + 1) * cw)
            g = (c * cw) // gw
            s = sbuf[slot, :, cols].reshape(TQ // SUBLANES, SUBLANES, cw)
            m_old = m_refs[c][...]
            m_new = jnp.maximum(m_old, _all_sublanes(jnp.max(s, axis=0), jnp.maximum))
            alpha = jnp.exp2(m_old - m_new)
            p = jnp.exp2(s - m_new[None])
            l_refs[c][...] = alpha * l_refs[c][...] + _all_sublanes(jnp.sum(p, axis=0), jnp.add)
            m_refs[c][...] = m_new
            pv = _dot(v_t[HEAD_DIM * g:HEAD_DIM * (g + 1), :], p.reshape(TQ, cw).astype(BF16))
            acc = acc_refs[c][...].reshape(HEAD_DIM // SUBLANES, SUBLANES, cw) * alpha[None]
            acc_refs[c][...] = acc.reshape(HEAD_DIM, cw) + pv

    def reset(st):
        m_refs, l_refs, acc_refs = st
        for c in range(nchunk):
            m_refs[c][...] = jnp.full(m_refs[c].shape, NEG, F32)
            l_refs[c][...] = jnp.zeros(l_refs[c].shape, F32)
            acc_refs[c][...] = jnp.zeros(acc_refs[c].shape, F32)

    def result(st):
        _, l_refs, acc_refs = st
        per_chunk = [(acc_refs[c][...].reshape(HEAD_DIM // SUBLANES, SUBLANES, cw)
                      * (1.0 / l_refs[c][...])[None]).reshape(HEAD_DIM, cw) for c in range(nchunk)]
        half = nchunk // N_KV
        return [jnp.concatenate(per_chunk[g * half:(g + 1) * half], axis=1) for g in range(N_KV)]

    def stream(br, first, kcol, vrow, slc):
        n = t - first + 1
        pad = n % 2

        def tile_of(i):
            return jnp.maximum(first + i - pad, first)

        def mask_of(i):
            kt = first + i - pad
            diag = jnp.where(kt == t, 1, 0)
            m = diag if slc else jnp.where(kt == t - nwin, 2, diag)
            return jnp.where(i < pad, 3, m)

        def score(i, par):
            score_tile(tile_of(i), 2 * br + par, kcol, slc, mask_of(i))

        def consume(i, par):
            consume_tile(tile_of(i), 2 * br + par, vrow, states[br])

        return dict(total=n + pad, st=states[br], score=score, consume=consume)

    slc_s = stream(0, 0, 2 * LANES, 0, True)
    win_s = stream(1, jnp.maximum(t - nwin, 0), 4 * LANES, LANES, False)
    streams = (slc_s, win_s)
    for sm in streams:
        reset(sm["st"])
        sm["score"](0, 0)

    def pair_step(j, active):
        i = 2 * j
        for sm in active:
            sm["score"](i + 1, 1)
        for sm in active:
            sm["consume"](i, 0)
        for sm in active:
            sm["score"](i + 2, 0)
        for sm in active:
            sm["consume"](i + 1, 1)

    def both_body(j, c):
        pair_step(j, (slc_s, win_s))
        return c

    def slc_body(j, c):
        pair_step(j, (slc_s,))
        return c

    merged = win_s["total"] // 2 - 1
    lax.fori_loop(0, merged, both_body, 0)
    lax.fori_loop(merged, slc_s["total"] // 2 - 1, slc_body, 0)
    for sm in streams:
        sm["score"](sm["total"] - 1, 1)
    for sm in streams:
        sm["consume"](sm["total"] - 2, 0)
    for sm in streams:
        sm["consume"](sm["total"] - 1, 1)
    o_slc = result(slc_s["st"])
    o_win = result(win_s["st"])

    g_t = jax.nn.sigmoid(gd_ref[0]).T
    gates = [_expand_l(egt_ref[br], g_t) for br in range(3)]
    for pr in range(N_HEADS // 2):
        g, j0 = pr // 2, 2 * (pr % 2)
        rows = slice(LANES * pr, LANES * (pr + 1))
        pair = lambda o: jnp.concatenate([o[g][:, j0 * TQ:(j0 + 1) * TQ], o[g][:, (j0 + 1) * TQ:(j0 + 2) * TQ]], axis=0)
        tot = gates[0][rows] * ocmpt_ref[0, rows, :] + gates[1][rows] * pair(o_slc) + gates[2][rows] * pair(o_win)
        o_ref[0, :, rows] = tot.T.astype(BF16)


def _attn(qt, kvb, vt, selt, ocmpt, gd, consts):
    bsz, _, t = qt.shape
    js = selt.shape[2]
    return pl.pallas_call(
        _attn_kernel,
        out_shape=jax.ShapeDtypeStruct((bsz, t, ATT_WIDTH), BF16),
        grid=(bsz, t // TQ),
        in_specs=[pl.BlockSpec((1, ATT_WIDTH, TQ), lambda b, i: (b, 0, i)),
                  pl.BlockSpec((1, t, 6 * LANES), lambda b, i: (b, 0, 0)),
                  pl.BlockSpec((1, t // TQ, 2 * LANES, TQ), lambda b, i: (b, 0, 0, 0)),
                  pl.BlockSpec((1, N_KV, js, TQ), lambda b, i: (b, 0, 0, i)),
                  pl.BlockSpec((1, ATT_WIDTH, TQ), lambda b, i: (b, 0, i)),
                  pl.BlockSpec((1, TQ, LANES), lambda b, i: (b, i, 0)),
                  _resident(consts["fk"].shape), _resident(consts["coef"].shape), _resident(consts["amask"].shape),
                  _resident(consts["e_gate_t"].shape)],
        out_specs=pl.BlockSpec((1, TQ, ATT_WIDTH), lambda b, i: (b, i, 0)),
        scratch_shapes=([pltpu.VMEM((js, NCOL), F32), pltpu.VMEM((4, TQ, NCOL), F32)]
                        + ([pltpu.VMEM((SUBLANES, 2 * TQ), F32)] * (2 * N_HEADS // 2)
                           + [pltpu.VMEM((HEAD_DIM, 2 * TQ), F32)] * (N_HEADS // 2)) * 2),
        compiler_params=_cparams(("arbitrary", "arbitrary")),
        name="slc_win_attn",
    )(qt, kvb, vt, selt, ocmpt, gd, consts["fk"], consts["coef"], consts["amask"], consts["e_gate_t"])


def _gated_norm(y, z, ng):
    yz = y * _silu(z)
    half = SSM_INNER // SSM_GROUPS
    outs = []
    for g in range(SSM_GROUPS):
        part = yz[:, half * g:half * (g + 1)]
        ms = jnp.mean(part * part, axis=-1, keepdims=True)
        outs.append(part * lax.rsqrt(ms + EPS))
    return jnp.concatenate(outs, axis=1) * ng


def _ssd_kernel(xbc_ref, z_ref, gd_ref, cw_ref, cb_ref, dtb_ref, a_ref, dsk_ref, ng_ref, edt_ref,
                y_ref, hfin_ref, ubuf, sstate, ybuf, *, lc):
    t = pl.program_id(1)
    nt = pl.num_programs(1)

    @pl.when(t == 0)
    def _():
        ubuf[0:SUBLANES, :] = jnp.zeros((SUBLANES, CONV_DIM), F32)
        sstate[...] = jnp.zeros(sstate.shape, F32)

    ubuf[SUBLANES:SUBLANES + lc, :] = xbc_ref[0]
    conv = cb_ref[...] + jnp.zeros((lc, CONV_DIM), F32)
    for k in range(SSM_CONV):
        off = SUBLANES - (SSM_CONV - 1) + k
        conv = conv + ubuf[off:off + lc, :] * cw_ref[k:k + 1, :]
    ubuf[0:SUBLANES, :] = ubuf[lc:lc + SUBLANES, :]
    act = _silu(conv)
    xm = act[:, :SSM_INNER]
    bm = act[:, SSM_INNER:SSM_INNER + LANES]
    cm = act[:, SSM_INNER + LANES:]

    dt = _softplus(gd_ref[0] + dtb_ref[...])
    a = dt * a_ref[...]
    ri = lax.broadcasted_iota(jnp.int32, (lc, lc), 0)
    ci = lax.broadcasted_iota(jnp.int32, (lc, lc), 1)
    tril = ri >= ci
    cs = jnp.dot(tril.astype(F32), a, preferred_element_type=F32, precision=lax.Precision.HIGHEST)
    cs_t = cs.T
    edt = edt_ref[...]
    dt_x = _expand(dt, edt)
    cs_x = _expand(cs, edt)
    cs_last = cs_x[lc - 1:lc, :]
    bm_t = bm.T
    xd = xm * dt_x
    xw = xm * (jnp.exp(cs_last - cs_x) * dt_x)
    s_old = sstate[...]
    half = SSM_INNER // SSM_GROUPS
    for g in range(SSM_GROUPS):
        bg = bm[:, D_STATE * g:D_STATE * (g + 1)].astype(BF16)
        cg = cm[:, D_STATE * g:D_STATE * (g + 1)].astype(BF16)
        cb = _dot_nt(cg, bg)
        for j in range(SSM_HEADS // SSM_GROUPS):
            h = g * (SSM_HEADS // SSM_GROUPS) + j
            col = cs[:, GD_DT + h:GD_DT + h + 1]
            row = cs_t[GD_DT + h:GD_DT + h + 1, :]
            lm = jnp.exp(jnp.where(tril, col - row, NEG))
            lo, hi = HEAD_DIM * h, HEAD_DIM * (h + 1)
            ybuf[:, lo:hi] = _dot((cb * lm).astype(BF16), xd[:, lo:hi].astype(BF16))
        sg = s_old[:, half * g:half * (g + 1)]
        y_off = _dot(cg, sg.astype(BF16))
        ybuf[:, half * g:half * (g + 1)] = ybuf[:, half * g:half * (g + 1)] + y_off * jnp.exp(cs_x[:, half * g:half * (g + 1)])
        st = _dot(bm_t[D_STATE * g:D_STATE * (g + 1), :].astype(BF16), xw[:, half * g:half * (g + 1)].astype(BF16))
        sstate[:, half * g:half * (g + 1)] = sg * jnp.exp(cs_last[:, half * g:half * (g + 1)]) + st

    y = ybuf[...] + dsk_ref[...] * xm
    y_ref[0] = _gated_norm(y, z_ref[0], ng_ref[...]).astype(BF16)

    @pl.when(t == nt - 1)
    def _():
        s_pad = jnp.concatenate([sstate[...], jnp.zeros((LANES - D_STATE, SSM_INNER), F32)], axis=0)
        hfin_ref[0] = s_pad.T[:, :D_STATE]


def _ssd(xbc, z, gd, sw, consts):
    bsz, t, _ = xbc.shape
    lc = SSM_CHUNK
    tile = lambda w: pl.BlockSpec((1, lc, w), lambda b, i: (b, i, 0))
    ws = [sw[n] for n in ("conv_w", "conv_b", "dt_bias", "a", "d_skip", "norm_g")] + [consts["e_dt"]]
    y, hfin = pl.pallas_call(
        functools.partial(_ssd_kernel, lc=lc),
        out_shape=[jax.ShapeDtypeStruct((bsz, t, SSM_INNER), BF16),
                   jax.ShapeDtypeStruct((bsz, SSM_INNER, D_STATE), F32)],
        grid=(bsz, t // lc),
        in_specs=[tile(CONV_DIM), tile(SSM_INNER), tile(LANES)] + [_wspec(w) for w in ws],
        out_specs=[tile(SSM_INNER), pl.BlockSpec((1, SSM_INNER, D_STATE), lambda b, i: (b, 0, 0))],
        scratch_shapes=[pltpu.VMEM((lc + SUBLANES, CONV_DIM), F32),
                        pltpu.VMEM((D_STATE, SSM_INNER), F32),
                        pltpu.VMEM((lc, SSM_INNER), F32)],
        compiler_params=_cparams(("arbitrary", "arbitrary")),
        name="ssd_scan",
    )(xbc, z, gd, *[_warr(w) for w in ws])
    return y, hfin.reshape(bsz, SSM_HEADS, HEAD_DIM, D_STATE)


def _mix_ffn_kernel(*refs, tm, carry_conv, last):
    if carry_conv:
        (x_ref, oa_ref, ys_ref, g1_ref, sh2_ref, sc2_ref, g2_ref, n2_ref, wo_ref, wu_ref, fcw_ref, fcb_ref,
         wd_ref, fg_ref, out_ref, ug_ref, ubuf) = refs
    else:
        (x_ref, oa_ref, ys_ref, g1_ref, sh2_ref, sc2_ref, g2_ref, n2_ref, wo_ref, wu_ref, fcw_ref, fcb_ref,
         wd_ref, fg_ref, p0_ref, p1_ref, out_ref, ug_ref) = refs
    t = pl.program_id(1)
    nt = pl.num_programs(1)
    x = x_ref[0]
    mix = _dot(oa_ref[0], wo_ref[:ATT_WIDTH, :]) + _dot(ys_ref[0], wo_ref[ATT_WIDTH:, :])
    x1 = x + g1_ref[0] * mix
    ms = jnp.mean(x1 * x1, axis=-1, keepdims=True)
    h2 = (x1 * lax.rsqrt(ms + EPS) * n2_ref[...]) * (1.0 + sc2_ref[0]) + sh2_ref[0]
    h2 = h2.astype(BF16)
    if carry_conv:
        @pl.when(t == 0)
        def _():
            ubuf[0:SUBLANES, :] = jnp.zeros((SUBLANES, D_FF), F32)

    def up(c):
        lo, hi = FF_CHUNKS[c]
        return _dot(h2, wu_ref[:, lo:hi]), _dot(h2, wu_ref[:, D_FF + lo:D_FF + hi])

    nxt = up(0)
    down = None
    for c, (lo, hi) in enumerate(FF_CHUNKS):
        ug, uv = nxt
        if c + 1 < len(FF_CHUNKS):
            nxt = up(c + 1)
        if carry_conv:
            ubuf[SUBLANES:SUBLANES + tm, lo:hi] = ug
            u1 = ubuf[SUBLANES - 1:SUBLANES - 1 + tm, lo:hi]
            u2 = ubuf[SUBLANES - 2:SUBLANES - 2 + tm, lo:hi]
        else:
            u1 = p1_ref[0, :, lo:hi]
            u2 = p0_ref[0, :, lo:hi]
            ug_ref[0, :, lo:hi] = ug
        ugc = (fcb_ref[:, lo:hi] + u2 * fcw_ref[0:1, lo:hi] + u1 * fcw_ref[1:2, lo:hi] + ug * fcw_ref[2:3, lo:hi])
        d = _dot((_silu(ugc) * uv).astype(BF16), wd_ref[lo:hi, :])
        down = d if down is None else down + d
    if carry_conv:
        @pl.when(t == nt - 1)
        def _():
            ug_ref[0] = ubuf[tm:tm + SUBLANES, :]

        ubuf[0:SUBLANES, :] = ubuf[tm:tm + SUBLANES, :]
    x2 = x1 + g2_ref[0] * down
    if last:
        ms2 = jnp.mean(x2 * x2, axis=-1, keepdims=True)
        out_ref[0] = x2 * lax.rsqrt(ms2 + EPS) * fg_ref[...]
    else:
        out_ref[0] = x2


def _mix_ffn(x, oatt, yssm, mod, fw, tm, prev, last):
    bsz, t, d = x.shape
    carry = prev is None
    row = lambda w: pl.BlockSpec((1, tm, w), lambda b, i: (b, i, 0))
    ws = [fw[n] for n in ("norm2_g", "w_out", "w_up", "conv_w", "conv_b", "w_down", "final_g")]
    in_specs = ([row(d), row(ATT_WIDTH), row(SSM_INNER)] + [_mod_spec(mod, k) for k in (2, 3, 4, 5)]
                + [_wspec(w) for w in ws])
    args = [x, oatt, yssm] + [mod[0]] * 4 + [_warr(w) for w in ws]
    if carry:
        ug_shape, ug_spec = (bsz, SUBLANES, D_FF), pl.BlockSpec((1, SUBLANES, D_FF), lambda b, i: (b, 0, 0))
        scratch = [pltpu.VMEM((tm + SUBLANES, D_FF), F32)]
    else:
        ug_shape, ug_spec = (bsz, t, D_FF), row(D_FF)
        scratch = []
        in_specs += [row(D_FF), row(D_FF)]
        args += list(prev)
    return pl.pallas_call(
        functools.partial(_mix_ffn_kernel, tm=tm, carry_conv=carry, last=last),
        out_shape=[jax.ShapeDtypeStruct((bsz, t, d), F32), jax.ShapeDtypeStruct(ug_shape, F32)],
        grid=(bsz, t // tm),
        in_specs=in_specs,
        out_specs=[row(d), ug_spec],
        scratch_shapes=scratch,
        compiler_params=_cparams(("arbitrary", "arbitrary")),
        name="mix_ffn",
    )(*args)


def _head_rows(row512):
    r = lax.broadcasted_iota(jnp.int32, (N_HEADS, ATT_WIDTH), 0)
    c = lax.broadcasted_iota(jnp.int32, (N_HEADS, ATT_WIDTH), 1)
    return jnp.where(c // HEAD_DIM == r, jnp.broadcast_to(row512, (N_HEADS, ATT_WIDTH)), 0.0)


def _head_diag(x8):
    r = lax.broadcasted_iota(jnp.int32, (N_HEADS, ATT_WIDTH), 0)
    c = lax.broadcasted_iota(jnp.int32, (N_HEADS, ATT_WIDTH), 1)
    return jnp.sum(jnp.where(c // HEAD_DIM == r, x8, 0.0), axis=0, keepdims=True)


def _slope_col():
    r = lax.broadcasted_iota(jnp.int32, (N_HEADS, 1), 0)
    s = jnp.zeros((N_HEADS, 1), F32)
    for h in range(N_HEADS):
        s = jnp.where(r == h, SLOPES[h], s)
    return s


def _s_cmp_kernel(pt_ref, q_ref, kvn_ref, gd_ref, ck_hbm, cv_hbm,
                  w1k_ref, b1k_ref, w2k_ref, w1v_ref, b1v_ref, w2v_ref, ov_ref, u_ref, e0_ref,
                  ocmp_ref, idx_ref, stage_k, stage_v, kbuf, vbuf, sem,
                  *, layer, n_pages, page, nh8, nhp, q_pos, n_slc):
    b = pl.program_id(0)
    nb = pl.num_programs(0)
    past = n_pages * page

    def copies(bb, p, slot):
        pg = pt_ref[bb * n_pages + p]
        return (pltpu.make_async_copy(ck_hbm.at[layer, pg], stage_k.at[slot, p], sem.at[0, slot]),
                pltpu.make_async_copy(cv_hbm.at[layer, pg], stage_v.at[slot, p], sem.at[1, slot]))

    unroll = 8 if n_pages % 8 == 0 else 1

    def issue(bb, slot):
        def body(i, c):
            for u in range(unroll):
                for cp in copies(bb, i * unroll + u, slot):
                    cp.start()
            return c
        lax.fori_loop(0, n_pages // unroll, body, 0)

    @pl.when(b == 0)
    def _():
        issue(0, 0)

    @pl.when(b + 1 < nb)
    def _():
        issue(b + 1, (b + 1) % 2)

    slot = b % 2

    def wait(i, c):
        for u in range(unroll):
            for cp in copies(b, i * unroll + u, slot):
                cp.wait()
        return c

    lax.fori_loop(0, n_pages // unroll, wait, 0)

    hpp = page // CMP_STRIDE

    def untranspose(i, c):
        for u in range(unroll):
            p = i * unroll + u
            base = pl.multiple_of(p * (hpp * ROW_PITCH), SUBLANES)
            kp = stage_k[slot, p].T
            vp = stage_v[slot, p].T
            for n in range(hpp):
                rows = pl.ds(base + n * ROW_PITCH, CMP_STRIDE)
                kbuf[rows, :] = kp[n * CMP_STRIDE:(n + 1) * CMP_STRIDE]
                vbuf[rows, :] = vp[n * CMP_STRIDE:(n + 1) * CMP_STRIDE]
        return c

    lax.fori_loop(0, n_pages // unroll, untranspose, 0)
    first_new = (past // CMP_STRIDE) * ROW_PITCH
    tail = nh8 * ROW_PITCH - first_new
    kbuf[first_new:, :] = jnp.zeros((tail, LANES), F32)
    vbuf[first_new:, :] = jnp.zeros((tail, LANES), F32)
    kbuf[first_new:first_new + 1, :] = kvn_ref[0, :, 0:LANES]
    vbuf[first_new:first_new + 1, :] = kvn_ref[0, :, LANES:2 * LANES]

    pad = jnp.zeros((nhp - nh8, ATT_WIDTH), F32)
    kc4 = jnp.concatenate([_compress_block(kbuf, nh8, w1k_ref, b1k_ref, w2k_ref, ROW_PITCH), pad], axis=0).astype(BF16)
    vc4 = jnp.concatenate([_compress_block(vbuf, nh8, w1v_ref, b1v_ref, w2v_ref, ROW_PITCH), pad], axis=0).astype(BF16)
    n_cmp = (past + 1 + CMP_STRIDE - 1) // CMP_STRIDE - 1

    qm = _head_rows(q_ref[0].astype(F32)).astype(BF16)
    nidx = lax.broadcasted_iota(jnp.int32, (N_HEADS, nhp), 1)
    dist_i = q_pos - (nidx * CMP_STRIDE + (CMP_BLOCK - 1))
    mask = (dist_i >= 0) & (nidx < n_cmp)
    s = _dot_nt(qm, kc4) - _slope_col() * dist_i.astype(F32)
    sm = jnp.where(mask, s, NEG)
    e = jnp.exp(sm - jnp.max(sm, axis=-1, keepdims=True))
    p = jnp.where(mask, e / jnp.sum(e, axis=-1, keepdims=True), 0.0)
    o = _head_diag(_dot(p.astype(BF16), vc4))
    g8 = jnp.broadcast_to(jax.nn.sigmoid(gd_ref[0]), (SUBLANES, LANES))
    gate = _expand(g8, e0_ref[...])[0:1, :]
    ocmp_ref[0] = gate * o

    js = ov_ref.shape[1]
    imp8 = _expand(p, ov_ref[...])
    hrow = lax.broadcasted_iota(jnp.int32, (N_HEADS, js), 0)
    jl = lax.broadcasted_iota(jnp.int32, (1, js), 1)
    cur = q_pos // SLC_BLOCK
    forced = (jl == 0) | (jl == cur) | (jl == cur - 1)
    valid = (jl * SLC_BLOCK <= q_pos) & (jl < n_slc)
    rj = lax.broadcasted_iota(jnp.int32, (js, js), 0)
    cj = lax.broadcasted_iota(jnp.int32, (js, js), 1)
    kk = lax.broadcasted_iota(jnp.int32, (2 * SUBLANES, js), 0)
    jvals = jnp.broadcast_to(jl.astype(F32), (SUBLANES, js)).astype(BF16)
    for g in range(N_KV):
        imp = jnp.sum(jnp.where(hrow // HG == g, imp8, 0.0), axis=0, keepdims=True)
        score = jnp.where(valid, imp + jnp.where(forced, FORCE_BONUS, 0.0), NEG)
        sb = jnp.broadcast_to(score, (js, js))
        col = jnp.sum(jnp.where(rj == cj, sb, 0.0), axis=1, keepdims=True)
        ahead = (col > sb) | ((col == sb) & (rj < cj))
        rank = jnp.sum(ahead.astype(F32), axis=0, keepdims=True)
        sel = ((rank < N_SEL) & (jl < n_slc)).astype(F32)
        pos = _dot(jnp.broadcast_to(sel, (SUBLANES, js)).astype(BF16), u_ref[...])[0:1, :]
        onehot = ((jnp.broadcast_to(pos, (2 * SUBLANES, js)) == kk.astype(F32))
                  & (jnp.broadcast_to(sel, (2 * SUBLANES, js)) > 0.5)).astype(BF16)
        idx = _dot_nt(jvals, onehot)[0:1, :]
        idx_ref[0, g:g + 1, :] = idx.astype(jnp.int32)


def _s_cmp(layer, page_table, q, kvn, gd, cache_k, cache_v, cw4, consts, q_pos):
    ns, n_pages = page_table.shape
    page = cache_k.shape[3]
    past = n_pages * page
    nh = (past + 1 + CMP_STRIDE - 1) // CMP_STRIDE
    nh8 = -(-nh // SUBLANES) * SUBLANES
    nhp = -(-nh // LANES) * LANES
    n_slc = past // SLC_BLOCK + 1
    ws = [cw4[n] for n in ("w1k", "b1k", "w2k", "w1v", "b1v", "w2v")] + [consts["ov_s"], consts["u_s"], consts["e_gate"][0]]
    full = lambda a: pl.BlockSpec(a.shape, lambda b, pt: (0,) * a.ndim)
    per_b = lambda a: pl.BlockSpec((1,) + a.shape[1:], lambda b, pt: (b,) + (0,) * (a.ndim - 1))
    grid_spec = pltpu.PrefetchScalarGridSpec(
        num_scalar_prefetch=1,
        grid=(ns,),
        in_specs=[per_b(q), per_b(kvn), per_b(gd), pl.BlockSpec(memory_space=pl.ANY), pl.BlockSpec(memory_space=pl.ANY)]
                 + [_wspec(w) for w in ws],
        out_specs=[pl.BlockSpec((1, 1, ATT_WIDTH), lambda b, pt: (b, 0, 0)),
                   pl.BlockSpec((1, N_KV, N_SEL), lambda b, pt: (b, 0, 0))],
        scratch_shapes=[pltpu.VMEM((2, n_pages, LANES, page), F32), pltpu.VMEM((2, n_pages, LANES, page), F32),
                        pltpu.VMEM((nh8 * ROW_PITCH, LANES), F32), pltpu.VMEM((nh8 * ROW_PITCH, LANES), F32),
                        pltpu.SemaphoreType.DMA((2, 2))],
    )
    return pl.pallas_call(
        functools.partial(_s_cmp_kernel, layer=layer, n_pages=n_pages, page=page, nh8=nh8, nhp=nhp, q_pos=q_pos,
                          n_slc=n_slc),
        out_shape=[jax.ShapeDtypeStruct((ns, 1, ATT_WIDTH), F32), jax.ShapeDtypeStruct((ns, N_KV, N_SEL), jnp.int32)],
        grid_spec=grid_spec,
        compiler_params=_cparams(("arbitrary",)),
        name="sample_cmp_select",
    )(page_table.reshape(-1), q, kvn, gd, cache_k, cache_v, *[_warr(w) for w in ws])


def _s_slc_win_kernel(idx_ref, pt_ref, q_ref, kvn_ref, gd_ref, ocmp_ref, wk_ref, wv_ref, sk_hbm, sv_hbm,
                      tile_ref, e1_ref, e2_ref, o_ref, wko_ref, wvo_ref, kb, vb, sem,
                      *, layer, n_pages, page, q_pos, wb):
    b = pl.program_id(0)
    nb = pl.num_programs(0)
    bpp = page // SLC_BLOCK
    n_past = n_pages * bpp
    nblk = N_KV * N_SEL
    blks = [idx_ref[b * nblk + i] for i in range(nblk)]
    slot = b % 2

    def copies(bb, sl, i):
        blk = idx_ref[bb * nblk + i]
        pg = pt_ref[bb * n_pages + jnp.minimum(blk, n_past - 1) // bpp]
        dst = slice(i * page, (i + 1) * page)
        return (pltpu.make_async_copy(sk_hbm.at[layer, pg], kb.at[sl, :, dst], sem.at[0, sl]),
                pltpu.make_async_copy(sv_hbm.at[layer, pg], vb.at[sl, :, dst], sem.at[1, sl]))

    def issue(bb, sl):
        for i in range(nblk):
            for cp in copies(bb, sl, i):
                cp.start()

    @pl.when(b == 0)
    def _():
        issue(0, 0)

    @pl.when(b + 1 < nb)
    def _():
        issue(b + 1, (b + 1) % 2)

    tile_m = tile_ref[...]
    q8 = _dot_nt(_head_rows(q_ref[0].astype(F32)).astype(BF16), tile_m).astype(BF16)
    slope = _slope_col()
    g8 = jnp.broadcast_to(jax.nn.sigmoid(gd_ref[0]), (SUBLANES, LANES))
    gate1 = _expand(g8, e1_ref[...])[0:1, :]
    gate2 = _expand(g8, e2_ref[...])[0:1, :]
    rgrp = lax.broadcasted_iota(jnp.int32, (N_HEADS, 1), 0) // HG

    def new_key(lane0):
        kn = kvn_ref[0, :, lane0:lane0 + LANES].astype(BF16).astype(F32)
        vn = kvn_ref[0, :, lane0 + LANES:lane0 + 2 * LANES].astype(BF16).astype(F32)
        return jnp.sum(q8.astype(F32) * kn, axis=-1, keepdims=True), vn

    def finish(e, e_new, v_t, v_new, l):
        o8 = (_dot_nt(e.astype(BF16), v_t.astype(BF16)) + e_new * v_new) / l
        return _head_diag(_expand(o8, tile_m))

    def shifted(w_t, new_row):
        col = jnp.broadcast_to(new_row, (LANES, LANES)).T[:, 0:1]
        lane = lax.broadcasted_iota(jnp.int32, (LANES, wb), 1)
        return jnp.where(lane == wb - 1, col, pltpu.roll(w_t, wb - 1, 1))

    wk_t = wk_ref[0, 0]
    wv_t = wv_ref[0, 0]
    kpos = (q_pos - wb) + lax.broadcasted_iota(jnp.int32, (N_HEADS, wb), 1)
    dist = q_pos - kpos
    wmask = (dist >= 0) & (dist <= WINDOW)
    s = jnp.where(wmask, _dot(q8, wk_t.astype(BF16)) - slope * dist.astype(F32), NEG)
    s_new, v_new = new_key(4 * LANES)
    m = jnp.maximum(jnp.max(s, axis=-1, keepdims=True), s_new)
    e = jnp.where(wmask, jnp.exp(s - m), 0.0)
    e_new = jnp.exp(s_new - m)
    o_win = finish(e, e_new, wv_t, v_new, jnp.sum(e, axis=-1, keepdims=True) + e_new)
    wko_ref[0] = shifted(wk_t, kvn_ref[0, :, 4 * LANES:5 * LANES])
    wvo_ref[0] = shifted(wv_t, kvn_ref[0, :, 5 * LANES:6 * LANES])

    for i in range(nblk):
        for cp in copies(b, slot, i):
            cp.wait()
    lane = lax.broadcasted_iota(jnp.int32, (1, page), 1)
    kpos_t, ok_t = [], []
    has_new = [jnp.zeros((1, 1), jnp.int32) for _ in range(N_KV)]
    for i in range(nblk):
        blk = blks[i]
        kpos_t.append(blk * SLC_BLOCK + lane % SLC_BLOCK)
        ok_t.append(((lane // SLC_BLOCK) == blk % bpp) & (blk < n_past))
        has_new[i // N_SEL] = jnp.maximum(has_new[i // N_SEL], (blk >= n_past).astype(jnp.int32))
    kpos = jnp.concatenate(kpos_t, axis=1)
    ok = jnp.concatenate(ok_t, axis=1)
    pgrp = lax.broadcasted_iota(jnp.int32, (1, nblk * page), 1) // (N_SEL * page)
    dist = q_pos - kpos
    kmask = ok & (dist >= 0) & (rgrp == pgrp)
    s = jnp.where(kmask, _dot(q8, kb[slot].astype(BF16)) - slope * dist.astype(F32), NEG)
    s_new, v_new = new_key(2 * LANES)
    new_on = jnp.where(rgrp == 0, has_new[0], has_new[1]) > 0
    sn = jnp.where(new_on, s_new, NEG)
    m = jnp.maximum(jnp.max(s, axis=-1, keepdims=True), sn)
    e = jnp.where(kmask, jnp.exp(s - m), 0.0)
    e_new = jnp.where(new_on, jnp.exp(sn - m), 0.0)
    o_slc = finish(e, e_new, vb[slot], v_new, jnp.sum(e, axis=-1, keepdims=True) + e_new)

    o_ref[0] = (ocmp_ref[0] + gate1 * o_slc + gate2 * o_win).astype(BF16)


def _s_slc_win(layer, idx, page_table, q, kvn, gd, ocmp, win_k, win_v, slc_k, slc_v, consts, q_pos):
    ns, n_pages = page_table.shape
    page = slc_k.shape[3]
    wb = win_k.shape[3]
    full = lambda a: pl.BlockSpec(a.shape, lambda b, i, pt: (0,) * a.ndim)
    per_b = lambda a: pl.BlockSpec((1,) + a.shape[1:], lambda b, i, pt: (b,) + (0,) * (a.ndim - 1))
    win = pl.BlockSpec((1, 1, LANES, wb), lambda b, i, pt: (layer, b, 0, 0))
    ws = [consts["tile_m"], consts["e_gate"][1], consts["e_gate"][2]]
    nlane = N_KV * N_SEL * page
    grid_spec = pltpu.PrefetchScalarGridSpec(
        num_scalar_prefetch=2,
        grid=(ns,),
        in_specs=[per_b(q), per_b(kvn), per_b(gd), per_b(ocmp), win, win,
                  pl.BlockSpec(memory_space=pl.ANY), pl.BlockSpec(memory_space=pl.ANY)] + [full(w) for w in ws],
        out_specs=[pl.BlockSpec((1, 1, ATT_WIDTH), lambda b, i, pt: (b, 0, 0)),
                   pl.BlockSpec((1, LANES, wb), lambda b, i, pt: (b, 0, 0)),
                   pl.BlockSpec((1, LANES, wb), lambda b, i, pt: (b, 0, 0))],
        scratch_shapes=[pltpu.VMEM((2, LANES, nlane), F32), pltpu.VMEM((2, LANES, nlane), F32),
                        pltpu.SemaphoreType.DMA((2, 2))],
    )
    return pl.pallas_call(
        functools.partial(_s_slc_win_kernel, layer=layer, n_pages=n_pages, page=page, q_pos=q_pos, wb=wb),
        out_shape=[jax.ShapeDtypeStruct((ns, 1, ATT_WIDTH), BF16),
                   jax.ShapeDtypeStruct((ns, LANES, wb), F32), jax.ShapeDtypeStruct((ns, LANES, wb), F32)],
        grid_spec=grid_spec,
        compiler_params=_cparams(("arbitrary",)),
        name="sample_slc_win_attn",
    )(idx.reshape(-1), page_table.reshape(-1), q, kvn, gd, ocmp, win_k, win_v, slc_k, slc_v, *ws)


def _col_bcast(row):
    blocks = [jnp.broadcast_to(row[:, LANES * i:LANES * (i + 1)], (LANES, LANES)).T for i in range(row.shape[1] // LANES)]
    return jnp.concatenate(blocks, axis=0)


def _s_ssd_kernel(xbc_ref, prev_ref, z_ref, gd_ref, h0_ref, cw_ref, cb_ref, dtb_ref, a_ref, dsk_ref, ng_ref,
                  edt_ref, y_ref, h_ref):
    conv = cb_ref[...] + xbc_ref[0] * cw_ref[SSM_CONV - 1:SSM_CONV, :]
    for k in range(SSM_CONV - 1):
        conv = conv + prev_ref[0, k:k + 1, :] * cw_ref[k:k + 1, :]
    act = _silu(conv)
    xm = act[:, :SSM_INNER]
    bm = act[:, SSM_INNER:SSM_INNER + LANES]
    cm = act[:, SSM_INNER + LANES:]
    dt = _softplus(gd_ref[0] + dtb_ref[...])
    a = dt * a_ref[...]
    edt = edt_ref[...]
    dt_x = _expand(jnp.broadcast_to(dt, (SUBLANES, LANES)), edt)[0:1, :]
    dec_x = jnp.exp(_expand(jnp.broadcast_to(a, (SUBLANES, LANES)), edt)[0:1, :])
    dtx = dt_x * xm
    h0 = h0_ref[0, 0].reshape(SSM_INNER, D_STATE)
    half = SSM_INNER // SSM_GROUPS
    rsel = lax.broadcasted_iota(jnp.int32, (SSM_INNER, 1), 0) // half
    lsel = lax.broadcasted_iota(jnp.int32, (1, SSM_INNER), 1) // half
    y_off = jnp.zeros((1, SSM_INNER), F32)
    cbx = jnp.zeros((1, SSM_INNER), F32)
    brow = jnp.zeros((SSM_INNER, D_STATE), F32)
    for g in range(SSM_GROUPS):
        bg = bm[:, D_STATE * g:D_STATE * (g + 1)]
        cg = cm[:, D_STATE * g:D_STATE * (g + 1)]
        c8 = jnp.broadcast_to(cg, (SUBLANES, D_STATE)).astype(BF16)
        yo = _dot_nt(c8, h0[half * g:half * (g + 1), :].astype(BF16))[0:1, :]
        y_off = jnp.where(lsel == g, jnp.concatenate([yo] * SSM_GROUPS, axis=1), y_off)
        cbx = jnp.where(lsel == g, jnp.sum(cg * bg, axis=-1, keepdims=True), cbx)
        brow = jnp.where(rsel == g, jnp.broadcast_to(bg, (SSM_INNER, D_STATE)), brow)
    y = y_off * dec_x + cbx * dtx + dsk_ref[...] * xm
    y_ref[0] = _gated_norm(y, z_ref[0], ng_ref[...]).astype(BF16)
    h_new = h0 * _col_bcast(dec_x)[:, :D_STATE] + _col_bcast(dtx)[:, :D_STATE] * brow
    h_ref[0] = h_new.reshape(SSM_HEADS, HEAD_DIM, D_STATE)


def _s_ssd(layer, xbc, prev, z, gd, state_ssm, sw, consts):
    ns = state_ssm.shape[1]
    full = lambda a: pl.BlockSpec(a.shape, lambda b: (0,) * a.ndim)
    per_b = lambda a: pl.BlockSpec((1,) + a.shape[1:], lambda b: (b,) + (0,) * (a.ndim - 1))
    hshape = state_ssm.shape[2:]
    ws = [sw[n] for n in ("conv_w", "conv_b", "dt_bias", "a", "d_skip", "norm_g")] + [consts["e_dt"]]
    return pl.pallas_call(
        _s_ssd_kernel,
        out_shape=[jax.ShapeDtypeStruct((ns, 1, SSM_INNER), BF16), jax.ShapeDtypeStruct((ns,) + hshape, F32)],
        grid=(ns,),
        in_specs=[per_b(xbc), per_b(prev), per_b(z), per_b(gd),
                  pl.BlockSpec((1, 1) + hshape, lambda b: (layer, b, 0, 0, 0))] + [_wspec(w) for w in ws],
        out_specs=[pl.BlockSpec((1, 1, SSM_INNER), lambda b: (b, 0, 0)),
                   pl.BlockSpec((1,) + hshape, lambda b: (b, 0, 0, 0))],
        compiler_params=_cparams(("arbitrary",)),
        name="sample_ssd_step",
    )(xbc, prev, z, gd, state_ssm, *[_warr(w) for w in ws])


def _constants(t_prompt, past_len):
    c = {}
    eg = np.zeros((3, LANES, ATT_WIDTH), np.float32)
    for br in range(3):
        for h in range(N_HEADS):
            eg[br, br * N_HEADS + h, h * HEAD_DIM:(h + 1) * HEAD_DIM] = 1.0
    c["e_gate"] = [jnp.asarray(eg[i], BF16) for i in range(3)]
    c["e_gate_t"] = jnp.asarray(eg.transpose(0, 2, 1), BF16)
    ed = np.zeros((LANES, SSM_INNER), np.float32)
    for h in range(SSM_HEADS):
        ed[GD_DT + h, h * HEAD_DIM:(h + 1) * HEAD_DIM] = 1.0
    c["e_dt"] = jnp.asarray(ed, BF16)

    def overlap(n_cmp_pad, n_slc, n_slc_pad):
        cs = np.arange(n_cmp_pad)[:, None] * CMP_STRIDE
        ss = np.arange(n_slc_pad)[None, :] * SLC_BLOCK
        ov = ((cs < ss + SLC_BLOCK) & (cs + CMP_BLOCK > ss) & (np.arange(n_slc_pad)[None, :] < n_slc))
        return ov.astype(np.float32)
    nh_p = t_prompt // CMP_STRIDE
    n_slc_p = t_prompt // SLC_BLOCK
    js_p = -(-n_slc_p // SUBLANES) * SUBLANES
    ov = overlap(nh_p, n_slc_p, js_p)
    ov[nh_p - 1:, :] = 0.0
    c["ot"] = jnp.asarray(ov.T, BF16)
    col = np.arange(NCOL)
    slope = np.asarray(SLOPES, np.float64)[col // TQ]
    c["srow2"] = jnp.asarray(LOG2E * slope[None, :], F32)
    c["c0"] = jnp.asarray(LOG2E * slope[None, :] * ((col % TQ)[None, :]
                                                    - (np.arange(nh_p)[:, None] * CMP_STRIDE + CMP_BLOCK - 1)), F32)
    def bf16_terms(x):
        terms, r = [], np.asarray(x, np.float64)
        for _ in range(3):
            tb = np.asarray(r, np.float32).astype(BF16).astype(np.float64)
            terms.append(tb)
            r = r - tb
        return terms
    coef = np.zeros((SUBLANES, NCOL), np.float64)
    coef[0:3] = np.stack(bf16_terms(LOG2E * SLC_BLOCK * slope))
    coef[3:6] = np.stack(bf16_terms(LOG2E * slope))
    c["coef"] = jnp.asarray(coef, F32)
    ntile = t_prompt // TQ
    kpos = np.arange(t_prompt).reshape(ntile, TQ)
    fk = np.zeros((ntile, TQ, LANES), np.float32)
    fk[:, :, 0:3] = (kpos // SLC_BLOCK)[:, :, None]
    fk[:, :, 3:6] = (kpos % SLC_BLOCK)[:, :, None]
    fk[:, :, 6] = (np.arange(TQ) < SLC_BLOCK)[None, :]
    fk[:, :, 7] = (np.arange(TQ) >= SLC_BLOCK)[None, :]
    c["fk"] = jnp.asarray(fk, BF16)
    rel = np.arange(TQ)[:, None] - (col % TQ)[None, :]
    c["amask"] = jnp.asarray(np.stack([np.zeros(rel.shape), np.where(rel > 0, -MASK_BIG, 0.0),
                                       np.where(rel < 0, -MASK_BIG, 0.0), np.full(rel.shape, -MASK_BIG)]), F32)
    nh_s = (past_len + 1 + CMP_STRIDE - 1) // CMP_STRIDE
    nhp_s = -(-nh_s // LANES) * LANES
    n_slc_s = past_len // SLC_BLOCK + 1
    js_s = -(-n_slc_s // LANES) * LANES
    ov_s = overlap(nhp_s, n_slc_s, js_s)
    ov_s[nh_s - 1:, :] = 0.0
    c["ov_s"] = jnp.asarray(ov_s, BF16)
    c["u_s"] = jnp.asarray(np.triu(np.ones((js_s, js_s), np.float32), 1), BF16)
    tm = np.zeros((LANES, ATT_WIDTH), np.float32)
    for h in range(N_HEADS):
        g = h // HG
        for d in range(HEAD_DIM):
            tm[g * HEAD_DIM + d, h * HEAD_DIM + d] = 1.0
    c["tile_m"] = jnp.asarray(tm, BF16)
    return c


def _pack_w_in(w_in):
    depth, d, _ = w_in.shape
    cuts = np.cumsum([ATT_WIDTH] + [LANES] * 6 + [3 * N_HEADS, SSM_INNER, CONV_DIM])
    q, kv, gt, z, xbc, dtc = (w_in[..., :cuts[0]], w_in[..., cuts[0]:cuts[6]], w_in[..., cuts[6]:cuts[7]],
                              w_in[..., cuts[7]:cuts[8]], w_in[..., cuts[8]:cuts[9]], w_in[..., cuts[9]:])
    pad = jnp.zeros((depth, d, P_W - P_GD - gt.shape[-1] - dtc.shape[-1]), w_in.dtype)
    return jnp.concatenate([q, kv, z, xbc, gt, dtc, pad], axis=-1).astype(BF16)


def _pack_compress(w1, b1, w2):
    depth = w1.shape[0]
    zeros = jnp.zeros((depth, CMP_STRIDE, HEAD_DIM, CMP_HID), w1.dtype)
    parts = []
    for ab in range(2):
        w = w1[:, ab * CMP_STRIDE:(ab + 1) * CMP_STRIDE]
        top = jnp.concatenate([w, zeros], axis=3)
        bot = jnp.concatenate([zeros, w], axis=3)
        parts.append(jnp.concatenate([top, bot], axis=2))
    w1b = jnp.concatenate(parts, axis=3)
    w1b = w1b.reshape(depth, CMP_STRIDE // 2, 2 * LANES, 4 * CMP_HID).astype(BF16)
    b1b = jnp.concatenate([b1, b1], axis=1).reshape(depth, 1, 2 * CMP_HID)
    z2 = jnp.zeros_like(w2)
    def blockdiag(rep):
        top = jnp.concatenate([w2] * rep + [z2] * rep, axis=2)
        bot = jnp.concatenate([z2] * rep + [w2] * rep, axis=2)
        return jnp.concatenate([top, bot], axis=1).astype(BF16)
    return w1b, b1b, blockdiag(1), blockdiag(HG)


def _lane_rows(vals, offset, width):
    depth, n = vals.shape
    return jnp.pad(vals.astype(F32), ((0, 0), (offset, width - offset - n))).reshape(depth, 1, width)


def _token_minor(a):
    lead = a.shape[:-3]
    n = len(lead)
    return jnp.transpose(a, tuple(range(n)) + (n + 1, n + 2, n)).reshape(lead + (N_KV * HEAD_DIM, a.shape[-3]))


def _token_major(a):
    lead = a.shape[:-2]
    n = len(lead)
    a = a.reshape(lead + (N_KV, HEAD_DIM, a.shape[-1]))
    return jnp.transpose(a, tuple(range(n)) + (n + 2, n, n + 1))


def kernel(x_prompt, x_sample, cache_cmp_k, cache_cmp_v, cache_slc_k, cache_slc_v, state_win_k, state_win_v, state_ssm, state_ssm_conv, state_ffn_conv, page_table, c_prompt, c_sample, ada_w, ada_b, norm1_g, norm2_g, w_in, cmpk_w1, cmpk_b1, cmpk_w2, cmpv_w1, cmpv_b1, cmpv_w2, ssm_conv_w, ssm_conv_b, dt_bias, a_log, d_skip, ssm_norm_g, w_out, ffn_w_up, ffn_conv_w, ffn_conv_b, ffn_w_down, final_g):
    bp, tp, d = x_prompt.shape
    ns = x_sample.shape[0]
    depth = w_in.shape[0]
    page = cache_cmp_k.shape[2]
    past_len = page_table.shape[1] * page
    assert x_sample.shape[1] == 1 and d == D_MODEL
    assert tp % SSM_CHUNK == 0 and tp >= WINDOW and past_len >= WINDOW and page % SLC_BLOCK == 0
    consts = _constants(tp, past_len)
    caches_t = [_token_minor(c) for c in (cache_cmp_k, cache_cmp_v, cache_slc_k, cache_slc_v)]
    win_t = [_token_minor(w) for w in (state_win_k, state_win_v)]

    mod = _mod(jnp.concatenate([c_sample, c_prompt], axis=0), ada_w, ada_b)
    mod_rows_p = mod.reshape(depth, ns + bp, 1, 6 * d)
    mod_rows_s = mod.reshape(depth, 1, ns + bp, 6 * d)
    tm_p = 512 if tp % 512 == 0 else SSM_CHUNK

    w_packed = _pack_w_in(w_in)
    w1k, b1k, w2k, w2k4 = _pack_compress(cmpk_w1, cmpk_b1, cmpk_w2)
    w1v, b1v, w2v, w2v4 = _pack_compress(cmpv_w1, cmpv_b1, cmpv_w2)
    norm1 = norm1_g.reshape(depth, 1, d)
    sw_all = dict(conv_w=ssm_conv_w, conv_b=ssm_conv_b.reshape(depth, 1, CONV_DIM),
                  dt_bias=_lane_rows(dt_bias, GD_DT, LANES), a=_lane_rows(-jnp.exp(a_log.astype(F32)), GD_DT, LANES),
                  d_skip=jnp.repeat(d_skip.astype(F32), HEAD_DIM, axis=1).reshape(depth, 1, SSM_INNER),
                  norm_g=ssm_norm_g.astype(F32).reshape(depth, 1, SSM_INNER))
    fw_all = dict(norm2_g=norm2_g.reshape(depth, 1, d), w_out=w_out.astype(BF16), w_up=ffn_w_up.astype(BF16),
                  conv_w=ffn_conv_w, conv_b=ffn_conv_b.reshape(depth, 1, D_FF), w_down=ffn_w_down.astype(BF16))

    xp = x_prompt
    xs = x_sample.reshape(1, ns, d)
    outs_p, outs_s = [], []
    for l in range(depth):
        last = l == depth - 1
        mod_p = (mod_rows_p, l, ns, False)
        mod_s = (mod_rows_s, l, ns, True)
        at = lambda a: (a, l)
        cw = dict(w1k=at(w1k), b1k=at(b1k), w2k=at(w2k), w1v=at(w1v), b1v=at(b1v), w2v=at(w2v))
        cw4 = dict(w1k=at(w1k), b1k=at(b1k), w2k=at(w2k4), w1v=at(w1v), b1v=at(b1v), w2v=at(w2v4))
        sw = {n: at(a) for n, a in sw_all.items()}
        fw = {n: at(a) for n, a in fw_all.items()}
        fw["final_g"] = final_g.reshape(1, d)

        (qt, kc, vc, kvb, vt, z, xbc, gd, kct, vct, kst, vst, kwt, vwt) = _inproj(
            xp, mod_p, at(norm1), at(w_packed), tm_p, True)
        kcmp, vcmpt = _compress(kc, vc, cw)
        ocmpt, selt = _cmp_sel(qt, kcmp, vcmpt, consts)
        oatt = _attn(qt, kvb, vt, selt, ocmpt, gd, consts)
        yssm, h_p = _ssd(xbc, z, gd, sw, consts)
        xp, ug_tail = _mix_ffn(xp, oatt, yssm, mod_p, fw, tm_p, None, last)
        outs_p.append((kct, vct, kst, vst, kwt[:, :, tp - WINDOW:], vwt[:, :, tp - WINDOW:], h_p,
                       xbc[:, tp - (SSM_CONV - 1):], ug_tail[:, SUBLANES - (FFN_CONV - 1):]))

        q, kvb, z, xbc, gd, kc, vc, ks, vs, kw, vw = _inproj(xs, mod_s, at(norm1), at(w_packed), ns, False)
        per_tok = lambda a: a.reshape(ns, 1, a.shape[-1])
        kvn = per_tok(jnp.concatenate([kc, vc, ks, vs, kw, vw], axis=2))
        q, z, xbc, gd = per_tok(q), per_tok(z), per_tok(xbc), per_tok(gd)
        ocmp, idx = _s_cmp(l, page_table, q, kvn, gd, caches_t[0], caches_t[1], cw4, consts, past_len)
        oatt, wk_new, wv_new = _s_slc_win(l, idx, page_table, q, kvn, gd, ocmp, win_t[0], win_t[1],
                                          caches_t[2], caches_t[3], consts, past_len)
        yssm, h_s = _s_ssd(l, xbc, state_ssm_conv[l], z, gd, state_ssm, sw, consts)
        prev_ffn = (state_ffn_conv[l][:, 0].reshape(1, ns, D_FF), state_ffn_conv[l][:, 1].reshape(1, ns, D_FF))
        xs, ug_s = _mix_ffn(xs, oatt.reshape(1, ns, ATT_WIDTH), yssm.reshape(1, ns, SSM_INNER), mod_s, fw, ns,
                            prev_ffn, last)
        s4 = lambda a: a.reshape(ns, 1, N_KV, HEAD_DIM)
        outs_s.append((s4(kc), s4(vc), s4(ks), s4(vs), wk_new, wv_new, h_s,
                       jnp.concatenate([state_ssm_conv[l][:, 1:], xbc.reshape(ns, 1, CONV_DIM)], axis=1),
                       jnp.concatenate([state_ffn_conv[l][:, 1:], ug_s.reshape(ns, 1, D_FF)], axis=1)))

    stack = lambda seq, i: jnp.stack([st[i] for st in seq])
    res = [xp, xs.reshape(ns, 1, d)]
    for i in range(9):
        p_i, s_i = stack(outs_p, i), stack(outs_s, i)
        if i < 6:
            p_i = _token_major(p_i)
        if i in (4, 5):
            s_i = _token_major(s_i)
        res += [p_i, s_i]
    return tuple(res)
```

```python
import functools

import numpy as np
import jax
import jax.numpy as jnp
from jax import lax
from jax.experimental import pallas as pl
from jax.experimental.pallas import tpu as pltpu

F32 = jnp.float32
BF16 = jnp.bfloat16

D_MODEL = 1024
HEAD_DIM = 64
ATT_WIDTH = 512
N_HEADS = 8
N_KV = 2
HG = 4
CMP_BLOCK = 32
CMP_STRIDE = 16
CMP_HID = 128
SLC_BLOCK = 64
N_SEL = 16
WINDOW = 512
FORCE_BONUS = 1e6
SSM_INNER = 512
SSM_HEADS = 8
SSM_GROUPS = 2
D_STATE = 64
SSM_CONV = 4
SSM_CHUNK = 128
CONV_DIM = 768
D_FF = 2816
FFN_CONV = 3
EPS = 1e-6
NEG = -1e30
SCALE = HEAD_DIM ** -0.5
LOG2E = 1.4426950408889634
SLOPES = tuple(2.0 ** (-(h + 1)) for h in range(N_HEADS))

LANES = 128
SUBLANES = 8
VMEM_LIMIT = 56 * 1024 * 1024

FF_CHUNKS = ((0, 768), (768, 1536), (1536, 2176), (2176, 2816))
ROW_PITCH = 24
TQ = 2 * SLC_BLOCK
NCOL = N_HEADS * TQ

P_Q = 0
P_KV = 512
P_Z = 1280
P_XBC = 1792
P_GD = 2560
P_W = 2688
GD_DT = 24
MASK_BIG = 2.0 ** 100


def _cparams(sem):
    return pltpu.CompilerParams(dimension_semantics=sem, vmem_limit_bytes=VMEM_LIMIT)


def _resident(shape):
    nd = len(shape)
    return pl.BlockSpec(shape, lambda *_: (0,) * nd, pipeline_mode=pl.Buffered(1))


def _wspec(w):
    if isinstance(w, tuple):
        arr, layer = w
        nd = arr.ndim
        return pl.BlockSpec((None,) + arr.shape[1:], lambda *_: (layer,) + (0,) * (nd - 1),
                            pipeline_mode=pl.Buffered(1))
    return _resident(w.shape)


def _warr(w):
    return w[0] if isinstance(w, tuple) else w


def _split3(x):
    hi = x.astype(BF16)
    r = x - hi.astype(F32)
    mid = r.astype(BF16)
    lo = (r - mid.astype(F32)).astype(BF16)
    return hi, mid, lo


def _dot(a, b):
    return jnp.dot(a, b, preferred_element_type=F32)


def _dot_nt(a, b):
    return lax.dot_general(a, b, (((1,), (1,)), ((), ())), preferred_element_type=F32)


def _expand(x, e):
    hi, mid, lo = _split3(x)
    return _dot(hi, e) + _dot(mid, e) + _dot(lo, e)


def _expand_l(e, x):
    hi, mid, lo = _split3(x)
    return _dot(e, hi) + _dot(e, mid) + _dot(e, lo)


def _silu(x):
    return x * jax.nn.sigmoid(x)


def _softplus(x):
    return jnp.maximum(x, 0.0) + jnp.log1p(jnp.exp(-jnp.abs(x)))


def _gelu_tanh(x):
    return 0.5 * x * (1.0 + jnp.tanh(np.sqrt(2.0 / np.pi) * (x + 0.044715 * (x * x * x))))


def _mod_kernel(c_ref, w_ref, b_ref, o_ref):
    c = c_ref[...]
    s = _silu(c).astype(BF16)
    o_ref[0] = _dot(s, w_ref[0].astype(BF16)) + b_ref[0]


def _mod(c_all, ada_w, ada_b):
    depth, d, n6 = ada_w.shape
    rows = c_all.shape[0]
    tn = 512
    return pl.pallas_call(
        _mod_kernel,
        out_shape=jax.ShapeDtypeStruct((depth, rows, n6), F32),
        grid=(depth, n6 // tn),
        in_specs=[pl.BlockSpec((rows, d), lambda l, j: (0, 0)),
                  pl.BlockSpec((1, d, tn), lambda l, j: (l, 0, j)),
                  pl.BlockSpec((1, 1, tn), lambda l, j: (l, 0, j))],
        out_specs=pl.BlockSpec((1, rows, tn), lambda l, j: (l, 0, j)),
        compiler_params=_cparams(("arbitrary", "arbitrary")),
        name="adaln_mod",
    )(c_all, ada_w, ada_b.reshape(depth, 1, n6))


def _mod_spec(mod, k):
    arr, layer, row0, per_row = mod
    if per_row:
        return pl.BlockSpec((None, 1, row0, D_MODEL), lambda b, t: (layer, 0, 0, k))
    return pl.BlockSpec((None, 1, 1, D_MODEL), lambda b, t: (layer, row0 + b, 0, k))


def _inproj_kernel(x_ref, sh_ref, sc_ref, g_ref, w_ref, *outs, tm, token_minor):
    x = x_ref[0]
    ms = jnp.mean(x * x, axis=-1, keepdims=True)
    xn = x * lax.rsqrt(ms + EPS) * g_ref[...]
    h = xn * (1.0 + sc_ref[0]) + sh_ref[0]
    proj = _dot(h.astype(BF16), w_ref[...])
    kv = [proj[:, P_KV + LANES * i:P_KV + LANES * (i + 1)] for i in range(6)]
    if token_minor:
        qt_ref, kc_ref, vc_ref, kvb_ref, vt_ref, z_ref, xbc_ref, gd_ref = outs[:8]
        qt_ref[0] = (proj[:, P_Q:P_Q + ATT_WIDTH] * (SCALE * LOG2E)).T.astype(BF16)
        kc_ref[0] = kv[0]
        vc_ref[0] = kv[1]
        kvt = [a.T for a in kv]
        for ref, a in zip(outs[8:], kvt):
            ref[0] = a
        for r in range(tm // TQ):
            cols = slice(r * TQ, (r + 1) * TQ)
            vt_ref[0, r] = jnp.concatenate([kvt[3][:, cols], kvt[5][:, cols]], axis=0).astype(BF16)
    else:
        q_ref, kvb_ref, z_ref, xbc_ref, gd_ref = outs[:5]
        q_ref[0] = (proj[:, P_Q:P_Q + ATT_WIDTH] * SCALE).astype(BF16)
        for ref, a in zip(outs[5:], kv):
            ref[0] = a
    kvb_ref[0] = proj[:, P_KV:P_Z].astype(BF16)
    z_ref[0] = proj[:, P_Z:P_XBC]
    xbc_ref[0] = proj[:, P_XBC:P_GD]
    gd_ref[0] = proj[:, P_GD:P_W]


def _inproj(x, mod, norm_g, w_packed, tm, token_minor):
    bsz, t, d = x.shape
    row = lambda w: pl.BlockSpec((1, tm, w), lambda b, i: (b, i, 0))
    col = lambda w: pl.BlockSpec((1, w, tm), lambda b, i: (b, 0, i))
    shp = lambda w, dt: jax.ShapeDtypeStruct((bsz, t, w), dt)
    shpt = lambda w, dt: jax.ShapeDtypeStruct((bsz, w, t), dt)
    if token_minor:
        out_shape = ([shpt(ATT_WIDTH, BF16), shp(LANES, F32), shp(LANES, F32), shp(6 * LANES, BF16),
                      jax.ShapeDtypeStruct((bsz, t // TQ, 2 * LANES, TQ), BF16),
                      shp(SSM_INNER, F32), shp(CONV_DIM, F32), shp(LANES, F32)] + [shpt(LANES, F32)] * 6)
        out_specs = ([col(ATT_WIDTH), row(LANES), row(LANES), row(6 * LANES),
                      pl.BlockSpec((1, tm // TQ, 2 * LANES, TQ), lambda b, i: (b, i, 0, 0)),
                      row(SSM_INNER), row(CONV_DIM), row(LANES)] + [col(LANES)] * 6)
    else:
        out_shape = ([shp(ATT_WIDTH, BF16), shp(6 * LANES, BF16), shp(SSM_INNER, F32), shp(CONV_DIM, F32),
                      shp(LANES, F32)] + [shp(LANES, F32)] * 6)
        out_specs = [row(ATT_WIDTH), row(6 * LANES), row(SSM_INNER), row(CONV_DIM), row(LANES)] + [row(LANES)] * 6
    return pl.pallas_call(
        functools.partial(_inproj_kernel, tm=tm, token_minor=token_minor),
        out_shape=out_shape,
        grid=(bsz, t // tm),
        in_specs=[row(d), _mod_spec(mod, 0), _mod_spec(mod, 1), _wspec(norm_g), _wspec(w_packed)],
        out_specs=out_specs,
        compiler_params=_cparams(("arbitrary", "arbitrary")),
        name="inproj",
    )(x, mod[0], mod[0], _warr(norm_g), _warr(w_packed))


def _compress_block(src_ref, nh, w1_ref, b1_ref, w2_ref, pitch=CMP_STRIDE):
    acc = jnp.zeros((nh, 4 * CMP_HID), F32)
    for i in range(CMP_STRIDE // 2):
        xa = src_ref[pl.ds(2 * i, nh, stride=pitch), :]
        xb = src_ref[pl.ds(2 * i + 1, nh, stride=pitch), :]
        acc = acc + _dot(jnp.concatenate([xa, xb], axis=1).astype(BF16), w1_ref[i])
    pa = acc[:, :2 * CMP_HID]
    pb = pltpu.roll(acc[:, 2 * CMP_HID:], nh - 1, 0)
    hid = _gelu_tanh(pa + pb + b1_ref[...])
    return _dot(hid.astype(BF16), w2_ref[...])


def _compress_kernel(k_ref, v_ref, w1k_ref, b1k_ref, w2k_ref, w1v_ref, b1v_ref, w2v_ref, ok_ref, ovt_ref, *, nh):
    ok_ref[0] = _compress_block(k_ref.at[0], nh, w1k_ref, b1k_ref, w2k_ref).astype(BF16)
    ovt_ref[0] = _compress_block(v_ref.at[0], nh, w1v_ref, b1v_ref, w2v_ref).T.astype(BF16)


def _compress(kc, vc, cw):
    bsz, t, _ = kc.shape
    nh = t // CMP_STRIDE
    seq = pl.BlockSpec((1, t, LANES), lambda b: (b, 0, 0))
    ws = [cw[n] for n in ("w1k", "b1k", "w2k", "w1v", "b1v", "w2v")]
    return pl.pallas_call(
        functools.partial(_compress_kernel, nh=nh),
        out_shape=[jax.ShapeDtypeStruct((bsz, nh, LANES), BF16), jax.ShapeDtypeStruct((bsz, LANES, nh), BF16)],
        grid=(bsz,),
        in_specs=[seq, seq] + [_wspec(w) for w in ws],
        out_specs=[pl.BlockSpec((1, nh, LANES), lambda b: (b, 0, 0)), pl.BlockSpec((1, LANES, nh), lambda b: (b, 0, 0))],
        compiler_params=_cparams(("arbitrary",)),
        name="compress",
    )(kc, vc, *[_warr(w) for w in ws])


def _all_sublanes(x, op):
    for shift in (4, 2, 1):
        x = op(x, pltpu.roll(x, shift, 0))
    return x


def _query_blockdiag(qt):
    zero = jnp.zeros((HEAD_DIM, HG * TQ), BF16)
    per_g = [jnp.concatenate([qt[HEAD_DIM * (HG * g + j):HEAD_DIM * (HG * g + j + 1), :] for j in range(HG)], axis=1)
             for g in range(N_KV)]
    return jnp.concatenate([jnp.concatenate([per_g[0], zero], axis=1),
                            jnp.concatenate([zero, per_g[1]], axis=1)], axis=0)


def _topk_mask_t(score, jrow, n_slc):
    rank = jnp.zeros(score.shape, F32)
    for jp in range(n_slc):
        sj = score[jp:jp + 1, :]
        ahead = (sj > score) | ((sj == score) & (jp < jrow))
        rank = rank + ahead.astype(F32)
    return (rank < N_SEL) & (jrow < n_slc)


def _cmp_sel_kernel(qt_ref, kc_ref, vct_ref, ot_ref, c0_ref, srow_ref, ocmpt_ref, selt_ref, *, n_cmp, n_slc, nsub):
    for sub in range(nsub):
        _cmp_sel_tile(pl.program_id(1) * nsub + sub, slice(sub * TQ, (sub + 1) * TQ), qt_ref, kc_ref, vct_ref,
                      ot_ref, c0_ref, srow_ref, ocmpt_ref, selt_ref, n_cmp, n_slc)


def _cmp_sel_tile(t, qcols, qt_ref, kc_ref, vct_ref, ot_ref, c0_ref, srow_ref, ocmpt_ref, selt_ref, n_cmp, n_slc):
    qbd = _query_blockdiag(qt_ref[0, :, qcols])
    kc = kc_ref[0]
    nl = kc.shape[0]
    srow = srow_ref[...]
    s = _dot(kc, qbd) - (c0_ref[...] + srow * (t * TQ).astype(F32))
    nidx = lax.broadcasted_iota(jnp.int32, (nl, NCOL), 0)
    qpos = t * TQ + lax.broadcasted_iota(jnp.int32, (nl, NCOL), 1) % TQ
    mask = (qpos - (nidx * CMP_STRIDE + (CMP_BLOCK - 1)) >= 0) & (nidx < n_cmp)
    sm = jnp.where(mask, s, NEG)
    e = jnp.exp2(sm - jnp.max(sm, axis=0, keepdims=True))
    p = jnp.where(mask, e * (1.0 / jnp.sum(e, axis=0, keepdims=True)), 0.0)
    pb = p.astype(BF16)
    vct = vct_ref[0]

    js = selt_ref.shape[2]
    jrow = lax.broadcasted_iota(jnp.int32, (js, TQ), 0)
    qp = t * TQ + lax.broadcasted_iota(jnp.int32, (js, TQ), 1)
    cur = qp // SLC_BLOCK
    forced = (jrow == 0) | (jrow == cur) | (jrow == cur - 1)
    valid = (jrow * SLC_BLOCK <= qp)
    for g in range(N_KV):
        c0 = g * HG * TQ
        og = _dot(vct[HEAD_DIM * g:HEAD_DIM * (g + 1), :], pb[:, c0:c0 + HG * TQ])
        pg = jnp.zeros((nl, TQ), F32)
        for j in range(HG):
            h = HG * g + j
            ocmpt_ref[0, HEAD_DIM * h:HEAD_DIM * (h + 1), qcols] = og[:, j * TQ:(j + 1) * TQ]
            pg = pg + p[:, c0 + j * TQ:c0 + (j + 1) * TQ]
        imp_t = _expand_l(ot_ref[...], pg)
        score = jnp.where(valid, imp_t + jnp.where(forced, FORCE_BONUS, 0.0), NEG)
        selt_ref[0, g, :, qcols] = _topk_mask_t(score, jrow, n_slc).astype(F32)


def _cmp_sel(qt, kcmp, vcmpt, consts):
    bsz, _, t = qt.shape
    nl = kcmp.shape[1]
    n_slc = t // SLC_BLOCK
    js = consts["ot"].shape[0]
    nsub = 2 if t % (2 * TQ) == 0 else 1
    tw = nsub * TQ
    return pl.pallas_call(
        functools.partial(_cmp_sel_kernel, n_cmp=t // CMP_STRIDE - 1, n_slc=n_slc, nsub=nsub),
        out_shape=[jax.ShapeDtypeStruct((bsz, ATT_WIDTH, t), F32),
                   jax.ShapeDtypeStruct((bsz, N_KV, js, t), F32)],
        grid=(bsz, t // tw),
        in_specs=[pl.BlockSpec((1, ATT_WIDTH, tw), lambda b, i: (b, 0, i)),
                  pl.BlockSpec((1, nl, LANES), lambda b, i: (b, 0, 0)),
                  pl.BlockSpec((1, LANES, nl), lambda b, i: (b, 0, 0)),
                  _resident(consts["ot"].shape), _resident(consts["c0"].shape), _resident(consts["srow2"].shape)],
        out_specs=[pl.BlockSpec((1, ATT_WIDTH, tw), lambda b, i: (b, 0, i)),
                   pl.BlockSpec((1, N_KV, js, tw), lambda b, i: (b, 0, 0, i))],
        compiler_params=_cparams(("arbitrary", "arbitrary")),
        name="cmp_select",
    )(qt, kcmp, vcmpt, consts["ot"], consts["c0"], consts["srow2"])


def _attn_kernel(qt_ref, kvb_ref, vt_ref, selt_ref, ocmpt_ref, gd_ref, fk_ref, coef_ref, amask_ref, egt_ref, o_ref,
                 pen_ref, sbuf, *state):
    nchunk = len(state) // 6
    states = [tuple(state[(3 * br + k) * nchunk:(3 * br + k + 1) * nchunk] for k in range(3)) for br in range(2)]
    t = pl.program_id(1)
    qbd = _query_blockdiag(qt_ref[0])
    gw = HG * TQ
    for g in range(N_KV):
        pen = (selt_ref[0, g] - 1.0) * MASK_BIG
        pen_ref[:, g * gw:(g + 1) * gw] = jnp.concatenate([pen] * HG, axis=1)
    nwin = WINDOW // TQ
    coef = coef_ref[...]
    zero8 = jnp.zeros((SUBLANES, NCOL), F32)
    cw = 2 * TQ
    zpad = jnp.zeros((LANES - 2 * SUBLANES, cw), BF16)
    rid = lax.broadcasted_iota(jnp.int32, (SUBLANES, NCOL), 0)

    def query_operands(c8):
        c16 = jnp.concatenate([c8, zero8], axis=0).astype(BF16)
        return [jnp.concatenate([qbd[:, c * cw:(c + 1) * cw], c16[:, c * cw:(c + 1) * cw], zpad], axis=0)
                for c in range(NCOL // cw)]

    q_win = query_operands(coef)

    def score_tile(kt, slot, kcol, slc, mask):
        r0 = pl.multiple_of(kt * TQ, TQ)
        k_aug = jnp.concatenate([kvb_ref[0, pl.ds(r0, TQ), kcol:kcol + LANES], fk_ref[kt]], axis=1)
        if slc:
            pa = jnp.broadcast_to(pen_ref[pl.ds(2 * kt, 1), :], (SUBLANES, NCOL))
            pb = jnp.broadcast_to(pen_ref[pl.ds(2 * kt + 1, 1), :], (SUBLANES, NCOL))
            q_aug = query_operands(jnp.where(rid == 6, pa, jnp.where(rid == 7, pb, coef)))
        else:
            q_aug = q_win
        for c in range(nchunk):
            cols = slice(c * cw, (c + 1) * cw)
            sbuf[slot, :, cols] = _dot(k_aug, q_aug[c]) + amask_ref[mask, :, cols]

    def consume_tile(kt, slot, vrow, st):
        m_refs, l_refs, acc_refs = st
        v_t = vt_ref[0, kt, vrow:vrow + LANES, :]
        for c in range(nchunk):
            cols = slice(c * cw, (c + 1) * cw)
            g = (c * cw) // gw
            s = sbuf[slot, :, cols].reshape(TQ // SUBLANES, SUBLANES, cw)
            m_old = m_refs[c][...]
            m_new = jnp.maximum(m_old, _all_sublanes(jnp.max(s, axis=0), jnp.maximum))
            alpha = jnp.exp2(m_old - m_new)
            p = jnp.exp2(s - m_new[None])
            l_refs[c][...] = alpha * l_refs[c][...] + _all_sublanes(jnp.sum(p, axis=0), jnp.add)
            m_refs[c][...] = m_new
            pv = _dot(v_t[HEAD_DIM * g:HEAD_DIM * (g + 1), :], p.reshape(TQ, cw).astype(BF16))
            acc = acc_refs[c][...].reshape(HEAD_DIM // SUBLANES, SUBLANES, cw) * alpha[None]
            acc_refs[c][...] = acc.reshape(HEAD_DIM, cw) + pv

    def reset(st):
        m_refs, l_refs, acc_refs = st
        for c in range(nchunk):
            m_refs[c][...] = jnp.full(m_refs[c].shape, NEG, F32)
            l_refs[c][...] = jnp.zeros(l_refs[c].shape, F32)
            acc_refs[c][...] = jnp.zeros(acc_refs[c].shape, F32)

    def result(st):
        _, l_refs, acc_refs = st
        per_chunk = [(acc_refs[c][...].reshape(HEAD_DIM // SUBLANES, SUBLANES, cw)
                      * (1.0 / l_refs[c][...])[None]).reshape(HEAD_DIM, cw) for c in range(nchunk)]
        half = nchunk // N_KV
        return [jnp.concatenate(per_chunk[g * half:(g + 1) * half], axis=1) for g in range(N_KV)]

    def stream(br, first, kcol, vrow, slc):
        n = t - first + 1
        pad = n % 2

        def tile_of(i):
            return jnp.maximum(first + i - pad, first)

        def mask_of(i):
            kt = first + i - pad
            diag = jnp.where(kt == t, 1, 0)
            m = diag if slc else jnp.where(kt == t - nwin, 2, diag)
            return jnp.where(i < pad, 3, m)

        def score(i, par):
            score_tile(tile_of(i), 2 * br + par, kcol, slc, mask_of(i))

        def consume(i, par):
            consume_tile(tile_of(i), 2 * br + par, vrow, states[br])

        return dict(total=n + pad, st=states[br], score=score, consume=consume)

    slc_s = stream(0, 0, 2 * LANES, 0, True)
    win_s = stream(1, jnp.maximum(t - nwin, 0), 4 * LANES, LANES, False)
    streams = (slc_s, win_s)
    for sm in streams:
        reset(sm["st"])
        sm["score"](0, 0)

    def pair_step(j, active):
        i = 2 * j
        for sm in active:
            sm["score"](i + 1, 1)
        for sm in active:
            sm["consume"](i, 0)
        for sm in active:
            sm["score"](i + 2, 0)
        for sm in active:
            sm["consume"](i + 1, 1)

    def both_body(j, c):
        pair_step(j, (slc_s, win_s))
        return c

    def slc_body(j, c):
        pair_step(j, (slc_s,))
        return c

    merged = win_s["total"] // 2 - 1
    lax.fori_loop(0, merged, both_body, 0)
    lax.fori_loop(merged, slc_s["total"] // 2 - 1, slc_body, 0)
    for sm in streams:
        sm["score"](sm["total"] - 1, 1)
    for sm in streams:
        sm["consume"](sm["total"] - 2, 0)
    for sm in streams:
        sm["consume"](sm["total"] - 1, 1)
    o_slc = result(slc_s["st"])
    o_win = result(win_s["st"])

    g_t = jax.nn.sigmoid(gd_ref[0]).T
    gates = [_expand_l(egt_ref[br], g_t) for br in range(3)]
    for pr in range(N_HEADS // 2):
        g, j0 = pr // 2, 2 * (pr % 2)
        rows = slice(LANES * pr, LANES * (pr + 1))
        pair = lambda o: jnp.concatenate([o[g][:, j0 * TQ:(j0 + 1) * TQ], o[g][:, (j0 + 1) * TQ:(j0 + 2) * TQ]], axis=0)
        tot = gates[0][rows] * ocmpt_ref[0, rows, :] + gates[1][rows] * pair(o_slc) + gates[2][rows] * pair(o_win)
        o_ref[0, :, rows] = tot.T.astype(BF16)


def _attn(qt, kvb, vt, selt, ocmpt, gd, consts):
    bsz, _, t = qt.shape
    js = selt.shape[2]
    return pl.pallas_call(
        _attn_kernel,
        out_shape=jax.ShapeDtypeStruct((bsz, t, ATT_WIDTH), BF16),
        grid=(bsz, t // TQ),
        in_specs=[pl.BlockSpec((1, ATT_WIDTH, TQ), lambda b, i: (b, 0, i)),
                  pl.BlockSpec((1, t, 6 * LANES), lambda b, i: (b, 0, 0)),
                  pl.BlockSpec((1, t // TQ, 2 * LANES, TQ), lambda b, i: (b, 0, 0, 0)),
                  pl.BlockSpec((1, N_KV, js, TQ), lambda b, i: (b, 0, 0, i)),
                  pl.BlockSpec((1, ATT_WIDTH, TQ), lambda b, i: (b, 0, i)),
                  pl.BlockSpec((1, TQ, LANES), lambda b, i: (b, i, 0)),
                  _resident(consts["fk"].shape), _resident(consts["coef"].shape), _resident(consts["amask"].shape),
                  _resident(consts["e_gate_t"].shape)],
        out_specs=pl.BlockSpec((1, TQ, ATT_WIDTH), lambda b, i: (b, i, 0)),
        scratch_shapes=([pltpu.VMEM((js, NCOL), F32), pltpu.VMEM((4, TQ, NCOL), F32)]
                        + ([pltpu.VMEM((SUBLANES, 2 * TQ), F32)] * (2 * N_HEADS // 2)
                           + [pltpu.VMEM((HEAD_DIM, 2 * TQ), F32)] * (N_HEADS // 2)) * 2),
        compiler_params=_cparams(("arbitrary", "arbitrary")),
        name="slc_win_attn",
    )(qt, kvb, vt, selt, ocmpt, gd, consts["fk"], consts["coef"], consts["amask"], consts["e_gate_t"])


def _gated_norm(y, z, ng):
    yz = y * _silu(z)
    half = SSM_INNER // SSM_GROUPS
    outs = []
    for g in range(SSM_GROUPS):
        part = yz[:, half * g:half * (g + 1)]
        ms = jnp.mean(part * part, axis=-1, keepdims=True)
        outs.append(part * lax.rsqrt(ms + EPS))
    return jnp.concatenate(outs, axis=1) * ng


def _ssd_kernel(xbc_ref, z_ref, gd_ref, cw_ref, cb_ref, dtb_ref, a_ref, dsk_ref, ng_ref, edt_ref,
                y_ref, hfin_ref, ubuf, sstate, ybufs, *, lc, nsub):
    t = pl.program_id(1)
    nt = pl.num_programs(1)

    @pl.when(t == 0)
    def _():
        ubuf[0:SUBLANES, :] = jnp.zeros((SUBLANES, CONV_DIM), F32)
        sstate[...] = jnp.zeros(sstate.shape, F32)

    ubuf[SUBLANES:SUBLANES + lc * nsub, :] = xbc_ref[0]
    for sub in range(nsub):
        _ssd_chunk(sub * lc, lc, ubuf, z_ref, gd_ref, cw_ref, cb_ref, dtb_ref, a_ref, dsk_ref, ng_ref, edt_ref,
                   y_ref, sstate, ybufs.at[sub])
    ubuf[0:SUBLANES, :] = ubuf[lc * nsub:lc * nsub + SUBLANES, :]

    @pl.when(t == nt - 1)
    def _():
        s_pad = jnp.concatenate([sstate[...], jnp.zeros((LANES - D_STATE, SSM_INNER), F32)], axis=0)
        hfin_ref[0] = s_pad.T[:, :D_STATE]


def _ssd_chunk(r0, lc, ubuf, z_ref, gd_ref, cw_ref, cb_ref, dtb_ref, a_ref, dsk_ref, ng_ref, edt_ref,
               y_ref, sstate, ybuf):
    conv = cb_ref[...] + jnp.zeros((lc, CONV_DIM), F32)
    for k in range(SSM_CONV):
        off = SUBLANES - (SSM_CONV - 1) + k + r0
        conv = conv + ubuf[off:off + lc, :] * cw_ref[k:k + 1, :]
    act = _silu(conv)
    xm = act[:, :SSM_INNER]
    bm = act[:, SSM_INNER:SSM_INNER + LANES]
    cm = act[:, SSM_INNER + LANES:]

    dt = _softplus(gd_ref[0, r0:r0 + lc, :] + dtb_ref[...])
    a = dt * a_ref[...]
    ri = lax.broadcasted_iota(jnp.int32, (lc, lc), 0)
    ci = lax.broadcasted_iota(jnp.int32, (lc, lc), 1)
    tril = ri >= ci
    cs = jnp.dot(tril.astype(F32), a, preferred_element_type=F32, precision=lax.Precision.HIGHEST)
    cs_t = cs.T
    edt = edt_ref[...]
    dt_x = _expand(dt, edt)
    cs_x = _expand(cs, edt)
    cs_last = cs_x[lc - 1:lc, :]
    bm_t = bm.T
    xd = xm * dt_x
    xw = xm * (jnp.exp(cs_last - cs_x) * dt_x)
    s_old = sstate[...]
    half = SSM_INNER // SSM_GROUPS
    for g in range(SSM_GROUPS):
        bg = bm[:, D_STATE * g:D_STATE * (g + 1)].astype(BF16)
        cg = cm[:, D_STATE * g:D_STATE * (g + 1)].astype(BF16)
        cb = _dot_nt(cg, bg)
        for j in range(SSM_HEADS // SSM_GROUPS):
            h = g * (SSM_HEADS // SSM_GROUPS) + j
            col = cs[:, GD_DT + h:GD_DT + h + 1]
            row = cs_t[GD_DT + h:GD_DT + h + 1, :]
            lm = jnp.exp(jnp.where(tril, col - row, NEG))
            lo, hi = HEAD_DIM * h, HEAD_DIM * (h + 1)
            ybuf[:, lo:hi] = _dot((cb * lm).astype(BF16), xd[:, lo:hi].astype(BF16))
        sg = s_old[:, half * g:half * (g + 1)]
        y_off = _dot(cg, sg.astype(BF16))
        ybuf[:, half * g:half * (g + 1)] = ybuf[:, half * g:half * (g + 1)] + y_off * jnp.exp(cs_x[:, half * g:half * (g + 1)])
        st = _dot(bm_t[D_STATE * g:D_STATE * (g + 1), :].astype(BF16), xw[:, half * g:half * (g + 1)].astype(BF16))
        sstate[:, half * g:half * (g + 1)] = sg * jnp.exp(cs_last[:, half * g:half * (g + 1)]) + st

    y = ybuf[...] + dsk_ref[...] * xm
    y_ref[0, r0:r0 + lc, :] = _gated_norm(y, z_ref[0, r0:r0 + lc, :], ng_ref[...]).astype(BF16)


def _ssd(xbc, z, gd, sw, consts):
    bsz, t, _ = xbc.shape
    lc = SSM_CHUNK
    nsub = 4 if t % (4 * lc) == 0 else 1
    rows = lc * nsub
    tile = lambda w: pl.BlockSpec((1, rows, w), lambda b, i: (b, i, 0))
    ws = [sw[n] for n in ("conv_w", "conv_b", "dt_bias", "a", "d_skip", "norm_g")] + [consts["e_dt"]]
    y, hfin = pl.pallas_call(
        functools.partial(_ssd_kernel, lc=lc, nsub=nsub),
        out_shape=[jax.ShapeDtypeStruct((bsz, t, SSM_INNER), BF16),
                   jax.ShapeDtypeStruct((bsz, SSM_INNER, D_STATE), F32)],
        grid=(bsz, t // rows),
        in_specs=[tile(CONV_DIM), tile(SSM_INNER), tile(LANES)] + [_wspec(w) for w in ws],
        out_specs=[tile(SSM_INNER), pl.BlockSpec((1, SSM_INNER, D_STATE), lambda b, i: (b, 0, 0))],
        scratch_shapes=[pltpu.VMEM((rows + SUBLANES, CONV_DIM), F32),
                        pltpu.VMEM((D_STATE, SSM_INNER), F32),
                        pltpu.VMEM((nsub, lc, SSM_INNER), F32)],
        compiler_params=_cparams(("arbitrary", "arbitrary")),
        name="ssd_scan",
    )(xbc, z, gd, *[_warr(w) for w in ws])
    return y, hfin.reshape(bsz, SSM_HEADS, HEAD_DIM, D_STATE)


def _mix_ffn_kernel(*refs, tm, carry_conv, last):
    if carry_conv:
        (x_ref, oa_ref, ys_ref, g1_ref, sh2_ref, sc2_ref, g2_ref, n2_ref, wo_ref, wu_ref, fcw_ref, fcb_ref,
         wd_ref, fg_ref, out_ref, ug_ref, ubuf) = refs
    else:
        (x_ref, oa_ref, ys_ref, g1_ref, sh2_ref, sc2_ref, g2_ref, n2_ref, wo_ref, wu_ref, fcw_ref, fcb_ref,
         wd_ref, fg_ref, p0_ref, p1_ref, out_ref, ug_ref) = refs
    t = pl.program_id(1)
    nt = pl.num_programs(1)
    x = x_ref[0]
    mix = _dot(oa_ref[0], wo_ref[:ATT_WIDTH, :]) + _dot(ys_ref[0], wo_ref[ATT_WIDTH:, :])
    x1 = x + g1_ref[0] * mix
    ms = jnp.mean(x1 * x1, axis=-1, keepdims=True)
    h2 = (x1 * lax.rsqrt(ms + EPS) * n2_ref[...]) * (1.0 + sc2_ref[0]) + sh2_ref[0]
    h2 = h2.astype(BF16)
    if carry_conv:
        @pl.when(t == 0)
        def _():
            ubuf[0:SUBLANES, :] = jnp.zeros((SUBLANES, D_FF), F32)

    def up(c):
        lo, hi = FF_CHUNKS[c]
        return _dot(h2, wu_ref[:, lo:hi]), _dot(h2, wu_ref[:, D_FF + lo:D_FF + hi])

    nxt = up(0)
    down = None
    for c, (lo, hi) in enumerate(FF_CHUNKS):
        ug, uv = nxt
        if c + 1 < len(FF_CHUNKS):
            nxt = up(c + 1)
        if carry_conv:
            ubuf[SUBLANES:SUBLANES + tm, lo:hi] = ug
            u1 = ubuf[SUBLANES - 1:SUBLANES - 1 + tm, lo:hi]
            u2 = ubuf[SUBLANES - 2:SUBLANES - 2 + tm, lo:hi]
        else:
            u1 = p1_ref[0, :, lo:hi]
            u2 = p0_ref[0, :, lo:hi]
            ug_ref[0, :, lo:hi] = ug
        ugc = (fcb_ref[:, lo:hi] + u2 * fcw_ref[0:1, lo:hi] + u1 * fcw_ref[1:2, lo:hi] + ug * fcw_ref[2:3, lo:hi])
        d = _dot((_silu(ugc) * uv).astype(BF16), wd_ref[lo:hi, :])
        down = d if down is None else down + d
    if carry_conv:
        @pl.when(t == nt - 1)
        def _():
            ug_ref[0] = ubuf[tm:tm + SUBLANES, :]

        ubuf[0:SUBLANES, :] = ubuf[tm:tm + SUBLANES, :]
    x2 = x1 + g2_ref[0] * down
    if last:
        ms2 = jnp.mean(x2 * x2, axis=-1, keepdims=True)
        out_ref[0] = x2 * lax.rsqrt(ms2 + EPS) * fg_ref[...]
    else:
        out_ref[0] = x2


def _mix_ffn(x, oatt, yssm, mod, fw, tm, prev, last):
    bsz, t, d = x.shape
    carry = prev is None
    row = lambda w: pl.BlockSpec((1, tm, w), lambda b, i: (b, i, 0))
    ws = [fw[n] for n in ("norm2_g", "w_out", "w_up", "conv_w", "conv_b", "w_down", "final_g")]
    in_specs = ([row(d), row(ATT_WIDTH), row(SSM_INNER)] + [_mod_spec(mod, k) for k in (2, 3, 4, 5)]
                + [_wspec(w) for w in ws])
    args = [x, oatt, yssm] + [mod[0]] * 4 + [_warr(w) for w in ws]
    if carry:
        ug_shape, ug_spec = (bsz, SUBLANES, D_FF), pl.BlockSpec((1, SUBLANES, D_FF), lambda b, i: (b, 0, 0))
        scratch = [pltpu.VMEM((tm + SUBLANES, D_FF), F32)]
    else:
        ug_shape, ug_spec = (bsz, t, D_FF), row(D_FF)
        scratch = []
        in_specs += [row(D_FF), row(D_FF)]
        args += list(prev)
    return pl.pallas_call(
        functools.partial(_mix_ffn_kernel, tm=tm, carry_conv=carry, last=last),
        out_shape=[jax.ShapeDtypeStruct((bsz, t, d), F32), jax.ShapeDtypeStruct(ug_shape, F32)],
        grid=(bsz, t // tm),
        in_specs=in_specs,
        out_specs=[row(d), ug_spec],
        scratch_shapes=scratch,
        compiler_params=_cparams(("arbitrary", "arbitrary")),
        name="mix_ffn",
    )(*args)


def _head_rows(row512):
    r = lax.broadcasted_iota(jnp.int32, (N_HEADS, ATT_WIDTH), 0)
    c = lax.broadcasted_iota(jnp.int32, (N_HEADS, ATT_WIDTH), 1)
    return jnp.where(c // HEAD_DIM == r, jnp.broadcast_to(row512, (N_HEADS, ATT_WIDTH)), 0.0)


def _head_diag(x8):
    r = lax.broadcasted_iota(jnp.int32, (N_HEADS, ATT_WIDTH), 0)
    c = lax.broadcasted_iota(jnp.int32, (N_HEADS, ATT_WIDTH), 1)
    return jnp.sum(jnp.where(c // HEAD_DIM == r, x8, 0.0), axis=0, keepdims=True)


def _slope_col():
    r = lax.broadcasted_iota(jnp.int32, (N_HEADS, 1), 0)
    s = jnp.zeros((N_HEADS, 1), F32)
    for h in range(N_HEADS):
        s = jnp.where(r == h, SLOPES[h], s)
    return s


def _s_cmp_kernel(pt_ref, q_ref, kvn_ref, gd_ref, ck_hbm, cv_hbm,
                  w1k_ref, b1k_ref, w2k_ref, w1v_ref, b1v_ref, w2v_ref, ov_ref, u_ref, e0_ref,
                  ocmp_ref, idx_ref, stage_k, stage_v, kbuf, vbuf, sem,
                  *, layer, n_pages, page, nh8, nhp, q_pos, n_slc):
    b = pl.program_id(0)
    nb = pl.num_programs(0)
    past = n_pages * page

    def copies(bb, p, slot):
        pg = pt_ref[bb * n_pages + p]
        return (pltpu.make_async_copy(ck_hbm.at[layer, pg], stage_k.at[slot, p], sem.at[0, slot]),
                pltpu.make_async_copy(cv_hbm.at[layer, pg], stage_v.at[slot, p], sem.at[1, slot]))

    unroll = 8 if n_pages % 8 == 0 else 1

    def issue(bb, slot):
        def body(i, c):
            for u in range(unroll):
                for cp in copies(bb, i * unroll + u, slot):
                    cp.start()
            return c
        lax.fori_loop(0, n_pages // unroll, body, 0)

    @pl.when(b == 0)
    def _():
        issue(0, 0)

    @pl.when(b + 1 < nb)
    def _():
        issue(b + 1, (b + 1) % 2)

    slot = b % 2

    def wait(i, c):
        for u in range(unroll):
            for cp in copies(b, i * unroll + u, slot):
                cp.wait()
        return c

    lax.fori_loop(0, n_pages // unroll, wait, 0)

    hpp = page // CMP_STRIDE

    def untranspose(i, c):
        for u in range(unroll):
            p = i * unroll + u
            base = pl.multiple_of(p * (hpp * ROW_PITCH), SUBLANES)
            kp = stage_k[slot, p].T
            vp = stage_v[slot, p].T
            for n in range(hpp):
                rows = pl.ds(base + n * ROW_PITCH, CMP_STRIDE)
                kbuf[rows, :] = kp[n * CMP_STRIDE:(n + 1) * CMP_STRIDE]
                vbuf[rows, :] = vp[n * CMP_STRIDE:(n + 1) * CMP_STRIDE]
        return c

    lax.fori_loop(0, n_pages // unroll, untranspose, 0)
    first_new = (past // CMP_STRIDE) * ROW_PITCH
    tail = nh8 * ROW_PITCH - first_new
    kbuf[first_new:, :] = jnp.zeros((tail, LANES), F32)
    vbuf[first_new:, :] = jnp.zeros((tail, LANES), F32)
    kbuf[first_new:first_new + 1, :] = kvn_ref[0, :, 0:LANES]
    vbuf[first_new:first_new + 1, :] = kvn_ref[0, :, LANES:2 * LANES]

    pad = jnp.zeros((nhp - nh8, ATT_WIDTH), F32)
    kc4 = jnp.concatenate([_compress_block(kbuf, nh8, w1k_ref, b1k_ref, w2k_ref, ROW_PITCH), pad], axis=0).astype(BF16)
    vc4 = jnp.concatenate([_compress_block(vbuf, nh8, w1v_ref, b1v_ref, w2v_ref, ROW_PITCH), pad], axis=0).astype(BF16)
    n_cmp = (past + 1 + CMP_STRIDE - 1) // CMP_STRIDE - 1

    qm = _head_rows(q_ref[0].astype(F32)).astype(BF16)
    nidx = lax.broadcasted_iota(jnp.int32, (N_HEADS, nhp), 1)
    dist_i = q_pos - (nidx * CMP_STRIDE + (CMP_BLOCK - 1))
    mask = (dist_i >= 0) & (nidx < n_cmp)
    s = _dot_nt(qm, kc4) - _slope_col() * dist_i.astype(F32)
    sm = jnp.where(mask, s, NEG)
    e = jnp.exp(sm - jnp.max(sm, axis=-1, keepdims=True))
    p = jnp.where(mask, e / jnp.sum(e, axis=-1, keepdims=True), 0.0)
    o = _head_diag(_dot(p.astype(BF16), vc4))
    g8 = jnp.broadcast_to(jax.nn.sigmoid(gd_ref[0]), (SUBLANES, LANES))
    gate = _expand(g8, e0_ref[...])[0:1, :]
    ocmp_ref[0] = gate * o

    js = ov_ref.shape[1]
    imp8 = _expand(p, ov_ref[...])
    hrow = lax.broadcasted_iota(jnp.int32, (N_HEADS, js), 0)
    jl = lax.broadcasted_iota(jnp.int32, (1, js), 1)
    cur = q_pos // SLC_BLOCK
    forced = (jl == 0) | (jl == cur) | (jl == cur - 1)
    valid = (jl * SLC_BLOCK <= q_pos) & (jl < n_slc)
    rj = lax.broadcasted_iota(jnp.int32, (js, js), 0)
    cj = lax.broadcasted_iota(jnp.int32, (js, js), 1)
    kk = lax.broadcasted_iota(jnp.int32, (2 * SUBLANES, js), 0)
    jvals = jnp.broadcast_to(jl.astype(F32), (SUBLANES, js)).astype(BF16)
    for g in range(N_KV):
        imp = jnp.sum(jnp.where(hrow // HG == g, imp8, 0.0), axis=0, keepdims=True)
        score = jnp.where(valid, imp + jnp.where(forced, FORCE_BONUS, 0.0), NEG)
        sb = jnp.broadcast_to(score, (js, js))
        col = jnp.sum(jnp.where(rj == cj, sb, 0.0), axis=1, keepdims=True)
        ahead = (col > sb) | ((col == sb) & (rj < cj))
        rank = jnp.sum(ahead.astype(F32), axis=0, keepdims=True)
        sel = ((rank < N_SEL) & (jl < n_slc)).astype(F32)
        pos = _dot(jnp.broadcast_to(sel, (SUBLANES, js)).astype(BF16), u_ref[...])[0:1, :]
        onehot = ((jnp.broadcast_to(pos, (2 * SUBLANES, js)) == kk.astype(F32))
                  & (jnp.broadcast_to(sel, (2 * SUBLANES, js)) > 0.5)).astype(BF16)
        idx = _dot_nt(jvals, onehot)[0:1, :]
        idx_ref[0, g:g + 1, :] = idx.astype(jnp.int32)


def _s_cmp(layer, page_table, q, kvn, gd, cache_k, cache_v, cw4, consts, q_pos):
    ns, n_pages = page_table.shape
    page = cache_k.shape[3]
    past = n_pages * page
    nh = (past + 1 + CMP_STRIDE - 1) // CMP_STRIDE
    nh8 = -(-nh // SUBLANES) * SUBLANES
    nhp = -(-nh // LANES) * LANES
    n_slc = past // SLC_BLOCK + 1
    ws = [cw4[n] for n in ("w1k", "b1k", "w2k", "w1v", "b1v", "w2v")] + [consts["ov_s"], consts["u_s"], consts["e_gate"][0]]
    full = lambda a: pl.BlockSpec(a.shape, lambda b, pt: (0,) * a.ndim)
    per_b = lambda a: pl.BlockSpec((1,) + a.shape[1:], lambda b, pt: (b,) + (0,) * (a.ndim - 1))
    grid_spec = pltpu.PrefetchScalarGridSpec(
        num_scalar_prefetch=1,
        grid=(ns,),
        in_specs=[per_b(q), per_b(kvn), per_b(gd), pl.BlockSpec(memory_space=pl.ANY), pl.BlockSpec(memory_space=pl.ANY)]
                 + [_wspec(w) for w in ws],
        out_specs=[pl.BlockSpec((1, 1, ATT_WIDTH), lambda b, pt: (b, 0, 0)),
                   pl.BlockSpec((1, N_KV, N_SEL), lambda b, pt: (b, 0, 0))],
        scratch_shapes=[pltpu.VMEM((2, n_pages, LANES, page), F32), pltpu.VMEM((2, n_pages, LANES, page), F32),
                        pltpu.VMEM((nh8 * ROW_PITCH, LANES), F32), pltpu.VMEM((nh8 * ROW_PITCH, LANES), F32),
                        pltpu.SemaphoreType.DMA((2, 2))],
    )
    return pl.pallas_call(
        functools.partial(_s_cmp_kernel, layer=layer, n_pages=n_pages, page=page, nh8=nh8, nhp=nhp, q_pos=q_pos,
                          n_slc=n_slc),
        out_shape=[jax.ShapeDtypeStruct((ns, 1, ATT_WIDTH), F32), jax.ShapeDtypeStruct((ns, N_KV, N_SEL), jnp.int32)],
        grid_spec=grid_spec,
        compiler_params=_cparams(("arbitrary",)),
        name="sample_cmp_select",
    )(page_table.reshape(-1), q, kvn, gd, cache_k, cache_v, *[_warr(w) for w in ws])


def _s_slc_win_kernel(idx_ref, pt_ref, q_ref, kvn_ref, gd_ref, ocmp_ref, wk_ref, wv_ref, sk_hbm, sv_hbm,
                      tile_ref, e1_ref, e2_ref, o_ref, wko_ref, wvo_ref, kb, vb, sem,
                      *, layer, n_pages, page, q_pos, wb):
    b = pl.program_id(0)
    nb = pl.num_programs(0)
    bpp = page // SLC_BLOCK
    n_past = n_pages * bpp
    nblk = N_KV * N_SEL
    blks = [idx_ref[b * nblk + i] for i in range(nblk)]
    slot = b % 2

    def copies(bb, sl, i):
        blk = idx_ref[bb * nblk + i]
        pg = pt_ref[bb * n_pages + jnp.minimum(blk, n_past - 1) // bpp]
        dst = slice(i * page, (i + 1) * page)
        return (pltpu.make_async_copy(sk_hbm.at[layer, pg], kb.at[sl, :, dst], sem.at[0, sl]),
                pltpu.make_async_copy(sv_hbm.at[layer, pg], vb.at[sl, :, dst], sem.at[1, sl]))

    def issue(bb, sl):
        for i in range(nblk):
            for cp in copies(bb, sl, i):
                cp.start()

    @pl.when(b == 0)
    def _():
        issue(0, 0)

    @pl.when(b + 1 < nb)
    def _():
        issue(b + 1, (b + 1) % 2)

    tile_m = tile_ref[...]
    q8 = _dot_nt(_head_rows(q_ref[0].astype(F32)).astype(BF16), tile_m).astype(BF16)
    slope = _slope_col()
    g8 = jnp.broadcast_to(jax.nn.sigmoid(gd_ref[0]), (SUBLANES, LANES))
    gate1 = _expand(g8, e1_ref[...])[0:1, :]
    gate2 = _expand(g8, e2_ref[...])[0:1, :]
    rgrp = lax.broadcasted_iota(jnp.int32, (N_HEADS, 1), 0) // HG

    def new_key(lane0):
        kn = kvn_ref[0, :, lane0:lane0 + LANES].astype(BF16).astype(F32)
        vn = kvn_ref[0, :, lane0 + LANES:lane0 + 2 * LANES].astype(BF16).astype(F32)
        return jnp.sum(q8.astype(F32) * kn, axis=-1, keepdims=True), vn

    def finish(e, e_new, v_t, v_new, l):
        o8 = (_dot_nt(e.astype(BF16), v_t.astype(BF16)) + e_new * v_new) / l
        return _head_diag(_expand(o8, tile_m))

    def shifted(w_t, new_row):
        col = jnp.broadcast_to(new_row, (LANES, LANES)).T[:, 0:1]
        lane = lax.broadcasted_iota(jnp.int32, (LANES, wb), 1)
        return jnp.where(lane == wb - 1, col, pltpu.roll(w_t, wb - 1, 1))

    wk_t = wk_ref[0, 0]
    wv_t = wv_ref[0, 0]
    kpos = (q_pos - wb) + lax.broadcasted_iota(jnp.int32, (N_HEADS, wb), 1)
    dist = q_pos - kpos
    wmask = (dist >= 0) & (dist <= WINDOW)
    s = jnp.where(wmask, _dot(q8, wk_t.astype(BF16)) - slope * dist.astype(F32), NEG)
    s_new, v_new = new_key(4 * LANES)
    m = jnp.maximum(jnp.max(s, axis=-1, keepdims=True), s_new)
    e = jnp.where(wmask, jnp.exp(s - m), 0.0)
    e_new = jnp.exp(s_new - m)
    o_win = finish(e, e_new, wv_t, v_new, jnp.sum(e, axis=-1, keepdims=True) + e_new)
    wko_ref[0] = shifted(wk_t, kvn_ref[0, :, 4 * LANES:5 * LANES])
    wvo_ref[0] = shifted(wv_t, kvn_ref[0, :, 5 * LANES:6 * LANES])

    for i in range(nblk):
        for cp in copies(b, slot, i):
            cp.wait()
    lane = lax.broadcasted_iota(jnp.int32, (1, page), 1)
    kpos_t, ok_t = [], []
    has_new = [jnp.zeros((1, 1), jnp.int32) for _ in range(N_KV)]
    for i in range(nblk):
        blk = blks[i]
        kpos_t.append(blk * SLC_BLOCK + lane % SLC_BLOCK)
        ok_t.append(((lane // SLC_BLOCK) == blk % bpp) & (blk < n_past))
        has_new[i // N_SEL] = jnp.maximum(has_new[i // N_SEL], (blk >= n_past).astype(jnp.int32))
    kpos = jnp.concatenate(kpos_t, axis=1)
    ok = jnp.concatenate(ok_t, axis=1)
    pgrp = lax.broadcasted_iota(jnp.int32, (1, nblk * page), 1) // (N_SEL * page)
    dist = q_pos - kpos
    kmask = ok & (dist >= 0) & (rgrp == pgrp)
    s = jnp.where(kmask, _dot(q8, kb[slot].astype(BF16)) - slope * dist.astype(F32), NEG)
    s_new, v_new = new_key(2 * LANES)
    new_on = jnp.where(rgrp == 0, has_new[0], has_new[1]) > 0
    sn = jnp.where(new_on, s_new, NEG)
    m = jnp.maximum(jnp.max(s, axis=-1, keepdims=True), sn)
    e = jnp.where(kmask, jnp.exp(s - m), 0.0)
    e_new = jnp.where(new_on, jnp.exp(sn - m), 0.0)
    o_slc = finish(e, e_new, vb[slot], v_new, jnp.sum(e, axis=-1, keepdims=True) + e_new)

    o_ref[0] = (ocmp_ref[0] + gate1 * o_slc + gate2 * o_win).astype(BF16)


def _s_slc_win(layer, idx, page_table, q, kvn, gd, ocmp, win_k, win_v, slc_k, slc_v, consts, q_pos):
    ns, n_pages = page_table.shape
    page = slc_k.shape[3]
    wb = win_k.shape[3]
    full = lambda a: pl.BlockSpec(a.shape, lambda b, i, pt: (0,) * a.ndim)
    per_b = lambda a: pl.BlockSpec((1,) + a.shape[1:], lambda b, i, pt: (b,) + (0,) * (a.ndim - 1))
    win = pl.BlockSpec((1, 1, LANES, wb), lambda b, i, pt: (layer, b, 0, 0))
    ws = [consts["tile_m"], consts["e_gate"][1], consts["e_gate"][2]]
    nlane = N_KV * N_SEL * page
    grid_spec = pltpu.PrefetchScalarGridSpec(
        num_scalar_prefetch=2,
        grid=(ns,),
        in_specs=[per_b(q), per_b(kvn), per_b(gd), per_b(ocmp), win, win,
                  pl.BlockSpec(memory_space=pl.ANY), pl.BlockSpec(memory_space=pl.ANY)] + [full(w) for w in ws],
        out_specs=[pl.BlockSpec((1, 1, ATT_WIDTH), lambda b, i, pt: (b, 0, 0)),
                   pl.BlockSpec((1, LANES, wb), lambda b, i, pt: (b, 0, 0)),
                   pl.BlockSpec((1, LANES, wb), lambda b, i, pt: (b, 0, 0))],
        scratch_shapes=[pltpu.VMEM((2, LANES, nlane), F32), pltpu.VMEM((2, LANES, nlane), F32),
                        pltpu.SemaphoreType.DMA((2, 2))],
    )
    return pl.pallas_call(
        functools.partial(_s_slc_win_kernel, layer=layer, n_pages=n_pages, page=page, q_pos=q_pos, wb=wb),
        out_shape=[jax.ShapeDtypeStruct((ns, 1, ATT_WIDTH), BF16),
                   jax.ShapeDtypeStruct((ns, LANES, wb), F32), jax.ShapeDtypeStruct((ns, LANES, wb), F32)],
        grid_spec=grid_spec,
        compiler_params=_cparams(("arbitrary",)),
        name="sample_slc_win_attn",
    )(idx.reshape(-1), page_table.reshape(-1), q, kvn, gd, ocmp, win_k, win_v, slc_k, slc_v, *ws)


def _col_bcast(row):
    blocks = [jnp.broadcast_to(row[:, LANES * i:LANES * (i + 1)], (LANES, LANES)).T for i in range(row.shape[1] // LANES)]
    return jnp.concatenate(blocks, axis=0)


def _s_ssd_kernel(xbc_ref, prev_ref, z_ref, gd_ref, h0_ref, cw_ref, cb_ref, dtb_ref, a_ref, dsk_ref, ng_ref,
                  edt_ref, y_ref, h_ref):
    conv = cb_ref[...] + xbc_ref[0] * cw_ref[SSM_CONV - 1:SSM_CONV, :]
    for k in range(SSM_CONV - 1):
        conv = conv + prev_ref[0, k:k + 1, :] * cw_ref[k:k + 1, :]
    act = _silu(conv)
    xm = act[:, :SSM_INNER]
    bm = act[:, SSM_INNER:SSM_INNER + LANES]
    cm = act[:, SSM_INNER + LANES:]
    dt = _softplus(gd_ref[0] + dtb_ref[...])
    a = dt * a_ref[...]
    edt = edt_ref[...]
    dt_x = _expand(jnp.broadcast_to(dt, (SUBLANES, LANES)), edt)[0:1, :]
    dec_x = jnp.exp(_expand(jnp.broadcast_to(a, (SUBLANES, LANES)), edt)[0:1, :])
    dtx = dt_x * xm
    h0 = h0_ref[0, 0].reshape(SSM_INNER, D_STATE)
    half = SSM_INNER // SSM_GROUPS
    rsel = lax.broadcasted_iota(jnp.int32, (SSM_INNER, 1), 0) // half
    lsel = lax.broadcasted_iota(jnp.int32, (1, SSM_INNER), 1) // half
    y_off = jnp.zeros((1, SSM_INNER), F32)
    cbx = jnp.zeros((1, SSM_INNER), F32)
    brow = jnp.zeros((SSM_INNER, D_STATE), F32)
    for g in range(SSM_GROUPS):
        bg = bm[:, D_STATE * g:D_STATE * (g + 1)]
        cg = cm[:, D_STATE * g:D_STATE * (g + 1)]
        c8 = jnp.broadcast_to(cg, (SUBLANES, D_STATE)).astype(BF16)
        yo = _dot_nt(c8, h0[half * g:half * (g + 1), :].astype(BF16))[0:1, :]
        y_off = jnp.where(lsel == g, jnp.concatenate([yo] * SSM_GROUPS, axis=1), y_off)
        cbx = jnp.where(lsel == g, jnp.sum(cg * bg, axis=-1, keepdims=True), cbx)
        brow = jnp.where(rsel == g, jnp.broadcast_to(bg, (SSM_INNER, D_STATE)), brow)
    y = y_off * dec_x + cbx * dtx + dsk_ref[...] * xm
    y_ref[0] = _gated_norm(y, z_ref[0], ng_ref[...]).astype(BF16)
    h_new = h0 * _col_bcast(dec_x)[:, :D_STATE] + _col_bcast(dtx)[:, :D_STATE] * brow
    h_ref[0] = h_new.reshape(SSM_HEADS, HEAD_DIM, D_STATE)


def _s_ssd(layer, xbc, prev, z, gd, state_ssm, sw, consts):
    ns = state_ssm.shape[1]
    full = lambda a: pl.BlockSpec(a.shape, lambda b: (0,) * a.ndim)
    per_b = lambda a: pl.BlockSpec((1,) + a.shape[1:], lambda b: (b,) + (0,) * (a.ndim - 1))
    hshape = state_ssm.shape[2:]
    ws = [sw[n] for n in ("conv_w", "conv_b", "dt_bias", "a", "d_skip", "norm_g")] + [consts["e_dt"]]
    return pl.pallas_call(
        _s_ssd_kernel,
        out_shape=[jax.ShapeDtypeStruct((ns, 1, SSM_INNER), BF16), jax.ShapeDtypeStruct((ns,) + hshape, F32)],
        grid=(ns,),
        in_specs=[per_b(xbc), per_b(prev), per_b(z), per_b(gd),
                  pl.BlockSpec((1, 1) + hshape, lambda b: (layer, b, 0, 0, 0))] + [_wspec(w) for w in ws],
        out_specs=[pl.BlockSpec((1, 1, SSM_INNER), lambda b: (b, 0, 0)),
                   pl.BlockSpec((1,) + hshape, lambda b: (b, 0, 0, 0))],
        compiler_params=_cparams(("arbitrary",)),
        name="sample_ssd_step",
    )(xbc, prev, z, gd, state_ssm, *[_warr(w) for w in ws])


def _constants(t_prompt, past_len):
    c = {}
    eg = np.zeros((3, LANES, ATT_WIDTH), np.float32)
    for br in range(3):
        for h in range(N_HEADS):
            eg[br, br * N_HEADS + h, h * HEAD_DIM:(h + 1) * HEAD_DIM] = 1.0
    c["e_gate"] = [jnp.asarray(eg[i], BF16) for i in range(3)]
    c["e_gate_t"] = jnp.asarray(eg.transpose(0, 2, 1), BF16)
    ed = np.zeros((LANES, SSM_INNER), np.float32)
    for h in range(SSM_HEADS):
        ed[GD_DT + h, h * HEAD_DIM:(h + 1) * HEAD_DIM] = 1.0
    c["e_dt"] = jnp.asarray(ed, BF16)

    def overlap(n_cmp_pad, n_slc, n_slc_pad):
        cs = np.arange(n_cmp_pad)[:, None] * CMP_STRIDE
        ss = np.arange(n_slc_pad)[None, :] * SLC_BLOCK
        ov = ((cs < ss + SLC_BLOCK) & (cs + CMP_BLOCK > ss) & (np.arange(n_slc_pad)[None, :] < n_slc))
        return ov.astype(np.float32)
    nh_p = t_prompt // CMP_STRIDE
    n_slc_p = t_prompt // SLC_BLOCK
    js_p = -(-n_slc_p // SUBLANES) * SUBLANES
    ov = overlap(nh_p, n_slc_p, js_p)
    ov[nh_p - 1:, :] = 0.0
    c["ot"] = jnp.asarray(ov.T, BF16)
    col = np.arange(NCOL)
    slope = np.asarray(SLOPES, np.float64)[col // TQ]
    c["srow2"] = jnp.asarray(LOG2E * slope[None, :], F32)
    c["c0"] = jnp.asarray(LOG2E * slope[None, :] * ((col % TQ)[None, :]
                                                    - (np.arange(nh_p)[:, None] * CMP_STRIDE + CMP_BLOCK - 1)), F32)
    def bf16_terms(x):
        terms, r = [], np.asarray(x, np.float64)
        for _ in range(3):
            tb = np.asarray(r, np.float32).astype(BF16).astype(np.float64)
            terms.append(tb)
            r = r - tb
        return terms
    coef = np.zeros((SUBLANES, NCOL), np.float64)
    coef[0:3] = np.stack(bf16_terms(LOG2E * SLC_BLOCK * slope))
    coef[3:6] = np.stack(bf16_terms(LOG2E * slope))
    c["coef"] = jnp.asarray(coef, F32)
    ntile = t_prompt // TQ
    kpos = np.arange(t_prompt).reshape(ntile, TQ)
    fk = np.zeros((ntile, TQ, LANES), np.float32)
    fk[:, :, 0:3] = (kpos // SLC_BLOCK)[:, :, None]
    fk[:, :, 3:6] = (kpos % SLC_BLOCK)[:, :, None]
    fk[:, :, 6] = (np.arange(TQ) < SLC_BLOCK)[None, :]
    fk[:, :, 7] = (np.arange(TQ) >= SLC_BLOCK)[None, :]
    c["fk"] = jnp.asarray(fk, BF16)
    rel = np.arange(TQ)[:, None] - (col % TQ)[None, :]
    c["amask"] = jnp.asarray(np.stack([np.zeros(rel.shape), np.where(rel > 0, -MASK_BIG, 0.0),
                                       np.where(rel < 0, -MASK_BIG, 0.0), np.full(rel.shape, -MASK_BIG)]), F32)
    nh_s = (past_len + 1 + CMP_STRIDE - 1) // CMP_STRIDE
    nhp_s = -(-nh_s // LANES) * LANES
    n_slc_s = past_len // SLC_BLOCK + 1
    js_s = -(-n_slc_s // LANES) * LANES
    ov_s = overlap(nhp_s, n_slc_s, js_s)
    ov_s[nh_s - 1:, :] = 0.0
    c["ov_s"] = jnp.asarray(ov_s, BF16)
    c["u_s"] = jnp.asarray(np.triu(np.ones((js_s, js_s), np.float32), 1), BF16)
    tm = np.zeros((LANES, ATT_WIDTH), np.float32)
    for h in range(N_HEADS):
        g = h // HG
        for d in range(HEAD_DIM):
            tm[g * HEAD_DIM + d, h * HEAD_DIM + d] = 1.0
    c["tile_m"] = jnp.asarray(tm, BF16)
    return c


def _pack_w_in(w_in):
    depth, d, _ = w_in.shape
    cuts = np.cumsum([ATT_WIDTH] + [LANES] * 6 + [3 * N_HEADS, SSM_INNER, CONV_DIM])
    q, kv, gt, z, xbc, dtc = (w_in[..., :cuts[0]], w_in[..., cuts[0]:cuts[6]], w_in[..., cuts[6]:cuts[7]],
                              w_in[..., cuts[7]:cuts[8]], w_in[..., cuts[8]:cuts[9]], w_in[..., cuts[9]:])
    pad = jnp.zeros((depth, d, P_W - P_GD - gt.shape[-1] - dtc.shape[-1]), w_in.dtype)
    return jnp.concatenate([q, kv, z, xbc, gt, dtc, pad], axis=-1).astype(BF16)


def _pack_compress(w1, b1, w2):
    depth = w1.shape[0]
    zeros = jnp.zeros((depth, CMP_STRIDE, HEAD_DIM, CMP_HID), w1.dtype)
    parts = []
    for ab in range(2):
        w = w1[:, ab * CMP_STRIDE:(ab + 1) * CMP_STRIDE]
        top = jnp.concatenate([w, zeros], axis=3)
        bot = jnp.concatenate([zeros, w], axis=3)
        parts.append(jnp.concatenate([top, bot], axis=2))
    w1b = jnp.concatenate(parts, axis=3)
    w1b = w1b.reshape(depth, CMP_STRIDE // 2, 2 * LANES, 4 * CMP_HID).astype(BF16)
    b1b = jnp.concatenate([b1, b1], axis=1).reshape(depth, 1, 2 * CMP_HID)
    z2 = jnp.zeros_like(w2)
    def blockdiag(rep):
        top = jnp.concatenate([w2] * rep + [z2] * rep, axis=2)
        bot = jnp.concatenate([z2] * rep + [w2] * rep, axis=2)
        return jnp.concatenate([top, bot], axis=1).astype(BF16)
    return w1b, b1b, blockdiag(1), blockdiag(HG)


def _lane_rows(vals, offset, width):
    depth, n = vals.shape
    return jnp.pad(vals.astype(F32), ((0, 0), (offset, width - offset - n))).reshape(depth, 1, width)


def _token_minor(a):
    lead = a.shape[:-3]
    n = len(lead)
    return jnp.transpose(a, tuple(range(n)) + (n + 1, n + 2, n)).reshape(lead + (N_KV * HEAD_DIM, a.shape[-3]))


def _token_major(a):
    lead = a.shape[:-2]
    n = len(lead)
    a = a.reshape(lead + (N_KV, HEAD_DIM, a.shape[-1]))
    return jnp.transpose(a, tuple(range(n)) + (n + 2, n, n + 1))


def kernel(x_prompt, x_sample, cache_cmp_k, cache_cmp_v, cache_slc_k, cache_slc_v, state_win_k, state_win_v, state_ssm, state_ssm_conv, state_ffn_conv, page_table, c_prompt, c_sample, ada_w, ada_b, norm1_g, norm2_g, w_in, cmpk_w1, cmpk_b1, cmpk_w2, cmpv_w1, cmpv_b1, cmpv_w2, ssm_conv_w, ssm_conv_b, dt_bias, a_log, d_skip, ssm_norm_g, w_out, ffn_w_up, ffn_conv_w, ffn_conv_b, ffn_w_down, final_g):
    bp, tp, d = x_prompt.shape
    ns = x_sample.shape[0]
    depth = w_in.shape[0]
    page = cache_cmp_k.shape[2]
    past_len = page_table.shape[1] * page
    assert x_sample.shape[1] == 1 and d == D_MODEL
    assert tp % SSM_CHUNK == 0 and tp >= WINDOW and past_len >= WINDOW and page % SLC_BLOCK == 0
    consts = _constants(tp, past_len)
    caches_t = [_token_minor(c) for c in (cache_cmp_k, cache_cmp_v, cache_slc_k, cache_slc_v)]
    win_t = [_token_minor(w) for w in (state_win_k, state_win_v)]

    mod = _mod(jnp.concatenate([c_sample, c_prompt], axis=0), ada_w, ada_b)
    mod_rows_p = mod.reshape(depth, ns + bp, 1, 6 * d)
    mod_rows_s = mod.reshape(depth, 1, ns + bp, 6 * d)
    tm_p = 512 if tp % 512 == 0 else SSM_CHUNK

    w_packed = _pack_w_in(w_in)
    w1k, b1k, w2k, w2k4 = _pack_compress(cmpk_w1, cmpk_b1, cmpk_w2)
    w1v, b1v, w2v, w2v4 = _pack_compress(cmpv_w1, cmpv_b1, cmpv_w2)
    norm1 = norm1_g.reshape(depth, 1, d)
    sw_all = dict(conv_w=ssm_conv_w, conv_b=ssm_conv_b.reshape(depth, 1, CONV_DIM),
                  dt_bias=_lane_rows(dt_bias, GD_DT, LANES), a=_lane_rows(-jnp.exp(a_log.astype(F32)), GD_DT, LANES),
                  d_skip=jnp.repeat(d_skip.astype(F32), HEAD_DIM, axis=1).reshape(depth, 1, SSM_INNER),
                  norm_g=ssm_norm_g.astype(F32).reshape(depth, 1, SSM_INNER))
    fw_all = dict(norm2_g=norm2_g.reshape(depth, 1, d), w_out=w_out.astype(BF16), w_up=ffn_w_up.astype(BF16),
                  conv_w=ffn_conv_w, conv_b=ffn_conv_b.reshape(depth, 1, D_FF), w_down=ffn_w_down.astype(BF16))

    xp = x_prompt
    xs = x_sample.reshape(1, ns, d)
    outs_p, outs_s = [], []
    for l in range(depth):
        last = l == depth - 1
        mod_p = (mod_rows_p, l, ns, False)
        mod_s = (mod_rows_s, l, ns, True)
        at = lambda a: (a, l)
        cw = dict(w1k=at(w1k), b1k=at(b1k), w2k=at(w2k), w1v=at(w1v), b1v=at(b1v), w2v=at(w2v))
        cw4 = dict(w1k=at(w1k), b1k=at(b1k), w2k=at(w2k4), w1v=at(w1v), b1v=at(b1v), w2v=at(w2v4))
        sw = {n: at(a) for n, a in sw_all.items()}
        fw = {n: at(a) for n, a in fw_all.items()}
        fw["final_g"] = final_g.reshape(1, d)

        (qt, kc, vc, kvb, vt, z, xbc, gd, kct, vct, kst, vst, kwt, vwt) = _inproj(
            xp, mod_p, at(norm1), at(w_packed), tm_p, True)
        kcmp, vcmpt = _compress(kc, vc, cw)
        ocmpt, selt = _cmp_sel(qt, kcmp, vcmpt, consts)
        oatt = _attn(qt, kvb, vt, selt, ocmpt, gd, consts)
        yssm, h_p = _ssd(xbc, z, gd, sw, consts)
        xp, ug_tail = _mix_ffn(xp, oatt, yssm, mod_p, fw, tm_p, None, last)
        outs_p.append((kct, vct, kst, vst, kwt[:, :, tp - WINDOW:], vwt[:, :, tp - WINDOW:], h_p,
                       xbc[:, tp - (SSM_CONV - 1):], ug_tail[:, SUBLANES - (FFN_CONV - 1):]))

        q, kvb, z, xbc, gd, kc, vc, ks, vs, kw, vw = _inproj(xs, mod_s, at(norm1), at(w_packed), ns, False)
        per_tok = lambda a: a.reshape(ns, 1, a.shape[-1])
        kvn = per_tok(jnp.concatenate([kc, vc, ks, vs, kw, vw], axis=2))
        q, z, xbc, gd = per_tok(q), per_tok(z), per_tok(xbc), per_tok(gd)
        ocmp, idx = _s_cmp(l, page_table, q, kvn, gd, caches_t[0], caches_t[1], cw4, consts, past_len)
        oatt, wk_new, wv_new = _s_slc_win(l, idx, page_table, q, kvn, gd, ocmp, win_t[0], win_t[1],
                                          caches_t[2], caches_t[3], consts, past_len)
        yssm, h_s = _s_ssd(l, xbc, state_ssm_conv[l], z, gd, state_ssm, sw, consts)
        prev_ffn = (state_ffn_conv[l][:, 0].reshape(1, ns, D_FF), state_ffn_conv[l][:, 1].reshape(1, ns, D_FF))
        xs, ug_s = _mix_ffn(xs, oatt.reshape(1, ns, ATT_WIDTH), yssm.reshape(1, ns, SSM_INNER), mod_s, fw, ns,
                            prev_ffn, last)
        s4 = lambda a: a.reshape(ns, 1, N_KV, HEAD_DIM)
        outs_s.append((s4(kc), s4(vc), s4(ks), s4(vs), wk_new, wv_new, h_s,
                       jnp.concatenate([state_ssm_conv[l][:, 1:], xbc.reshape(ns, 1, CONV_DIM)], axis=1),
                       jnp.concatenate([state_ffn_conv[l][:, 1:], ug_s.reshape(ns, 1, D_FF)], axis=1)))

    stack = lambda seq, i: jnp.stack([st[i] for st in seq])
    res = [xp, xs.reshape(ns, 1, d)]
    for i in range(9):
        p_i, s_i = stack(outs_p, i), stack(outs_s, i)
        if i < 6:
            p_i = _token_major(p_i)
        if i in (4, 5):
            s_i = _token_major(s_i)
        res += [p_i, s_i]
    return tuple(res)
```

```python
import functools

import numpy as np
import jax
import jax.numpy as jnp
from jax import lax
from jax.experimental import pallas as pl
from jax.experimental.pallas import tpu as pltpu

F32 = jnp.float32
BF16 = jnp.bfloat16

D_MODEL = 1024
HEAD_DIM = 64
ATT_WIDTH = 512
N_HEADS = 8
N_KV = 2
HG = 4
CMP_BLOCK = 32
CMP_STRIDE = 16
CMP_HID = 128
SLC_BLOCK = 64
N_SEL = 16
WINDOW = 512
FORCE_BONUS = 1e6
SSM_INNER = 512
SSM_HEADS = 8
SSM_GROUPS = 2
D_STATE = 64
SSM_CONV = 4
SSM_CHUNK = 128
CONV_DIM = 768
D_FF = 2816
FFN_CONV = 3
EPS = 1e-6
NEG = -1e30
SCALE = HEAD_DIM ** -0.5
LOG2E = 1.4426950408889634
SLOPES = tuple(2.0 ** (-(h + 1)) for h in range(N_HEADS))

LANES = 128
SUBLANES = 8
VMEM_LIMIT = 56 * 1024 * 1024

FF_CHUNKS = ((0, 768), (768, 1536), (1536, 2176), (2176, 2816))
ROW_PITCH = 24
TQ = 2 * SLC_BLOCK
NCOL = N_HEADS * TQ

P_Q = 0
P_KV = 512
P_Z = 1280
P_XBC = 1792
P_GD = 2560
P_W = 2688
GD_DT = 24
MASK_BIG = 2.0 ** 100


def _cparams(sem):
    return pltpu.CompilerParams(dimension_semantics=sem, vmem_limit_bytes=VMEM_LIMIT)


def _resident(shape):
    nd = len(shape)
    return pl.BlockSpec(shape, lambda *_: (0,) * nd, pipeline_mode=pl.Buffered(1))


def _wspec(w):
    if isinstance(w, tuple):
        arr, layer = w
        nd = arr.ndim
        return pl.BlockSpec((None,) + arr.shape[1:], lambda *_: (layer,) + (0,) * (nd - 1),
                            pipeline_mode=pl.Buffered(1))
    return _resident(w.shape)


def _warr(w):
    return w[0] if isinstance(w, tuple) else w


def _split3(x):
    hi = x.astype(BF16)
    r = x - hi.astype(F32)
    mid = r.astype(BF16)
    lo = (r - mid.astype(F32)).astype(BF16)
    return hi, mid, lo


def _dot(a, b):
    return jnp.dot(a, b, preferred_element_type=F32)


def _dot_nt(a, b):
    return lax.dot_general(a, b, (((1,), (1,)), ((), ())), preferred_element_type=F32)


def _expand(x, e):
    hi, mid, lo = _split3(x)
    return _dot(hi, e) + _dot(mid, e) + _dot(lo, e)


def _expand_l(e, x):
    hi, mid, lo = _split3(x)
    return _dot(e, hi) + _dot(e, mid) + _dot(e, lo)


def _silu(x):
    return x * jax.nn.sigmoid(x)


def _softplus(x):
    return jnp.maximum(x, 0.0) + jnp.log1p(jnp.exp(-jnp.abs(x)))


def _gelu_tanh(x):
    return 0.5 * x * (1.0 + jnp.tanh(np.sqrt(2.0 / np.pi) * (x + 0.044715 * (x * x * x))))


def _mod_kernel(c_ref, w_ref, b_ref, o_ref):
    c = c_ref[...]
    s = _silu(c).astype(BF16)
    o_ref[0] = _dot(s, w_ref[0].astype(BF16)) + b_ref[0]


def _mod(c_all, ada_w, ada_b):
    depth, d, n6 = ada_w.shape
    rows = c_all.shape[0]
    tn = 512
    return pl.pallas_call(
        _mod_kernel,
        out_shape=jax.ShapeDtypeStruct((depth, rows, n6), F32),
        grid=(depth, n6 // tn),
        in_specs=[pl.BlockSpec((rows, d), lambda l, j: (0, 0)),
                  pl.BlockSpec((1, d, tn), lambda l, j: (l, 0, j)),
                  pl.BlockSpec((1, 1, tn), lambda l, j: (l, 0, j))],
        out_specs=pl.BlockSpec((1, rows, tn), lambda l, j: (l, 0, j)),
        compiler_params=_cparams(("arbitrary", "arbitrary")),
        name="adaln_mod",
    )(c_all, ada_w, ada_b.reshape(depth, 1, n6))


def _mod_spec(mod, k):
    arr, layer, n, per_row = mod
    if per_row:
        return pl.BlockSpec((None, 1, n, D_MODEL), lambda b, t: (layer, 0, 0, k))
    return pl.BlockSpec((None, 1, 1, D_MODEL), lambda b, t: (layer, n + b, 0, k))


def _inproj_kernel(x_ref, sh_ref, sc_ref, g_ref, w_ref, *outs, tm, token_minor):
    x = x_ref[0]
    ms = jnp.mean(x * x, axis=-1, keepdims=True)
    xn = x * lax.rsqrt(ms + EPS) * g_ref[...]
    h = xn * (1.0 + sc_ref[0]) + sh_ref[0]
    proj = _dot(h.astype(BF16), w_ref[...])
    kv = [proj[:, P_KV + LANES * i:P_KV + LANES * (i + 1)] for i in range(6)]
    if token_minor:
        qt_ref, kc_ref, vc_ref, kvb_ref, vt_ref, z_ref, xbc_ref, gd_ref = outs[:8]
        qt_ref[0] = (proj[:, P_Q:P_Q + ATT_WIDTH] * (SCALE * LOG2E)).T.astype(BF16)
        kc_ref[0] = kv[0]
        vc_ref[0] = kv[1]
        kvt = [a.T for a in kv]
        for ref, a in zip(outs[8:], kvt):
            ref[0] = a
        for r in range(tm // TQ):
            cols = slice(r * TQ, (r + 1) * TQ)
            vt_ref[0, r] = jnp.concatenate([kvt[3][:, cols], kvt[5][:, cols]], axis=0).astype(BF16)
    else:
        q_ref, kvb_ref, z_ref, xbc_ref, gd_ref = outs[:5]
        q_ref[0] = (proj[:, P_Q:P_Q + ATT_WIDTH] * SCALE).astype(BF16)
        for ref, a in zip(outs[5:], kv):
            ref[0] = a
    kvb_ref[0] = proj[:, P_KV:P_Z].astype(BF16)
    z_ref[0] = proj[:, P_Z:P_XBC]
    xbc_ref[0] = proj[:, P_XBC:P_GD]
    gd_ref[0] = proj[:, P_GD:P_W]


def _inproj(x, mod, norm_g, w_packed, tm, token_minor):
    bsz, t, d = x.shape
    row = lambda w: pl.BlockSpec((1, tm, w), lambda b, i: (b, i, 0))
    col = lambda w: pl.BlockSpec((1, w, tm), lambda b, i: (b, 0, i))
    shp = lambda w, dt: jax.ShapeDtypeStruct((bsz, t, w), dt)
    shpt = lambda w, dt: jax.ShapeDtypeStruct((bsz, w, t), dt)
    if token_minor:
        out_shape = ([shpt(ATT_WIDTH, BF16), shp(LANES, F32), shp(LANES, F32), shp(6 * LANES, BF16),
                      jax.ShapeDtypeStruct((bsz, t // TQ, 2 * LANES, TQ), BF16),
                      shp(SSM_INNER, F32), shp(CONV_DIM, F32), shp(LANES, F32)] + [shpt(LANES, F32)] * 6)
        out_specs = ([col(ATT_WIDTH), row(LANES), row(LANES), row(6 * LANES),
                      pl.BlockSpec((1, tm // TQ, 2 * LANES, TQ), lambda b, i: (b, i, 0, 0)),
                      row(SSM_INNER), row(CONV_DIM), row(LANES)] + [col(LANES)] * 6)
    else:
        out_shape = ([shp(ATT_WIDTH, BF16), shp(6 * LANES, BF16), shp(SSM_INNER, F32), shp(CONV_DIM, F32),
                      shp(LANES, F32)] + [shp(LANES, F32)] * 6)
        out_specs = [row(ATT_WIDTH), row(6 * LANES), row(SSM_INNER), row(CONV_DIM), row(LANES)] + [row(LANES)] * 6
    return pl.pallas_call(
        functools.partial(_inproj_kernel, tm=tm, token_minor=token_minor),
        out_shape=out_shape,
        grid=(bsz, t // tm),
        in_specs=[row(d), _mod_spec(mod, 0), _mod_spec(mod, 1), _wspec(norm_g), _wspec(w_packed)],
        out_specs=out_specs,
        compiler_params=_cparams(("arbitrary", "arbitrary")),
        name="inproj",
    )(x, mod[0], mod[0], _warr(norm_g), _warr(w_packed))


def _compress_products(src_ref, nh, w1_ref, pitch=CMP_STRIDE):
    acc = jnp.zeros((nh, 4 * CMP_HID), F32)
    for i in range(CMP_STRIDE // 2):
        xa = src_ref[pl.ds(2 * i, nh, stride=pitch), :]
        xb = src_ref[pl.ds(2 * i + 1, nh, stride=pitch), :]
        acc = acc + _dot(jnp.concatenate([xa, xb], axis=1).astype(BF16), w1_ref[i])
    return acc


def _compress_summaries(acc, nh, b1_ref, w2_ref):
    pa = acc[:, :2 * CMP_HID]
    pb = pltpu.roll(acc[:, 2 * CMP_HID:], nh - 1, 0)
    hid = _gelu_tanh(pa + pb + b1_ref[...])
    return _dot(hid.astype(BF16), w2_ref[...])


def _compress_kernel(k_ref, v_ref, w1k_ref, b1k_ref, w2k_ref, w1v_ref, b1v_ref, w2v_ref, ok_ref, ovt_ref, *, nh):
    acc_k = _compress_products(k_ref.at[0], nh, w1k_ref)
    acc_v = _compress_products(v_ref.at[0], nh, w1v_ref)
    ok_ref[0] = _compress_summaries(acc_k, nh, b1k_ref, w2k_ref).astype(BF16)
    ovt_ref[0] = _compress_summaries(acc_v, nh, b1v_ref, w2v_ref).T.astype(BF16)


def _compress(kc, vc, cw):
    bsz, t, _ = kc.shape
    nh = t // CMP_STRIDE
    seq = pl.BlockSpec((1, t, LANES), lambda b: (b, 0, 0))
    ws = [cw[n] for n in ("w1k", "b1k", "w2k", "w1v", "b1v", "w2v")]
    return pl.pallas_call(
        functools.partial(_compress_kernel, nh=nh),
        out_shape=[jax.ShapeDtypeStruct((bsz, nh, LANES), BF16), jax.ShapeDtypeStruct((bsz, LANES, nh), BF16)],
        grid=(bsz,),
        in_specs=[seq, seq] + [_wspec(w) for w in ws],
        out_specs=[pl.BlockSpec((1, nh, LANES), lambda b: (b, 0, 0)), pl.BlockSpec((1, LANES, nh), lambda b: (b, 0, 0))],
        compiler_params=_cparams(("arbitrary",)),
        name="compress",
    )(kc, vc, *[_warr(w) for w in ws])


def _all_sublanes(x, op):
    for shift in (4, 2, 1):
        x = op(x, pltpu.roll(x, shift, 0))
    return x


def _query_blockdiag(qt):
    zero = jnp.zeros((HEAD_DIM, HG * TQ), BF16)
    per_g = [jnp.concatenate([qt[HEAD_DIM * (HG * g + j):HEAD_DIM * (HG * g + j + 1), :] for j in range(HG)], axis=1)
             for g in range(N_KV)]
    return jnp.concatenate([jnp.concatenate([per_g[0], zero], axis=1),
                            jnp.concatenate([zero, per_g[1]], axis=1)], axis=0)


def _topk_mask_t(score, jrow, n_slc):
    rank = jnp.zeros(score.shape, F32)
    for jp in range(n_slc):
        sj = score[jp:jp + 1, :]
        ahead = (sj > score) | ((sj == score) & (jp < jrow))
        rank = rank + ahead.astype(F32)
    return (rank < N_SEL) & (jrow < n_slc)


def _cmp_sel_kernel(qt_ref, kc_ref, vct_ref, ot_ref, c0_ref, srow_ref, ocmpt_ref, selt_ref, *, n_cmp, n_slc, nsub):
    for sub in range(nsub):
        _cmp_sel_tile(pl.program_id(1) * nsub + sub, slice(sub * TQ, (sub + 1) * TQ), qt_ref, kc_ref, vct_ref,
                      ot_ref, c0_ref, srow_ref, ocmpt_ref, selt_ref, n_cmp, n_slc)


def _cmp_sel_tile(t, qcols, qt_ref, kc_ref, vct_ref, ot_ref, c0_ref, srow_ref, ocmpt_ref, selt_ref, n_cmp, n_slc):
    qbd = _query_blockdiag(qt_ref[0, :, qcols])
    kc = kc_ref[0]
    nl = kc.shape[0]
    srow = srow_ref[...]
    s = _dot(kc, qbd) - (c0_ref[...] + srow * (t * TQ).astype(F32))
    nidx = lax.broadcasted_iota(jnp.int32, (nl, NCOL), 0)
    qpos = t * TQ + lax.broadcasted_iota(jnp.int32, (nl, NCOL), 1) % TQ
    mask = (qpos - (nidx * CMP_STRIDE + (CMP_BLOCK - 1)) >= 0) & (nidx < n_cmp)
    sm = jnp.where(mask, s, NEG)
    e = jnp.exp2(sm - jnp.max(sm, axis=0, keepdims=True))
    p = jnp.where(mask, e * (1.0 / jnp.sum(e, axis=0, keepdims=True)), 0.0)
    pb = p.astype(BF16)
    vct = vct_ref[0]

    js = selt_ref.shape[2]
    jrow = lax.broadcasted_iota(jnp.int32, (js, TQ), 0)
    qp = t * TQ + lax.broadcasted_iota(jnp.int32, (js, TQ), 1)
    cur = qp // SLC_BLOCK
    forced = (jrow == 0) | (jrow == cur) | (jrow == cur - 1)
    valid = (jrow * SLC_BLOCK <= qp)
    for g in range(N_KV):
        c0 = g * HG * TQ
        og = _dot(vct[HEAD_DIM * g:HEAD_DIM * (g + 1), :], pb[:, c0:c0 + HG * TQ])
        pg = jnp.zeros((nl, TQ), F32)
        for j in range(HG):
            h = HG * g + j
            ocmpt_ref[0, HEAD_DIM * h:HEAD_DIM * (h + 1), qcols] = og[:, j * TQ:(j + 1) * TQ]
            pg = pg + p[:, c0 + j * TQ:c0 + (j + 1) * TQ]
        imp_t = _expand_l(ot_ref[...], pg)
        score = jnp.where(valid, imp_t + jnp.where(forced, FORCE_BONUS, 0.0), NEG)
        selt_ref[0, g, :, qcols] = _topk_mask_t(score, jrow, n_slc).astype(F32)


def _cmp_sel(qt, kcmp, vcmpt, consts):
    bsz, _, t = qt.shape
    nl = kcmp.shape[1]
    n_slc = t // SLC_BLOCK
    js = consts["ot"].shape[0]
    nsub = 2 if t % (2 * TQ) == 0 else 1
    tw = nsub * TQ
    return pl.pallas_call(
        functools.partial(_cmp_sel_kernel, n_cmp=t // CMP_STRIDE - 1, n_slc=n_slc, nsub=nsub),
        out_shape=[jax.ShapeDtypeStruct((bsz, ATT_WIDTH, t), F32),
                   jax.ShapeDtypeStruct((bsz, N_KV, js, t), F32)],
        grid=(bsz, t // tw),
        in_specs=[pl.BlockSpec((1, ATT_WIDTH, tw), lambda b, i: (b, 0, i)),
                  pl.BlockSpec((1, nl, LANES), lambda b, i: (b, 0, 0)),
                  pl.BlockSpec((1, LANES, nl), lambda b, i: (b, 0, 0)),
                  _resident(consts["ot"].shape), _resident(consts["c0"].shape), _resident(consts["srow2"].shape)],
        out_specs=[pl.BlockSpec((1, ATT_WIDTH, tw), lambda b, i: (b, 0, i)),
                   pl.BlockSpec((1, N_KV, js, tw), lambda b, i: (b, 0, 0, i))],
        compiler_params=_cparams(("arbitrary", "arbitrary")),
        name="cmp_select",
    )(qt, kcmp, vcmpt, consts["ot"], consts["c0"], consts["srow2"])


def _attn_kernel(qt_ref, kvb_ref, vt_ref, selt_ref, ocmpt_ref, gd_ref, fk_ref, coef_ref, amask_ref, egt_ref, o_ref,
                 pen_ref, sbuf, *state):
    nchunk = len(state) // 6
    states = [tuple(state[(3 * br + k) * nchunk:(3 * br + k + 1) * nchunk] for k in range(3)) for br in range(2)]
    t = pl.program_id(1)
    qbd = _query_blockdiag(qt_ref[0])
    gw = HG * TQ
    for g in range(N_KV):
        pen = (selt_ref[0, g] - 1.0) * MASK_BIG
        pen_ref[:, g * gw:(g + 1) * gw] = jnp.concatenate([pen] * HG, axis=1)
    nwin = WINDOW // TQ
    coef = coef_ref[...]
    zero8 = jnp.zeros((SUBLANES, NCOL), F32)
    cw = 2 * TQ
    zpad = jnp.zeros((LANES - 2 * SUBLANES, cw), BF16)
    rid = lax.broadcasted_iota(jnp.int32, (SUBLANES, NCOL), 0)

    def query_operands(c8):
        c16 = jnp.concatenate([c8, zero8], axis=0).astype(BF16)
        return [jnp.concatenate([qbd[:, c * cw:(c + 1) * cw], c16[:, c * cw:(c + 1) * cw], zpad], axis=0)
                for c in range(NCOL // cw)]

    q_win = query_operands(coef)

    def score_tile(kt, slot, kcol, slc, mask):
        r0 = pl.multiple_of(kt * TQ, TQ)
        k_aug = jnp.concatenate([kvb_ref[0, pl.ds(r0, TQ), kcol:kcol + LANES], fk_ref[kt]], axis=1)
        if slc:
            pa = jnp.broadcast_to(pen_ref[pl.ds(2 * kt, 1), :], (SUBLANES, NCOL))
            pb = jnp.broadcast_to(pen_ref[pl.ds(2 * kt + 1, 1), :], (SUBLANES, NCOL))
            q_aug = query_operands(jnp.where(rid == 6, pa, jnp.where(rid == 7, pb, coef)))
        else:
            q_aug = q_win
        for c in range(nchunk):
            cols = slice(c * cw, (c + 1) * cw)
            sbuf[slot, :, cols] = _dot(k_aug, q_aug[c]) + amask_ref[mask, :, cols]

    def consume_tile(kt, slot, vrow, st):
        m_refs, l_refs, acc_refs = st
        v_t = vt_ref[0, kt, vrow:vrow + LANES, :]
        for c in range(nchunk):
            cols = slice(c * cw, (c + 1) * cw)
            g = (c * cw) // gw
            s = sbuf[slot, :, cols].reshape(TQ // SUBLANES, SUBLANES, cw)
            m_old = m_refs[c][...]
            m_new = jnp.maximum(m_old, _all_sublanes(jnp.max(s, axis=0), jnp.maximum))
            alpha = jnp.exp2(m_old - m_new)
            p = jnp.exp2(s - m_new[None])
            l_refs[c][...] = alpha * l_refs[c][...] + _all_sublanes(jnp.sum(p, axis=0), jnp.add)
            m_refs[c][...] = m_new
            pv = _dot(v_t[HEAD_DIM * g:HEAD_DIM * (g + 1), :], p.reshape(TQ, cw).astype(BF16))
            acc = acc_refs[c][...].reshape(HEAD_DIM // SUBLANES, SUBLANES, cw) * alpha[None]
            acc_refs[c][...] = acc.reshape(HEAD_DIM, cw) + pv

    def reset(st):
        m_refs, l_refs, acc_refs = st
        for c in range(nchunk):
            m_refs[c][...] = jnp.full(m_refs[c].shape, NEG, F32)
            l_refs[c][...] = jnp.zeros(l_refs[c].shape, F32)
            acc_refs[c][...] = jnp.zeros(acc_refs[c].shape, F32)

    def result(st):
        _, l_refs, acc_refs = st
        per_chunk = [(acc_refs[c][...].reshape(HEAD_DIM // SUBLANES, SUBLANES, cw)
                      * (1.0 / l_refs[c][...])[None]).reshape(HEAD_DIM, cw) for c in range(nchunk)]
        half = nchunk // N_KV
        return [jnp.concatenate(per_chunk[g * half:(g + 1) * half], axis=1) for g in range(N_KV)]

    def stream(br, first, kcol, vrow, slc):
        n = t - first + 1
        pad = n % 2

        def tile_of(i):
            return jnp.maximum(first + i - pad, first)

        def mask_of(i):
            kt = first + i - pad
            diag = jnp.where(kt == t, 1, 0)
            m = diag if slc else jnp.where(kt == t - nwin, 2, diag)
            return jnp.where(i < pad, 3, m)

        def score(i, par):
            score_tile(tile_of(i), 2 * br + par, kcol, slc, mask_of(i))

        def consume(i, par):
            consume_tile(tile_of(i), 2 * br + par, vrow, states[br])

        return dict(total=n + pad, st=states[br], score=score, consume=consume)

    slc_s = stream(0, 0, 2 * LANES, 0, True)
    win_s = stream(1, jnp.maximum(t - nwin, 0), 4 * LANES, LANES, False)
    streams = (slc_s, win_s)
    for sm in streams:
        reset(sm["st"])
        sm["score"](0, 0)

    def pair_step(j, active):
        i = 2 * j
        for sm in active:
            sm["score"](i + 1, 1)
        for sm in active:
            sm["consume"](i, 0)
        for sm in active:
            sm["score"](i + 2, 0)
        for sm in active:
            sm["consume"](i + 1, 1)

    def both_body(j, c):
        pair_step(j, (slc_s, win_s))
        return c

    def slc_body(j, c):
        pair_step(j, (slc_s,))
        return c

    merged = win_s["total"] // 2 - 1
    lax.fori_loop(0, merged, both_body, 0)
    lax.fori_loop(merged, slc_s["total"] // 2 - 1, slc_body, 0)
    for sm in streams:
        sm["score"](sm["total"] - 1, 1)
    for sm in streams:
        sm["consume"](sm["total"] - 2, 0)
    for sm in streams:
        sm["consume"](sm["total"] - 1, 1)
    o_slc = result(slc_s["st"])
    o_win = result(win_s["st"])

    g_t = jax.nn.sigmoid(gd_ref[0]).T
    gates = [_expand_l(egt_ref[br], g_t) for br in range(3)]
    for pr in range(N_HEADS // 2):
        g, j0 = pr // 2, 2 * (pr % 2)
        rows = slice(LANES * pr, LANES * (pr + 1))
        pair = lambda o: jnp.concatenate([o[g][:, j0 * TQ:(j0 + 1) * TQ], o[g][:, (j0 + 1) * TQ:(j0 + 2) * TQ]], axis=0)
        tot = gates[0][rows] * ocmpt_ref[0, rows, :] + gates[1][rows] * pair(o_slc) + gates[2][rows] * pair(o_win)
        o_ref[0, :, rows] = tot.T.astype(BF16)


def _attn(qt, kvb, vt, selt, ocmpt, gd, consts):
    bsz, _, t = qt.shape
    js = selt.shape[2]
    return pl.pallas_call(
        _attn_kernel,
        out_shape=jax.ShapeDtypeStruct((bsz, t, ATT_WIDTH), BF16),
        grid=(bsz, t // TQ),
        in_specs=[pl.BlockSpec((1, ATT_WIDTH, TQ), lambda b, i: (b, 0, i)),
                  pl.BlockSpec((1, t, 6 * LANES), lambda b, i: (b, 0, 0)),
                  pl.BlockSpec((1, t // TQ, 2 * LANES, TQ), lambda b, i: (b, 0, 0, 0)),
                  pl.BlockSpec((1, N_KV, js, TQ), lambda b, i: (b, 0, 0, i)),
                  pl.BlockSpec((1, ATT_WIDTH, TQ), lambda b, i: (b, 0, i)),
                  pl.BlockSpec((1, TQ, LANES), lambda b, i: (b, i, 0)),
                  _resident(consts["fk"].shape), _resident(consts["coef"].shape), _resident(consts["amask"].shape),
                  _resident(consts["e_gate_t"].shape)],
        out_specs=pl.BlockSpec((1, TQ, ATT_WIDTH), lambda b, i: (b, i, 0)),
        scratch_shapes=([pltpu.VMEM((js, NCOL), F32), pltpu.VMEM((4, TQ, NCOL), F32)]
                        + ([pltpu.VMEM((SUBLANES, 2 * TQ), F32)] * (2 * N_HEADS // 2)
                           + [pltpu.VMEM((HEAD_DIM, 2 * TQ), F32)] * (N_HEADS // 2)) * 2),
        compiler_params=_cparams(("arbitrary", "arbitrary")),
        name="slc_win_attn",
    )(qt, kvb, vt, selt, ocmpt, gd, consts["fk"], consts["coef"], consts["amask"], consts["e_gate_t"])


def _gated_norm(y, z, ng):
    yz = y * _silu(z)
    half = SSM_INNER // SSM_GROUPS
    outs = []
    for g in range(SSM_GROUPS):
        part = yz[:, half * g:half * (g + 1)]
        ms = jnp.mean(part * part, axis=-1, keepdims=True)
        outs.append(part * lax.rsqrt(ms + EPS))
    return jnp.concatenate(outs, axis=1) * ng


def _ssd_kernel(xbc_ref, z_ref, gd_ref, cw_ref, cb_ref, dtb_ref, a_ref, dsk_ref, ng_ref, edt_ref,
                y_ref, hfin_ref, ubuf, sstate, ybufs, *, lc, nsub):
    t = pl.program_id(1)
    nt = pl.num_programs(1)

    @pl.when(t == 0)
    def _():
        ubuf[0:SUBLANES, :] = jnp.zeros((SUBLANES, CONV_DIM), F32)
        sstate[...] = jnp.zeros(sstate.shape, F32)

    ubuf[SUBLANES:SUBLANES + lc * nsub, :] = xbc_ref[0]
    for sub in range(nsub):
        _ssd_chunk(sub * lc, lc, ubuf, z_ref, gd_ref, cw_ref, cb_ref, dtb_ref, a_ref, dsk_ref, ng_ref, edt_ref,
                   y_ref, sstate, ybufs.at[sub])
    ubuf[0:SUBLANES, :] = ubuf[lc * nsub:lc * nsub + SUBLANES, :]

    @pl.when(t == nt - 1)
    def _():
        s_pad = jnp.concatenate([sstate[...], jnp.zeros((LANES - D_STATE, SSM_INNER), F32)], axis=0)
        hfin_ref[0] = s_pad.T[:, :D_STATE]


def _ssd_chunk(r0, lc, ubuf, z_ref, gd_ref, cw_ref, cb_ref, dtb_ref, a_ref, dsk_ref, ng_ref, edt_ref,
               y_ref, sstate, ybuf):
    conv = cb_ref[...] + jnp.zeros((lc, CONV_DIM), F32)
    for k in range(SSM_CONV):
        off = SUBLANES - (SSM_CONV - 1) + k + r0
        conv = conv + ubuf[off:off + lc, :] * cw_ref[k:k + 1, :]
    act = _silu(conv)
    xm = act[:, :SSM_INNER]
    bm = act[:, SSM_INNER:SSM_INNER + LANES]
    cm = act[:, SSM_INNER + LANES:]

    dt = _softplus(gd_ref[0, r0:r0 + lc, :] + dtb_ref[...])
    a = dt * a_ref[...]
    ri = lax.broadcasted_iota(jnp.int32, (lc, lc), 0)
    ci = lax.broadcasted_iota(jnp.int32, (lc, lc), 1)
    tril = ri >= ci
    cs = jnp.dot(tril.astype(F32), a, preferred_element_type=F32, precision=lax.Precision.HIGHEST)
    cs_t = cs.T
    edt = edt_ref[...]
    dt_x = _expand(dt, edt)
    cs_x = _expand(cs, edt)
    cs_last = cs_x[lc - 1:lc, :]
    bm_t = bm.T
    xd = xm * dt_x
    xw = xm * (jnp.exp(cs_last - cs_x) * dt_x)
    s_old = sstate[...]
    half = SSM_INNER // SSM_GROUPS
    for g in range(SSM_GROUPS):
        bg = bm[:, D_STATE * g:D_STATE * (g + 1)].astype(BF16)
        cg = cm[:, D_STATE * g:D_STATE * (g + 1)].astype(BF16)
        cb = _dot_nt(cg, bg)
        for j in range(SSM_HEADS // SSM_GROUPS):
            h = g * (SSM_HEADS // SSM_GROUPS) + j
            col = cs[:, GD_DT + h:GD_DT + h + 1]
            row = cs_t[GD_DT + h:GD_DT + h + 1, :]
            lm = jnp.exp(jnp.where(tril, col - row, NEG))
            lo, hi = HEAD_DIM * h, HEAD_DIM * (h + 1)
            ybuf[:, lo:hi] = _dot((cb * lm).astype(BF16), xd[:, lo:hi].astype(BF16))
        sg = s_old[:, half * g:half * (g + 1)]
        y_off = _dot(cg, sg.astype(BF16))
        ybuf[:, half * g:half * (g + 1)] = ybuf[:, half * g:half * (g + 1)] + y_off * jnp.exp(cs_x[:, half * g:half * (g + 1)])
        st = _dot(bm_t[D_STATE * g:D_STATE * (g + 1), :].astype(BF16), xw[:, half * g:half * (g + 1)].astype(BF16))
        sstate[:, half * g:half * (g + 1)] = sg * jnp.exp(cs_last[:, half * g:half * (g + 1)]) + st

    y = ybuf[...] + dsk_ref[...] * xm
    y_ref[0, r0:r0 + lc, :] = _gated_norm(y, z_ref[0, r0:r0 + lc, :], ng_ref[...]).astype(BF16)


def _ssd(xbc, z, gd, sw, consts):
    bsz, t, _ = xbc.shape
    lc = SSM_CHUNK
    nsub = 4 if t % (4 * lc) == 0 else 1
    rows = lc * nsub
    tile = lambda w: pl.BlockSpec((1, rows, w), lambda b, i: (b, i, 0))
    ws = [sw[n] for n in ("conv_w", "conv_b", "dt_bias", "a", "d_skip", "norm_g")] + [consts["e_dt"]]
    y, hfin = pl.pallas_call(
        functools.partial(_ssd_kernel, lc=lc, nsub=nsub),
        out_shape=[jax.ShapeDtypeStruct((bsz, t, SSM_INNER), BF16),
                   jax.ShapeDtypeStruct((bsz, SSM_INNER, D_STATE), F32)],
        grid=(bsz, t // rows),
        in_specs=[tile(CONV_DIM), tile(SSM_INNER), tile(LANES)] + [_wspec(w) for w in ws],
        out_specs=[tile(SSM_INNER), pl.BlockSpec((1, SSM_INNER, D_STATE), lambda b, i: (b, 0, 0))],
        scratch_shapes=[pltpu.VMEM((rows + SUBLANES, CONV_DIM), F32),
                        pltpu.VMEM((D_STATE, SSM_INNER), F32),
                        pltpu.VMEM((nsub, lc, SSM_INNER), F32)],
        compiler_params=_cparams(("arbitrary", "arbitrary")),
        name="ssd_scan",
    )(xbc, z, gd, *[_warr(w) for w in ws])
    return y, hfin.reshape(bsz, SSM_HEADS, HEAD_DIM, D_STATE)


def _mix_ffn_kernel(*refs, tm, carry_conv, last):
    if carry_conv:
        (x_ref, oa_ref, ys_ref, g1_ref, sh2_ref, sc2_ref, g2_ref, n2_ref, wo_ref, wu_ref, fcw_ref, fcb_ref,
         wd_ref, fg_ref, out_ref, ug_ref, ubuf) = refs
    else:
        (x_ref, oa_ref, ys_ref, g1_ref, sh2_ref, sc2_ref, g2_ref, n2_ref, wo_ref, wu_ref, fcw_ref, fcb_ref,
         wd_ref, fg_ref, p0_ref, p1_ref, out_ref, ug_ref) = refs
    t = pl.program_id(1)
    nt = pl.num_programs(1)
    x = x_ref[0]
    mix = _dot(oa_ref[0], wo_ref[:ATT_WIDTH, :]) + _dot(ys_ref[0], wo_ref[ATT_WIDTH:, :])
    x1 = x + g1_ref[0] * mix
    ms = jnp.mean(x1 * x1, axis=-1, keepdims=True)
    h2 = (x1 * lax.rsqrt(ms + EPS) * n2_ref[...]) * (1.0 + sc2_ref[0]) + sh2_ref[0]
    h2 = h2.astype(BF16)
    if carry_conv:
        @pl.when(t == 0)
        def _():
            ubuf[0:SUBLANES, :] = jnp.zeros((SUBLANES, D_FF), F32)

    def up(c):
        lo, hi = FF_CHUNKS[c]
        return _dot(h2, wu_ref[:, lo:hi]), _dot(h2, wu_ref[:, D_FF + lo:D_FF + hi])

    nxt = up(0)
    down = None
    for c, (lo, hi) in enumerate(FF_CHUNKS):
        ug, uv = nxt
        if c + 1 < len(FF_CHUNKS):
            nxt = up(c + 1)
        if carry_conv:
            ubuf[SUBLANES:SUBLANES + tm, lo:hi] = ug
            u1 = ubuf[SUBLANES - 1:SUBLANES - 1 + tm, lo:hi]
            u2 = ubuf[SUBLANES - 2:SUBLANES - 2 + tm, lo:hi]
        else:
            u1 = p1_ref[0, :, lo:hi]
            u2 = p0_ref[0, :, lo:hi]
            ug_ref[0, :, lo:hi] = ug
        ugc = (fcb_ref[:, lo:hi] + u2 * fcw_ref[0:1, lo:hi] + u1 * fcw_ref[1:2, lo:hi] + ug * fcw_ref[2:3, lo:hi])
        d = _dot((_silu(ugc) * uv).astype(BF16), wd_ref[lo:hi, :])
        down = d if down is None else down + d
    if carry_conv:
        @pl.when(t == nt - 1)
        def _():
            ug_ref[0] = ubuf[tm:tm + SUBLANES, :]

        ubuf[0:SUBLANES, :] = ubuf[tm:tm + SUBLANES, :]
    x2 = x1 + g2_ref[0] * down
    if last:
        ms2 = jnp.mean(x2 * x2, axis=-1, keepdims=True)
        out_ref[0] = x2 * lax.rsqrt(ms2 + EPS) * fg_ref[...]
    else:
        out_ref[0] = x2


def _mix_ffn(x, oatt, yssm, mod, fw, tm, prev, last):
    bsz, t, d = x.shape
    carry = prev is None
    row = lambda w: pl.BlockSpec((1, tm, w), lambda b, i: (b, i, 0))
    ws = [fw[n] for n in ("norm2_g", "w_out", "w_up", "conv_w", "conv_b", "w_down", "final_g")]
    in_specs = ([row(d), row(ATT_WIDTH), row(SSM_INNER)] + [_mod_spec(mod, k) for k in (2, 3, 4, 5)]
                + [_wspec(w) for w in ws])
    args = [x, oatt, yssm] + [mod[0]] * 4 + [_warr(w) for w in ws]
    if carry:
        ug_shape, ug_spec = (bsz, SUBLANES, D_FF), pl.BlockSpec((1, SUBLANES, D_FF), lambda b, i: (b, 0, 0))
        scratch = [pltpu.VMEM((tm + SUBLANES, D_FF), F32)]
    else:
        ug_shape, ug_spec = (bsz, t, D_FF), row(D_FF)
        scratch = []
        in_specs += [row(D_FF), row(D_FF)]
        args += list(prev)
    return pl.pallas_call(
        functools.partial(_mix_ffn_kernel, tm=tm, carry_conv=carry, last=last),
        out_shape=[jax.ShapeDtypeStruct((bsz, t, d), F32), jax.ShapeDtypeStruct(ug_shape, F32)],
        grid=(bsz, t // tm),
        in_specs=in_specs,
        out_specs=[row(d), ug_spec],
        scratch_shapes=scratch,
        compiler_params=_cparams(("arbitrary", "arbitrary")),
        name="mix_ffn",
    )(*args)


def _head_rows(row512):
    r = lax.broadcasted_iota(jnp.int32, (N_HEADS, ATT_WIDTH), 0)
    c = lax.broadcasted_iota(jnp.int32, (N_HEADS, ATT_WIDTH), 1)
    return jnp.where(c // HEAD_DIM == r, jnp.broadcast_to(row512, (N_HEADS, ATT_WIDTH)), 0.0)


def _head_diag(x8):
    r = lax.broadcasted_iota(jnp.int32, (N_HEADS, ATT_WIDTH), 0)
    c = lax.broadcasted_iota(jnp.int32, (N_HEADS, ATT_WIDTH), 1)
    return jnp.sum(jnp.where(c // HEAD_DIM == r, x8, 0.0), axis=0, keepdims=True)


def _slope_col():
    r = lax.broadcasted_iota(jnp.int32, (N_HEADS, 1), 0)
    s = jnp.zeros((N_HEADS, 1), F32)
    for h in range(N_HEADS):
        s = jnp.where(r == h, SLOPES[h], s)
    return s


def _s_cmp_kernel(pt_ref, q_ref, kvn_ref, gd_ref, ck_hbm, cv_hbm,
                  w1k_ref, b1k_ref, w2k_ref, w1v_ref, b1v_ref, w2v_ref, ov_ref, u_ref, e0_ref,
                  ocmp_ref, idx_ref, stage_k, stage_v, kbuf, vbuf, sem,
                  *, layer, n_pages, page, nh8, nhp, q_pos, n_slc):
    b = pl.program_id(0)
    nb = pl.num_programs(0)
    past = n_pages * page

    def copies(bb, p, slot):
        pg = pt_ref[bb * n_pages + p]
        return (pltpu.make_async_copy(ck_hbm.at[layer, pg], stage_k.at[slot, p], sem.at[0, slot]),
                pltpu.make_async_copy(cv_hbm.at[layer, pg], stage_v.at[slot, p], sem.at[1, slot]))

    unroll = 8 if n_pages % 8 == 0 else 1

    def issue(bb, slot):
        def body(i, c):
            for u in range(unroll):
                for cp in copies(bb, i * unroll + u, slot):
                    cp.start()
            return c
        lax.fori_loop(0, n_pages // unroll, body, 0)

    @pl.when(b == 0)
    def _():
        issue(0, 0)

    @pl.when(b + 1 < nb)
    def _():
        issue(b + 1, (b + 1) % 2)

    slot = b % 2

    def wait(i, c):
        for u in range(unroll):
            for cp in copies(b, i * unroll + u, slot):
                cp.wait()
        return c

    lax.fori_loop(0, n_pages // unroll, wait, 0)

    hpp = page // CMP_STRIDE

    def untranspose(i, c):
        for u in range(unroll):
            p = i * unroll + u
            base = pl.multiple_of(p * (hpp * ROW_PITCH), SUBLANES)
            kp = stage_k[slot, p].T
            vp = stage_v[slot, p].T
            for n in range(hpp):
                rows = pl.ds(base + n * ROW_PITCH, CMP_STRIDE)
                kbuf[rows, :] = kp[n * CMP_STRIDE:(n + 1) * CMP_STRIDE]
                vbuf[rows, :] = vp[n * CMP_STRIDE:(n + 1) * CMP_STRIDE]
        return c

    lax.fori_loop(0, n_pages // unroll, untranspose, 0)
    first_new = (past // CMP_STRIDE) * ROW_PITCH
    tail = nh8 * ROW_PITCH - first_new
    kbuf[first_new:, :] = jnp.zeros((tail, LANES), F32)
    vbuf[first_new:, :] = jnp.zeros((tail, LANES), F32)
    kbuf[first_new:first_new + 1, :] = kvn_ref[0, :, 0:LANES]
    vbuf[first_new:first_new + 1, :] = kvn_ref[0, :, LANES:2 * LANES]

    pad = jnp.zeros((nhp - nh8, ATT_WIDTH), F32)
    acc_k = _compress_products(kbuf, nh8, w1k_ref, ROW_PITCH)
    acc_v = _compress_products(vbuf, nh8, w1v_ref, ROW_PITCH)
    kc4 = jnp.concatenate([_compress_summaries(acc_k, nh8, b1k_ref, w2k_ref), pad], axis=0).astype(BF16)
    vc4 = jnp.concatenate([_compress_summaries(acc_v, nh8, b1v_ref, w2v_ref), pad], axis=0).astype(BF16)
    n_cmp = (past + 1 + CMP_STRIDE - 1) // CMP_STRIDE - 1

    qm = _head_rows(q_ref[0].astype(F32)).astype(BF16)
    nidx = lax.broadcasted_iota(jnp.int32, (N_HEADS, nhp), 1)
    dist_i = q_pos - (nidx * CMP_STRIDE + (CMP_BLOCK - 1))
    mask = (dist_i >= 0) & (nidx < n_cmp)
    s = _dot_nt(qm, kc4) - _slope_col() * dist_i.astype(F32)
    sm = jnp.where(mask, s, NEG)
    e = jnp.exp(sm - jnp.max(sm, axis=-1, keepdims=True))
    p = jnp.where(mask, e / jnp.sum(e, axis=-1, keepdims=True), 0.0)
    o = _head_diag(_dot(p.astype(BF16), vc4))
    g8 = jnp.broadcast_to(jax.nn.sigmoid(gd_ref[0]), (SUBLANES, LANES))
    gate = _expand(g8, e0_ref[...])[0:1, :]
    ocmp_ref[0] = gate * o

    js = ov_ref.shape[1]
    imp8 = _expand(p, ov_ref[...])
    hrow = lax.broadcasted_iota(jnp.int32, (N_HEADS, js), 0)
    jl = lax.broadcasted_iota(jnp.int32, (1, js), 1)
    cur = q_pos // SLC_BLOCK
    forced = (jl == 0) | (jl == cur) | (jl == cur - 1)
    valid = (jl * SLC_BLOCK <= q_pos) & (jl < n_slc)
    rj = lax.broadcasted_iota(jnp.int32, (js, js), 0)
    cj = lax.broadcasted_iota(jnp.int32, (js, js), 1)
    kk = lax.broadcasted_iota(jnp.int32, (2 * SUBLANES, js), 0)
    jvals = jnp.broadcast_to(jl.astype(F32), (SUBLANES, js)).astype(BF16)
    for g in range(N_KV):
        imp = jnp.sum(jnp.where(hrow // HG == g, imp8, 0.0), axis=0, keepdims=True)
        score = jnp.where(valid, imp + jnp.where(forced, FORCE_BONUS, 0.0), NEG)
        sb = jnp.broadcast_to(score, (js, js))
        col = jnp.sum(jnp.where(rj == cj, sb, 0.0), axis=1, keepdims=True)
        ahead = (col > sb) | ((col == sb) & (rj < cj))
        rank = jnp.sum(ahead.astype(F32), axis=0, keepdims=True)
        sel = ((rank < N_SEL) & (jl < n_slc)).astype(F32)
        pos = _dot(jnp.broadcast_to(sel, (SUBLANES, js)).astype(BF16), u_ref[...])[0:1, :]
        onehot = ((jnp.broadcast_to(pos, (2 * SUBLANES, js)) == kk.astype(F32))
                  & (jnp.broadcast_to(sel, (2 * SUBLANES, js)) > 0.5)).astype(BF16)
        idx = _dot_nt(jvals, onehot)[0:1, :]
        idx_ref[0, g:g + 1, :] = idx.astype(jnp.int32)


def _s_cmp(layer, page_table, q, kvn, gd, cache_k, cache_v, cw4, consts, q_pos):
    ns, n_pages = page_table.shape
    page = cache_k.shape[3]
    past = n_pages * page
    nh = (past + 1 + CMP_STRIDE - 1) // CMP_STRIDE
    nh8 = -(-nh // SUBLANES) * SUBLANES
    nhp = -(-nh // LANES) * LANES
    n_slc = past // SLC_BLOCK + 1
    ws = [cw4[n] for n in ("w1k", "b1k", "w2k", "w1v", "b1v", "w2v")] + [consts["ov_s"], consts["u_s"], consts["e_gate"][0]]
    full = lambda a: pl.BlockSpec(a.shape, lambda b, pt: (0,) * a.ndim)
    per_b = lambda a: pl.BlockSpec((1,) + a.shape[1:], lambda b, pt: (b,) + (0,) * (a.ndim - 1))
    grid_spec = pltpu.PrefetchScalarGridSpec(
        num_scalar_prefetch=1,
        grid=(ns,),
        in_specs=[per_b(q), per_b(kvn), per_b(gd), pl.BlockSpec(memory_space=pl.ANY), pl.BlockSpec(memory_space=pl.ANY)]
                 + [_wspec(w) for w in ws],
        out_specs=[pl.BlockSpec((1, 1, ATT_WIDTH), lambda b, pt: (b, 0, 0)),
                   pl.BlockSpec((1, N_KV, N_SEL), lambda b, pt: (b, 0, 0))],
        scratch_shapes=[pltpu.VMEM((2, n_pages, LANES, page), F32), pltpu.VMEM((2, n_pages, LANES, page), F32),
                        pltpu.VMEM((nh8 * ROW_PITCH, LANES), F32), pltpu.VMEM((nh8 * ROW_PITCH, LANES), F32),
                        pltpu.SemaphoreType.DMA((2, 2))],
    )
    return pl.pallas_call(
        functools.partial(_s_cmp_kernel, layer=layer, n_pages=n_pages, page=page, nh8=nh8, nhp=nhp, q_pos=q_pos,
                          n_slc=n_slc),
        out_shape=[jax.ShapeDtypeStruct((ns, 1, ATT_WIDTH), F32), jax.ShapeDtypeStruct((ns, N_KV, N_SEL), jnp.int32)],
        grid_spec=grid_spec,
        compiler_params=_cparams(("arbitrary",)),
        name="sample_cmp_select",
    )(page_table.reshape(-1), q, kvn, gd, cache_k, cache_v, *[_warr(w) for w in ws])


def _s_slc_win_kernel(idx_ref, pt_ref, q_ref, kvn_ref, gd_ref, ocmp_ref, wk_ref, wv_ref, sk_hbm, sv_hbm,
                      tile_ref, e1_ref, e2_ref, o_ref, wko_ref, wvo_ref, kb, vb, sem,
                      *, layer, n_pages, page, q_pos, wb):
    b = pl.program_id(0)
    nb = pl.num_programs(0)
    bpp = page // SLC_BLOCK
    n_past = n_pages * bpp
    nblk = N_KV * N_SEL
    blks = [idx_ref[b * nblk + i] for i in range(nblk)]
    slot = b % 2

    def copies(bb, sl, i):
        blk = idx_ref[bb * nblk + i]
        pg = pt_ref[bb * n_pages + jnp.minimum(blk, n_past - 1) // bpp]
        dst = slice(i * page, (i + 1) * page)
        return (pltpu.make_async_copy(sk_hbm.at[layer, pg], kb.at[sl, :, dst], sem.at[0, sl]),
                pltpu.make_async_copy(sv_hbm.at[layer, pg], vb.at[sl, :, dst], sem.at[1, sl]))

    def issue(bb, sl):
        for i in range(nblk):
            for cp in copies(bb, sl, i):
                cp.start()

    @pl.when(b == 0)
    def _():
        issue(0, 0)

    @pl.when(b + 1 < nb)
    def _():
        issue(b + 1, (b + 1) % 2)

    tile_m = tile_ref[...]
    q8 = _dot_nt(_head_rows(q_ref[0].astype(F32)).astype(BF16), tile_m).astype(BF16)
    slope = _slope_col()
    g8 = jnp.broadcast_to(jax.nn.sigmoid(gd_ref[0]), (SUBLANES, LANES))
    gate1 = _expand(g8, e1_ref[...])[0:1, :]
    gate2 = _expand(g8, e2_ref[...])[0:1, :]
    rgrp = lax.broadcasted_iota(jnp.int32, (N_HEADS, 1), 0) // HG

    def new_key(lane0):
        kn = kvn_ref[0, :, lane0:lane0 + LANES].astype(BF16).astype(F32)
        vn = kvn_ref[0, :, lane0 + LANES:lane0 + 2 * LANES].astype(BF16).astype(F32)
        return jnp.sum(q8.astype(F32) * kn, axis=-1, keepdims=True), vn

    def finish(e, e_new, v_t, v_new, l):
        o8 = (_dot_nt(e.astype(BF16), v_t.astype(BF16)) + e_new * v_new) / l
        return _head_diag(_expand(o8, tile_m))

    def shifted(w_t, new_row):
        col = jnp.broadcast_to(new_row, (LANES, LANES)).T[:, 0:1]
        lane = lax.broadcasted_iota(jnp.int32, (LANES, wb), 1)
        return jnp.where(lane == wb - 1, col, pltpu.roll(w_t, wb - 1, 1))

    wk_t = wk_ref[0, 0]
    wv_t = wv_ref[0, 0]
    kpos = (q_pos - wb) + lax.broadcasted_iota(jnp.int32, (N_HEADS, wb), 1)
    dist = q_pos - kpos
    wmask = (dist >= 0) & (dist <= WINDOW)
    s = jnp.where(wmask, _dot(q8, wk_t.astype(BF16)) - slope * dist.astype(F32), NEG)
    s_new, v_new = new_key(4 * LANES)
    m = jnp.maximum(jnp.max(s, axis=-1, keepdims=True), s_new)
    e = jnp.where(wmask, jnp.exp(s - m), 0.0)
    e_new = jnp.exp(s_new - m)
    o_win = finish(e, e_new, wv_t, v_new, jnp.sum(e, axis=-1, keepdims=True) + e_new)
    wko_ref[0] = shifted(wk_t, kvn_ref[0, :, 4 * LANES:5 * LANES])
    wvo_ref[0] = shifted(wv_t, kvn_ref[0, :, 5 * LANES:6 * LANES])

    for i in range(nblk):
        for cp in copies(b, slot, i):
            cp.wait()
    lane = lax.broadcasted_iota(jnp.int32, (1, page), 1)
    kpos_t, ok_t = [], []
    has_new = [jnp.zeros((1, 1), jnp.int32) for _ in range(N_KV)]
    for i in range(nblk):
        blk = blks[i]
        kpos_t.append(blk * SLC_BLOCK + lane % SLC_BLOCK)
        ok_t.append(((lane // SLC_BLOCK) == blk % bpp) & (blk < n_past))
        has_new[i // N_SEL] = jnp.maximum(has_new[i // N_SEL], (blk >= n_past).astype(jnp.int32))
    kpos = jnp.concatenate(kpos_t, axis=1)
    ok = jnp.concatenate(ok_t, axis=1)
    pgrp = lax.broadcasted_iota(jnp.int32, (1, nblk * page), 1) // (N_SEL * page)
    dist = q_pos - kpos
    kmask = ok & (dist >= 0) & (rgrp == pgrp)
    s = jnp.where(kmask, _dot(q8, kb[slot].astype(BF16)) - slope * dist.astype(F32), NEG)
    s_new, v_new = new_key(2 * LANES)
    new_on = jnp.where(rgrp == 0, has_new[0], has_new[1]) > 0
    sn = jnp.where(new_on, s_new, NEG)
    m = jnp.maximum(jnp.max(s, axis=-1, keepdims=True), sn)
    e = jnp.where(kmask, jnp.exp(s - m), 0.0)
    e_new = jnp.where(new_on, jnp.exp(sn - m), 0.0)
    o_slc = finish(e, e_new, vb[slot], v_new, jnp.sum(e, axis=-1, keepdims=True) + e_new)

    o_ref[0] = (ocmp_ref[0] + gate1 * o_slc + gate2 * o_win).astype(BF16)


def _s_slc_win(layer, idx, page_table, q, kvn, gd, ocmp, win_k, win_v, slc_k, slc_v, consts, q_pos):
    ns, n_pages = page_table.shape
    page = slc_k.shape[3]
    wb = win_k.shape[3]
    full = lambda a: pl.BlockSpec(a.shape, lambda b, i, pt: (0,) * a.ndim)
    per_b = lambda a: pl.BlockSpec((1,) + a.shape[1:], lambda b, i, pt: (b,) + (0,) * (a.ndim - 1))
    win = pl.BlockSpec((1, 1, LANES, wb), lambda b, i, pt: (layer, b, 0, 0))
    ws = [consts["tile_m"], consts["e_gate"][1], consts["e_gate"][2]]
    nlane = N_KV * N_SEL * page
    grid_spec = pltpu.PrefetchScalarGridSpec(
        num_scalar_prefetch=2,
        grid=(ns,),
        in_specs=[per_b(q), per_b(kvn), per_b(gd), per_b(ocmp), win, win,
                  pl.BlockSpec(memory_space=pl.ANY), pl.BlockSpec(memory_space=pl.ANY)] + [full(w) for w in ws],
        out_specs=[pl.BlockSpec((1, 1, ATT_WIDTH), lambda b, i, pt: (b, 0, 0)),
                   pl.BlockSpec((1, LANES, wb), lambda b, i, pt: (b, 0, 0)),
                   pl.BlockSpec((1, LANES, wb), lambda b, i, pt: (b, 0, 0))],
        scratch_shapes=[pltpu.VMEM((2, LANES, nlane), F32), pltpu.VMEM((2, LANES, nlane), F32),
                        pltpu.SemaphoreType.DMA((2, 2))],
    )
    return pl.pallas_call(
        functools.partial(_s_slc_win_kernel, layer=layer, n_pages=n_pages, page=page, q_pos=q_pos, wb=wb),
        out_shape=[jax.ShapeDtypeStruct((ns, 1, ATT_WIDTH), BF16),
                   jax.ShapeDtypeStruct((ns, LANES, wb), F32), jax.ShapeDtypeStruct((ns, LANES, wb), F32)],
        grid_spec=grid_spec,
        compiler_params=_cparams(("arbitrary",)),
        name="sample_slc_win_attn",
    )(idx.reshape(-1), page_table.reshape(-1), q, kvn, gd, ocmp, win_k, win_v, slc_k, slc_v, *ws)


def _col_bcast(row):
    blocks = [jnp.broadcast_to(row[:, LANES * i:LANES * (i + 1)], (LANES, LANES)).T for i in range(row.shape[1] // LANES)]
    return jnp.concatenate(blocks, axis=0)


def _s_ssd_kernel(xbc_ref, prev_ref, z_ref, gd_ref, h0_ref, cw_ref, cb_ref, dtb_ref, a_ref, dsk_ref, ng_ref,
                  edt_ref, y_ref, h_ref):
    conv = cb_ref[...] + xbc_ref[0] * cw_ref[SSM_CONV - 1:SSM_CONV, :]
    for k in range(SSM_CONV - 1):
        conv = conv + prev_ref[0, k:k + 1, :] * cw_ref[k:k + 1, :]
    act = _silu(conv)
    xm = act[:, :SSM_INNER]
    bm = act[:, SSM_INNER:SSM_INNER + LANES]
    cm = act[:, SSM_INNER + LANES:]
    dt = _softplus(gd_ref[0] + dtb_ref[...])
    a = dt * a_ref[...]
    edt = edt_ref[...]
    dt_x = _expand(jnp.broadcast_to(dt, (SUBLANES, LANES)), edt)[0:1, :]
    dec_x = jnp.exp(_expand(jnp.broadcast_to(a, (SUBLANES, LANES)), edt)[0:1, :])
    dtx = dt_x * xm
    h0 = h0_ref[0, 0].reshape(SSM_INNER, D_STATE)
    half = SSM_INNER // SSM_GROUPS
    rsel = lax.broadcasted_iota(jnp.int32, (SSM_INNER, 1), 0) // half
    lsel = lax.broadcasted_iota(jnp.int32, (1, SSM_INNER), 1) // half
    y_off = jnp.zeros((1, SSM_INNER), F32)
    cbx = jnp.zeros((1, SSM_INNER), F32)
    brow = jnp.zeros((SSM_INNER, D_STATE), F32)
    for g in range(SSM_GROUPS):
        bg = bm[:, D_STATE * g:D_STATE * (g + 1)]
        cg = cm[:, D_STATE * g:D_STATE * (g + 1)]
        c8 = jnp.broadcast_to(cg, (SUBLANES, D_STATE)).astype(BF16)
        yo = _dot_nt(c8, h0[half * g:half * (g + 1), :].astype(BF16))[0:1, :]
        y_off = jnp.where(lsel == g, jnp.concatenate([yo] * SSM_GROUPS, axis=1), y_off)
        cbx = jnp.where(lsel == g, jnp.sum(cg * bg, axis=-1, keepdims=True), cbx)
        brow = jnp.where(rsel == g, jnp.broadcast_to(bg, (SSM_INNER, D_STATE)), brow)
    y = y_off * dec_x + cbx * dtx + dsk_ref[...] * xm
    y_ref[0] = _gated_norm(y, z_ref[0], ng_ref[...]).astype(BF16)
    h_new = h0 * _col_bcast(dec_x)[:, :D_STATE] + _col_bcast(dtx)[:, :D_STATE] * brow
    h_ref[0] = h_new.reshape(SSM_HEADS, HEAD_DIM, D_STATE)


def _s_ssd(layer, xbc, prev, z, gd, state_ssm, sw, consts):
    ns = state_ssm.shape[1]
    full = lambda a: pl.BlockSpec(a.shape, lambda b: (0,) * a.ndim)
    per_b = lambda a: pl.BlockSpec((1,) + a.shape[1:], lambda b: (b,) + (0,) * (a.ndim - 1))
    hshape = state_ssm.shape[2:]
    ws = [sw[n] for n in ("conv_w", "conv_b", "dt_bias", "a", "d_skip", "norm_g")] + [consts["e_dt"]]
    return pl.pallas_call(
        _s_ssd_kernel,
        out_shape=[jax.ShapeDtypeStruct((ns, 1, SSM_INNER), BF16), jax.ShapeDtypeStruct((ns,) + hshape, F32)],
        grid=(ns,),
        in_specs=[per_b(xbc), per_b(prev), per_b(z), per_b(gd),
                  pl.BlockSpec((1, 1) + hshape, lambda b: (layer, b, 0, 0, 0))] + [_wspec(w) for w in ws],
        out_specs=[pl.BlockSpec((1, 1, SSM_INNER), lambda b: (b, 0, 0)),
                   pl.BlockSpec((1,) + hshape, lambda b: (b, 0, 0, 0))],
        compiler_params=_cparams(("arbitrary",)),
        name="sample_ssd_step",
    )(xbc, prev, z, gd, state_ssm, *[_warr(w) for w in ws])


def _constants(t_prompt, past_len):
    c = {}
    eg = np.zeros((3, LANES, ATT_WIDTH), np.float32)
    for br in range(3):
        for h in range(N_HEADS):
            eg[br, br * N_HEADS + h, h * HEAD_DIM:(h + 1) * HEAD_DIM] = 1.0
    c["e_gate"] = [jnp.asarray(eg[i], BF16) for i in range(3)]
    c["e_gate_t"] = jnp.asarray(eg.transpose(0, 2, 1), BF16)
    ed = np.zeros((LANES, SSM_INNER), np.float32)
    for h in range(SSM_HEADS):
        ed[GD_DT + h, h * HEAD_DIM:(h + 1) * HEAD_DIM] = 1.0
    c["e_dt"] = jnp.asarray(ed, BF16)

    def overlap(n_cmp_pad, n_slc, n_slc_pad):
        cs = np.arange(n_cmp_pad)[:, None] * CMP_STRIDE
        ss = np.arange(n_slc_pad)[None, :] * SLC_BLOCK
        ov = ((cs < ss + SLC_BLOCK) & (cs + CMP_BLOCK > ss) & (np.arange(n_slc_pad)[None, :] < n_slc))
        return ov.astype(np.float32)
    nh_p = t_prompt // CMP_STRIDE
    n_slc_p = t_prompt // SLC_BLOCK
    js_p = -(-n_slc_p // SUBLANES) * SUBLANES
    ov = overlap(nh_p, n_slc_p, js_p)
    ov[nh_p - 1:, :] = 0.0
    c["ot"] = jnp.asarray(ov.T, BF16)
    col = np.arange(NCOL)
    slope = np.asarray(SLOPES, np.float64)[col // TQ]
    c["srow2"] = jnp.asarray(LOG2E * slope[None, :], F32)
    c["c0"] = jnp.asarray(LOG2E * slope[None, :] * ((col % TQ)[None, :]
                                                    - (np.arange(nh_p)[:, None] * CMP_STRIDE + CMP_BLOCK - 1)), F32)
    def bf16_terms(x):
        terms, r = [], np.asarray(x, np.float64)
        for _ in range(3):
            tb = np.asarray(r, np.float32).astype(BF16).astype(np.float64)
            terms.append(tb)
            r = r - tb
        return terms
    coef = np.zeros((SUBLANES, NCOL), np.float64)
    coef[0:3] = np.stack(bf16_terms(LOG2E * SLC_BLOCK * slope))
    coef[3:6] = np.stack(bf16_terms(LOG2E * slope))
    c["coef"] = jnp.asarray(coef, F32)
    ntile = t_prompt // TQ
    kpos = np.arange(t_prompt).reshape(ntile, TQ)
    fk = np.zeros((ntile, TQ, LANES), np.float32)
    fk[:, :, 0:3] = (kpos // SLC_BLOCK)[:, :, None]
    fk[:, :, 3:6] = (kpos % SLC_BLOCK)[:, :, None]
    fk[:, :, 6] = (np.arange(TQ) < SLC_BLOCK)[None, :]
    fk[:, :, 7] = (np.arange(TQ) >= SLC_BLOCK)[None, :]
    c["fk"] = jnp.asarray(fk, BF16)
    rel = np.arange(TQ)[:, None] - (col % TQ)[None, :]
    c["amask"] = jnp.asarray(np.stack([np.zeros(rel.shape), np.where(rel > 0, -MASK_BIG, 0.0),
                                       np.where(rel < 0, -MASK_BIG, 0.0), np.full(rel.shape, -MASK_BIG)]), F32)
    nh_s = (past_len + 1 + CMP_STRIDE - 1) // CMP_STRIDE
    nhp_s = -(-nh_s // LANES) * LANES
    n_slc_s = past_len // SLC_BLOCK + 1
    js_s = -(-n_slc_s // LANES) * LANES
    ov_s = overlap(nhp_s, n_slc_s, js_s)
    ov_s[nh_s - 1:, :] = 0.0
    c["ov_s"] = jnp.asarray(ov_s, BF16)
    c["u_s"] = jnp.asarray(np.triu(np.ones((js_s, js_s), np.float32), 1), BF16)
    tm = np.zeros((LANES, ATT_WIDTH), np.float32)
    for h in range(N_HEADS):
        g = h // HG
        for d in range(HEAD_DIM):
            tm[g * HEAD_DIM + d, h * HEAD_DIM + d] = 1.0
    c["tile_m"] = jnp.asarray(tm, BF16)
    return c


def _pack_w_in(w_in):
    depth, d, _ = w_in.shape
    cuts = np.cumsum([ATT_WIDTH] + [LANES] * 6 + [3 * N_HEADS, SSM_INNER, CONV_DIM])
    q, kv, gt, z, xbc, dtc = (w_in[..., :cuts[0]], w_in[..., cuts[0]:cuts[6]], w_in[..., cuts[6]:cuts[7]],
                              w_in[..., cuts[7]:cuts[8]], w_in[..., cuts[8]:cuts[9]], w_in[..., cuts[9]:])
    pad = jnp.zeros((depth, d, P_W - P_GD - gt.shape[-1] - dtc.shape[-1]), w_in.dtype)
    return jnp.concatenate([q, kv, z, xbc, gt, dtc, pad], axis=-1).astype(BF16)


def _pack_compress(w1, b1, w2):
    depth = w1.shape[0]
    zeros = jnp.zeros((depth, CMP_STRIDE, HEAD_DIM, CMP_HID), w1.dtype)
    parts = []
    for ab in range(2):
        w = w1[:, ab * CMP_STRIDE:(ab + 1) * CMP_STRIDE]
        top = jnp.concatenate([w, zeros], axis=3)
        bot = jnp.concatenate([zeros, w], axis=3)
        parts.append(jnp.concatenate([top, bot], axis=2))
    w1b = jnp.concatenate(parts, axis=3)
    w1b = w1b.reshape(depth, CMP_STRIDE // 2, 2 * LANES, 4 * CMP_HID).astype(BF16)
    b1b = jnp.concatenate([b1, b1], axis=1).reshape(depth, 1, 2 * CMP_HID)
    z2 = jnp.zeros_like(w2)
    def blockdiag(rep):
        top = jnp.concatenate([w2] * rep + [z2] * rep, axis=2)
        bot = jnp.concatenate([z2] * rep + [w2] * rep, axis=2)
        return jnp.concatenate([top, bot], axis=1).astype(BF16)
    return w1b, b1b, blockdiag(1), blockdiag(HG)


def _lane_rows(vals, offset, width):
    depth, n = vals.shape
    return jnp.pad(vals.astype(F32), ((0, 0), (offset, width - offset - n))).reshape(depth, 1, width)


def _token_minor(a):
    lead = a.shape[:-3]
    n = len(lead)
    return jnp.transpose(a, tuple(range(n)) + (n + 1, n + 2, n)).reshape(lead + (N_KV * HEAD_DIM, a.shape[-3]))


def _token_major(a):
    lead = a.shape[:-2]
    n = len(lead)
    a = a.reshape(lead + (N_KV, HEAD_DIM, a.shape[-1]))
    return jnp.transpose(a, tuple(range(n)) + (n + 2, n, n + 1))


def kernel(x_prompt, x_sample, cache_cmp_k, cache_cmp_v, cache_slc_k, cache_slc_v, state_win_k, state_win_v, state_ssm, state_ssm_conv, state_ffn_conv, page_table, c_prompt, c_sample, ada_w, ada_b, norm1_g, norm2_g, w_in, cmpk_w1, cmpk_b1, cmpk_w2, cmpv_w1, cmpv_b1, cmpv_w2, ssm_conv_w, ssm_conv_b, dt_bias, a_log, d_skip, ssm_norm_g, w_out, ffn_w_up, ffn_conv_w, ffn_conv_b, ffn_w_down, final_g):
    bp, tp, d = x_prompt.shape
    ns = x_sample.shape[0]
    depth = w_in.shape[0]
    page = cache_cmp_k.shape[2]
    past_len = page_table.shape[1] * page
    assert x_sample.shape[1] == 1 and d == D_MODEL
    assert tp % SSM_CHUNK == 0 and tp >= WINDOW and past_len >= WINDOW and page % SLC_BLOCK == 0
    consts = _constants(tp, past_len)
    caches_t = [_token_minor(c) for c in (cache_cmp_k, cache_cmp_v, cache_slc_k, cache_slc_v)]
    win_t = [_token_minor(w) for w in (state_win_k, state_win_v)]

    mod = _mod(jnp.concatenate([c_sample, c_prompt], axis=0), ada_w, ada_b)
    mod_rows_p = mod.reshape(depth, ns + bp, 1, 6 * d)
    mod_rows_s = mod.reshape(depth, 1, ns + bp, 6 * d)
    tm_p = 512 if tp % 512 == 0 else SSM_CHUNK

    w_packed = _pack_w_in(w_in)
    w1k, b1k, w2k, w2k4 = _pack_compress(cmpk_w1, cmpk_b1, cmpk_w2)
    w1v, b1v, w2v, w2v4 = _pack_compress(cmpv_w1, cmpv_b1, cmpv_w2)
    norm1 = norm1_g.reshape(depth, 1, d)
    sw_all = dict(conv_w=ssm_conv_w, conv_b=ssm_conv_b.reshape(depth, 1, CONV_DIM),
                  dt_bias=_lane_rows(dt_bias, GD_DT, LANES), a=_lane_rows(-jnp.exp(a_log.astype(F32)), GD_DT, LANES),
                  d_skip=jnp.repeat(d_skip.astype(F32), HEAD_DIM, axis=1).reshape(depth, 1, SSM_INNER),
                  norm_g=ssm_norm_g.astype(F32).reshape(depth, 1, SSM_INNER))
    fw_all = dict(norm2_g=norm2_g.reshape(depth, 1, d), w_out=w_out.astype(BF16), w_up=ffn_w_up.astype(BF16),
                  conv_w=ffn_conv_w, conv_b=ffn_conv_b.reshape(depth, 1, D_FF), w_down=ffn_w_down.astype(BF16))

    xp = x_prompt
    xs = x_sample.reshape(1, ns, d)
    outs_p, outs_s = [], []
    for l in range(depth):
        last = l == depth - 1
        mod_p = (mod_rows_p, l, ns, False)
        mod_s = (mod_rows_s, l, ns, True)
        at = lambda a: (a, l)
        cw = dict(w1k=at(w1k), b1k=at(b1k), w2k=at(w2k), w1v=at(w1v), b1v=at(b1v), w2v=at(w2v))
        cw4 = dict(w1k=at(w1k), b1k=at(b1k), w2k=at(w2k4), w1v=at(w1v), b1v=at(b1v), w2v=at(w2v4))
        sw = {n: at(a) for n, a in sw_all.items()}
        fw = {n: at(a) for n, a in fw_all.items()}
        fw["final_g"] = final_g.reshape(1, d)

        (qt, kc, vc, kvb, vt, z, xbc, gd, kct, vct, kst, vst, kwt, vwt) = _inproj(
            xp, mod_p, at(norm1), at(w_packed), tm_p, True)
        kcmp, vcmpt = _compress(kc, vc, cw)
        ocmpt, selt = _cmp_sel(qt, kcmp, vcmpt, consts)
        oatt = _attn(qt, kvb, vt, selt, ocmpt, gd, consts)
        yssm, h_p = _ssd(xbc, z, gd, sw, consts)
        xp, ug_tail = _mix_ffn(xp, oatt, yssm, mod_p, fw, tm_p, None, last)
        outs_p.append((kct, vct, kst, vst, kwt[:, :, tp - WINDOW:], vwt[:, :, tp - WINDOW:], h_p,
                       xbc[:, tp - (SSM_CONV - 1):], ug_tail[:, SUBLANES - (FFN_CONV - 1):]))

        q, kvb, z, xbc, gd, kc, vc, ks, vs, kw, vw = _inproj(xs, mod_s, at(norm1), at(w_packed), ns, False)
        per_tok = lambda a: a.reshape(ns, 1, a.shape[-1])
        kvn = per_tok(jnp.concatenate([kc, vc, ks, vs, kw, vw], axis=2))
        q, z, xbc, gd = per_tok(q), per_tok(z), per_tok(xbc), per_tok(gd)
        ocmp, idx = _s_cmp(l, page_table, q, kvn, gd, caches_t[0], caches_t[1], cw4, consts, past_len)
        oatt, wk_new, wv_new = _s_slc_win(l, idx, page_table, q, kvn, gd, ocmp, win_t[0], win_t[1],
                                          caches_t[2], caches_t[3], consts, past_len)
        yssm, h_s = _s_ssd(l, xbc, state_ssm_conv[l], z, gd, state_ssm, sw, consts)
        prev_ffn = (state_ffn_conv[l][:, 0].reshape(1, ns, D_FF), state_ffn_conv[l][:, 1].reshape(1, ns, D_FF))
        xs, ug_s = _mix_ffn(xs, oatt.reshape(1, ns, ATT_WIDTH), yssm.reshape(1, ns, SSM_INNER), mod_s, fw, ns,
                            prev_ffn, last)
        s4 = lambda a: a.reshape(ns, 1, N_KV, HEAD_DIM)
        outs_s.append((s4(kc), s4(vc), s4(ks), s4(vs), wk_new, wv_new, h_s,
                       jnp.concatenate([state_ssm_conv[l][:, 1:], xbc.reshape(ns, 1, CONV_DIM)], axis=1),
                       jnp.concatenate([state_ffn_conv[l][:, 1:], ug_s.reshape(ns, 1, D_FF)], axis=1)))

    stack = lambda seq, i: jnp.stack([st[i] for st in seq])
    res = [xp, xs.reshape(ns, 1, d)]
    for i in range(9):
        p_i, s_i = stack(outs_p, i), stack(outs_s, i)
        if i < 6:
            p_i = _token_major(p_i)
        if i in (4, 5):
            s_i = _token_major(s_i)
        res += [p_i, s_i]
    return tuple(res)
```

```python
import functools

import numpy as np
import jax
import jax.numpy as jnp
from jax import lax
from jax.experimental import pallas as pl
from jax.experimental.pallas import tpu as pltpu

F32 = jnp.float32
BF16 = jnp.bfloat16

D_MODEL = 1024
HEAD_DIM = 64
ATT_WIDTH = 512
N_HEADS = 8
N_KV = 2
HG = 4
CMP_BLOCK = 32
CMP_STRIDE = 16
CMP_HID = 128
SLC_BLOCK = 64
N_SEL = 16
WINDOW = 512
FORCE_BONUS = 1e6
SSM_INNER = 512
SSM_HEADS = 8
SSM_GROUPS = 2
D_STATE = 64
SSM_CONV = 4
SSM_CHUNK = 128
CONV_DIM = 768
D_FF = 2816
FFN_CONV = 3
EPS = 1e-6
NEG = -1e30
SCALE = HEAD_DIM ** -0.5
LOG2E = 1.4426950408889634
SLOPES = tuple(2.0 ** (-(h + 1)) for h in range(N_HEADS))

LANES = 128
SUBLANES = 8
VMEM_LIMIT = 56 * 1024 * 1024

FF_CHUNKS = ((0, 768), (768, 1536), (1536, 2176), (2176, 2816))
ROW_PITCH = 24
TQ = 2 * SLC_BLOCK
NCOL = N_HEADS * TQ

P_Q = 0
P_KV = 512
P_Z = 1280
P_XBC = 1792
P_GD = 2560
P_W = 2688
GD_DT = 24
MASK_BIG = 2.0 ** 100


def _cparams(sem):
    return pltpu.CompilerParams(dimension_semantics=sem, vmem_limit_bytes=VMEM_LIMIT)


def _resident(shape):
    nd = len(shape)
    return pl.BlockSpec(shape, lambda *_: (0,) * nd, pipeline_mode=pl.Buffered(1))


def _wspec(w):
    if isinstance(w, tuple):
        arr, layer = w
        nd = arr.ndim
        return pl.BlockSpec((None,) + arr.shape[1:], lambda *_: (layer,) + (0,) * (nd - 1),
                            pipeline_mode=pl.Buffered(1))
    return _resident(w.shape)


def _warr(w):
    return w[0] if isinstance(w, tuple) else w


def _split3(x):
    hi = x.astype(BF16)
    r = x - hi.astype(F32)
    mid = r.astype(BF16)
    lo = (r - mid.astype(F32)).astype(BF16)
    return hi, mid, lo


def _dot(a, b):
    return jnp.dot(a, b, preferred_element_type=F32)


def _dot_nt(a, b):
    return lax.dot_general(a, b, (((1,), (1,)), ((), ())), preferred_element_type=F32)


def _expand(x, e):
    hi, mid, lo = _split3(x)
    return _dot(hi, e) + _dot(mid, e) + _dot(lo, e)


def _expand_l(e, x):
    hi, mid, lo = _split3(x)
    return _dot(e, hi) + _dot(e, mid) + _dot(e, lo)


def _silu(x):
    return x * jax.nn.sigmoid(x)


def _softplus(x):
    return jnp.maximum(x, 0.0) + jnp.log1p(jnp.exp(-jnp.abs(x)))


def _gelu_tanh(x):
    return 0.5 * x * (1.0 + jnp.tanh(np.sqrt(2.0 / np.pi) * (x + 0.044715 * (x * x * x))))


def _mod_kernel(c_ref, w_ref, b_ref, o_ref):
    c = c_ref[...]
    s = _silu(c).astype(BF16)
    o_ref[0] = _dot(s, w_ref[0].astype(BF16)) + b_ref[0]


def _mod(c_all, ada_w, ada_b):
    depth, d, n6 = ada_w.shape
    rows = c_all.shape[0]
    tn = 512
    return pl.pallas_call(
        _mod_kernel,
        out_shape=jax.ShapeDtypeStruct((depth, rows, n6), F32),
        grid=(depth, n6 // tn),
        in_specs=[pl.BlockSpec((rows, d), lambda l, j: (0, 0)),
                  pl.BlockSpec((1, d, tn), lambda l, j: (l, 0, j)),
                  pl.BlockSpec((1, 1, tn), lambda l, j: (l, 0, j))],
        out_specs=pl.BlockSpec((1, rows, tn), lambda l, j: (l, 0, j)),
        compiler_params=_cparams(("arbitrary", "arbitrary")),
        name="adaln_mod",
    )(c_all, ada_w, ada_b.reshape(depth, 1, n6))


def _mod_spec(mod, k):
    arr, layer, n, per_row = mod
    if per_row:
        return pl.BlockSpec((None, 1, n, D_MODEL), lambda b, t: (layer, 0, 0, k))
    return pl.BlockSpec((None, 1, 1, D_MODEL), lambda b, t: (layer, n + b, 0, k))


def _inproj_kernel(x_ref, sh_ref, sc_ref, g_ref, w_ref, *outs, tm, token_minor):
    x = x_ref[0]
    ms = jnp.mean(x * x, axis=-1, keepdims=True)
    xn = x * lax.rsqrt(ms + EPS) * g_ref[...]
    h = xn * (1.0 + sc_ref[0]) + sh_ref[0]
    proj = _dot(h.astype(BF16), w_ref[...])
    kv = [proj[:, P_KV + LANES * i:P_KV + LANES * (i + 1)] for i in range(6)]
    if token_minor:
        qt_ref, kc_ref, vc_ref, kvb_ref, vt_ref, z_ref, xbc_ref, gd_ref = outs[:8]
        qt_ref[0] = (proj[:, P_Q:P_Q + ATT_WIDTH] * (SCALE * LOG2E)).T.astype(BF16)
        kc_ref[0] = kv[0]
        vc_ref[0] = kv[1]
        kvt = [a.T for a in kv]
        for ref, a in zip(outs[8:], kvt):
            ref[0] = a
        for r in range(tm // TQ):
            cols = slice(r * TQ, (r + 1) * TQ)
            vt_ref[0, r] = jnp.concatenate([kvt[3][:, cols], kvt[5][:, cols]], axis=0).astype(BF16)
    else:
        q_ref, kvb_ref, z_ref, xbc_ref, gd_ref = outs[:5]
        q_ref[0] = (proj[:, P_Q:P_Q + ATT_WIDTH] * SCALE).astype(BF16)
        for ref, a in zip(outs[5:], kv):
            ref[0] = a
    kvb_ref[0] = proj[:, P_KV:P_Z].astype(BF16)
    z_ref[0] = proj[:, P_Z:P_XBC]
    xbc_ref[0] = proj[:, P_XBC:P_GD]
    gd_ref[0] = proj[:, P_GD:P_W]


def _inproj(x, mod, norm_g, w_packed, tm, token_minor):
    bsz, t, d = x.shape
    row = lambda w: pl.BlockSpec((1, tm, w), lambda b, i: (b, i, 0))
    col = lambda w: pl.BlockSpec((1, w, tm), lambda b, i: (b, 0, i))
    shp = lambda w, dt: jax.ShapeDtypeStruct((bsz, t, w), dt)
    shpt = lambda w, dt: jax.ShapeDtypeStruct((bsz, w, t), dt)
    if token_minor:
        out_shape = ([shpt(ATT_WIDTH, BF16), shp(LANES, F32), shp(LANES, F32), shp(6 * LANES, BF16),
                      jax.ShapeDtypeStruct((bsz, t // TQ, 2 * LANES, TQ), BF16),
                      shp(SSM_INNER, F32), shp(CONV_DIM, F32), shp(LANES, F32)] + [shpt(LANES, F32)] * 6)
        out_specs = ([col(ATT_WIDTH), row(LANES), row(LANES), row(6 * LANES),
                      pl.BlockSpec((1, tm // TQ, 2 * LANES, TQ), lambda b, i: (b, i, 0, 0)),
                      row(SSM_INNER), row(CONV_DIM), row(LANES)] + [col(LANES)] * 6)
    else:
        out_shape = ([shp(ATT_WIDTH, BF16), shp(6 * LANES, BF16), shp(SSM_INNER, F32), shp(CONV_DIM, F32),
                      shp(LANES, F32)] + [shp(LANES, F32)] * 6)
        out_specs = [row(ATT_WIDTH), row(6 * LANES), row(SSM_INNER), row(CONV_DIM), row(LANES)] + [row(LANES)] * 6
    return pl.pallas_call(
        functools.partial(_inproj_kernel, tm=tm, token_minor=token_minor),
        out_shape=out_shape,
        grid=(bsz, t // tm),
        in_specs=[row(d), _mod_spec(mod, 0), _mod_spec(mod, 1), _wspec(norm_g), _wspec(w_packed)],
        out_specs=out_specs,
        compiler_params=_cparams(("arbitrary", "arbitrary")),
        name="inproj",
    )(x, mod[0], mod[0], _warr(norm_g), _warr(w_packed))


def _compress_products(src_ref, nh, w1_ref, pitch=CMP_STRIDE):
    acc = jnp.zeros((nh, 4 * CMP_HID), F32)
    for i in range(CMP_STRIDE // 2):
        xa = src_ref[pl.ds(2 * i, nh, stride=pitch), :]
        xb = src_ref[pl.ds(2 * i + 1, nh, stride=pitch), :]
        acc = acc + _dot(jnp.concatenate([xa, xb], axis=1).astype(BF16), w1_ref[i])
    return acc


def _compress_summaries(acc, nh, b1_ref, w2_ref):
    pa = acc[:, :2 * CMP_HID]
    pb = pltpu.roll(acc[:, 2 * CMP_HID:], nh - 1, 0)
    hid = _gelu_tanh(pa + pb + b1_ref[...])
    return _dot(hid.astype(BF16), w2_ref[...])


def _compress_kernel(k_ref, v_ref, w1k_ref, b1k_ref, w2k_ref, w1v_ref, b1v_ref, w2v_ref, ok_ref, ovt_ref, *, nh):
    acc_k = _compress_products(k_ref.at[0], nh, w1k_ref)
    acc_v = _compress_products(v_ref.at[0], nh, w1v_ref)
    ok_ref[0] = _compress_summaries(acc_k, nh, b1k_ref, w2k_ref).astype(BF16)
    ovt_ref[0] = _compress_summaries(acc_v, nh, b1v_ref, w2v_ref).T.astype(BF16)


def _compress(kc, vc, cw):
    bsz, t, _ = kc.shape
    nh = t // CMP_STRIDE
    seq = pl.BlockSpec((1, t, LANES), lambda b: (b, 0, 0))
    ws = [cw[n] for n in ("w1k", "b1k", "w2k", "w1v", "b1v", "w2v")]
    return pl.pallas_call(
        functools.partial(_compress_kernel, nh=nh),
        out_shape=[jax.ShapeDtypeStruct((bsz, nh, LANES), BF16), jax.ShapeDtypeStruct((bsz, LANES, nh), BF16)],
        grid=(bsz,),
        in_specs=[seq, seq] + [_wspec(w) for w in ws],
        out_specs=[pl.BlockSpec((1, nh, LANES), lambda b: (b, 0, 0)), pl.BlockSpec((1, LANES, nh), lambda b: (b, 0, 0))],
        compiler_params=_cparams(("arbitrary",)),
        name="compress",
    )(kc, vc, *[_warr(w) for w in ws])


def _all_sublanes(x, op):
    for shift in (4, 2, 1):
        x = op(x, pltpu.roll(x, shift, 0))
    return x


def _query_blockdiag(qt):
    zero = jnp.zeros((HEAD_DIM, HG * TQ), BF16)
    per_g = [jnp.concatenate([qt[HEAD_DIM * (HG * g + j):HEAD_DIM * (HG * g + j + 1), :] for j in range(HG)], axis=1)
             for g in range(N_KV)]
    return jnp.concatenate([jnp.concatenate([per_g[0], zero], axis=1),
                            jnp.concatenate([zero, per_g[1]], axis=1)], axis=0)


def _topk_mask_t(score, jrow, n_slc):
    rank = jnp.zeros(score.shape, F32)
    for jp in range(n_slc):
        sj = score[jp:jp + 1, :]
        ahead = (sj > score) | ((sj == score) & (jp < jrow))
        rank = rank + ahead.astype(F32)
    return (rank < N_SEL) & (jrow < n_slc)


def _cmp_sel_kernel(qt_ref, kc_ref, vct_ref, ot_ref, c0_ref, srow_ref, ocmpt_ref, selt_ref, *, n_cmp, n_slc, nsub):
    for sub in range(nsub):
        _cmp_sel_tile(pl.program_id(1) * nsub + sub, slice(sub * TQ, (sub + 1) * TQ), qt_ref, kc_ref, vct_ref,
                      ot_ref, c0_ref, srow_ref, ocmpt_ref, selt_ref, n_cmp, n_slc)


def _cmp_sel_tile(t, qcols, qt_ref, kc_ref, vct_ref, ot_ref, c0_ref, srow_ref, ocmpt_ref, selt_ref, n_cmp, n_slc):
    qbd = _query_blockdiag(qt_ref[0, :, qcols])
    kc = kc_ref[0]
    nl = kc.shape[0]
    srow = srow_ref[...]
    s = _dot(kc, qbd) - (c0_ref[...] + srow * (t * TQ).astype(F32))
    nidx = lax.broadcasted_iota(jnp.int32, (nl, NCOL), 0)
    qpos = t * TQ + lax.broadcasted_iota(jnp.int32, (nl, NCOL), 1) % TQ
    mask = (qpos - (nidx * CMP_STRIDE + (CMP_BLOCK - 1)) >= 0) & (nidx < n_cmp)
    sm = jnp.where(mask, s, NEG)
    e = jnp.exp2(sm - jnp.max(sm, axis=0, keepdims=True))
    p = jnp.where(mask, e * (1.0 / jnp.sum(e, axis=0, keepdims=True)), 0.0)
    pb = p.astype(BF16)
    vct = vct_ref[0]

    js = selt_ref.shape[2]
    jrow = lax.broadcasted_iota(jnp.int32, (js, TQ), 0)
    qp = t * TQ + lax.broadcasted_iota(jnp.int32, (js, TQ), 1)
    cur = qp // SLC_BLOCK
    forced = (jrow == 0) | (jrow == cur) | (jrow == cur - 1)
    valid = (jrow * SLC_BLOCK <= qp)
    for g in range(N_KV):
        c0 = g * HG * TQ
        og = _dot(vct[HEAD_DIM * g:HEAD_DIM * (g + 1), :], pb[:, c0:c0 + HG * TQ])
        pg = jnp.zeros((nl, TQ), F32)
        for j in range(HG):
            h = HG * g + j
            ocmpt_ref[0, HEAD_DIM * h:HEAD_DIM * (h + 1), qcols] = og[:, j * TQ:(j + 1) * TQ]
            pg = pg + p[:, c0 + j * TQ:c0 + (j + 1) * TQ]
        imp_t = _expand_l(ot_ref[...], pg)
        score = jnp.where(valid, imp_t + jnp.where(forced, FORCE_BONUS, 0.0), NEG)
        selt_ref[0, g, :, qcols] = _topk_mask_t(score, jrow, n_slc).astype(F32)


def _cmp_sel(qt, kcmp, vcmpt, consts):
    bsz, _, t = qt.shape
    nl = kcmp.shape[1]
    n_slc = t // SLC_BLOCK
    js = consts["ot"].shape[0]
    nsub = 2 if t % (2 * TQ) == 0 else 1
    tw = nsub * TQ
    return pl.pallas_call(
        functools.partial(_cmp_sel_kernel, n_cmp=t // CMP_STRIDE - 1, n_slc=n_slc, nsub=nsub),
        out_shape=[jax.ShapeDtypeStruct((bsz, ATT_WIDTH, t), F32),
                   jax.ShapeDtypeStruct((bsz, N_KV, js, t), F32)],
        grid=(bsz, t // tw),
        in_specs=[pl.BlockSpec((1, ATT_WIDTH, tw), lambda b, i: (b, 0, i)),
                  pl.BlockSpec((1, nl, LANES), lambda b, i: (b, 0, 0)),
                  pl.BlockSpec((1, LANES, nl), lambda b, i: (b, 0, 0)),
                  _resident(consts["ot"].shape), _resident(consts["c0"].shape), _resident(consts["srow2"].shape)],
        out_specs=[pl.BlockSpec((1, ATT_WIDTH, tw), lambda b, i: (b, 0, i)),
                   pl.BlockSpec((1, N_KV, js, tw), lambda b, i: (b, 0, 0, i))],
        compiler_params=_cparams(("arbitrary", "arbitrary")),
        name="cmp_select",
    )(qt, kcmp, vcmpt, consts["ot"], consts["c0"], consts["srow2"])


def _attn_kernel(qt_ref, kvb_ref, vt_ref, selt_ref, ocmpt_ref, gd_ref, fk_ref, coef_ref, amask_ref, egt_ref, o_ref,
                 pen_ref, sbuf, *state):
    nchunk = len(state) // 6
    states = [tuple(state[(3 * br + k) * nchunk:(3 * br + k + 1) * nchunk] for k in range(3)) for br in range(2)]
    t = pl.program_id(1)
    qbd = _query_blockdiag(qt_ref[0])
    gw = HG * TQ
    for g in range(N_KV):
        pen = (selt_ref[0, g] - 1.0) * MASK_BIG
        pen_ref[:, g * gw:(g + 1) * gw] = jnp.concatenate([pen] * HG, axis=1)
    nwin = WINDOW // TQ
    coef = coef_ref[...]
    zero8 = jnp.zeros((SUBLANES, NCOL), F32)
    cw = 2 * TQ
    zpad = jnp.zeros((LANES - 2 * SUBLANES, cw), BF16)
    rid = lax.broadcasted_iota(jnp.int32, (SUBLANES, NCOL), 0)

    def query_operands(c8):
        c16 = jnp.concatenate([c8, zero8], axis=0).astype(BF16)
        return [jnp.concatenate([qbd[:, c * cw:(c + 1) * cw], c16[:, c * cw:(c + 1) * cw], zpad], axis=0)
                for c in range(NCOL // cw)]

    q_win = query_operands(coef)

    def score_tile(kt, slot, kcol, slc, mask):
        r0 = pl.multiple_of(kt * TQ, TQ)
        k_aug = jnp.concatenate([kvb_ref[0, pl.ds(r0, TQ), kcol:kcol + LANES], fk_ref[kt]], axis=1)
        if slc:
            pa = jnp.broadcast_to(pen_ref[pl.ds(2 * kt, 1), :], (SUBLANES, NCOL))
            pb = jnp.broadcast_to(pen_ref[pl.ds(2 * kt + 1, 1), :], (SUBLANES, NCOL))
            q_aug = query_operands(jnp.where(rid == 6, pa, jnp.where(rid == 7, pb, coef)))
        else:
            q_aug = q_win
        for c in range(nchunk):
            cols = slice(c * cw, (c + 1) * cw)
            sbuf[slot, :, cols] = _dot(k_aug, q_aug[c]) + amask_ref[mask, :, cols]

    def consume_tile(kt, slot, vrow, st):
        m_refs, l_refs, acc_refs = st
        v_t = vt_ref[0, kt, vrow:vrow + LANES, :]
        for c in range(nchunk):
            cols = slice(c * cw, (c + 1) * cw)
            g = (c * cw) // gw
            s = sbuf[slot, :, cols].reshape(TQ // SUBLANES, SUBLANES, cw)
            m_old = m_refs[c][...]
            m_new = jnp.maximum(m_old, _all_sublanes(jnp.max(s, axis=0), jnp.maximum))
            alpha = jnp.exp2(m_old - m_new)
            p = jnp.exp2(s - m_new[None])
            l_refs[c][...] = alpha * l_refs[c][...] + _all_sublanes(jnp.sum(p, axis=0), jnp.add)
            m_refs[c][...] = m_new
            pv = _dot(v_t[HEAD_DIM * g:HEAD_DIM * (g + 1), :], p.reshape(TQ, cw).astype(BF16))
            acc = acc_refs[c][...].reshape(HEAD_DIM // SUBLANES, SUBLANES, cw) * alpha[None]
            acc_refs[c][...] = acc.reshape(HEAD_DIM, cw) + pv

    def reset(st):
        m_refs, l_refs, acc_refs = st
        for c in range(nchunk):
            m_refs[c][...] = jnp.full(m_refs[c].shape, NEG, F32)
            l_refs[c][...] = jnp.zeros(l_refs[c].shape, F32)
            acc_refs[c][...] = jnp.zeros(acc_refs[c].shape, F32)

    def result(st):
        _, l_refs, acc_refs = st
        per_chunk = [(acc_refs[c][...].reshape(HEAD_DIM // SUBLANES, SUBLANES, cw)
                      * (1.0 / l_refs[c][...])[None]).reshape(HEAD_DIM, cw) for c in range(nchunk)]
        half = nchunk // N_KV
        return [jnp.concatenate(per_chunk[g * half:(g + 1) * half], axis=1) for g in range(N_KV)]

    def stream(br, first, kcol, vrow, slc):
        n = t - first + 1
        pad = n % 2

        def tile_of(i):
            return jnp.maximum(first + i - pad, first)

        def mask_of(i):
            kt = first + i - pad
            diag = jnp.where(kt == t, 1, 0)
            m = diag if slc else jnp.where(kt == t - nwin, 2, diag)
            return jnp.where(i < pad, 3, m)

        def score(i, par):
            score_tile(tile_of(i), 2 * br + par, kcol, slc, mask_of(i))

        def consume(i, par):
            consume_tile(tile_of(i), 2 * br + par, vrow, states[br])

        return dict(total=n + pad, st=states[br], score=score, consume=consume)

    slc_s = stream(0, 0, 2 * LANES, 0, True)
    win_s = stream(1, jnp.maximum(t - nwin, 0), 4 * LANES, LANES, False)
    streams = (slc_s, win_s)
    for sm in streams:
        reset(sm["st"])
        sm["score"](0, 0)

    def pair_step(j, active):
        i = 2 * j
        for sm in active:
            sm["score"](i + 1, 1)
        for sm in active:
            sm["consume"](i, 0)
        for sm in active:
            sm["score"](i + 2, 0)
        for sm in active:
            sm["consume"](i + 1, 1)

    def both_body(j, c):
        pair_step(j, (slc_s, win_s))
        return c

    def slc_body(j, c):
        pair_step(j, (slc_s,))
        return c

    merged = win_s["total"] // 2 - 1
    lax.fori_loop(0, merged, both_body, 0)
    lax.fori_loop(merged, slc_s["total"] // 2 - 1, slc_body, 0)
    for sm in streams:
        sm["score"](sm["total"] - 1, 1)
    for sm in streams:
        sm["consume"](sm["total"] - 2, 0)
    for sm in streams:
        sm["consume"](sm["total"] - 1, 1)
    o_slc = result(slc_s["st"])
    o_win = result(win_s["st"])

    g_t = jax.nn.sigmoid(gd_ref[0]).T
    gates = [_expand_l(egt_ref[br], g_t) for br in range(3)]
    for pr in range(N_HEADS // 2):
        g, j0 = pr // 2, 2 * (pr % 2)
        rows = slice(LANES * pr, LANES * (pr + 1))
        pair = lambda o: jnp.concatenate([o[g][:, j0 * TQ:(j0 + 1) * TQ], o[g][:, (j0 + 1) * TQ:(j0 + 2) * TQ]], axis=0)
        tot = gates[0][rows] * ocmpt_ref[0, rows, :] + gates[1][rows] * pair(o_slc) + gates[2][rows] * pair(o_win)
        o_ref[0, :, rows] = tot.T.astype(BF16)


def _attn(qt, kvb, vt, selt, ocmpt, gd, consts):
    bsz, _, t = qt.shape
    js = selt.shape[2]
    return pl.pallas_call(
        _attn_kernel,
        out_shape=jax.ShapeDtypeStruct((bsz, t, ATT_WIDTH), BF16),
        grid=(bsz, t // TQ),
        in_specs=[pl.BlockSpec((1, ATT_WIDTH, TQ), lambda b, i: (b, 0, i)),
                  pl.BlockSpec((1, t, 6 * LANES), lambda b, i: (b, 0, 0)),
                  pl.BlockSpec((1, t // TQ, 2 * LANES, TQ), lambda b, i: (b, 0, 0, 0)),
                  pl.BlockSpec((1, N_KV, js, TQ), lambda b, i: (b, 0, 0, i)),
                  pl.BlockSpec((1, ATT_WIDTH, TQ), lambda b, i: (b, 0, i)),
                  pl.BlockSpec((1, TQ, LANES), lambda b, i: (b, i, 0)),
                  _resident(consts["fk"].shape), _resident(consts["coef"].shape), _resident(consts["amask"].shape),
                  _resident(consts["e_gate_t"].shape)],
        out_specs=pl.BlockSpec((1, TQ, ATT_WIDTH), lambda b, i: (b, i, 0)),
        scratch_shapes=([pltpu.VMEM((js, NCOL), F32), pltpu.VMEM((4, TQ, NCOL), F32)]
                        + ([pltpu.VMEM((SUBLANES, 2 * TQ), F32)] * (2 * N_HEADS // 2)
                           + [pltpu.VMEM((HEAD_DIM, 2 * TQ), F32)] * (N_HEADS // 2)) * 2),
        compiler_params=_cparams(("arbitrary", "arbitrary")),
        name="slc_win_attn",
    )(qt, kvb, vt, selt, ocmpt, gd, consts["fk"], consts["coef"], consts["amask"], consts["e_gate_t"])


def _gated_norm(y, z, ng):
    yz = y * _silu(z)
    half = SSM_INNER // SSM_GROUPS
    outs = []
    for g in range(SSM_GROUPS):
        part = yz[:, half * g:half * (g + 1)]
        ms = jnp.mean(part * part, axis=-1, keepdims=True)
        outs.append(part * lax.rsqrt(ms + EPS))
    return jnp.concatenate(outs, axis=1) * ng


def _ssd_kernel(xbc_ref, z_ref, gd_ref, cw_ref, cb_ref, dtb_ref, a_ref, dsk_ref, ng_ref, edt_ref,
                y_ref, hfin_ref, ubuf, sstate, ybufs, *, lc, nsub):
    t = pl.program_id(1)
    nt = pl.num_programs(1)

    @pl.when(t == 0)
    def _():
        ubuf[0:SUBLANES, :] = jnp.zeros((SUBLANES, CONV_DIM), F32)
        sstate[...] = jnp.zeros(sstate.shape, F32)

    ubuf[SUBLANES:SUBLANES + lc * nsub, :] = xbc_ref[0]
    for sub in range(nsub):
        _ssd_chunk(sub * lc, lc, ubuf, z_ref, gd_ref, cw_ref, cb_ref, dtb_ref, a_ref, dsk_ref, ng_ref, edt_ref,
                   y_ref, sstate, ybufs.at[sub])
    ubuf[0:SUBLANES, :] = ubuf[lc * nsub:lc * nsub + SUBLANES, :]

    @pl.when(t == nt - 1)
    def _():
        s_pad = jnp.concatenate([sstate[...], jnp.zeros((LANES - D_STATE, SSM_INNER), F32)], axis=0)
        hfin_ref[0] = s_pad.T[:, :D_STATE]


def _ssd_chunk(r0, lc, ubuf, z_ref, gd_ref, cw_ref, cb_ref, dtb_ref, a_ref, dsk_ref, ng_ref, edt_ref,
               y_ref, sstate, ybuf):
    conv = cb_ref[...] + jnp.zeros((lc, CONV_DIM), F32)
    for k in range(SSM_CONV):
        off = SUBLANES - (SSM_CONV - 1) + k + r0
        conv = conv + ubuf[off:off + lc, :] * cw_ref[k:k + 1, :]
    act = _silu(conv)
    xm = act[:, :SSM_INNER]
    bm = act[:, SSM_INNER:SSM_INNER + LANES]
    cm = act[:, SSM_INNER + LANES:]

    dt = _softplus(gd_ref[0, r0:r0 + lc, :] + dtb_ref[...])
    a = dt * a_ref[...]
    ri = lax.broadcasted_iota(jnp.int32, (lc, lc), 0)
    ci = lax.broadcasted_iota(jnp.int32, (lc, lc), 1)
    tril = ri >= ci
    cs = jnp.dot(tril.astype(F32), a, preferred_element_type=F32, precision=lax.Precision.HIGHEST)
    cs_t = cs.T
    edt = edt_ref[...]
    dt_x = _expand(dt, edt)
    cs_x = _expand(cs, edt)
    cs_last = cs_x[lc - 1:lc, :]
    bm_t = bm.T
    xd = xm * dt_x
    xw = xm * (jnp.exp(cs_last - cs_x) * dt_x)
    s_old = sstate[...]
    half = SSM_INNER // SSM_GROUPS
    for g in range(SSM_GROUPS):
        bg = bm[:, D_STATE * g:D_STATE * (g + 1)].astype(BF16)
        cg = cm[:, D_STATE * g:D_STATE * (g + 1)].astype(BF16)
        cb = _dot_nt(cg, bg)
        for j in range(SSM_HEADS // SSM_GROUPS):
            h = g * (SSM_HEADS // SSM_GROUPS) + j
            col = cs[:, GD_DT + h:GD_DT + h + 1]
            row = cs_t[GD_DT + h:GD_DT + h + 1, :]
            lm = jnp.exp(jnp.where(tril, col - row, NEG))
            lo, hi = HEAD_DIM * h, HEAD_DIM * (h + 1)
            ybuf[:, lo:hi] = _dot((cb * lm).astype(BF16), xd[:, lo:hi].astype(BF16))
        sg = s_old[:, half * g:half * (g + 1)]
        y_off = _dot(cg, sg.astype(BF16))
        ybuf[:, half * g:half * (g + 1)] = ybuf[:, half * g:half * (g + 1)] + y_off * jnp.exp(cs_x[:, half * g:half * (g + 1)])
        st = _dot(bm_t[D_STATE * g:D_STATE * (g + 1), :].astype(BF16), xw[:, half * g:half * (g + 1)].astype(BF16))
        sstate[:, half * g:half * (g + 1)] = sg * jnp.exp(cs_last[:, half * g:half * (g + 1)]) + st

    y = ybuf[...] + dsk_ref[...] * xm
    y_ref[0, r0:r0 + lc, :] = _gated_norm(y, z_ref[0, r0:r0 + lc, :], ng_ref[...]).astype(BF16)


def _ssd(xbc, z, gd, sw, consts):
    bsz, t, _ = xbc.shape
    lc = SSM_CHUNK
    nsub = 4 if t % (4 * lc) == 0 else 1
    rows = lc * nsub
    tile = lambda w: pl.BlockSpec((1, rows, w), lambda b, i: (b, i, 0))
    ws = [sw[n] for n in ("conv_w", "conv_b", "dt_bias", "a", "d_skip", "norm_g")] + [consts["e_dt"]]
    y, hfin = pl.pallas_call(
        functools.partial(_ssd_kernel, lc=lc, nsub=nsub),
        out_shape=[jax.ShapeDtypeStruct((bsz, t, SSM_INNER), BF16),
                   jax.ShapeDtypeStruct((bsz, SSM_INNER, D_STATE), F32)],
        grid=(bsz, t // rows),
        in_specs=[tile(CONV_DIM), tile(SSM_INNER), tile(LANES)] + [_wspec(w) for w in ws],
        out_specs=[tile(SSM_INNER), pl.BlockSpec((1, SSM_INNER, D_STATE), lambda b, i: (b, 0, 0))],
        scratch_shapes=[pltpu.VMEM((rows + SUBLANES, CONV_DIM), F32),
                        pltpu.VMEM((D_STATE, SSM_INNER), F32),
                        pltpu.VMEM((nsub, lc, SSM_INNER), F32)],
        compiler_params=_cparams(("arbitrary", "arbitrary")),
        name="ssd_scan",
    )(xbc, z, gd, *[_warr(w) for w in ws])
    return y, hfin.reshape(bsz, SSM_HEADS, HEAD_DIM, D_STATE)


def _mix_ffn_kernel(*refs, tm, carry_conv, last):
    if carry_conv:
        (x_ref, oa_ref, ys_ref, g1_ref, sh2_ref, sc2_ref, g2_ref, n2_ref, wo_ref, wu_ref, fcw_ref, fcb_ref,
         wd_ref, fg_ref, out_ref, ug_ref, ubuf) = refs
    else:
        (x_ref, oa_ref, ys_ref, g1_ref, sh2_ref, sc2_ref, g2_ref, n2_ref, wo_ref, wu_ref, fcw_ref, fcb_ref,
         wd_ref, fg_ref, p0_ref, p1_ref, out_ref, ug_ref) = refs
    t = pl.program_id(1)
    nt = pl.num_programs(1)
    x = x_ref[0]
    mix = _dot(oa_ref[0], wo_ref[:ATT_WIDTH, :]) + _dot(ys_ref[0], wo_ref[ATT_WIDTH:, :])
    x1 = x + g1_ref[0] * mix
    ms = jnp.mean(x1 * x1, axis=-1, keepdims=True)
    h2 = (x1 * lax.rsqrt(ms + EPS) * n2_ref[...]) * (1.0 + sc2_ref[0]) + sh2_ref[0]
    h2 = h2.astype(BF16)
    if carry_conv:
        @pl.when(t == 0)
        def _():
            ubuf[0:SUBLANES, :] = jnp.zeros((SUBLANES, D_FF), F32)

    def up(c):
        lo, hi = FF_CHUNKS[c]
        return _dot(h2, wu_ref[:, lo:hi]), _dot(h2, wu_ref[:, D_FF + lo:D_FF + hi])

    nxt = up(0)
    down = None
    for c, (lo, hi) in enumerate(FF_CHUNKS):
        ug, uv = nxt
        if c + 1 < len(FF_CHUNKS):
            nxt = up(c + 1)
        if carry_conv:
            ubuf[SUBLANES:SUBLANES + tm, lo:hi] = ug
            u1 = ubuf[SUBLANES - 1:SUBLANES - 1 + tm, lo:hi]
            u2 = ubuf[SUBLANES - 2:SUBLANES - 2 + tm, lo:hi]
        else:
            u1 = p1_ref[0, :, lo:hi]
            u2 = p0_ref[0, :, lo:hi]
            ug_ref[0, :, lo:hi] = ug
        ugc = (fcb_ref[:, lo:hi] + u2 * fcw_ref[0:1, lo:hi] + u1 * fcw_ref[1:2, lo:hi] + ug * fcw_ref[2:3, lo:hi])
        d = _dot((_silu(ugc) * uv).astype(BF16), wd_ref[lo:hi, :])
        down = d if down is None else down + d
    if carry_conv:
        @pl.when(t == nt - 1)
        def _():
            ug_ref[0] = ubuf[tm:tm + SUBLANES, :]

        ubuf[0:SUBLANES, :] = ubuf[tm:tm + SUBLANES, :]
    x2 = x1 + g2_ref[0] * down
    if last:
        ms2 = jnp.mean(x2 * x2, axis=-1, keepdims=True)
        out_ref[0] = x2 * lax.rsqrt(ms2 + EPS) * fg_ref[...]
    else:
        out_ref[0] = x2


def _mix_ffn(x, oatt, yssm, mod, fw, tm, prev, last):
    bsz, t, d = x.shape
    carry = prev is None
    row = lambda w: pl.BlockSpec((1, tm, w), lambda b, i: (b, i, 0))
    ws = [fw[n] for n in ("norm2_g", "w_out", "w_up", "conv_w", "conv_b", "w_down", "final_g")]
    in_specs = ([row(d), row(ATT_WIDTH), row(SSM_INNER)] + [_mod_spec(mod, k) for k in (2, 3, 4, 5)]
                + [_wspec(w) for w in ws])
    args = [x, oatt, yssm] + [mod[0]] * 4 + [_warr(w) for w in ws]
    if carry:
        ug_shape, ug_spec = (bsz, SUBLANES, D_FF), pl.BlockSpec((1, SUBLANES, D_FF), lambda b, i: (b, 0, 0))
        scratch = [pltpu.VMEM((tm + SUBLANES, D_FF), F32)]
    else:
        ug_shape, ug_spec = (bsz, t, D_FF), row(D_FF)
        scratch = []
        in_specs += [row(D_FF), row(D_FF)]
        args += list(prev)
    return pl.pallas_call(
        functools.partial(_mix_ffn_kernel, tm=tm, carry_conv=carry, last=last),
        out_shape=[jax.ShapeDtypeStruct((bsz, t, d), F32), jax.ShapeDtypeStruct(ug_shape, F32)],
        grid=(bsz, t // tm),
        in_specs=in_specs,
        out_specs=[row(d), ug_spec],
        scratch_shapes=scratch,
        compiler_params=_cparams(("arbitrary", "arbitrary")),
        name="mix_ffn",
    )(*args)


def _head_rows(row512):
    r = lax.broadcasted_iota(jnp.int32, (N_HEADS, ATT_WIDTH), 0)
    c = lax.broadcasted_iota(jnp.int32, (N_HEADS, ATT_WIDTH), 1)
    return jnp.where(c // HEAD_DIM == r, jnp.broadcast_to(row512, (N_HEADS, ATT_WIDTH)), 0.0)


def _head_diag(x8):
    r = lax.broadcasted_iota(jnp.int32, (N_HEADS, ATT_WIDTH), 0)
    c = lax.broadcasted_iota(jnp.int32, (N_HEADS, ATT_WIDTH), 1)
    return jnp.sum(jnp.where(c // HEAD_DIM == r, x8, 0.0), axis=0, keepdims=True)


def _slope_col():
    r = lax.broadcasted_iota(jnp.int32, (N_HEADS, 1), 0)
    s = jnp.zeros((N_HEADS, 1), F32)
    for h in range(N_HEADS):
        s = jnp.where(r == h, SLOPES[h], s)
    return s


def _s_cmp_kernel(pt_ref, q_ref, kvn_ref, gd_ref, ck_hbm, cv_hbm,
                  w1k_ref, b1k_ref, w2k_ref, w1v_ref, b1v_ref, w2v_ref, ov_ref, u_ref, e0_ref,
                  ocmp_ref, idx_ref, stage_k, stage_v, kbuf, vbuf, sem,
                  *, layer, n_pages, page, nh8, nhp, q_pos, n_slc):
    b = pl.program_id(0)
    nb = pl.num_programs(0)
    past = n_pages * page

    def copies(bb, p, slot):
        pg = pt_ref[bb * n_pages + p]
        return (pltpu.make_async_copy(ck_hbm.at[layer, pg], stage_k.at[slot, p], sem.at[0, slot]),
                pltpu.make_async_copy(cv_hbm.at[layer, pg], stage_v.at[slot, p], sem.at[1, slot]))

    unroll = 8 if n_pages % 8 == 0 else 1

    def issue(bb, slot):
        def body(i, c):
            for u in range(unroll):
                for cp in copies(bb, i * unroll + u, slot):
                    cp.start(priority=u % 2)
            return c
        lax.fori_loop(0, n_pages // unroll, body, 0)

    @pl.when(b == 0)
    def _():
        issue(0, 0)

    @pl.when(b + 1 < nb)
    def _():
        issue(b + 1, (b + 1) % 2)

    slot = b % 2

    def wait(i, c):
        for u in range(unroll):
            for cp in copies(b, i * unroll + u, slot):
                cp.wait()
        return c

    lax.fori_loop(0, n_pages // unroll, wait, 0)

    hpp = page // CMP_STRIDE

    def untranspose(i, c):
        for u in range(unroll):
            p = i * unroll + u
            base = pl.multiple_of(p * (hpp * ROW_PITCH), SUBLANES)
            kp = stage_k[slot, p].T
            vp = stage_v[slot, p].T
            for n in range(hpp):
                rows = pl.ds(base + n * ROW_PITCH, CMP_STRIDE)
                kbuf[rows, :] = kp[n * CMP_STRIDE:(n + 1) * CMP_STRIDE]
                vbuf[rows, :] = vp[n * CMP_STRIDE:(n + 1) * CMP_STRIDE]
        return c

    lax.fori_loop(0, n_pages // unroll, untranspose, 0)
    first_new = (past // CMP_STRIDE) * ROW_PITCH
    tail = nh8 * ROW_PITCH - first_new
    kbuf[first_new:, :] = jnp.zeros((tail, LANES), F32)
    vbuf[first_new:, :] = jnp.zeros((tail, LANES), F32)
    kbuf[first_new:first_new + 1, :] = kvn_ref[0, :, 0:LANES]
    vbuf[first_new:first_new + 1, :] = kvn_ref[0, :, LANES:2 * LANES]

    pad = jnp.zeros((nhp - nh8, ATT_WIDTH), F32)
    acc_k = _compress_products(kbuf, nh8, w1k_ref, ROW_PITCH)
    acc_v = _compress_products(vbuf, nh8, w1v_ref, ROW_PITCH)
    kc4 = jnp.concatenate([_compress_summaries(acc_k, nh8, b1k_ref, w2k_ref), pad], axis=0).astype(BF16)
    vc4 = jnp.concatenate([_compress_summaries(acc_v, nh8, b1v_ref, w2v_ref), pad], axis=0).astype(BF16)
    n_cmp = (past + 1 + CMP_STRIDE - 1) // CMP_STRIDE - 1

    qm = _head_rows(q_ref[0].astype(F32)).astype(BF16)
    nidx = lax.broadcasted_iota(jnp.int32, (N_HEADS, nhp), 1)
    dist_i = q_pos - (nidx * CMP_STRIDE + (CMP_BLOCK - 1))
    mask = (dist_i >= 0) & (nidx < n_cmp)
    s = _dot_nt(qm, kc4) - _slope_col() * dist_i.astype(F32)
    sm = jnp.where(mask, s, NEG)
    e = jnp.exp(sm - jnp.max(sm, axis=-1, keepdims=True))
    p = jnp.where(mask, e / jnp.sum(e, axis=-1, keepdims=True), 0.0)
    o = _head_diag(_dot(p.astype(BF16), vc4))
    g8 = jnp.broadcast_to(jax.nn.sigmoid(gd_ref[0]), (SUBLANES, LANES))
    gate = _expand(g8, e0_ref[...])[0:1, :]
    ocmp_ref[0] = gate * o

    js = ov_ref.shape[1]
    imp8 = _expand(p, ov_ref[...])
    hrow = lax.broadcasted_iota(jnp.int32, (N_HEADS, js), 0)
    jl = lax.broadcasted_iota(jnp.int32, (1, js), 1)
    cur = q_pos // SLC_BLOCK
    forced = (jl == 0) | (jl == cur) | (jl == cur - 1)
    valid = (jl * SLC_BLOCK <= q_pos) & (jl < n_slc)
    rj = lax.broadcasted_iota(jnp.int32, (js, js), 0)
    cj = lax.broadcasted_iota(jnp.int32, (js, js), 1)
    kk = lax.broadcasted_iota(jnp.int32, (2 * SUBLANES, js), 0)
    jvals = jnp.broadcast_to(jl.astype(F32), (SUBLANES, js)).astype(BF16)
    for g in range(N_KV):
        imp = jnp.sum(jnp.where(hrow // HG == g, imp8, 0.0), axis=0, keepdims=True)
        score = jnp.where(valid, imp + jnp.where(forced, FORCE_BONUS, 0.0), NEG)
        sb = jnp.broadcast_to(score, (js, js))
        col = jnp.sum(jnp.where(rj == cj, sb, 0.0), axis=1, keepdims=True)
        ahead = (col > sb) | ((col == sb) & (rj < cj))
        rank = jnp.sum(ahead.astype(F32), axis=0, keepdims=True)
        sel = ((rank < N_SEL) & (jl < n_slc)).astype(F32)
        pos = _dot(jnp.broadcast_to(sel, (SUBLANES, js)).astype(BF16), u_ref[...])[0:1, :]
        onehot = ((jnp.broadcast_to(pos, (2 * SUBLANES, js)) == kk.astype(F32))
                  & (jnp.broadcast_to(sel, (2 * SUBLANES, js)) > 0.5)).astype(BF16)
        idx = _dot_nt(jvals, onehot)[0:1, :]
        idx_ref[0, g:g + 1, :] = idx.astype(jnp.int32)


def _s_cmp(layer, page_table, q, kvn, gd, cache_k, cache_v, cw4, consts, q_pos):
    ns, n_pages = page_table.shape
    page = cache_k.shape[3]
    past = n_pages * page
    nh = (past + 1 + CMP_STRIDE - 1) // CMP_STRIDE
    nh8 = -(-nh // SUBLANES) * SUBLANES
    nhp = -(-nh // LANES) * LANES
    n_slc = past // SLC_BLOCK + 1
    ws = [cw4[n] for n in ("w1k", "b1k", "w2k", "w1v", "b1v", "w2v")] + [consts["ov_s"], consts["u_s"], consts["e_gate"][0]]
    full = lambda a: pl.BlockSpec(a.shape, lambda b, pt: (0,) * a.ndim)
    per_b = lambda a: pl.BlockSpec((1,) + a.shape[1:], lambda b, pt: (b,) + (0,) * (a.ndim - 1))
    grid_spec = pltpu.PrefetchScalarGridSpec(
        num_scalar_prefetch=1,
        grid=(ns,),
        in_specs=[per_b(q), per_b(kvn), per_b(gd), pl.BlockSpec(memory_space=pl.ANY), pl.BlockSpec(memory_space=pl.ANY)]
                 + [_wspec(w) for w in ws],
        out_specs=[pl.BlockSpec((1, 1, ATT_WIDTH), lambda b, pt: (b, 0, 0)),
                   pl.BlockSpec((1, N_KV, N_SEL), lambda b, pt: (b, 0, 0))],
        scratch_shapes=[pltpu.VMEM((2, n_pages, LANES, page), F32), pltpu.VMEM((2, n_pages, LANES, page), F32),
                        pltpu.VMEM((nh8 * ROW_PITCH, LANES), F32), pltpu.VMEM((nh8 * ROW_PITCH, LANES), F32),
                        pltpu.SemaphoreType.DMA((2, 2))],
    )
    return pl.pallas_call(
        functools.partial(_s_cmp_kernel, layer=layer, n_pages=n_pages, page=page, nh8=nh8, nhp=nhp, q_pos=q_pos,
                          n_slc=n_slc),
        out_shape=[jax.ShapeDtypeStruct((ns, 1, ATT_WIDTH), F32), jax.ShapeDtypeStruct((ns, N_KV, N_SEL), jnp.int32)],
        grid_spec=grid_spec,
        compiler_params=_cparams(("arbitrary",)),
        name="sample_cmp_select",
    )(page_table.reshape(-1), q, kvn, gd, cache_k, cache_v, *[_warr(w) for w in ws])


def _s_slc_win_kernel(idx_ref, pt_ref, q_ref, kvn_ref, gd_ref, ocmp_ref, wk_ref, wv_ref, sk_hbm, sv_hbm,
                      tile_ref, e1_ref, e2_ref, o_ref, wko_ref, wvo_ref, kb, vb, sem,
                      *, layer, n_pages, page, q_pos, wb):
    b = pl.program_id(0)
    nb = pl.num_programs(0)
    bpp = page // SLC_BLOCK
    n_past = n_pages * bpp
    nblk = N_KV * N_SEL
    blks = [idx_ref[b * nblk + i] for i in range(nblk)]
    slot = b % 2

    def copies(bb, sl, i):
        blk = idx_ref[bb * nblk + i]
        pg = pt_ref[bb * n_pages + jnp.minimum(blk, n_past - 1) // bpp]
        dst = slice(i * page, (i + 1) * page)
        return (pltpu.make_async_copy(sk_hbm.at[layer, pg], kb.at[sl, :, dst], sem.at[0, sl]),
                pltpu.make_async_copy(sv_hbm.at[layer, pg], vb.at[sl, :, dst], sem.at[1, sl]))

    def issue(bb, sl):
        for i in range(nblk):
            for cp in copies(bb, sl, i):
                cp.start(priority=i % 2)

    @pl.when(b == 0)
    def _():
        issue(0, 0)

    @pl.when(b + 1 < nb)
    def _():
        issue(b + 1, (b + 1) % 2)

    tile_m = tile_ref[...]
    q8 = _dot_nt(_head_rows(q_ref[0].astype(F32)).astype(BF16), tile_m).astype(BF16)
    slope = _slope_col()
    g8 = jnp.broadcast_to(jax.nn.sigmoid(gd_ref[0]), (SUBLANES, LANES))
    gate1 = _expand(g8, e1_ref[...])[0:1, :]
    gate2 = _expand(g8, e2_ref[...])[0:1, :]
    rgrp = lax.broadcasted_iota(jnp.int32, (N_HEADS, 1), 0) // HG

    def new_key(lane0):
        kn = kvn_ref[0, :, lane0:lane0 + LANES].astype(BF16).astype(F32)
        vn = kvn_ref[0, :, lane0 + LANES:lane0 + 2 * LANES].astype(BF16).astype(F32)
        return jnp.sum(q8.astype(F32) * kn, axis=-1, keepdims=True), vn

    def finish(e, e_new, v_t, v_new, l):
        o8 = (_dot_nt(e.astype(BF16), v_t.astype(BF16)) + e_new * v_new) / l
        return _head_diag(_expand(o8, tile_m))

    def shifted(w_t, new_row):
        col = jnp.broadcast_to(new_row, (LANES, LANES)).T[:, 0:1]
        lane = lax.broadcasted_iota(jnp.int32, (LANES, wb), 1)
        return jnp.where(lane == wb - 1, col, pltpu.roll(w_t, wb - 1, 1))

    wk_t = wk_ref[0, 0]
    wv_t = wv_ref[0, 0]
    kpos = (q_pos - wb) + lax.broadcasted_iota(jnp.int32, (N_HEADS, wb), 1)
    dist = q_pos - kpos
    wmask = (dist >= 0) & (dist <= WINDOW)
    s = jnp.where(wmask, _dot(q8, wk_t.astype(BF16)) - slope * dist.astype(F32), NEG)
    s_new, v_new = new_key(4 * LANES)
    m = jnp.maximum(jnp.max(s, axis=-1, keepdims=True), s_new)
    e = jnp.where(wmask, jnp.exp(s - m), 0.0)
    e_new = jnp.exp(s_new - m)
    o_win = finish(e, e_new, wv_t, v_new, jnp.sum(e, axis=-1, keepdims=True) + e_new)
    wko_ref[0] = shifted(wk_t, kvn_ref[0, :, 4 * LANES:5 * LANES])
    wvo_ref[0] = shifted(wv_t, kvn_ref[0, :, 5 * LANES:6 * LANES])

    for i in range(nblk):
        for cp in copies(b, slot, i):
            cp.wait()
    lane = lax.broadcasted_iota(jnp.int32, (1, page), 1)
    kpos_t, ok_t = [], []
    has_new = [jnp.zeros((1, 1), jnp.int32) for _ in range(N_KV)]
    for i in range(nblk):
        blk = blks[i]
        kpos_t.append(blk * SLC_BLOCK + lane % SLC_BLOCK)
        ok_t.append(((lane // SLC_BLOCK) == blk % bpp) & (blk < n_past))
        has_new[i // N_SEL] = jnp.maximum(has_new[i // N_SEL], (blk >= n_past).astype(jnp.int32))
    kpos = jnp.concatenate(kpos_t, axis=1)
    ok = jnp.concatenate(ok_t, axis=1)
    pgrp = lax.broadcasted_iota(jnp.int32, (1, nblk * page), 1) // (N_SEL * page)
    dist = q_pos - kpos
    kmask = ok & (dist >= 0) & (rgrp == pgrp)
    s = jnp.where(kmask, _dot(q8, kb[slot].astype(BF16)) - slope * dist.astype(F32), NEG)
    s_new, v_new = new_key(2 * LANES)
    new_on = jnp.where(rgrp == 0, has_new[0], has_new[1]) > 0
    sn = jnp.where(new_on, s_new, NEG)
    m = jnp.maximum(jnp.max(s, axis=-1, keepdims=True), sn)
    e = jnp.where(kmask, jnp.exp(s - m), 0.0)
    e_new = jnp.where(new_on, jnp.exp(sn - m), 0.0)
    o_slc = finish(e, e_new, vb[slot], v_new, jnp.sum(e, axis=-1, keepdims=True) + e_new)

    o_ref[0] = (ocmp_ref[0] + gate1 * o_slc + gate2 * o_win).astype(BF16)


def _s_slc_win(layer, idx, page_table, q, kvn, gd, ocmp, win_k, win_v, slc_k, slc_v, consts, q_pos):
    ns, n_pages = page_table.shape
    page = slc_k.shape[3]
    wb = win_k.shape[3]
    full = lambda a: pl.BlockSpec(a.shape, lambda b, i, pt: (0,) * a.ndim)
    per_b = lambda a: pl.BlockSpec((1,) + a.shape[1:], lambda b, i, pt: (b,) + (0,) * (a.ndim - 1))
    win = pl.BlockSpec((1, 1, LANES, wb), lambda b, i, pt: (layer, b, 0, 0))
    ws = [consts["tile_m"], consts["e_gate"][1], consts["e_gate"][2]]
    nlane = N_KV * N_SEL * page
    grid_spec = pltpu.PrefetchScalarGridSpec(
        num_scalar_prefetch=2,
        grid=(ns,),
        in_specs=[per_b(q), per_b(kvn), per_b(gd), per_b(ocmp), win, win,
                  pl.BlockSpec(memory_space=pl.ANY), pl.BlockSpec(memory_space=pl.ANY)] + [full(w) for w in ws],
        out_specs=[pl.BlockSpec((1, 1, ATT_WIDTH), lambda b, i, pt: (b, 0, 0)),
                   pl.BlockSpec((1, LANES, wb), lambda b, i, pt: (b, 0, 0)),
                   pl.BlockSpec((1, LANES, wb), lambda b, i, pt: (b, 0, 0))],
        scratch_shapes=[pltpu.VMEM((2, LANES, nlane), F32), pltpu.VMEM((2, LANES, nlane), F32),
                        pltpu.SemaphoreType.DMA((2, 2))],
    )
    return pl.pallas_call(
        functools.partial(_s_slc_win_kernel, layer=layer, n_pages=n_pages, page=page, q_pos=q_pos, wb=wb),
        out_shape=[jax.ShapeDtypeStruct((ns, 1, ATT_WIDTH), BF16),
                   jax.ShapeDtypeStruct((ns, LANES, wb), F32), jax.ShapeDtypeStruct((ns, LANES, wb), F32)],
        grid_spec=grid_spec,
        compiler_params=_cparams(("arbitrary",)),
        name="sample_slc_win_attn",
    )(idx.reshape(-1), page_table.reshape(-1), q, kvn, gd, ocmp, win_k, win_v, slc_k, slc_v, *ws)


def _col_bcast(row):
    blocks = [jnp.broadcast_to(row[:, LANES * i:LANES * (i + 1)], (LANES, LANES)).T for i in range(row.shape[1] // LANES)]
    return jnp.concatenate(blocks, axis=0)


def _s_ssd_kernel(xbc_ref, prev_ref, z_ref, gd_ref, h0_ref, cw_ref, cb_ref, dtb_ref, a_ref, dsk_ref, ng_ref,
                  edt_ref, y_ref, h_ref):
    conv = cb_ref[...] + xbc_ref[0] * cw_ref[SSM_CONV - 1:SSM_CONV, :]
    for k in range(SSM_CONV - 1):
        conv = conv + prev_ref[0, k:k + 1, :] * cw_ref[k:k + 1, :]
    act = _silu(conv)
    xm = act[:, :SSM_INNER]
    bm = act[:, SSM_INNER:SSM_INNER + LANES]
    cm = act[:, SSM_INNER + LANES:]
    dt = _softplus(gd_ref[0] + dtb_ref[...])
    a = dt * a_ref[...]
    edt = edt_ref[...]
    dt_x = _expand(jnp.broadcast_to(dt, (SUBLANES, LANES)), edt)[0:1, :]
    dec_x = jnp.exp(_expand(jnp.broadcast_to(a, (SUBLANES, LANES)), edt)[0:1, :])
    dtx = dt_x * xm
    h0 = h0_ref[0, 0].reshape(SSM_INNER, D_STATE)
    half = SSM_INNER // SSM_GROUPS
    rsel = lax.broadcasted_iota(jnp.int32, (SSM_INNER, 1), 0) // half
    lsel = lax.broadcasted_iota(jnp.int32, (1, SSM_INNER), 1) // half
    y_off = jnp.zeros((1, SSM_INNER), F32)
    cbx = jnp.zeros((1, SSM_INNER), F32)
    brow = jnp.zeros((SSM_INNER, D_STATE), F32)
    for g in range(SSM_GROUPS):
        bg = bm[:, D_STATE * g:D_STATE * (g + 1)]
        cg = cm[:, D_STATE * g:D_STATE * (g + 1)]
        c8 = jnp.broadcast_to(cg, (SUBLANES, D_STATE)).astype(BF16)
        yo = _dot_nt(c8, h0[half * g:half * (g + 1), :].astype(BF16))[0:1, :]
        y_off = jnp.where(lsel == g, jnp.concatenate([yo] * SSM_GROUPS, axis=1), y_off)
        cbx = jnp.where(lsel == g, jnp.sum(cg * bg, axis=-1, keepdims=True), cbx)
        brow = jnp.where(rsel == g, jnp.broadcast_to(bg, (SSM_INNER, D_STATE)), brow)
    y = y_off * dec_x + cbx * dtx + dsk_ref[...] * xm
    y_ref[0] = _gated_norm(y, z_ref[0], ng_ref[...]).astype(BF16)
    h_new = h0 * _col_bcast(dec_x)[:, :D_STATE] + _col_bcast(dtx)[:, :D_STATE] * brow
    h_ref[0] = h_new.reshape(SSM_HEADS, HEAD_DIM, D_STATE)


def _s_ssd(layer, xbc, prev, z, gd, state_ssm, sw, consts):
    ns = state_ssm.shape[1]
    full = lambda a: pl.BlockSpec(a.shape, lambda b: (0,) * a.ndim)
    per_b = lambda a: pl.BlockSpec((1,) + a.shape[1:], lambda b: (b,) + (0,) * (a.ndim - 1))
    hshape = state_ssm.shape[2:]
    ws = [sw[n] for n in ("conv_w", "conv_b", "dt_bias", "a", "d_skip", "norm_g")] + [consts["e_dt"]]
    return pl.pallas_call(
        _s_ssd_kernel,
        out_shape=[jax.ShapeDtypeStruct((ns, 1, SSM_INNER), BF16), jax.ShapeDtypeStruct((ns,) + hshape, F32)],
        grid=(ns,),
        in_specs=[per_b(xbc), per_b(prev), per_b(z), per_b(gd),
                  pl.BlockSpec((1, 1) + hshape, lambda b: (layer, b, 0, 0, 0))] + [_wspec(w) for w in ws],
        out_specs=[pl.BlockSpec((1, 1, SSM_INNER), lambda b: (b, 0, 0)),
                   pl.BlockSpec((1,) + hshape, lambda b: (b, 0, 0, 0))],
        compiler_params=_cparams(("arbitrary",)),
        name="sample_ssd_step",
    )(xbc, prev, z, gd, state_ssm, *[_warr(w) for w in ws])


def _constants(t_prompt, past_len):
    c = {}
    eg = np.zeros((3, LANES, ATT_WIDTH), np.float32)
    for br in range(3):
        for h in range(N_HEADS):
            eg[br, br * N_HEADS + h, h * HEAD_DIM:(h + 1) * HEAD_DIM] = 1.0
    c["e_gate"] = [jnp.asarray(eg[i], BF16) for i in range(3)]
    c["e_gate_t"] = jnp.asarray(eg.transpose(0, 2, 1), BF16)
    ed = np.zeros((LANES, SSM_INNER), np.float32)
    for h in range(SSM_HEADS):
        ed[GD_DT + h, h * HEAD_DIM:(h + 1) * HEAD_DIM] = 1.0
    c["e_dt"] = jnp.asarray(ed, BF16)

    def overlap(n_cmp_pad, n_slc, n_slc_pad):
        cs = np.arange(n_cmp_pad)[:, None] * CMP_STRIDE
        ss = np.arange(n_slc_pad)[None, :] * SLC_BLOCK
        ov = ((cs < ss + SLC_BLOCK) & (cs + CMP_BLOCK > ss) & (np.arange(n_slc_pad)[None, :] < n_slc))
        return ov.astype(np.float32)
    nh_p = t_prompt // CMP_STRIDE
    n_slc_p = t_prompt // SLC_BLOCK
    js_p = -(-n_slc_p // SUBLANES) * SUBLANES
    ov = overlap(nh_p, n_slc_p, js_p)
    ov[nh_p - 1:, :] = 0.0
    c["ot"] = jnp.asarray(ov.T, BF16)
    col = np.arange(NCOL)
    slope = np.asarray(SLOPES, np.float64)[col // TQ]
    c["srow2"] = jnp.asarray(LOG2E * slope[None, :], F32)
    c["c0"] = jnp.asarray(LOG2E * slope[None, :] * ((col % TQ)[None, :]
                                                    - (np.arange(nh_p)[:, None] * CMP_STRIDE + CMP_BLOCK - 1)), F32)
    def bf16_terms(x):
        terms, r = [], np.asarray(x, np.float64)
        for _ in range(3):
            tb = np.asarray(r, np.float32).astype(BF16).astype(np.float64)
            terms.append(tb)
            r = r - tb
        return terms
    coef = np.zeros((SUBLANES, NCOL), np.float64)
    coef[0:3] = np.stack(bf16_terms(LOG2E * SLC_BLOCK * slope))
    coef[3:6] = np.stack(bf16_terms(LOG2E * slope))
    c["coef"] = jnp.asarray(coef, F32)
    ntile = t_prompt // TQ
    kpos = np.arange(t_prompt).reshape(ntile, TQ)
    fk = np.zeros((ntile, TQ, LANES), np.float32)
    fk[:, :, 0:3] = (kpos // SLC_BLOCK)[:, :, None]
    fk[:, :, 3:6] = (kpos % SLC_BLOCK)[:, :, None]
    fk[:, :, 6] = (np.arange(TQ) < SLC_BLOCK)[None, :]
    fk[:, :, 7] = (np.arange(TQ) >= SLC_BLOCK)[None, :]
    c["fk"] = jnp.asarray(fk, BF16)
    rel = np.arange(TQ)[:, None] - (col % TQ)[None, :]
    c["amask"] = jnp.asarray(np.stack([np.zeros(rel.shape), np.where(rel > 0, -MASK_BIG, 0.0),
                                       np.where(rel < 0, -MASK_BIG, 0.0), np.full(rel.shape, -MASK_BIG)]), F32)
    nh_s = (past_len + 1 + CMP_STRIDE - 1) // CMP_STRIDE
    nhp_s = -(-nh_s // LANES) * LANES
    n_slc_s = past_len // SLC_BLOCK + 1
    js_s = -(-n_slc_s // LANES) * LANES
    ov_s = overlap(nhp_s, n_slc_s, js_s)
    ov_s[nh_s - 1:, :] = 0.0
    c["ov_s"] = jnp.asarray(ov_s, BF16)
    c["u_s"] = jnp.asarray(np.triu(np.ones((js_s, js_s), np.float32), 1), BF16)
    tm = np.zeros((LANES, ATT_WIDTH), np.float32)
    for h in range(N_HEADS):
        g = h // HG
        for d in range(HEAD_DIM):
            tm[g * HEAD_DIM + d, h * HEAD_DIM + d] = 1.0
    c["tile_m"] = jnp.asarray(tm, BF16)
    return c


def _pack_w_in(w_in):
    depth, d, _ = w_in.shape
    cuts = np.cumsum([ATT_WIDTH] + [LANES] * 6 + [3 * N_HEADS, SSM_INNER, CONV_DIM])
    q, kv, gt, z, xbc, dtc = (w_in[..., :cuts[0]], w_in[..., cuts[0]:cuts[6]], w_in[..., cuts[6]:cuts[7]],
                              w_in[..., cuts[7]:cuts[8]], w_in[..., cuts[8]:cuts[9]], w_in[..., cuts[9]:])
    pad = jnp.zeros((depth, d, P_W - P_GD - gt.shape[-1] - dtc.shape[-1]), w_in.dtype)
    return jnp.concatenate([q, kv, z, xbc, gt, dtc, pad], axis=-1).astype(BF16)


def _pack_compress(w1, b1, w2):
    depth = w1.shape[0]
    zeros = jnp.zeros((depth, CMP_STRIDE, HEAD_DIM, CMP_HID), w1.dtype)
    parts = []
    for ab in range(2):
        w = w1[:, ab * CMP_STRIDE:(ab + 1) * CMP_STRIDE]
        top = jnp.concatenate([w, zeros], axis=3)
        bot = jnp.concatenate([zeros, w], axis=3)
        parts.append(jnp.concatenate([top, bot], axis=2))
    w1b = jnp.concatenate(parts, axis=3)
    w1b = w1b.reshape(depth, CMP_STRIDE // 2, 2 * LANES, 4 * CMP_HID).astype(BF16)
    b1b = jnp.concatenate([b1, b1], axis=1).reshape(depth, 1, 2 * CMP_HID)
    z2 = jnp.zeros_like(w2)
    def blockdiag(rep):
        top = jnp.concatenate([w2] * rep + [z2] * rep, axis=2)
        bot = jnp.concatenate([z2] * rep + [w2] * rep, axis=2)
        return jnp.concatenate([top, bot], axis=1).astype(BF16)
    return w1b, b1b, blockdiag(1), blockdiag(HG)


def _lane_rows(vals, offset, width):
    depth, n = vals.shape
    return jnp.pad(vals.astype(F32), ((0, 0), (offset, width - offset - n))).reshape(depth, 1, width)


def _token_minor(a):
    lead = a.shape[:-3]
    n = len(lead)
    return jnp.transpose(a, tuple(range(n)) + (n + 1, n + 2, n)).reshape(lead + (N_KV * HEAD_DIM, a.shape[-3]))


def _token_major(a):
    lead = a.shape[:-2]
    n = len(lead)
    a = a.reshape(lead + (N_KV, HEAD_DIM, a.shape[-1]))
    return jnp.transpose(a, tuple(range(n)) + (n + 2, n, n + 1))


def kernel(x_prompt, x_sample, cache_cmp_k, cache_cmp_v, cache_slc_k, cache_slc_v, state_win_k, state_win_v, state_ssm, state_ssm_conv, state_ffn_conv, page_table, c_prompt, c_sample, ada_w, ada_b, norm1_g, norm2_g, w_in, cmpk_w1, cmpk_b1, cmpk_w2, cmpv_w1, cmpv_b1, cmpv_w2, ssm_conv_w, ssm_conv_b, dt_bias, a_log, d_skip, ssm_norm_g, w_out, ffn_w_up, ffn_conv_w, ffn_conv_b, ffn_w_down, final_g):
    bp, tp, d = x_prompt.shape
    ns = x_sample.shape[0]
    depth = w_in.shape[0]
    page = cache_cmp_k.shape[2]
    past_len = page_table.shape[1] * page
    assert x_sample.shape[1] == 1 and d == D_MODEL
    assert tp % SSM_CHUNK == 0 and tp >= WINDOW and past_len >= WINDOW and page % SLC_BLOCK == 0
    consts = _constants(tp, past_len)
    caches_t = [_token_minor(c) for c in (cache_cmp_k, cache_cmp_v, cache_slc_k, cache_slc_v)]
    win_t = [_token_minor(w) for w in (state_win_k, state_win_v)]

    mod = _mod(jnp.concatenate([c_sample, c_prompt], axis=0), ada_w, ada_b)
    mod_rows_p = mod.reshape(depth, ns + bp, 1, 6 * d)
    mod_rows_s = mod.reshape(depth, 1, ns + bp, 6 * d)
    tm_p = 512 if tp % 512 == 0 else SSM_CHUNK

    w_packed = _pack_w_in(w_in)
    w1k, b1k, w2k, w2k4 = _pack_compress(cmpk_w1, cmpk_b1, cmpk_w2)
    w1v, b1v, w2v, w2v4 = _pack_compress(cmpv_w1, cmpv_b1, cmpv_w2)
    norm1 = norm1_g.reshape(depth, 1, d)
    sw_all = dict(conv_w=ssm_conv_w, conv_b=ssm_conv_b.reshape(depth, 1, CONV_DIM),
                  dt_bias=_lane_rows(dt_bias, GD_DT, LANES), a=_lane_rows(-jnp.exp(a_log.astype(F32)), GD_DT, LANES),
                  d_skip=jnp.repeat(d_skip.astype(F32), HEAD_DIM, axis=1).reshape(depth, 1, SSM_INNER),
                  norm_g=ssm_norm_g.astype(F32).reshape(depth, 1, SSM_INNER))
    fw_all = dict(norm2_g=norm2_g.reshape(depth, 1, d), w_out=w_out.astype(BF16), w_up=ffn_w_up.astype(BF16),
                  conv_w=ffn_conv_w, conv_b=ffn_conv_b.reshape(depth, 1, D_FF), w_down=ffn_w_down.astype(BF16))

    xp = x_prompt
    xs = x_sample.reshape(1, ns, d)
    outs_p, outs_s = [], []
    for l in range(depth):
        last = l == depth - 1
        mod_p = (mod_rows_p, l, ns, False)
        mod_s = (mod_rows_s, l, ns, True)
        at = lambda a: (a, l)
        cw = dict(w1k=at(w1k), b1k=at(b1k), w2k=at(w2k), w1v=at(w1v), b1v=at(b1v), w2v=at(w2v))
        cw4 = dict(w1k=at(w1k), b1k=at(b1k), w2k=at(w2k4), w1v=at(w1v), b1v=at(b1v), w2v=at(w2v4))
        sw = {n: at(a) for n, a in sw_all.items()}
        fw = {n: at(a) for n, a in fw_all.items()}
        fw["final_g"] = final_g.reshape(1, d)

        (qt, kc, vc, kvb, vt, z, xbc, gd, kct, vct, kst, vst, kwt, vwt) = _inproj(
            xp, mod_p, at(norm1), at(w_packed), tm_p, True)
        kcmp, vcmpt = _compress(kc, vc, cw)
        ocmpt, selt = _cmp_sel(qt, kcmp, vcmpt, consts)
        oatt = _attn(qt, kvb, vt, selt, ocmpt, gd, consts)
        yssm, h_p = _ssd(xbc, z, gd, sw, consts)
        xp, ug_tail = _mix_ffn(xp, oatt, yssm, mod_p, fw, tm_p, None, last)
        outs_p.append((kct, vct, kst, vst, kwt[:, :, tp - WINDOW:], vwt[:, :, tp - WINDOW:], h_p,
                       xbc[:, tp - (SSM_CONV - 1):], ug_tail[:, SUBLANES - (FFN_CONV - 1):]))

        q, kvb, z, xbc, gd, kc, vc, ks, vs, kw, vw = _inproj(xs, mod_s, at(norm1), at(w_packed), ns, False)
        per_tok = lambda a: a.reshape(ns, 1, a.shape[-1])
        kvn = per_tok(jnp.concatenate([kc, vc, ks, vs, kw, vw], axis=2))
        q, z, xbc, gd = per_tok(q), per_tok(z), per_tok(xbc), per_tok(gd)
        ocmp, idx = _s_cmp(l, page_table, q, kvn, gd, caches_t[0], caches_t[1], cw4, consts, past_len)
        oatt, wk_new, wv_new = _s_slc_win(l, idx, page_table, q, kvn, gd, ocmp, win_t[0], win_t[1],
                                          caches_t[2], caches_t[3], consts, past_len)
        yssm, h_s = _s_ssd(l, xbc, state_ssm_conv[l], z, gd, state_ssm, sw, consts)
        prev_ffn = (state_ffn_conv[l][:, 0].reshape(1, ns, D_FF), state_ffn_conv[l][:, 1].reshape(1, ns, D_FF))
        xs, ug_s = _mix_ffn(xs, oatt.reshape(1, ns, ATT_WIDTH), yssm.reshape(1, ns, SSM_INNER), mod_s, fw, ns,
                            prev_ffn, last)
        s4 = lambda a: a.reshape(ns, 1, N_KV, HEAD_DIM)
        outs_s.append((s4(kc), s4(vc), s4(ks), s4(vs), wk_new, wv_new, h_s,
                       jnp.concatenate([state_ssm_conv[l][:, 1:], xbc.reshape(ns, 1, CONV_DIM)], axis=1),
                       jnp.concatenate([state_ffn_conv[l][:, 1:], ug_s.reshape(ns, 1, D_FF)], axis=1)))

    stack = lambda seq, i: jnp.stack([st[i] for st in seq])
    res = [xp, xs.reshape(ns, 1, d)]
    for i in range(9):
        p_i, s_i = stack(outs_p, i), stack(outs_s, i)
        if i < 6:
            p_i = _token_major(p_i)
        if i in (4, 5):
            s_i = _token_major(s_i)
        res += [p_i, s_i]
    return tuple(res)
```
